```python
import math
import jax
import jax.numpy as jnp
from jax import lax
import numpy as np

D_MODEL = 2048
BATCH = 4
SEQ = 2048
DEPTH = 1
DEC_BATCH = 128
DEC_SEQ = 4
PAST_LEN = 2048
PAGE_SIZE = 128

N_HEADS = 16
N_KV_HEADS = 4
D_HEAD = 64
ATTN_W = N_HEADS * D_HEAD
KV_W = N_KV_HEADS * D_HEAD
IDX_HEADS = 8
D_IDX = 64
TOPK_MAX = 256
Q_BLOCK = 128
S5_W = D_MODEL // 2
S5_GROUP = 16
S5_GROUPS = S5_W // S5_GROUP
S5_STATE = 64
STEP_MIN = 0.001
STEP_MAX = 0.1
D_FF = 11 * D_MODEL // 4
CONV_W = 3
EPS = 1e-6
IN_SPLITS = (ATTN_W, KV_W, KV_W, IDX_HEADS * D_IDX, D_IDX, IDX_HEADS, S5_W, D_MODEL, D_MODEL)
IN_COLS = sum(IN_SPLITS)

kernel_name = "hybrid_dsa_s5_convffn_step"


def _rmsnorm(x, g):
    xf = x.astype(jnp.float32)
    y = xf * lax.rsqrt(jnp.mean(xf * xf, axis=-1, keepdims=True) + EPS) * g.astype(jnp.float32)
    return y.astype(x.dtype)


def _dsa_block(q, qi, wi, qpos, k, v, ki, kpos):
    b, t = q.shape[0], q.shape[1]
    k_sel = min(TOPK_MAX, k.shape[1] // 4)
    logits = jax.nn.relu(jnp.einsum('bthd,bsd->btsh', qi.astype(jnp.float32), ki.astype(jnp.float32)) * (D_IDX ** -0.5))
    score = jnp.einsum('btsh,bth->bts', logits, wi.astype(jnp.float32) * (IDX_HEADS ** -0.5))
    causal = kpos[None, :] <= qpos[:, None]
    score = jnp.where(causal[None], score, -jnp.inf)
    _, idx = lax.top_k(score, k_sel)
    valid = kpos[idx] <= qpos[None, :, None]
    gather = jax.vmap(lambda rows, ii: rows[ii])
    k_g = gather(k, idx).astype(jnp.float32)
    v_g = gather(v, idx).astype(jnp.float32)
    qg = q.reshape(b, t, N_KV_HEADS, N_HEADS // N_KV_HEADS, D_HEAD).astype(jnp.float32)
    s = jnp.einsum('btngd,btjnd->btngj', qg, k_g) * (D_HEAD ** -0.5)
    s = jnp.where(valid[:, :, None, None, :], s, -jnp.inf)
    p = jax.nn.softmax(s, axis=-1)
    o = jnp.einsum('btngj,btjnd->btngd', p, v_g)
    return o.reshape(b, t, ATTN_W).astype(q.dtype)


def _dsa_attend(q, qi, wi, qpos, k, v, ki, kpos):
    b, t = q.shape[0], q.shape[1]
    if t > Q_BLOCK and t % Q_BLOCK == 0:
        nb = t // Q_BLOCK

        def blocks(a):
            return jnp.moveaxis(a.reshape(b, nb, Q_BLOCK, *a.shape[2:]), 1, 0)

        out = lax.map(lambda args: _dsa_block(*args, k, v, ki, kpos),
                      (blocks(q), blocks(qi), blocks(wi), qpos.reshape(nb, Q_BLOCK)))
        return jnp.moveaxis(out, 0, 1).reshape(b, t, ATTN_W)
    return _dsa_block(q, qi, wi, qpos, k, v, ki, kpos)


def _lin_combine(left, right):
    a1, b1 = left
    a2, b2 = right
    return a2 * a1, a2 * b1 + b2


def _s5(u, x0_re, x0_im, a_re, a_im, log_step, b_re, b_im, c_re, c_im, d_skip):
    b, t = u.shape[0], u.shape[1]
    f32 = jnp.float32
    uf = u.astype(f32).reshape(b, t, S5_GROUPS, S5_GROUP)
    lam = lax.complex(a_re.astype(f32), a_im.astype(f32))
    step = jnp.exp(log_step.astype(f32))[:, None]
    lam_bar = jnp.exp(lam * step)
    b_bar = ((lam_bar - 1.0) / lam)[..., None] * lax.complex(b_re.astype(f32), b_im.astype(f32))
    bu = jnp.einsum('gpc,btgc->btgp', b_bar, uf.astype(jnp.complex64))
    x0 = lax.complex(x0_re.astype(f32), x0_im.astype(f32))
    bu = bu.at[:, 0].add(lam_bar[None] * x0)
    decay = jnp.broadcast_to(lam_bar, bu.shape)
    _, xs = lax.associative_scan(_lin_combine, (decay, bu), axis=1)
    c = lax.complex(c_re.astype(f32), c_im.astype(f32))
    y = jnp.real(jnp.einsum('gcp,btgp->btgc', c, xs)) + d_skip.astype(f32).reshape(S5_GROUPS, S5_GROUP) * uf
    last = xs[:, -1]
    return y.reshape(b, t, S5_W), jnp.real(last), jnp.imag(last)


def _conv_ffn(h, conv_state, w_up, conv_w, conv_b, w_down):
    t = h.shape[1]
    up = h @ w_up
    padded = jnp.concatenate([conv_state.astype(up.dtype), up], axis=1)
    mixed = conv_b
    for j in range(CONV_W):
        mixed = mixed + conv_w[j] * padded[:, j:j + t]
    gate, val = jnp.split(mixed, 2, axis=-1)
    out = (jax.nn.silu(gate) * val) @ w_down
    return out, padded[:, t:]


def _split_points():
    pts, acc = [], 0
    for w in IN_SPLITS[:-1]:
        acc += w
        pts.append(acc)
    return pts


def _layer(x, k_past, v_past, ki_past, s_re, s_im, conv_state, lp):
    (norm_mix, w_in, w_attn_proj, a_re, a_im, log_step, b_re, b_im, c_re, c_im, d_skip,
     w_glu, w_out, norm_ffn, w_up, conv_w, conv_b, w_down) = lp
    b, t = x.shape[0], x.shape[1]
    h = _rmsnorm(x, norm_mix)
    proj = h @ w_in
    q, k, v, qi, ki, wi, u, ga, gb = jnp.split(proj, _split_points(), axis=-1)
    q = q.reshape(b, t, N_HEADS, D_HEAD)
    k = k.reshape(b, t, N_KV_HEADS, D_HEAD)
    v = v.reshape(b, t, N_KV_HEADS, D_HEAD)
    qi = qi.reshape(b, t, IDX_HEADS, D_IDX)
    past = k_past.shape[1]
    k_all = jnp.concatenate([k_past.astype(k.dtype), k], axis=1)
    v_all = jnp.concatenate([v_past.astype(v.dtype), v], axis=1)
    ki_all = jnp.concatenate([ki_past.astype(ki.dtype), ki], axis=1)
    kpos = jnp.arange(past + t, dtype=jnp.int32)
    qpos = past + jnp.arange(t, dtype=jnp.int32)
    attn = _dsa_attend(q, qi, wi, qpos, k_all, v_all, ki_all, kpos)
    y_a = attn @ w_attn_proj
    s5_y, new_re, new_im = _s5(u, s_re, s_im, a_re, a_im, log_step, b_re, b_im, c_re, c_im, d_skip)
    z = jax.nn.gelu(s5_y).astype(x.dtype)
    glu_a, glu_g = jnp.split(z @ w_glu, 2, axis=-1)
    y_b = glu_a * jax.nn.sigmoid(glu_g)
    merged = jax.nn.sigmoid(ga) * y_a + jax.nn.sigmoid(gb) * y_b
    x = x + merged @ w_out
    ffn, new_conv = _conv_ffn(_rmsnorm(x, norm_ffn), conv_state, w_up, conv_w, conv_b, w_down)
    x = x + ffn
    return x, k, v, ki, new_re, new_im, new_conv


def setup_inputs(seed: int = 0) -> dict:
    key = jax.random.key(seed)
    ks = iter(jax.random.split(key, 40))
    f32 = jnp.float32

    def nrm(shape, scale):
        return jax.random.normal(next(ks), shape, f32) * scale

    n_pages = PAST_LEN // PAGE_SIZE
    n_pool = (DEC_BATCH * n_pages * 5) // 4
    page_table = jax.random.permutation(next(ks), n_pool)[:DEC_BATCH * n_pages].reshape(DEC_BATCH, n_pages).astype(jnp.int32)
    a_im_init = jnp.pi * jnp.arange(S5_STATE, dtype=f32)
    inp = {
        "x_prompt": nrm((BATCH, SEQ, D_MODEL), 1.0),
        "x_sample": nrm((DEC_BATCH, DEC_SEQ, D_MODEL), 1.0),
        "cache_k": nrm((DEPTH, n_pool, PAGE_SIZE, N_KV_HEADS, D_HEAD), 1.0),
        "cache_v": nrm((DEPTH, n_pool, PAGE_SIZE, N_KV_HEADS, D_HEAD), 1.0),
        "cache_kidx": nrm((DEPTH, n_pool, PAGE_SIZE, D_IDX), 1.0),
        "state_s5_re": nrm((DEPTH, DEC_BATCH, S5_GROUPS, S5_STATE), 0.3),
        "state_s5_im": nrm((DEPTH, DEC_BATCH, S5_GROUPS, S5_STATE), 0.3),
        "state_ffn_conv": nrm((DEPTH, DEC_BATCH, CONV_W - 1, 2 * D_FF), 1.0),
        "page_table": page_table,
        "norm_mix": 1.0 + nrm((DEPTH, D_MODEL), 0.02),
        "w_in": nrm((DEPTH, D_MODEL, IN_COLS), D_MODEL ** -0.5),
        "w_attn_proj": nrm((DEPTH, ATTN_W, D_MODEL), ATTN_W ** -0.5),
        "s5_a_re": -0.5 + nrm((DEPTH, S5_GROUPS, S5_STATE), 0.01),
        "s5_a_im": a_im_init + nrm((DEPTH, S5_GROUPS, S5_STATE), 0.01),
        "s5_log_step": jax.random.uniform(next(ks), (DEPTH, S5_GROUPS), f32, math.log(STEP_MIN), math.log(STEP_MAX)),
        "s5_b_re": nrm((DEPTH, S5_GROUPS, S5_STATE, S5_GROUP), (2 * S5_GROUP) ** -0.5),
        "s5_b_im": nrm((DEPTH, S5_GROUPS, S5_STATE, S5_GROUP), (2 * S5_GROUP) ** -0.5),
        "s5_c_re": nrm((DEPTH, S5_GROUPS, S5_GROUP, S5_STATE), (2 * S5_STATE) ** -0.5),
        "s5_c_im": nrm((DEPTH, S5_GROUPS, S5_GROUP, S5_STATE), (2 * S5_STATE) ** -0.5),
        "s5_d": nrm((DEPTH, S5_W), 1.0),
        "w_glu": nrm((DEPTH, S5_W, 2 * D_MODEL), S5_W ** -0.5),
        "w_out": nrm((DEPTH, D_MODEL, D_MODEL), D_MODEL ** -0.5),
        "norm_ffn": 1.0 + nrm((DEPTH, D_MODEL), 0.02),
        "w_up": nrm((DEPTH, D_MODEL, 2 * D_FF), D_MODEL ** -0.5),
        "conv_w": nrm((DEPTH, CONV_W, 2 * D_FF), CONV_W ** -0.5),
        "conv_b": nrm((DEPTH, 2 * D_FF), 0.02),
        "w_down": nrm((DEPTH, D_FF, D_MODEL), D_FF ** -0.5),
        "norm_final": 1.0 + nrm((D_MODEL,), 0.02),
    }
    return inp


def reference(x_prompt, x_sample, cache_k, cache_v, cache_kidx, state_s5_re, state_s5_im, state_ffn_conv,
              page_table, norm_mix, w_in, w_attn_proj, s5_a_re, s5_a_im, s5_log_step, s5_b_re, s5_b_im,
              s5_c_re, s5_c_im, s5_d, w_glu, w_out, norm_ffn, w_up, conv_w, conv_b, w_down, norm_final):
    xp, xs = x_prompt, x_sample
    bp, bs = xp.shape[0], xs.shape[0]
    kp_l, vp_l, kip_l, srp_l, sip_l, cp_l = [], [], [], [], [], []
    ks_l, vs_l, kis_l, srs_l, sis_l, cs_l = [], [], [], [], [], []
    for l in range(DEPTH):
        lp = (norm_mix[l], w_in[l], w_attn_proj[l], s5_a_re[l], s5_a_im[l], s5_log_step[l],
              s5_b_re[l], s5_b_im[l], s5_c_re[l], s5_c_im[l], s5_d[l], w_glu[l], w_out[l],
              norm_ffn[l], w_up[l], conv_w[l], conv_b[l], w_down[l])
        zk = jnp.zeros((bp, 0, N_KV_HEADS, D_HEAD), xp.dtype)
        zki = jnp.zeros((bp, 0, D_IDX), xp.dtype)
        zs = jnp.zeros((bp, S5_GROUPS, S5_STATE), jnp.float32)
        zc = jnp.zeros((bp, CONV_W - 1, 2 * D_FF), xp.dtype)
        xp, kp, vp, kip, srp, sip, cp = _layer(xp, zk, zk, zki, zs, zs, zc, lp)
        pk = cache_k[l][page_table].reshape(bs, -1, N_KV_HEADS, D_HEAD)
        pv = cache_v[l][page_table].reshape(bs, -1, N_KV_HEADS, D_HEAD)
        pki = cache_kidx[l][page_table].reshape(bs, -1, D_IDX)
        xs, ksm, vsm, kism, srs, sis, cs = _layer(xs, pk, pv, pki, state_s5_re[l], state_s5_im[l],
                                                 state_ffn_conv[l], lp)
        kp_l.append(kp); vp_l.append(vp); kip_l.append(kip); srp_l.append(srp); sip_l.append(sip); cp_l.append(cp)
        ks_l.append(ksm); vs_l.append(vsm); kis_l.append(kism); srs_l.append(srs); sis_l.append(sis); cs_l.append(cs)
    y_prompt = _rmsnorm(xp, norm_final)
    y_sample = _rmsnorm(xs, norm_final)
    return (y_prompt, y_sample,
            jnp.stack(kp_l), jnp.stack(vp_l), jnp.stack(kip_l), jnp.stack(srp_l), jnp.stack(sip_l), jnp.stack(cp_l),
            jnp.stack(ks_l), jnp.stack(vs_l), jnp.stack(kis_l), jnp.stack(srs_l), jnp.stack(sis_l), jnp.stack(cs_l))
```

```python
import functools
import math

import jax
import jax.numpy as jnp
from jax import lax
from jax.experimental import pallas as pl
from jax.experimental.pallas import tpu as pltpu

F32 = jnp.float32
BF16 = jnp.bfloat16

EPS = 1e-6
TOPK_MAX = 256
N_HEADS = 16
N_KV_HEADS = 4
IDX_HEADS = 8
S5_GROUP = 16
STEP_CHUNK = 8
LANES = 128
GROUPS_PER_TILE = LANES // S5_GROUP
FF_TILE = 512
VMEM_LIMIT = 48 * 1024 * 1024
NEG_INF = float("-inf")
INT_MIN = -2 ** 31


def _params(*sem):
    return pltpu.CompilerParams(dimension_semantics=sem, vmem_limit_bytes=VMEM_LIMIT)


def _sigmoid(x):
    return 1.0 / (1.0 + jnp.exp(-x))


def _gelu_tanh(x):
    c = math.sqrt(2.0 / math.pi)
    return 0.5 * x * (1.0 + jnp.tanh(c * (x + 0.044715 * (x * x * x))))


def _dot_nt(a, b):
    return lax.dot_general(a, b, (((1,), (1,)), ((), ())), preferred_element_type=F32)


def _rms_kernel(x_ref, g_ref, o_ref):
    x = x_ref[...]
    ms = jnp.mean(x * x, axis=-1, keepdims=True)
    o_ref[...] = (x * lax.rsqrt(ms + EPS) * g_ref[...]).astype(o_ref.dtype)


def _rmsnorm(x, g, tm, out_dtype):
    m, d = x.shape
    return pl.pallas_call(
        _rms_kernel,
        grid=(m // tm,),
        in_specs=[pl.BlockSpec((tm, d), lambda i: (i, 0)),
                  pl.BlockSpec((1, d), lambda i: (0, 0))],
        out_specs=pl.BlockSpec((tm, d), lambda i: (i, 0)),
        out_shape=jax.ShapeDtypeStruct((m, d), out_dtype),
        compiler_params=_params("arbitrary"),
        name="rmsnorm",
    )(x, g.reshape(1, d))


def _mm_kernel(h_ref, w_ref, o_ref):
    o_ref[...] = jnp.dot(h_ref[...], w_ref[...], preferred_element_type=F32).astype(o_ref.dtype)


def _matmul(h, w, tm, tn, out_dtype, name):
    m, k = h.shape
    n = w.shape[1]
    return pl.pallas_call(
        _mm_kernel,
        grid=(n // tn, m // tm),
        in_specs=[pl.BlockSpec((tm, k), lambda j, i: (i, 0)),
                  pl.BlockSpec((k, tn), lambda j, i: (0, j))],
        out_specs=pl.BlockSpec((tm, tn), lambda j, i: (i, j)),
        out_shape=jax.ShapeDtypeStruct((m, n), out_dtype),
        compiler_params=_params("arbitrary", "arbitrary"),
        name=name,
    )(h, w)


def _count(mask):
    return jnp.sum(jnp.where(mask, 1.0, 0.0), axis=1, keepdims=True)


def _topk_mask(score, col, k_sel, n_cols):
    bits = lax.bitcast_convert_type(score, jnp.int32)
    keys = jnp.where(bits < 0, bits ^ jnp.int32(0x7FFFFFFF), bits)
    kf = float(k_sel)
    rows = score.shape[0]

    cand0 = jnp.where(_count(keys >= 0) >= kf, jnp.int32(0), jnp.int32(INT_MIN))
    cand0 = jnp.broadcast_to(cand0, (rows, 1)).astype(jnp.int32)

    def value_step(it, cand):
        trial = cand + lax.shift_left(jnp.int32(1), jnp.int32(30) - it)
        return jnp.where(_count(keys >= trial) >= kf, trial, cand)

    thr = lax.fori_loop(0, 31, value_step, cand0)

    above = keys > thr
    ties = keys == thr
    need = kf - _count(above)
    n_bits = max(1, (n_cols - 1).bit_length())

    def index_step(it, m):
        trial = m + lax.shift_left(jnp.int32(1), jnp.int32(n_bits - 1) - it)
        taken = _count(jnp.logical_and(ties, col < trial))
        return jnp.where(taken <= need - 1.0, trial, m)

    last = lax.fori_loop(0, n_bits, index_step, jnp.zeros((rows, 1), jnp.int32))
    return jnp.logical_or(above, jnp.logical_and(ties, col <= last))


def _index_scores(qi_rows, w_cols, ki, rows):
    lg = _dot_nt(qi_rows, ki)
    score = jnp.zeros((rows, ki.shape[0]), F32)
    for h in range(IDX_HEADS):
        score = score + jnp.maximum(lg[h * rows:(h + 1) * rows], 0.0) * w_cols[h]
    return score


def _masked_attention(q, keep, k_ref, v_ref, d_head, out_ref_write):
    rows = q.shape[0]
    group = N_HEADS // N_KV_HEADS
    scale = d_head ** -0.5
    bias = jnp.where(keep, 0.0, NEG_INF)
    bias_g = jnp.concatenate([bias] * group, axis=0)
    for n in range(N_KV_HEADS):
        qs = jnp.concatenate(
            [q[:, (n * group + g) * d_head:(n * group + g + 1) * d_head] for g in range(group)],
            axis=0)
        qs = (qs * scale).astype(BF16)
        kn = k_ref[:, n * d_head:(n + 1) * d_head]
        vn = v_ref[:, n * d_head:(n + 1) * d_head]
        s = _dot_nt(qs, kn) + bias_g
        m = jnp.max(s, axis=1, keepdims=True)
        p = jnp.exp(s - m)
        l = jnp.sum(p, axis=1, keepdims=True)
        o = jnp.dot(p.astype(BF16), vn, preferred_element_type=F32) / l
        for g in range(group):
            out_ref_write(n * group + g, o[g * rows:(g + 1) * rows])


def _prompt_attn_kernel(q_ref, qi_ref, kwq_ref, kwk_ref, k_ref, v_ref, o_ref, kb_ref, vb_ref,
                        *, tq, n_keys, k_sel, d_head, d_idx, q_block0):
    i = pl.program_id(1)
    kb_ref[...] = k_ref[...].astype(BF16)
    vb_ref[...] = v_ref[...].astype(BF16)

    qi = qi_ref[...]
    qi_rows = jnp.concatenate([qi[:, h * d_idx:(h + 1) * d_idx] for h in range(IDX_HEADS)],
                              axis=0).astype(BF16)
    w_scale = (d_idx ** -0.5) * (IDX_HEADS ** -0.5)
    kwq = kwq_ref[...]
    w_cols = [kwq[:, d_idx + h:d_idx + h + 1] * w_scale for h in range(IDX_HEADS)]
    ki = kwk_ref[:, 0:d_idx].astype(BF16)
    score = _index_scores(qi_rows, w_cols, ki, tq)

    col = lax.broadcasted_iota(jnp.int32, (tq, n_keys), 1)
    qpos = (q_block0 + i) * tq + lax.broadcasted_iota(jnp.int32, (tq, n_keys), 0)
    causal = col <= qpos
    score = jnp.where(causal, score, NEG_INF)
    keep = jnp.logical_and(_topk_mask(score, col, k_sel, n_keys), causal)

    def write(head, val):
        o_ref[:, head * d_head:(head + 1) * d_head] = val.astype(o_ref.dtype)

    _masked_attention(q_ref[...], keep, kb_ref, vb_ref, d_head, write)


def _prompt_attention(q, small, batch, seq, d_head, d_idx, k_sel, tq):
    attn_w = q.shape[1]
    kv_w = N_KV_HEADS * d_head
    qi_w = IDX_HEADS * d_idx
    nq = seq // tq
    small3 = small.reshape(batch, seq, small.shape[1])
    kw_blk = (qi_w + 2 * kv_w) // LANES
    kern = functools.partial(_prompt_attn_kernel, tq=tq, n_keys=seq, k_sel=k_sel,
                             d_head=d_head, d_idx=d_idx, q_block0=0)
    return pl.pallas_call(
        kern,
        grid=(batch, nq),
        in_specs=[
            pl.BlockSpec((tq, attn_w), lambda b, i: (b * nq + i, 0)),
            pl.BlockSpec((tq, qi_w), lambda b, i: (b * nq + i, 0)),
            pl.BlockSpec((tq, LANES), lambda b, i: (b * nq + i, kw_blk)),
            pl.BlockSpec((None, seq, LANES), lambda b, i: (b, 0, kw_blk)),
            pl.BlockSpec((None, seq, kv_w), lambda b, i: (b, 0, qi_w // kv_w)),
            pl.BlockSpec((None, seq, kv_w), lambda b, i: (b, 0, qi_w // kv_w + 1)),
        ],
        out_specs=pl.BlockSpec((tq, attn_w), lambda b, i: (b * nq + i, 0)),
        out_shape=jax.ShapeDtypeStruct((batch * seq, attn_w), BF16),
        scratch_shapes=[pltpu.VMEM((seq, kv_w), BF16), pltpu.VMEM((seq, kv_w), BF16)],
        compiler_params=_params("arbitrary", "arbitrary"),
        name="prompt_attention",
    )(q, small, small, small3, small3, small3)


def _sample_attn_kernel(pt_ref, q_ref, qi_ref, kw_ref, kn_ref, vn_ref, kin_ref, *rest,
                        n_pages, page, t_new, rows, tail, k_sel, d_head, d_idx):
    del pt_ref
    kpages = rest[:n_pages]
    vpages = rest[n_pages:2 * n_pages]
    ipages = rest[2 * n_pages:3 * n_pages]
    o_ref, kb_ref, vb_ref, ib_ref = rest[3 * n_pages:]
    past = n_pages * page
    n_keys = past + tail

    for p in range(n_pages):
        kb_ref[p * page:(p + 1) * page, :] = kpages[p][...].astype(BF16)
        vb_ref[p * page:(p + 1) * page, :] = vpages[p][...].astype(BF16)
        ib_ref[p * page:(p + 1) * page, :] = ipages[p][...].astype(BF16)

    def put_tail(dst, new_ref):
        new = new_ref[...]
        pad = jnp.zeros((tail - new.shape[0], new.shape[1]), F32)
        dst[past:past + tail, :] = jnp.concatenate([new, pad], axis=0).astype(BF16)

    put_tail(kb_ref, kn_ref)
    put_tail(vb_ref, vn_ref)
    put_tail(ib_ref, kin_ref)

    qi = qi_ref[...]
    qi_rows = jnp.concatenate([qi[:, h * d_idx:(h + 1) * d_idx] for h in range(IDX_HEADS)],
                              axis=0).astype(BF16)
    w_scale = (d_idx ** -0.5) * (IDX_HEADS ** -0.5)
    kw = kw_ref[...]
    w_cols = [kw[:, d_idx + h:d_idx + h + 1] * w_scale for h in range(IDX_HEADS)]
    score = _index_scores(qi_rows, w_cols, ib_ref[...], rows)

    col = lax.broadcasted_iota(jnp.int32, (rows, n_keys), 1)
    t = jnp.minimum(lax.broadcasted_iota(jnp.int32, (rows, n_keys), 0), t_new - 1)
    causal = col <= past + t
    score = jnp.where(causal, score, NEG_INF)
    keep = jnp.logical_and(_topk_mask(score, col, k_sel, n_keys), causal)

    def write(head, val):
        o_ref[:, head * d_head:(head + 1) * d_head] = val.astype(o_ref.dtype)

    _masked_attention(q_ref[...], keep, kb_ref, vb_ref, d_head, write)


def _page_map(b, pt, *, p, n_pages):
    return (pt[b * n_pages + p], 0, 0)


def _sample_attention(q, small, cache_k, cache_v, cache_ki, page_table, t_new, d_head, d_idx):
    nb, n_pages = page_table.shape
    n_pool, page = cache_k.shape[0], cache_k.shape[1]
    attn_w = q.shape[1]
    kv_w = N_KV_HEADS * d_head
    qi_w = IDX_HEADS * d_idx
    rows = 8
    new_rows = 16
    tail = LANES
    past = n_pages * page
    k_sel = min(TOPK_MAX, (past + t_new) // 4)

    def pad_rows(a, r):
        a = a.reshape(nb, t_new, a.shape[-1])
        return jnp.pad(a, ((0, 0), (0, r - t_new), (0, 0)))

    q8 = pad_rows(q, rows)
    qi8 = pad_rows(small[:, :qi_w], rows)
    kw8 = pad_rows(small[:, qi_w + 2 * kv_w:], rows)
    kn = pad_rows(small[:, qi_w:qi_w + kv_w], new_rows)
    vn = pad_rows(small[:, qi_w + kv_w:qi_w + 2 * kv_w], new_rows)
    kin = pad_rows(small[:, qi_w + 2 * kv_w:qi_w + 2 * kv_w + d_idx], new_rows)

    ck = cache_k.reshape(n_pool, page, kv_w)
    cv = cache_v.reshape(n_pool, page, kv_w)

    def row_spec(r, w):
        return pl.BlockSpec((None, r, w), lambda b, pt: (b, 0, 0))

    def page_specs(w):
        return [pl.BlockSpec((None, page, w), functools.partial(_page_map, p=p, n_pages=n_pages))
                for p in range(n_pages)]

    kern = functools.partial(_sample_attn_kernel, n_pages=n_pages, page=page, t_new=t_new,
                             rows=rows, tail=tail, k_sel=k_sel, d_head=d_head, d_idx=d_idx)
    grid_spec = pltpu.PrefetchScalarGridSpec(
        num_scalar_prefetch=1,
        grid=(nb,),
        in_specs=[row_spec(rows, attn_w), row_spec(rows, qi_w), row_spec(rows, LANES),
                  row_spec(new_rows, kv_w), row_spec(new_rows, kv_w), row_spec(new_rows, d_idx)]
                 + page_specs(kv_w) + page_specs(kv_w) + page_specs(d_idx),
        out_specs=pl.BlockSpec((None, rows, attn_w), lambda b, pt: (b, 0, 0)),
        scratch_shapes=[pltpu.VMEM((past + tail, kv_w), BF16),
                        pltpu.VMEM((past + tail, kv_w), BF16),
                        pltpu.VMEM((past + tail, d_idx), BF16)],
    )
    out = pl.pallas_call(
        kern,
        grid_spec=grid_spec,
        out_shape=jax.ShapeDtypeStruct((nb, rows, attn_w), BF16),
        compiler_params=_params("arbitrary"),
        name="sample_attention",
    )(page_table.reshape(-1), q8, qi8, kw8, kn, vn, kin,
      *([ck] * n_pages), *([cv] * n_pages), *([cache_ki] * n_pages))
    return out[:, :t_new].reshape(nb * t_new, attn_w)


def _s5_step(xr, xi, ar, ai, bu, half):
    br, bi = bu[:, :half], bu[:, half:]
    if xr is None:
        return br, bi
    return ar * xr - ai * xi + br, ar * xi + ai * xr + bi


def _s5_local_kernel(u_ref, b_ref, ar_ref, ai_ref, z_ref, *, steps, width, n_tiles):
    sw = b_ref.shape[2]
    half = sw // 2
    for j in range(n_tiles):
        ar, ai = ar_ref[j:j + 1, :], ai_ref[j:j + 1, :]
        xr = xi = None
        for s in range(steps):
            c0 = s * width + j * LANES
            bu = jnp.dot(u_ref[:, c0:c0 + LANES].astype(BF16), b_ref[j], preferred_element_type=F32)
            xr, xi = _s5_step(xr, xi, ar, ai, bu, half)
        z_ref[:, j * sw:j * sw + half] = xr
        z_ref[:, j * sw + half:(j + 1) * sw] = xi


def _s5_carry_kernel(z_ref, ar_ref, ai_ref, xc_ref, xf_ref, *, batch, n_chunks):
    half = z_ref.shape[1] // 2
    ar, ai = ar_ref[...], ai_ref[...]

    def body(c, carry):
        new = []
        for b in range(batch):
            xr, xi = carry[b]
            row = b * n_chunks + c
            xc_ref[pl.ds(row, 1), :] = jnp.concatenate([xr, xi], axis=1)
            z = z_ref[pl.ds(row, 1), :]
            new.append((ar * xr - ai * xi + z[:, :half], ar * xi + ai * xr + z[:, half:]))
        return tuple(new)

    zero = jnp.zeros((1, half), F32)
    final = lax.fori_loop(0, n_chunks, body, tuple((zero, zero) for _ in range(batch)))
    for b in range(batch):
        xf_ref[b:b + 1, :] = jnp.concatenate(list(final[b]), axis=1)


def _s5_out_kernel(u_ref, x0_ref, b_ref, c_ref, ar_ref, ai_ref, d_ref, y_ref, *maybe_xf,
                   steps, width, n_tiles):
    sw = b_ref.shape[2]
    half = sw // 2
    for j in range(n_tiles):
        ar, ai = ar_ref[j:j + 1, :], ai_ref[j:j + 1, :]
        xr = x0_ref[:, j * sw:j * sw + half]
        xi = x0_ref[:, j * sw + half:(j + 1) * sw]
        d = d_ref[:, j * LANES:(j + 1) * LANES]
        for s in range(steps):
            c0 = s * width + j * LANES
            u = u_ref[:, c0:c0 + LANES]
            bu = jnp.dot(u.astype(BF16), b_ref[j], preferred_element_type=F32)
            xr, xi = _s5_step(xr, xi, ar, ai, bu, half)
            x = jnp.concatenate([xr, xi], axis=1).astype(BF16)
            y_ref[:, c0:c0 + LANES] = jnp.dot(x, c_ref[j], preferred_element_type=F32) + d * u
        if maybe_xf:
            maybe_xf[0][:, j * sw:j * sw + half] = xr
            maybe_xf[0][:, j * sw + half:(j + 1) * sw] = xi


def _s5_params(a_re, a_im, log_step, b_re, b_im, c_re, c_im, chunk_steps):
    g, p = a_re.shape
    nt = g // GROUPS_PER_TILE
    hi = lax.Precision.HIGHEST
    lam = lax.complex(a_re.astype(F32), a_im.astype(F32))
    step = jnp.exp(log_step.astype(F32))[:, None]
    lam_bar = jnp.exp(lam * step)
    lam_chunk = jnp.exp(lam * (step * float(chunk_steps)))
    b_bar = ((lam_bar - 1.0) / lam)[..., None] * lax.complex(b_re.astype(F32), b_im.astype(F32))
    eye = jnp.eye(GROUPS_PER_TILE, dtype=F32)

    bb = jnp.stack([jnp.real(b_bar), jnp.imag(b_bar)], axis=0)
    bb = bb.reshape(2, nt, GROUPS_PER_TILE, p, S5_GROUP)
    b_blk = jnp.einsum("rjgpc,gh->jgcrhp", bb, eye, precision=hi)
    b_blk = b_blk.reshape(nt, LANES, 2 * GROUPS_PER_TILE * p).astype(BF16)

    cc = jnp.stack([c_re.astype(F32), -c_im.astype(F32)], axis=0)
    cc = cc.reshape(2, nt, GROUPS_PER_TILE, S5_GROUP, p)
    c_blk = jnp.einsum("rjgcp,gh->jrgphc", cc, eye, precision=hi)
    c_blk = c_blk.reshape(nt, 2 * GROUPS_PER_TILE * p, LANES).astype(BF16)

    def tiles(z):
        return z.reshape(nt, GROUPS_PER_TILE * p)

    return (b_blk, c_blk, tiles(jnp.real(lam_bar)), tiles(jnp.imag(lam_bar)),
            tiles(jnp.real(lam_chunk)), tiles(jnp.imag(lam_chunk)))


def _state_to_tiles(s_re, s_im):
    nb, g, p = s_re.shape
    nt = g // GROUPS_PER_TILE
    st = jnp.stack([s_re.reshape(nb, nt, GROUPS_PER_TILE * p),
                    s_im.reshape(nb, nt, GROUPS_PER_TILE * p)], axis=2)
    return st.reshape(nb, nt * 2 * GROUPS_PER_TILE * p)


def _tiles_to_state(x, g, p):
    nb = x.shape[0]
    nt = g // GROUPS_PER_TILE
    st = x.reshape(nb, nt, 2, GROUPS_PER_TILE, p)
    return st[:, :, 0].reshape(nb, g, p), st[:, :, 1].reshape(nb, g, p)


def _s5_outputs(u2, x0, prm, d_skip, steps, tr, want_final):
    b_blk, c_blk, ar, ai = prm
    r, uw = u2.shape
    width = uw // steps
    nt, _, sw = b_blk.shape
    kern = functools.partial(_s5_out_kernel, steps=steps, width=width, n_tiles=nt)
    full3 = lambda a: pl.BlockSpec(a.shape, lambda i: (0, 0, 0))
    full2 = lambda a: pl.BlockSpec(a.shape, lambda i: (0, 0))
    d2 = d_skip.reshape(1, width).astype(F32)
    out_shape = [jax.ShapeDtypeStruct((r, uw), F32)]
    out_specs = [pl.BlockSpec((tr, uw), lambda i: (i, 0))]
    if want_final:
        out_shape.append(jax.ShapeDtypeStruct((r, nt * sw), F32))
        out_specs.append(pl.BlockSpec((tr, nt * sw), lambda i: (i, 0)))
    return pl.pallas_call(
        kern,
        grid=(r // tr,),
        in_specs=[pl.BlockSpec((tr, uw), lambda i: (i, 0)),
                  pl.BlockSpec((tr, nt * sw), lambda i: (i, 0)),
                  full3(b_blk), full3(c_blk), full2(ar), full2(ai), full2(d2)],
        out_specs=out_specs,
        out_shape=out_shape,
        compiler_params=_params("arbitrary"),
        name="s5_outputs",
    )(u2, x0, b_blk, c_blk, ar, ai, d2)


def _s5_prompt(u, batch, seq, prm_all, d_skip, g, p):
    b_blk, c_blk, ar, ai, ar_c, ai_c = prm_all
    steps = STEP_CHUNK
    n_chunks = seq // steps
    r = batch * n_chunks
    width = u.shape[1]
    u2 = u.reshape(r, steps * width)
    nt, _, sw = b_blk.shape
    tr = 128
    z = pl.pallas_call(
        functools.partial(_s5_local_kernel, steps=steps, width=width, n_tiles=nt),
        grid=(r // tr,),
        in_specs=[pl.BlockSpec((tr, steps * width), lambda i: (i, 0)),
                  pl.BlockSpec(b_blk.shape, lambda i: (0, 0, 0)),
                  pl.BlockSpec(ar.shape, lambda i: (0, 0)),
                  pl.BlockSpec(ai.shape, lambda i: (0, 0))],
        out_specs=pl.BlockSpec((tr, nt * sw), lambda i: (i, 0)),
        out_shape=jax.ShapeDtypeStruct((r, nt * sw), F32),
        compiler_params=_params("arbitrary"),
        name="s5_chunk_states",
    )(u2, b_blk, ar, ai)
    half = sw // 2
    xc, xf = pl.pallas_call(
        functools.partial(_s5_carry_kernel, batch=batch, n_chunks=n_chunks),
        grid=(nt,),
        in_specs=[pl.BlockSpec((r, sw), lambda j: (0, j)),
                  pl.BlockSpec((None, 1, half), lambda j: (j, 0, 0)),
                  pl.BlockSpec((None, 1, half), lambda j: (j, 0, 0))],
        out_specs=[pl.BlockSpec((r, sw), lambda j: (0, j)),
                   pl.BlockSpec((batch, sw), lambda j: (0, j))],
        out_shape=[jax.ShapeDtypeStruct((r, nt * sw), F32),
                   jax.ShapeDtypeStruct((batch, nt * sw), F32)],
        compiler_params=_params("arbitrary"),
        name="s5_carry",
    )(z, ar_c.reshape(nt, 1, half), ai_c.reshape(nt, 1, half))
    (y2,) = _s5_outputs(u2, xc, (b_blk, c_blk, ar, ai), d_skip, steps, tr, False)
    s_re, s_im = _tiles_to_state(xf, g, p)
    return y2.reshape(batch * seq, width), s_re, s_im


def _s5_sample(u, nb, t_new, prm_all, d_skip, s_re, s_im):
    b_blk, c_blk, ar, ai, _, _ = prm_all
    g, p = s_re.shape[1], s_re.shape[2]
    width = u.shape[1]
    u2 = u.reshape(nb, t_new * width)
    x0 = _state_to_tiles(s_re.astype(F32), s_im.astype(F32))
    y2, xf = _s5_outputs(u2, x0, (b_blk, c_blk, ar, ai), d_skip, t_new, nb, True)
    n_re, n_im = _tiles_to_state(xf, g, p)
    return y2.reshape(nb * t_new, width), n_re, n_im


def _merge_kernel(attn_ref, y_ref, ga_ref, gb_ref, wa_ref, wg_ref, o_ref):
    d = o_ref.shape[1]
    y_a = jnp.dot(attn_ref[...], wa_ref[...], preferred_element_type=F32)
    z = _gelu_tanh(y_ref[...]).astype(BF16)
    glu = jnp.dot(z, wg_ref[...], preferred_element_type=F32)
    y_b = glu[:, :d] * _sigmoid(glu[:, d:])
    o_ref[...] = (_sigmoid(ga_ref[...]) * y_a + _sigmoid(gb_ref[...]) * y_b).astype(o_ref.dtype)


def _merge(attn, y, ga, gb, w_attn, w_glu, tm):
    m, d = ga.shape
    row = lambda w: pl.BlockSpec((tm, w), lambda i: (i, 0))
    return pl.pallas_call(
        _merge_kernel,
        grid=(m // tm,),
        in_specs=[row(attn.shape[1]), row(y.shape[1]), row(d), row(d),
                  pl.BlockSpec(w_attn.shape, lambda i: (0, 0)),
                  pl.BlockSpec(w_glu.shape, lambda i: (0, 0))],
        out_specs=row(d),
        out_shape=jax.ShapeDtypeStruct((m, d), BF16),
        compiler_params=_params("arbitrary"),
        name="merge",
    )(attn, y, ga, gb, w_attn, w_glu)


def _out_proj_kernel(m_ref, x_ref, w_ref, g_ref, x1_ref, h_ref):
    x1 = x_ref[...] + jnp.dot(m_ref[...], w_ref[...], preferred_element_type=F32)
    x1_ref[...] = x1
    ms = jnp.mean(x1 * x1, axis=-1, keepdims=True)
    h_ref[...] = (x1 * lax.rsqrt(ms + EPS) * g_ref[...]).astype(h_ref.dtype)


def _out_proj(merged, x, w_out, g, tm):
    m, d = x.shape
    row = pl.BlockSpec((tm, d), lambda i: (i, 0))
    return pl.pallas_call(
        _out_proj_kernel,
        grid=(m // tm,),
        in_specs=[row, row, pl.BlockSpec(w_out.shape, lambda i: (0, 0)),
                  pl.BlockSpec((1, d), lambda i: (0, 0))],
        out_specs=[row, row],
        out_shape=[jax.ShapeDtypeStruct((m, d), F32), jax.ShapeDtypeStruct((m, d), BF16)],
        compiler_params=_params("arbitrary"),
        name="out_proj",
    )(merged, x, w_out, g.reshape(1, d))


def _ffn_tail(j, n_j, final_norm, up, u1, u2, cw_ref, cb_ref, wd_ref, x1_ref, g_ref, y_ref,
              acc_ref):
    mixed = cb_ref[...] + cw_ref[0:1, :] * u2 + cw_ref[1:2, :] * u1 + cw_ref[2:3, :] * up
    gate, val = mixed[:, :FF_TILE], mixed[:, FF_TILE:]
    act = (gate * _sigmoid(gate) * val).astype(BF16)
    part = jnp.dot(act, wd_ref[...], preferred_element_type=F32)

    @pl.when(j == 0)
    def _():
        acc_ref[...] = part

    @pl.when(j > 0)
    def _():
        acc_ref[...] += part

    @pl.when(j == n_j - 1)
    def _():
        x2 = x1_ref[...] + acc_ref[...]
        if final_norm:
            ms = jnp.mean(x2 * x2, axis=-1, keepdims=True)
            x2 = x2 * lax.rsqrt(ms + EPS) * g_ref[...]
        y_ref[...] = x2


def _ffn_prompt_kernel(h_ref, wu_ref, cw_ref, cb_ref, wd_ref, x1_ref, g_ref, y_ref, conv_ref,
                       acc_ref, ubuf_ref, carry_ref, *, tm, tiles_per_seq, n_j, halo, final_norm):
    i, j = pl.program_id(0), pl.program_id(1)
    up = jnp.dot(h_ref[...], wu_ref[...], preferred_element_type=F32)
    ubuf_ref[halo:halo + tm, :] = up

    @pl.when(i % tiles_per_seq == 0)
    def _():
        ubuf_ref[0:halo, :] = jnp.zeros((halo, ubuf_ref.shape[1]), F32)

    @pl.when(i % tiles_per_seq != 0)
    def _():
        ubuf_ref[0:halo, :] = carry_ref[j]

    carry_ref[j] = ubuf_ref[tm:tm + halo, :]
    conv_ref[...] = ubuf_ref[halo + tm - 2:halo + tm, :]
    u1 = ubuf_ref[halo - 1:halo - 1 + tm, :]
    u2 = ubuf_ref[halo - 2:halo - 2 + tm, :]
    _ffn_tail(j, n_j, final_norm, up, u1, u2, cw_ref, cb_ref, wd_ref, x1_ref, g_ref, y_ref,
              acc_ref)


def _ffn_sample_kernel(h_ref, wu_ref, cw_ref, cb_ref, wd_ref, x1_ref, g_ref, st_ref, y_ref,
                       conv_ref, acc_ref, *, nb, t_new, n_j, final_norm):
    j = pl.program_id(0)
    up = jnp.dot(h_ref[...], wu_ref[...], preferred_element_type=F32)
    s0, s1 = st_ref[0], st_ref[1]
    u1 = jnp.concatenate([s1, up[:(t_new - 1) * nb]], axis=0)
    u2 = jnp.concatenate([s0, s1, up[:(t_new - 2) * nb]], axis=0)
    conv_ref[0] = up[(t_new - 2) * nb:(t_new - 1) * nb]
    conv_ref[1] = up[(t_new - 1) * nb:]
    _ffn_tail(j, n_j, final_norm, up, u1, u2, cw_ref, cb_ref, wd_ref, x1_ref, g_ref, y_ref,
              acc_ref)


def _ffn_prompt(h2, x1, wu, cw, cb, wd, g, batch, seq, tm, final_norm):
    m, d = x1.shape
    n_j = wd.shape[0] // FF_TILE
    tw = 2 * FF_TILE
    halo = 8
    tps = seq // tm
    kern = functools.partial(_ffn_prompt_kernel, tm=tm, tiles_per_seq=tps, n_j=n_j, halo=halo,
                             final_norm=final_norm)
    return pl.pallas_call(
        kern,
        grid=(m // tm, n_j),
        in_specs=[pl.BlockSpec((tm, d), lambda i, j: (i, 0)),
                  pl.BlockSpec((d, tw), lambda i, j: (0, j)),
                  pl.BlockSpec((3, tw), lambda i, j: (0, j)),
                  pl.BlockSpec((1, tw), lambda i, j: (0, j)),
                  pl.BlockSpec((FF_TILE, d), lambda i, j: (j, 0)),
                  pl.BlockSpec((tm, d), lambda i, j: (i, 0)),
                  pl.BlockSpec((1, d), lambda i, j: (0, 0))],
        out_specs=[pl.BlockSpec((tm, d), lambda i, j: (i, 0)),
                   pl.BlockSpec((None, 2, tw), lambda i, j: (i // tps, 0, j))],
        out_shape=[jax.ShapeDtypeStruct((m, d), F32),
                   jax.ShapeDtypeStruct((batch, 2, n_j * tw), F32)],
        scratch_shapes=[pltpu.VMEM((tm, d), F32), pltpu.VMEM((tm + halo, tw), F32),
                        pltpu.VMEM((n_j, halo, tw), F32)],
        compiler_params=_params("arbitrary", "arbitrary"),
        name="ffn_prompt",
    )(h2, wu, cw, cb, wd, x1, g.reshape(1, d))


def _ffn_sample(h2, x1, wu, cw, cb, wd, g, state, nb, t_new, final_norm):
    m, d = x1.shape
    n_j = wd.shape[0] // FF_TILE
    tw = 2 * FF_TILE
    kern = functools.partial(_ffn_sample_kernel, nb=nb, t_new=t_new, n_j=n_j,
                             final_norm=final_norm)
    return pl.pallas_call(
        kern,
        grid=(n_j,),
        in_specs=[pl.BlockSpec((m, d), lambda j: (0, 0)),
                  pl.BlockSpec((d, tw), lambda j: (0, j)),
                  pl.BlockSpec((3, tw), lambda j: (0, j)),
                  pl.BlockSpec((1, tw), lambda j: (0, j)),
                  pl.BlockSpec((FF_TILE, d), lambda j: (j, 0)),
                  pl.BlockSpec((m, d), lambda j: (0, 0)),
                  pl.BlockSpec((1, d), lambda j: (0, 0)),
                  pl.BlockSpec((2, nb, tw), lambda j: (0, 0, j))],
        out_specs=[pl.BlockSpec((m, d), lambda j: (0, 0)),
                   pl.BlockSpec((2, nb, tw), lambda j: (0, 0, j))],
        out_shape=[jax.ShapeDtypeStruct((m, d), F32),
                   jax.ShapeDtypeStruct((2, nb, n_j * tw), F32)],
        scratch_shapes=[pltpu.VMEM((m, d), F32)],
        compiler_params=_params("arbitrary"),
        name="ffn_sample",
    )(h2, wu, cw, cb, wd, x1, g.reshape(1, d), state)


def _ff_tile_order(a, n_j):
    lead = a.shape[:-1]
    a = a.reshape(*lead, 2, n_j, FF_TILE)
    return jnp.swapaxes(a, -3, -2).reshape(*lead, n_j * 2 * FF_TILE)


def _ff_natural_order(a, n_j):
    lead = a.shape[:-1]
    a = a.reshape(*lead, n_j, 2, FF_TILE)
    return jnp.swapaxes(a, -3, -2).reshape(*lead, 2 * n_j * FF_TILE)


def kernel(x_prompt, x_sample, cache_k, cache_v, cache_kidx, state_s5_re, state_s5_im, state_ffn_conv, page_table, norm_mix, w_in, w_attn_proj, s5_a_re, s5_a_im, s5_log_step, s5_b_re, s5_b_im, s5_c_re, s5_c_im, s5_d, w_glu, w_out, norm_ffn, w_up, conv_w, conv_b, w_down, norm_final):
    depth = w_in.shape[0]
    batch, seq, d_model = x_prompt.shape
    nb, t_new, _ = x_sample.shape
    d_head = cache_k.shape[-1]
    d_idx = cache_kidx.shape[-1]
    attn_w = N_HEADS * d_head
    kv_w = N_KV_HEADS * d_head
    qi_w = IDX_HEADS * d_idx
    groups, p_state = s5_a_re.shape[1], s5_a_re.shape[2]
    s5_w = groups * S5_GROUP
    d_ff = w_down.shape[1]
    n_j = d_ff // FF_TILE
    small_w = qi_w + 2 * kv_w + LANES

    xp = x_prompt.reshape(batch * seq, d_model)
    xs = x_sample.reshape(nb * t_new, d_model)
    tm_p = min(512, batch * seq)
    tm_s = nb * t_new

    outs = {name: [] for name in ("kp", "vp", "kip", "srp", "sip", "cp",
                                  "ks", "vs", "kis", "srs", "sis", "cs")}
    for l in range(depth):
        w = w_in[l]
        o = 0
        seg = {}
        for name, width in (("q", attn_w), ("k", kv_w), ("v", kv_w), ("qi", qi_w), ("ki", d_idx),
                            ("wi", IDX_HEADS), ("u", s5_w), ("ga", d_model), ("gb", d_model)):
            seg[name] = w[:, o:o + width]
            o += width
        w_q = seg["q"].astype(BF16)
        w_u = seg["u"].astype(BF16)
        w_ga = seg["ga"].astype(BF16)
        w_gb = seg["gb"].astype(BF16)
        pad = jnp.zeros((d_model, LANES - d_idx - IDX_HEADS), F32)
        w_small = jnp.concatenate([seg["qi"], seg["k"], seg["v"], seg["ki"], seg["wi"], pad],
                                  axis=1).astype(BF16)
        w_attn = w_attn_proj[l].astype(BF16)
        w_g = w_glu[l].astype(BF16)
        w_o = w_out[l].astype(BF16)
        wu = _ff_tile_order(w_up[l], n_j).astype(BF16)
        cw = _ff_tile_order(conv_w[l].astype(F32), n_j)
        cb = _ff_tile_order(conv_b[l].astype(F32).reshape(1, -1), n_j)
        wd = w_down[l].astype(BF16)
        s5p = _s5_params(s5_a_re[l], s5_a_im[l], s5_log_step[l], s5_b_re[l], s5_b_im[l],
                         s5_c_re[l], s5_c_im[l], STEP_CHUNK)

        def project(x, tm):
            h = _rmsnorm(x, norm_mix[l], tm, BF16)
            q = _matmul(h, w_q, tm, attn_w, F32, "proj_q")
            u = _matmul(h, w_u, tm, s5_w, F32, "proj_u")
            ga = _matmul(h, w_ga, tm, d_model, F32, "proj_ga")
            gb = _matmul(h, w_gb, tm, d_model, F32, "proj_gb")
            small = _matmul(h, w_small, tm, small_w, F32, "proj_small")
            return q, u, ga, gb, small

        def new_rows(small, lead):
            k = small[:, qi_w:qi_w + kv_w].reshape(*lead, N_KV_HEADS, d_head)
            v = small[:, qi_w + kv_w:qi_w + 2 * kv_w].reshape(*lead, N_KV_HEADS, d_head)
            ki = small[:, qi_w + 2 * kv_w:qi_w + 2 * kv_w + d_idx].reshape(*lead, d_idx)
            return k, v, ki

        q, u, ga, gb, small = project(xp, tm_p)
        k_sel = min(TOPK_MAX, seq // 4)
        attn = _prompt_attention(q, small, batch, seq, d_head, d_idx, k_sel, min(128, seq))
        y5, srp, sip = _s5_prompt(u, batch, seq, s5p, s5_d[l], groups, p_state)
        merged = _merge(attn, y5, ga, gb, w_attn, w_g, min(256, batch * seq))
        x1, h2 = _out_proj(merged, xp, w_o, norm_ffn[l], tm_p)
        last = l == depth - 1
        xp, conv_p = _ffn_prompt(h2, x1, wu, cw, cb, wd, norm_final, batch, seq, tm_p, last)
        kp, vp, kip = new_rows(small, (batch, seq))
        outs["kp"].append(kp); outs["vp"].append(vp); outs["kip"].append(kip)
        outs["srp"].append(srp); outs["sip"].append(sip)
        outs["cp"].append(_ff_natural_order(conv_p, n_j))

        q, u, ga, gb, small = project(xs, tm_s)
        attn = _sample_attention(q, small, cache_k[l], cache_v[l], cache_kidx[l], page_table,
                                 t_new, d_head, d_idx)
        y5, srs, sis = _s5_sample(u, nb, t_new, s5p, s5_d[l], state_s5_re[l], state_s5_im[l])
        merged = _merge(attn, y5, ga, gb, w_attn, w_g, min(256, nb * t_new))
        x1, h2 = _out_proj(merged, xs, w_o, norm_ffn[l], tm_s)

        def time_major(a):
            return a.reshape(nb, t_new, -1).swapaxes(0, 1).reshape(nb * t_new, -1)

        st = _ff_tile_order(state_ffn_conv[l].astype(F32), n_j).swapaxes(0, 1)
        y_tm, conv_s = _ffn_sample(time_major(h2), time_major(x1), wu, cw, cb, wd, norm_final,
                                   st, nb, t_new, last)
        xs = y_tm.reshape(t_new, nb, d_model).swapaxes(0, 1).reshape(nb * t_new, d_model)
        ks, vs, kis = new_rows(small, (nb, t_new))
        outs["ks"].append(ks); outs["vs"].append(vs); outs["kis"].append(kis)
        outs["srs"].append(srs); outs["sis"].append(sis)
        outs["cs"].append(_ff_natural_order(conv_s.swapaxes(0, 1), n_j))

    stk = {name: jnp.stack(v) for name, v in outs.items()}
    return (xp.reshape(batch, seq, d_model), xs.reshape(nb, t_new, d_model),
            stk["kp"], stk["vp"], stk["kip"], stk["srp"], stk["sip"], stk["cp"],
            stk["ks"], stk["vs"], stk["kis"], stk["srs"], stk["sis"], stk["cs"])
```

```python
import functools
import math

import jax
import jax.numpy as jnp
from jax import lax
from jax.experimental import pallas as pl
from jax.experimental.pallas import tpu as pltpu

F32 = jnp.float32
BF16 = jnp.bfloat16

EPS = 1e-6
TOPK_MAX = 256
N_HEADS = 16
N_KV_HEADS = 4
KV_GROUP = N_HEADS // N_KV_HEADS
IDX_HEADS = 8
S5_GROUP = 16
STEP_CHUNK = 8
LANES = 128
SUBLANES = 8
GROUPS_PER_TILE = LANES // S5_GROUP
FF_TILE = 512
VMEM_LIMIT = 48 * 1024 * 1024
NEG_INF = float("-inf")
INT_MIN = -2 ** 31


def _params(*sem):
    return pltpu.CompilerParams(dimension_semantics=sem, vmem_limit_bytes=VMEM_LIMIT)


def _sigmoid(x):
    return 1.0 / (1.0 + jnp.exp(-x))


def _gelu_tanh(x):
    c = math.sqrt(2.0 / math.pi)
    return 0.5 * x * (1.0 + jnp.tanh(c * (x + 0.044715 * (x * x * x))))


def _dot_nt(a, b):
    return lax.dot_general(a, b, (((1,), (1,)), ((), ())), preferred_element_type=F32)


def _rms_kernel(x_ref, g_ref, o_ref):
    x = x_ref[...]
    ms = jnp.mean(x * x, axis=-1, keepdims=True)
    o_ref[...] = (x * lax.rsqrt(ms + EPS) * g_ref[...]).astype(o_ref.dtype)


def _rmsnorm(x, g, tm, out_dtype):
    m, d = x.shape
    return pl.pallas_call(
        _rms_kernel,
        grid=(m // tm,),
        in_specs=[pl.BlockSpec((tm, d), lambda i: (i, 0)),
                  pl.BlockSpec((1, d), lambda i: (0, 0))],
        out_specs=pl.BlockSpec((tm, d), lambda i: (i, 0)),
        out_shape=jax.ShapeDtypeStruct((m, d), out_dtype),
        compiler_params=_params("arbitrary"),
        name="rmsnorm",
    )(x, g.reshape(1, d))


def _mm_kernel(h_ref, w_ref, o_ref):
    o_ref[...] = jnp.dot(h_ref[...], w_ref[...], preferred_element_type=F32).astype(o_ref.dtype)


def _matmul(h, w, tm, tn, out_dtype, name):
    m, k = h.shape
    n = w.shape[1]
    return pl.pallas_call(
        _mm_kernel,
        grid=(n // tn, m // tm),
        in_specs=[pl.BlockSpec((tm, k), lambda j, i: (i, 0)),
                  pl.BlockSpec((k, tn), lambda j, i: (0, j))],
        out_specs=pl.BlockSpec((tm, tn), lambda j, i: (i, j)),
        out_shape=jax.ShapeDtypeStruct((m, n), out_dtype),
        compiler_params=_params("arbitrary", "arbitrary"),
        name=name,
    )(h, w)


def _kv_t_kernel(w_ref, h_ref, k_ref, v_ref, ki_ref, *, kv_w):
    out = _dot_nt(w_ref[...], h_ref[...])
    k_ref[...] = out[:kv_w]
    v_ref[...] = out[kv_w:2 * kv_w]
    ki_ref[...] = out[2 * kv_w:]


def _kv_transposed(h, w_t, batch, seq, kv_w, d_idx, tn):
    d = h.shape[1]
    nt = seq // tn
    out = lambda rows: pl.BlockSpec((None, rows, tn), lambda b, i: (b, 0, i))
    shape = lambda rows: jax.ShapeDtypeStruct((batch, rows, seq), F32)
    return pl.pallas_call(
        functools.partial(_kv_t_kernel, kv_w=kv_w),
        grid=(batch, nt),
        in_specs=[pl.BlockSpec(w_t.shape, lambda b, i: (0, 0)),
                  pl.BlockSpec((tn, d), lambda b, i: (b * nt + i, 0))],
        out_specs=[out(kv_w), out(kv_w), out(d_idx)],
        out_shape=[shape(kv_w), shape(kv_w), shape(d_idx)],
        compiler_params=_params("arbitrary", "arbitrary"),
        name="proj_kv_transposed",
    )(w_t, h)


def _count(mask):
    return jnp.sum(jnp.where(mask, 1.0, 0.0), axis=1, keepdims=True)


def _topk_mask(score, col, k_sel, n_cols):
    kf = float(k_sel)
    rows = score.shape[0]

    def as_float(key):
        bits = jnp.where(key < 0, key ^ jnp.int32(0x7FFFFFFF), key)
        return lax.bitcast_convert_type(bits, F32)

    cand0 = jnp.where(_count(score >= 0.0) >= kf, jnp.int32(0), jnp.int32(INT_MIN))
    cand0 = jnp.broadcast_to(cand0, (rows, 1)).astype(jnp.int32)

    def value_step(it, cand):
        trial = cand + lax.shift_left(jnp.int32(1), jnp.int32(30) - it)
        return jnp.where(_count(score >= as_float(trial)) >= kf, trial, cand)

    thr = as_float(lax.fori_loop(0, 31, value_step, cand0))

    above = score > thr
    ties = score == thr
    need = kf - _count(above)
    n_bits = max(1, (n_cols - 1).bit_length())

    def index_step(it, m):
        trial = m + lax.shift_left(jnp.int32(1), jnp.int32(n_bits - 1) - it)
        taken = _count(jnp.logical_and(ties, col < trial))
        return jnp.where(taken <= need - 1.0, trial, m)

    last = lax.fori_loop(0, n_bits, index_step, jnp.zeros((rows, 1), jnp.int32))
    take_tie = jnp.logical_and(jnp.logical_and(ties, col <= last), need >= 1.0)
    top = jnp.logical_or(above, take_tie)
    finite = score > NEG_INF
    few = _count(finite) <= kf
    return jnp.logical_and(finite, jnp.logical_or(few, top))


def _index_weights(kw, d_idx):
    w_scale = (d_idx ** -0.5) * (IDX_HEADS ** -0.5)
    return [kw[:, d_idx + h:d_idx + h + 1] * w_scale for h in range(IDX_HEADS)]


def _head_rows(qi, d_idx):
    return jnp.concatenate([qi[:, h * d_idx:(h + 1) * d_idx] for h in range(IDX_HEADS)], axis=0)


def _weighted_relu_sum(lg, w_cols, rows):
    score = jnp.zeros((rows, lg.shape[1]), F32)
    for h in range(IDX_HEADS):
        score = score + jnp.maximum(lg[h * rows:(h + 1) * rows], 0.0) * w_cols[h]
    return score


def _masked_attention(q, keep, k_ref, v_ref, d_head, out_ref_write):
    rows = q.shape[0]
    scale = d_head ** -0.5
    bias = jnp.where(keep, 0.0, NEG_INF)
    bias_g = jnp.concatenate([bias] * KV_GROUP, axis=0)
    for n in range(N_KV_HEADS):
        qs = jnp.concatenate(
            [q[:, (n * KV_GROUP + g) * d_head:(n * KV_GROUP + g + 1) * d_head]
             for g in range(KV_GROUP)], axis=0)
        qs = (qs * scale).astype(BF16)
        kn = k_ref[:, n * d_head:(n + 1) * d_head]
        vn = v_ref[:, n * d_head:(n + 1) * d_head]
        s = _dot_nt(qs, kn) + bias_g
        m = jnp.max(s, axis=1, keepdims=True)
        p = jnp.exp(s - m)
        l = jnp.sum(p, axis=1, keepdims=True)
        o = jnp.dot(p.astype(BF16), vn, preferred_element_type=F32) / l
        for g in range(KV_GROUP):
            out_ref_write(n * KV_GROUP + g, o[g * rows:(g + 1) * rows])


def _prompt_attn_kernel(q_ref, qi_ref, kwq_ref, kwk_ref, k_ref, v_ref, o_ref, kb_ref, vb_ref,
                        *, tq, n_keys, k_sel, d_head, d_idx, q_block0):
    i = pl.program_id(1)
    kb_ref[...] = k_ref[...].astype(BF16)
    vb_ref[...] = v_ref[...].astype(BF16)

    qi_rows = _head_rows(qi_ref[...], d_idx).astype(BF16)
    w_cols = _index_weights(kwq_ref[...], d_idx)
    ki = kwk_ref[:, 0:d_idx].astype(BF16)
    score = _weighted_relu_sum(_dot_nt(qi_rows, ki), w_cols, tq)

    col = lax.broadcasted_iota(jnp.int32, (tq, n_keys), 1)
    qpos = (q_block0 + i) * tq + lax.broadcasted_iota(jnp.int32, (tq, n_keys), 0)
    causal = col <= qpos
    score = jnp.where(causal, score, NEG_INF)
    keep = jnp.logical_and(_topk_mask(score, col, k_sel, n_keys), causal)

    def write(head, val):
        o_ref[:, head * d_head:(head + 1) * d_head] = val.astype(o_ref.dtype)

    _masked_attention(q_ref[...], keep, kb_ref, vb_ref, d_head, write)


def _prompt_attention(q, small, batch, seq, d_head, d_idx, k_sel, tq):
    attn_w = q.shape[1]
    kv_w = N_KV_HEADS * d_head
    qi_w = IDX_HEADS * d_idx
    nq = seq // tq
    small3 = small.reshape(batch, seq, small.shape[1])
    kw_blk = (qi_w + 2 * kv_w) // LANES
    kern = functools.partial(_prompt_attn_kernel, tq=tq, n_keys=seq, k_sel=k_sel,
                             d_head=d_head, d_idx=d_idx, q_block0=0)
    return pl.pallas_call(
        kern,
        grid=(batch, nq),
        in_specs=[
            pl.BlockSpec((tq, attn_w), lambda b, i: (b * nq + i, 0)),
            pl.BlockSpec((tq, qi_w), lambda b, i: (b * nq + i, 0)),
            pl.BlockSpec((tq, LANES), lambda b, i: (b * nq + i, kw_blk)),
            pl.BlockSpec((None, seq, LANES), lambda b, i: (b, 0, kw_blk)),
            pl.BlockSpec((None, seq, kv_w), lambda b, i: (b, 0, qi_w // kv_w)),
            pl.BlockSpec((None, seq, kv_w), lambda b, i: (b, 0, qi_w // kv_w + 1)),
        ],
        out_specs=pl.BlockSpec((tq, attn_w), lambda b, i: (b * nq + i, 0)),
        out_shape=jax.ShapeDtypeStruct((batch * seq, attn_w), BF16),
        scratch_shapes=[pltpu.VMEM((seq, kv_w), BF16), pltpu.VMEM((seq, kv_w), BF16)],
        compiler_params=_params("arbitrary", "arbitrary"),
        name="prompt_attention",
    )(q, small, small, small3, small3, small3)


def _lane_columns(cols, width):
    rows = cols[0].shape[0]
    lane = lax.broadcasted_iota(jnp.int32, (rows, width), 1)
    out = jnp.zeros((rows, width), F32)
    for j, c in enumerate(cols):
        out = jnp.where(lane == j, c, out)
    return out


def _sample_score_kernel(pt_ref, qi_ref, kw_ref, kin_ref, *rest, n_pages, page, t_new, d_idx):
    del pt_ref
    ipages, s_ref = rest[:n_pages], rest[n_pages]
    past = n_pages * page
    rows = SUBLANES
    qi_rows = _head_rows(qi_ref[...], d_idx)
    qi_bf = qi_rows.astype(BF16)
    w_cols = _index_weights(kw_ref[...], d_idx)
    for p in range(n_pages):
        lg = jnp.dot(qi_bf, ipages[p][...].astype(BF16), preferred_element_type=F32)
        s_ref[:, p * page:(p + 1) * page] = _weighted_relu_sum(lg, w_cols, rows)
    kin = kin_ref[...]
    new_cols = []
    for j in range(t_new):
        lg = jnp.sum(qi_rows * kin[j:j + 1, :], axis=1, keepdims=True)
        new_cols.append(_weighted_relu_sum(lg, w_cols, rows))
    lane = lax.broadcasted_iota(jnp.int32, (rows, LANES), 1)
    t = jnp.minimum(lax.broadcasted_iota(jnp.int32, (rows, LANES), 0), t_new - 1)
    s_ref[:, past:] = jnp.where(lane <= t, _lane_columns(new_cols, LANES), NEG_INF)


def _select_kernel(s_ref, b_ref, *, k_sel):
    score = s_ref[...]
    col = lax.broadcasted_iota(jnp.int32, score.shape, 1)
    keep = _topk_mask(score, col, k_sel, score.shape[1])
    b_ref[...] = jnp.where(keep, 0.0, NEG_INF)


def _sample_attn_kernel(pt_ref, q_ref, kn_ref, vn_ref, bias_ref, *rest,
                        n_pages, page, t_new, d_head):
    del pt_ref
    kpages = rest[:n_pages]
    vpages = rest[n_pages:2 * n_pages]
    o_ref, kt_ref, vt_ref = rest[2 * n_pages:]
    past = n_pages * page
    rows = SUBLANES
    kv_w = N_KV_HEADS * d_head

    for p in range(n_pages):
        kt_ref[:, p * page:(p + 1) * page] = kpages[p][...].astype(BF16)
        vt_ref[:, p * page:(p + 1) * page] = vpages[p][...].astype(BF16)

    bias = jnp.concatenate([bias_ref[...]] * N_HEADS, axis=0)

    q = q_ref[...] * (d_head ** -0.5)
    lane_head = lax.broadcasted_iota(jnp.int32, (rows, kv_w), 1) // d_head
    q_rows = jnp.concatenate(
        [jnp.where(lane_head == n, q[:, g * kv_w:(g + 1) * kv_w], 0.0)
         for g in range(KV_GROUP) for n in range(N_KV_HEADS)], axis=0)

    s_past = jnp.dot(q_rows.astype(BF16), kt_ref[...], preferred_element_type=F32)
    s_past = s_past + bias[:, :past]
    kn = kn_ref[...]
    s_new = _lane_columns(
        [jnp.sum(q_rows * kn[j:j + 1, :], axis=1, keepdims=True) for j in range(t_new)], LANES)
    s_new = s_new + bias[:, past:]

    m = jnp.maximum(jnp.max(s_past, axis=1, keepdims=True), jnp.max(s_new, axis=1, keepdims=True))
    p_past = jnp.exp(s_past - m)
    p_new = jnp.exp(s_new - m)
    l = jnp.sum(p_past, axis=1, keepdims=True) + jnp.sum(p_new, axis=1, keepdims=True)
    o = _dot_nt(p_past.astype(BF16), vt_ref[...])
    vn = vn_ref[...]
    for j in range(t_new):
        o = o + p_new[:, j:j + 1] * vn[j:j + 1, :]
    o = o / l

    chunks = []
    for g in range(KV_GROUP):
        acc = jnp.zeros((rows, kv_w), F32)
        for n in range(N_KV_HEADS):
            r0 = (g * N_KV_HEADS + n) * rows
            acc = acc + jnp.where(lane_head == n, o[r0:r0 + rows], 0.0)
        chunks.append(acc)
    o_ref[...] = jnp.concatenate(chunks, axis=1).astype(o_ref.dtype)


def _page_map(b, pt, *, p, n_pages):
    return (pt[b * n_pages + p], 0, 0)


def _sample_attention(q, small, cache_k, cache_v, cache_ki, page_table, t_new, d_head, d_idx):
    nb, n_pages = page_table.shape
    n_pool, page = cache_k.shape[0], cache_k.shape[1]
    attn_w = q.shape[1]
    kv_w = N_KV_HEADS * d_head
    qi_w = IDX_HEADS * d_idx
    rows = SUBLANES
    past = n_pages * page
    k_sel = min(TOPK_MAX, (past + t_new) // 4)

    def pad_rows(a):
        a = a.reshape(nb, t_new, a.shape[-1])
        return jnp.pad(a, ((0, 0), (0, rows - t_new), (0, 0)))

    q8 = pad_rows(q)
    qi8 = pad_rows(small[:, :qi_w])
    kw8 = pad_rows(small[:, qi_w + 2 * kv_w:])
    kn = pad_rows(small[:, qi_w:qi_w + kv_w])
    vn = pad_rows(small[:, qi_w + kv_w:qi_w + 2 * kv_w])
    kin = pad_rows(small[:, qi_w + 2 * kv_w:qi_w + 2 * kv_w + d_idx])

    ck = jnp.transpose(cache_k, (0, 2, 3, 1)).reshape(n_pool, kv_w, page)
    cv = jnp.transpose(cache_v, (0, 2, 3, 1)).reshape(n_pool, kv_w, page)
    ci = jnp.transpose(cache_ki, (0, 2, 1))

    def row_spec(w):
        return pl.BlockSpec((None, rows, w), lambda b, pt: (b, 0, 0))

    def page_specs(w):
        return [pl.BlockSpec((None, w, page), functools.partial(_page_map, p=p, n_pages=n_pages))
                for p in range(n_pages)]

    pt_flat = page_table.reshape(-1)
    n_keys = past + LANES

    score = pl.pallas_call(
        functools.partial(_sample_score_kernel, n_pages=n_pages, page=page, t_new=t_new,
                          d_idx=d_idx),
        grid_spec=pltpu.PrefetchScalarGridSpec(
            num_scalar_prefetch=1,
            grid=(nb,),
            in_specs=[row_spec(qi_w), row_spec(LANES), row_spec(d_idx)] + page_specs(d_idx),
            out_specs=pl.BlockSpec((None, rows, n_keys), lambda b, pt: (b, 0, 0)),
        ),
        out_shape=jax.ShapeDtypeStruct((nb, rows, n_keys), F32),
        compiler_params=_params("arbitrary"),
        name="sample_scores",
    )(pt_flat, qi8, kw8, kin, *([ci] * n_pages))

    sel_rows = math.gcd(nb * rows, 256)
    bias = pl.pallas_call(
        functools.partial(_select_kernel, k_sel=k_sel),
        grid=(nb * rows // sel_rows,),
        in_specs=[pl.BlockSpec((sel_rows, n_keys), lambda i: (i, 0))],
        out_specs=pl.BlockSpec((sel_rows, n_keys), lambda i: (i, 0)),
        out_shape=jax.ShapeDtypeStruct((nb * rows, n_keys), F32),
        compiler_params=_params("arbitrary"),
        name="sample_select",
    )(score.reshape(nb * rows, n_keys)).reshape(nb, rows, n_keys)

    out = pl.pallas_call(
        functools.partial(_sample_attn_kernel, n_pages=n_pages, page=page, t_new=t_new,
                          d_head=d_head),
        grid_spec=pltpu.PrefetchScalarGridSpec(
            num_scalar_prefetch=1,
            grid=(nb,),
            in_specs=[row_spec(attn_w), row_spec(kv_w), row_spec(kv_w), row_spec(n_keys)]
                     + page_specs(kv_w) + page_specs(kv_w),
            out_specs=pl.BlockSpec((None, rows, attn_w), lambda b, pt: (b, 0, 0)),
            scratch_shapes=[pltpu.VMEM((kv_w, past), BF16), pltpu.VMEM((kv_w, past), BF16)],
        ),
        out_shape=jax.ShapeDtypeStruct((nb, rows, attn_w), BF16),
        compiler_params=_params("arbitrary"),
        name="sample_attention",
    )(pt_flat, q8, kn, vn, bias, *([ck] * n_pages), *([cv] * n_pages))
    return out[:, :t_new].reshape(nb * t_new, attn_w)


def _group_major_heads(a, d_head):
    n = a.shape[1]
    a = a.reshape(N_KV_HEADS, KV_GROUP, d_head, n)
    return jnp.swapaxes(a, 0, 1).reshape(N_HEADS * d_head, n)


def _s5_step(xr, xi, ar, ai, bu, half):
    br, bi = bu[:, :half], bu[:, half:]
    if xr is None:
        return br, bi
    return ar * xr - ai * xi + br, ar * xi + ai * xr + bi


def _s5_local_kernel(u_ref, b_ref, ar_ref, ai_ref, z_ref, *, steps, width, n_tiles):
    sw = b_ref.shape[2]
    half = sw // 2
    for j in range(n_tiles):
        ar, ai = ar_ref[j:j + 1, :], ai_ref[j:j + 1, :]
        xr = xi = None
        for s in range(steps):
            c0 = s * width + j * LANES
            bu = jnp.dot(u_ref[:, c0:c0 + LANES].astype(BF16), b_ref[j], preferred_element_type=F32)
            xr, xi = _s5_step(xr, xi, ar, ai, bu, half)
        z_ref[:, j * sw:j * sw + half] = xr
        z_ref[:, j * sw + half:(j + 1) * sw] = xi


def _s5_carry_kernel(z_ref, ar_ref, ai_ref, xc_ref, xf_ref, *, batch, n_chunks):
    half = z_ref.shape[1] // 2
    ar, ai = ar_ref[...], ai_ref[...]

    def body(c, carry):
        new = []
        for b in range(batch):
            xr, xi = carry[b]
            row = b * n_chunks + c
            xc_ref[pl.ds(row, 1), :] = jnp.concatenate([xr, xi], axis=1)
            z = z_ref[pl.ds(row, 1), :]
            new.append((ar * xr - ai * xi + z[:, :half], ar * xi + ai * xr + z[:, half:]))
        return tuple(new)

    zero = jnp.zeros((1, half), F32)
    final = lax.fori_loop(0, n_chunks, body, tuple((zero, zero) for _ in range(batch)))
    for b in range(batch):
        xf_ref[b:b + 1, :] = jnp.concatenate(list(final[b]), axis=1)


def _s5_out_kernel(u_ref, x0_ref, b_ref, c_ref, ar_ref, ai_ref, d_ref, y_ref, *maybe_xf,
                   steps, width, n_tiles):
    sw = b_ref.shape[2]
    half = sw // 2
    for j in range(n_tiles):
        ar, ai = ar_ref[j:j + 1, :], ai_ref[j:j + 1, :]
        xr = x0_ref[:, j * sw:j * sw + half]
        xi = x0_ref[:, j * sw + half:(j + 1) * sw]
        d = d_ref[:, j * LANES:(j + 1) * LANES]
        for s in range(steps):
            c0 = s * width + j * LANES
            u = u_ref[:, c0:c0 + LANES]
            bu = jnp.dot(u.astype(BF16), b_ref[j], preferred_element_type=F32)
            xr, xi = _s5_step(xr, xi, ar, ai, bu, half)
            x = jnp.concatenate([xr, xi], axis=1).astype(BF16)
            y_ref[:, c0:c0 + LANES] = jnp.dot(x, c_ref[j], preferred_element_type=F32) + d * u
        if maybe_xf:
            maybe_xf[0][:, j * sw:j * sw + half] = xr
            maybe_xf[0][:, j * sw + half:(j + 1) * sw] = xi


def _s5_params(a_re, a_im, log_step, b_re, b_im, c_re, c_im, chunk_steps):
    g, p = a_re.shape
    nt = g // GROUPS_PER_TILE
    a_re, a_im = a_re.astype(F32), a_im.astype(F32)
    step = jnp.exp(log_step.astype(F32))[:, None]

    def discretise(n_steps):
        mag = jnp.exp(a_re * step * n_steps)
        return mag * jnp.cos(a_im * step * n_steps), mag * jnp.sin(a_im * step * n_steps)

    lr, li = discretise(1.0)
    cr, ci = discretise(float(chunk_steps))
    den = a_re * a_re + a_im * a_im
    fr = ((lr - 1.0) * a_re + li * a_im) / den
    fi = (li * a_re - (lr - 1.0) * a_im) / den
    b_re, b_im = b_re.astype(F32), b_im.astype(F32)
    bb_re = fr[..., None] * b_re - fi[..., None] * b_im
    bb_im = fr[..., None] * b_im + fi[..., None] * b_re
    eye = jnp.eye(GROUPS_PER_TILE, dtype=F32)

    bb = jnp.stack([bb_re, bb_im], axis=0).reshape(2, nt, GROUPS_PER_TILE, p, S5_GROUP)
    bb = jnp.transpose(bb, (1, 2, 4, 0, 3))
    b_blk = bb[:, :, :, :, None, :] * eye[None, :, None, None, :, None]
    b_blk = b_blk.reshape(nt, LANES, 2 * GROUPS_PER_TILE * p).astype(BF16)

    cc = jnp.stack([c_re.astype(F32), -c_im.astype(F32)], axis=0)
    cc = cc.reshape(2, nt, GROUPS_PER_TILE, S5_GROUP, p)
    cc = jnp.transpose(cc, (1, 0, 2, 4, 3))
    c_blk = cc[:, :, :, :, None, :] * eye[None, None, :, None, :, None]
    c_blk = c_blk.reshape(nt, 2 * GROUPS_PER_TILE * p, LANES).astype(BF16)

    def tiles(z):
        return z.reshape(nt, GROUPS_PER_TILE * p)

    return b_blk, c_blk, tiles(lr), tiles(li), tiles(cr), tiles(ci)


def _state_to_tiles(s_re, s_im):
    nb, g, p = s_re.shape
    nt = g // GROUPS_PER_TILE
    st = jnp.stack([s_re.reshape(nb, nt, GROUPS_PER_TILE * p),
                    s_im.reshape(nb, nt, GROUPS_PER_TILE * p)], axis=2)
    return st.reshape(nb, nt * 2 * GROUPS_PER_TILE * p)


def _tiles_to_state(x, g, p):
    nb = x.shape[0]
    nt = g // GROUPS_PER_TILE
    st = x.reshape(nb, nt, 2, GROUPS_PER_TILE, p)
    return st[:, :, 0].reshape(nb, g, p), st[:, :, 1].reshape(nb, g, p)


def _s5_outputs(u2, x0, prm, d_skip, steps, tr, want_final):
    b_blk, c_blk, ar, ai = prm
    r, uw = u2.shape
    width = uw // steps
    nt, _, sw = b_blk.shape
    kern = functools.partial(_s5_out_kernel, steps=steps, width=width, n_tiles=nt)
    full3 = lambda a: pl.BlockSpec(a.shape, lambda i: (0, 0, 0))
    full2 = lambda a: pl.BlockSpec(a.shape, lambda i: (0, 0))
    d2 = d_skip.reshape(1, width).astype(F32)
    out_shape = [jax.ShapeDtypeStruct((r, uw), F32)]
    out_specs = [pl.BlockSpec((tr, uw), lambda i: (i, 0))]
    if want_final:
        out_shape.append(jax.ShapeDtypeStruct((r, nt * sw), F32))
        out_specs.append(pl.BlockSpec((tr, nt * sw), lambda i: (i, 0)))
    return pl.pallas_call(
        kern,
        grid=(r // tr,),
        in_specs=[pl.BlockSpec((tr, uw), lambda i: (i, 0)),
                  pl.BlockSpec((tr, nt * sw), lambda i: (i, 0)),
                  full3(b_blk), full3(c_blk), full2(ar), full2(ai), full2(d2)],
        out_specs=out_specs,
        out_shape=out_shape,
        compiler_params=_params("arbitrary"),
        name="s5_outputs",
    )(u2, x0, b_blk, c_blk, ar, ai, d2)


def _s5_prompt(u, batch, seq, prm_all, d_skip, g, p):
    b_blk, c_blk, ar, ai, ar_c, ai_c = prm_all
    steps = STEP_CHUNK
    n_chunks = seq // steps
    r = batch * n_chunks
    width = u.shape[1]
    u2 = u.reshape(r, steps * width)
    nt, _, sw = b_blk.shape
    tr = 128
    z = pl.pallas_call(
        functools.partial(_s5_local_kernel, steps=steps, width=width, n_tiles=nt),
        grid=(r // tr,),
        in_specs=[pl.BlockSpec((tr, steps * width), lambda i: (i, 0)),
                  pl.BlockSpec(b_blk.shape, lambda i: (0, 0, 0)),
                  pl.BlockSpec(ar.shape, lambda i: (0, 0)),
                  pl.BlockSpec(ai.shape, lambda i: (0, 0))],
        out_specs=pl.BlockSpec((tr, nt * sw), lambda i: (i, 0)),
        out_shape=jax.ShapeDtypeStruct((r, nt * sw), F32),
        compiler_params=_params("arbitrary"),
        name="s5_chunk_states",
    )(u2, b_blk, ar, ai)
    half = sw // 2
    xc, xf = pl.pallas_call(
        functools.partial(_s5_carry_kernel, batch=batch, n_chunks=n_chunks),
        grid=(nt,),
        in_specs=[pl.BlockSpec((r, sw), lambda j: (0, j)),
                  pl.BlockSpec((None, 1, half), lambda j: (j, 0, 0)),
                  pl.BlockSpec((None, 1, half), lambda j: (j, 0, 0))],
        out_specs=[pl.BlockSpec((r, sw), lambda j: (0, j)),
                   pl.BlockSpec((batch, sw), lambda j: (0, j))],
        out_shape=[jax.ShapeDtypeStruct((r, nt * sw), F32),
                   jax.ShapeDtypeStruct((batch, nt * sw), F32)],
        compiler_params=_params("arbitrary"),
        name="s5_carry",
    )(z, ar_c.reshape(nt, 1, half), ai_c.reshape(nt, 1, half))
    (y2,) = _s5_outputs(u2, xc, (b_blk, c_blk, ar, ai), d_skip, steps, tr, False)
    s_re, s_im = _tiles_to_state(xf, g, p)
    return y2.reshape(batch * seq, width), s_re, s_im


def _s5_sample(u, nb, t_new, prm_all, d_skip, s_re, s_im):
    b_blk, c_blk, ar, ai, _, _ = prm_all
    g, p = s_re.shape[1], s_re.shape[2]
    width = u.shape[1]
    u2 = u.reshape(nb, t_new * width)
    x0 = _state_to_tiles(s_re.astype(F32), s_im.astype(F32))
    y2, xf = _s5_outputs(u2, x0, (b_blk, c_blk, ar, ai), d_skip, t_new, nb, True)
    n_re, n_im = _tiles_to_state(xf, g, p)
    return y2.reshape(nb * t_new, width), n_re, n_im


def _merge_kernel(attn_ref, y_ref, ga_ref, gb_ref, wa_ref, wg_ref, o_ref):
    d = o_ref.shape[1]
    y_a = jnp.dot(attn_ref[...], wa_ref[...], preferred_element_type=F32)
    z = _gelu_tanh(y_ref[...]).astype(BF16)
    glu = jnp.dot(z, wg_ref[...], preferred_element_type=F32)
    y_b = glu[:, :d] * _sigmoid(glu[:, d:])
    o_ref[...] = (_sigmoid(ga_ref[...]) * y_a + _sigmoid(gb_ref[...]) * y_b).astype(o_ref.dtype)


def _merge(attn, y, ga, gb, w_attn, w_glu, tm):
    m, d = ga.shape
    row = lambda w: pl.BlockSpec((tm, w), lambda i: (i, 0))
    return pl.pallas_call(
        _merge_kernel,
        grid=(m // tm,),
        in_specs=[row(attn.shape[1]), row(y.shape[1]), row(d), row(d),
                  pl.BlockSpec(w_attn.shape, lambda i: (0, 0)),
                  pl.BlockSpec(w_glu.shape, lambda i: (0, 0))],
        out_specs=row(d),
        out_shape=jax.ShapeDtypeStruct((m, d), BF16),
        compiler_params=_params("arbitrary"),
        name="merge",
    )(attn, y, ga, gb, w_attn, w_glu)


def _out_proj_kernel(m_ref, x_ref, w_ref, g_ref, x1_ref, h_ref):
    x1 = x_ref[...] + jnp.dot(m_ref[...], w_ref[...], preferred_element_type=F32)
    x1_ref[...] = x1
    ms = jnp.mean(x1 * x1, axis=-1, keepdims=True)
    h_ref[...] = (x1 * lax.rsqrt(ms + EPS) * g_ref[...]).astype(h_ref.dtype)


def _out_proj(merged, x, w_out, g, tm):
    m, d = x.shape
    row = pl.BlockSpec((tm, d), lambda i: (i, 0))
    return pl.pallas_call(
        _out_proj_kernel,
        grid=(m // tm,),
        in_specs=[row, row, pl.BlockSpec(w_out.shape, lambda i: (0, 0)),
                  pl.BlockSpec((1, d), lambda i: (0, 0))],
        out_specs=[row, row],
        out_shape=[jax.ShapeDtypeStruct((m, d), F32), jax.ShapeDtypeStruct((m, d), BF16)],
        compiler_params=_params("arbitrary"),
        name="out_proj",
    )(merged, x, w_out, g.reshape(1, d))


def _ffn_up(h_ref, wg_ref, wv_ref):
    h = h_ref[...]
    return jnp.concatenate([jnp.dot(h, wg_ref[...], preferred_element_type=F32),
                            jnp.dot(h, wv_ref[...], preferred_element_type=F32)], axis=1)


def _ffn_tail(j, n_j, final_norm, up, u1, u2, cwg_ref, cwv_ref, cbg_ref, cbv_ref, wd_ref,
              x1_ref, g_ref, y_ref, acc_ref):
    cw = jnp.concatenate([cwg_ref[...], cwv_ref[...]], axis=1)
    cb = jnp.concatenate([cbg_ref[...], cbv_ref[...]], axis=1)
    mixed = cb + cw[0:1, :] * u2 + cw[1:2, :] * u1 + cw[2:3, :] * up
    gate, val = mixed[:, :FF_TILE], mixed[:, FF_TILE:]
    act = (gate * _sigmoid(gate) * val).astype(BF16)
    part = jnp.dot(act, wd_ref[...], preferred_element_type=F32)

    @pl.when(j == 0)
    def _():
        acc_ref[...] = part

    @pl.when(j > 0)
    def _():
        acc_ref[...] += part

    @pl.when(j == n_j - 1)
    def _():
        x2 = x1_ref[...] + acc_ref[...]
        if final_norm:
            ms = jnp.mean(x2 * x2, axis=-1, keepdims=True)
            x2 = x2 * lax.rsqrt(ms + EPS) * g_ref[...]
        y_ref[...] = x2


def _ffn_prompt_kernel(h_ref, wg_ref, wv_ref, cwg_ref, cwv_ref, cbg_ref, cbv_ref, wd_ref, x1_ref,
                       g_ref, y_ref, convg_ref, convv_ref, acc_ref, ubuf_ref, carry_ref,
                       *, tm, tiles_per_seq, n_j, halo, final_norm):
    i, j = pl.program_id(0), pl.program_id(1)
    up = _ffn_up(h_ref, wg_ref, wv_ref)
    ubuf_ref[halo:halo + tm, :] = up

    @pl.when(i % tiles_per_seq == 0)
    def _():
        ubuf_ref[0:halo, :] = jnp.zeros((halo, ubuf_ref.shape[1]), F32)

    @pl.when(i % tiles_per_seq != 0)
    def _():
        ubuf_ref[0:halo, :] = carry_ref[j]

    carry_ref[j] = ubuf_ref[tm:tm + halo, :]
    last2 = ubuf_ref[halo + tm - 2:halo + tm, :]
    convg_ref[...] = last2[:, :FF_TILE]
    convv_ref[...] = last2[:, FF_TILE:]
    u1 = ubuf_ref[halo - 1:halo - 1 + tm, :]
    u2 = ubuf_ref[halo - 2:halo - 2 + tm, :]
    _ffn_tail(j, n_j, final_norm, up, u1, u2, cwg_ref, cwv_ref, cbg_ref, cbv_ref, wd_ref,
              x1_ref, g_ref, y_ref, acc_ref)


def _ffn_sample_kernel(h_ref, wg_ref, wv_ref, cwg_ref, cwv_ref, cbg_ref, cbv_ref, wd_ref, x1_ref,
                       g_ref, s0g_ref, s0v_ref, s1g_ref, s1v_ref, y_ref, convg_ref, convv_ref,
                       acc_ref, *, nb, t_new, n_j, final_norm):
    j = pl.program_id(0)
    up = _ffn_up(h_ref, wg_ref, wv_ref)
    s0 = jnp.concatenate([s0g_ref[...], s0v_ref[...]], axis=1)
    s1 = jnp.concatenate([s1g_ref[...], s1v_ref[...]], axis=1)
    u1 = jnp.concatenate([s1, up[:(t_new - 1) * nb]], axis=0)
    u2 = jnp.concatenate([s0, s1, up[:(t_new - 2) * nb]], axis=0)
    for r in range(2):
        rows = up[(t_new - 2 + r) * nb:(t_new - 1 + r) * nb]
        convg_ref[r] = rows[:, :FF_TILE]
        convv_ref[r] = rows[:, FF_TILE:]
    _ffn_tail(j, n_j, final_norm, up, u1, u2, cwg_ref, cwv_ref, cbg_ref, cbv_ref, wd_ref,
              x1_ref, g_ref, y_ref, acc_ref)


def _ffn_weight_specs(d, n_j, ix):
    gate = lambda rows: pl.BlockSpec((rows, FF_TILE), ix(lambda j: (0, j)))
    val = lambda rows: pl.BlockSpec((rows, FF_TILE), ix(lambda j: (0, n_j + j)))
    return [gate(d), val(d), gate(3), val(3), gate(1), val(1),
            pl.BlockSpec((FF_TILE, d), ix(lambda j: (j, 0)))]


def _ffn_prompt(h2, x1, wu, cw, cb, wd, g, batch, seq, tm, final_norm):
    m, d = x1.shape
    d_ff = wd.shape[0]
    n_j = d_ff // FF_TILE
    halo = SUBLANES
    tps = seq // tm
    kern = functools.partial(_ffn_prompt_kernel, tm=tm, tiles_per_seq=tps, n_j=n_j, halo=halo,
                             final_norm=final_norm)
    ix = lambda f: (lambda i, j: f(j))
    conv_spec = pl.BlockSpec((None, 2, FF_TILE), lambda i, j: (i, 0, j))
    y, cg, cv = pl.pallas_call(
        kern,
        grid=(m // tm, n_j),
        in_specs=[pl.BlockSpec((tm, d), lambda i, j: (i, 0))] + _ffn_weight_specs(d, n_j, ix)
                 + [pl.BlockSpec((tm, d), lambda i, j: (i, 0)),
                    pl.BlockSpec((1, d), lambda i, j: (0, 0))],
        out_specs=[pl.BlockSpec((tm, d), lambda i, j: (i, 0)), conv_spec, conv_spec],
        out_shape=[jax.ShapeDtypeStruct((m, d), F32),
                   jax.ShapeDtypeStruct((m // tm, 2, d_ff), F32),
                   jax.ShapeDtypeStruct((m // tm, 2, d_ff), F32)],
        scratch_shapes=[pltpu.VMEM((tm, d), F32), pltpu.VMEM((tm + halo, 2 * FF_TILE), F32),
                        pltpu.VMEM((n_j, halo, 2 * FF_TILE), F32)],
        compiler_params=_params("arbitrary", "arbitrary"),
        name="ffn_prompt",
    )(h2, wu, wu, cw, cw, cb, cb, wd, x1, g.reshape(1, d))
    return y, jnp.concatenate([cg, cv], axis=-1)[tps - 1::tps]


def _ffn_sample(h2, x1, wu, cw, cb, wd, g, state, nb, t_new, final_norm):
    m, d = x1.shape
    d_ff = wd.shape[0]
    n_j = d_ff // FF_TILE
    kern = functools.partial(_ffn_sample_kernel, nb=nb, t_new=t_new, n_j=n_j,
                             final_norm=final_norm)
    ix = lambda f: f
    s0, s1 = state[:, 0, :], state[:, 1, :]
    st_gate = pl.BlockSpec((nb, FF_TILE), lambda j: (0, j))
    st_val = pl.BlockSpec((nb, FF_TILE), lambda j: (0, n_j + j))
    conv_spec = pl.BlockSpec((2, nb, FF_TILE), lambda j: (0, 0, j))
    y, cg, cv = pl.pallas_call(
        kern,
        grid=(n_j,),
        in_specs=[pl.BlockSpec((m, d), lambda j: (0, 0))] + _ffn_weight_specs(d, n_j, ix)
                 + [pl.BlockSpec((m, d), lambda j: (0, 0)),
                    pl.BlockSpec((1, d), lambda j: (0, 0)),
                    st_gate, st_val, st_gate, st_val],
        out_specs=[pl.BlockSpec((m, d), lambda j: (0, 0)), conv_spec, conv_spec],
        out_shape=[jax.ShapeDtypeStruct((m, d), F32),
                   jax.ShapeDtypeStruct((2, nb, d_ff), F32),
                   jax.ShapeDtypeStruct((2, nb, d_ff), F32)],
        scratch_shapes=[pltpu.VMEM((m, d), F32)],
        compiler_params=_params("arbitrary"),
        name="ffn_sample",
    )(h2, wu, wu, cw, cw, cb, cb, wd, x1, g.reshape(1, d), s0, s0, s1, s1)
    return y, jnp.swapaxes(jnp.concatenate([cg, cv], axis=-1), 0, 1)


def kernel(x_prompt, x_sample, cache_k, cache_v, cache_kidx, state_s5_re, state_s5_im, state_ffn_conv, page_table, norm_mix, w_in, w_attn_proj, s5_a_re, s5_a_im, s5_log_step, s5_b_re, s5_b_im, s5_c_re, s5_c_im, s5_d, w_glu, w_out, norm_ffn, w_up, conv_w, conv_b, w_down, norm_final):
    depth = w_in.shape[0]
    batch, seq, d_model = x_prompt.shape
    nb, t_new, _ = x_sample.shape
    d_head = cache_k.shape[-1]
    d_idx = cache_kidx.shape[-1]
    attn_w = N_HEADS * d_head
    kv_w = N_KV_HEADS * d_head
    qi_w = IDX_HEADS * d_idx
    groups, p_state = s5_a_re.shape[1], s5_a_re.shape[2]
    s5_w = groups * S5_GROUP
    small_w = qi_w + 2 * kv_w + LANES

    xp = x_prompt.reshape(batch * seq, d_model)
    xs = x_sample.reshape(nb * t_new, d_model)
    tm_p = min(512, batch * seq)
    tm_s = nb * t_new

    outs = {name: [] for name in ("kp", "vp", "kip", "srp", "sip", "cp",
                                  "ks", "vs", "kis", "srs", "sis", "cs")}
    for l in range(depth):
        w_t = jnp.swapaxes(w_in[l], 0, 1)
        o = 0
        seg = {}
        for name, width in (("q", attn_w), ("k", kv_w), ("v", kv_w), ("qi", qi_w), ("ki", d_idx),
                            ("wi", IDX_HEADS), ("u", s5_w), ("ga", d_model), ("gb", d_model)):
            seg[name] = w_t[o:o + width]
            o += width
        cols = lambda rows_t: jnp.swapaxes(rows_t, 0, 1).astype(BF16)
        w_q = cols(seg["q"])
        w_q_grouped = cols(_group_major_heads(seg["q"], d_head))
        w_u = cols(seg["u"])
        w_ga = cols(seg["ga"])
        w_gb = cols(seg["gb"])
        pad = jnp.zeros((LANES - d_idx - IDX_HEADS, d_model), F32)
        w_small = cols(jnp.concatenate([seg["qi"], seg["k"], seg["v"], seg["ki"], seg["wi"], pad],
                                       axis=0))
        w_kv_t = jnp.concatenate([seg["k"], seg["v"], seg["ki"]], axis=0).astype(BF16)
        w_attn = w_attn_proj[l].astype(BF16)
        w_attn_grouped = _group_major_heads(w_attn_proj[l], d_head).astype(BF16)
        w_g = w_glu[l].astype(BF16)
        w_o = w_out[l].astype(BF16)
        wu = w_up[l].astype(BF16)
        cw = conv_w[l].astype(F32)
        cb = conv_b[l].astype(F32).reshape(1, -1)
        wd = w_down[l].astype(BF16)
        s5p = _s5_params(s5_a_re[l], s5_a_im[l], s5_log_step[l], s5_b_re[l], s5_b_im[l],
                         s5_c_re[l], s5_c_im[l], STEP_CHUNK)
        last = l == depth - 1

        def project(h, tm, wq):
            q = _matmul(h, wq, tm, attn_w, F32, "proj_q")
            u = _matmul(h, w_u, tm, s5_w, F32, "proj_u")
            ga = _matmul(h, w_ga, tm, d_model, F32, "proj_ga")
            gb = _matmul(h, w_gb, tm, d_model, F32, "proj_gb")
            small = _matmul(h, w_small, tm, small_w, F32, "proj_small")
            return q, u, ga, gb, small

        h = _rmsnorm(xp, norm_mix[l], tm_p, BF16)
        q, u, ga, gb, small = project(h, tm_p, w_q)
        k_t, v_t, ki_t = _kv_transposed(h, w_kv_t, batch, seq, kv_w, d_idx, min(512, seq))
        k_sel = min(TOPK_MAX, seq // 4)
        attn = _prompt_attention(q, small, batch, seq, d_head, d_idx, k_sel, min(128, seq))
        y5, srp, sip = _s5_prompt(u, batch, seq, s5p, s5_d[l], groups, p_state)
        merged = _merge(attn, y5, ga, gb, w_attn, w_g, min(256, batch * seq))
        x1, h2 = _out_proj(merged, xp, w_o, norm_ffn[l], tm_p)
        xp, conv_p = _ffn_prompt(h2, x1, wu, cw, cb, wd, norm_final, batch, seq, tm_p, last)

        def heads_last(a_t):
            return jnp.transpose(a_t.reshape(batch, N_KV_HEADS, d_head, seq), (0, 3, 1, 2))

        outs["kp"].append(heads_last(k_t)); outs["vp"].append(heads_last(v_t))
        outs["kip"].append(jnp.swapaxes(ki_t, 1, 2))
        outs["srp"].append(srp); outs["sip"].append(sip); outs["cp"].append(conv_p)

        h = _rmsnorm(xs, norm_mix[l], tm_s, BF16)
        q, u, ga, gb, small = project(h, tm_s, w_q_grouped)
        attn = _sample_attention(q, small, cache_k[l], cache_v[l], cache_kidx[l], page_table,
                                 t_new, d_head, d_idx)
        y5, srs, sis = _s5_sample(u, nb, t_new, s5p, s5_d[l], state_s5_re[l], state_s5_im[l])
        merged = _merge(attn, y5, ga, gb, w_attn_grouped, w_g, min(256, nb * t_new))
        x1, h2 = _out_proj(merged, xs, w_o, norm_ffn[l], tm_s)

        def time_major(a):
            return a.reshape(nb, t_new, -1).swapaxes(0, 1).reshape(nb * t_new, -1)

        y_tm, conv_s = _ffn_sample(time_major(h2), time_major(x1), wu, cw, cb, wd, norm_final,
                                   state_ffn_conv[l].astype(F32), nb, t_new, last)
        xs = y_tm.reshape(t_new, nb, d_model).swapaxes(0, 1).reshape(nb * t_new, d_model)
        k_new = small[:, qi_w:qi_w + kv_w].reshape(nb, t_new, N_KV_HEADS, d_head)
        v_new = small[:, qi_w + kv_w:qi_w + 2 * kv_w].reshape(nb, t_new, N_KV_HEADS, d_head)
        ki_new = small[:, qi_w + 2 * kv_w:qi_w + 2 * kv_w + d_idx].reshape(nb, t_new, d_idx)
        outs["ks"].append(k_new); outs["vs"].append(v_new); outs["kis"].append(ki_new)
        outs["srs"].append(srs); outs["sis"].append(sis); outs["cs"].append(conv_s)

    stk = {name: jnp.stack(v) for name, v in outs.items()}
    return (xp.reshape(batch, seq, d_model), xs.reshape(nb, t_new, d_model),
            stk["kp"], stk["vp"], stk["kip"], stk["srp"], stk["sip"], stk["cp"],
            stk["ks"], stk["vs"], stk["kis"], stk["srs"], stk["sis"], stk["cs"])
```

```python
import functools
import math

import jax
import jax.numpy as jnp
from jax import lax
from jax.experimental import pallas as pl
from jax.experimental.pallas import tpu as pltpu

F32 = jnp.float32
BF16 = jnp.bfloat16

EPS = 1e-6
TOPK_MAX = 256
N_HEADS = 16
N_KV_HEADS = 4
KV_GROUP = N_HEADS // N_KV_HEADS
IDX_HEADS = 8
S5_GROUP = 16
STEP_CHUNK = 8
LANES = 128
SUBLANES = 8
GROUPS_PER_TILE = LANES // S5_GROUP
FF_TILE = 512
VMEM_LIMIT = 48 * 1024 * 1024
NEG_INF = float("-inf")
INT_MIN = -2 ** 31


def _params(*sem):
    return pltpu.CompilerParams(dimension_semantics=sem, vmem_limit_bytes=VMEM_LIMIT)


def _sigmoid(x):
    return 1.0 / (1.0 + jnp.exp(-x))


def _gelu_tanh(x):
    c = math.sqrt(2.0 / math.pi)
    return 0.5 * x * (1.0 + jnp.tanh(c * (x + 0.044715 * (x * x * x))))


def _dot_nt(a, b):
    return lax.dot_general(a, b, (((1,), (1,)), ((), ())), preferred_element_type=F32)


def _rms_kernel(x_ref, g_ref, o_ref):
    x = x_ref[...]
    ms = jnp.mean(x * x, axis=-1, keepdims=True)
    o_ref[...] = (x * lax.rsqrt(ms + EPS) * g_ref[...]).astype(o_ref.dtype)


def _rmsnorm(x, g, tm, out_dtype):
    m, d = x.shape
    return pl.pallas_call(
        _rms_kernel,
        grid=(m // tm,),
        in_specs=[pl.BlockSpec((tm, d), lambda i: (i, 0)),
                  pl.BlockSpec((1, d), lambda i: (0, 0))],
        out_specs=pl.BlockSpec((tm, d), lambda i: (i, 0)),
        out_shape=jax.ShapeDtypeStruct((m, d), out_dtype),
        compiler_params=_params("arbitrary"),
        name="rmsnorm",
    )(x, g.reshape(1, d))


def _mm_kernel(h_ref, w_ref, o_ref):
    o_ref[...] = jnp.dot(h_ref[...], w_ref[...], preferred_element_type=F32).astype(o_ref.dtype)


def _matmul(h, w, tm, tn, out_dtype, name):
    m, k = h.shape
    n = w.shape[1]
    return pl.pallas_call(
        _mm_kernel,
        grid=(n // tn, m // tm),
        in_specs=[pl.BlockSpec((tm, k), lambda j, i: (i, 0)),
                  pl.BlockSpec((k, tn), lambda j, i: (0, j))],
        out_specs=pl.BlockSpec((tm, tn), lambda j, i: (i, j)),
        out_shape=jax.ShapeDtypeStruct((m, n), out_dtype),
        compiler_params=_params("arbitrary", "arbitrary"),
        name=name,
    )(h, w)


def _kv_t_kernel(w_ref, h_ref, k_ref, v_ref, ki_ref, kb_ref, vb_ref, kib_ref, *, kv_w):
    out = _dot_nt(w_ref[...], h_ref[...])
    for lo, hi, full_ref, half_ref in ((0, kv_w, k_ref, kb_ref), (kv_w, 2 * kv_w, v_ref, vb_ref),
                                       (2 * kv_w, out.shape[0], ki_ref, kib_ref)):
        full_ref[...] = out[lo:hi]
        half_ref[...] = out[lo:hi].astype(BF16)


def _kv_transposed(h, w_t, batch, seq, kv_w, d_idx, tn):
    d = h.shape[1]
    nt = seq // tn
    out = lambda rows: pl.BlockSpec((None, rows, tn), lambda b, i: (b, 0, i))
    shape = lambda rows, dt: jax.ShapeDtypeStruct((batch, rows, seq), dt)
    widths = (kv_w, kv_w, d_idx)
    return pl.pallas_call(
        functools.partial(_kv_t_kernel, kv_w=kv_w),
        grid=(batch, nt),
        in_specs=[pl.BlockSpec(w_t.shape, lambda b, i: (0, 0)),
                  pl.BlockSpec((tn, d), lambda b, i: (b * nt + i, 0))],
        out_specs=[out(w) for w in widths] * 2,
        out_shape=[shape(w, F32) for w in widths] + [shape(w, BF16) for w in widths],
        compiler_params=_params("arbitrary", "arbitrary"),
        name="proj_kv_transposed",
    )(w_t, h)


def _count(mask):
    return jnp.sum(jnp.where(mask, 1.0, 0.0), axis=1, keepdims=True)


def _topk_mask(score, col, k_sel, n_cols):
    kf = float(k_sel)
    rows = score.shape[0]

    def as_float(key):
        bits = jnp.where(key < 0, key ^ jnp.int32(0x7FFFFFFF), key)
        return lax.bitcast_convert_type(bits, F32)

    cand0 = jnp.where(_count(score >= 0.0) >= kf, jnp.int32(0), jnp.int32(INT_MIN))
    cand0 = jnp.broadcast_to(cand0, (rows, 1)).astype(jnp.int32)

    def value_step(it, cand):
        trial = cand + lax.shift_left(jnp.int32(1), jnp.int32(30) - it)
        return jnp.where(_count(score >= as_float(trial)) >= kf, trial, cand)

    thr = as_float(lax.fori_loop(0, 31, value_step, cand0))

    above = score > thr
    ties = score == thr
    need = kf - _count(above)
    finite = score > NEG_INF
    few = _count(finite) <= kf
    surplus = jnp.where(few, 0.0, _count(score >= thr) - kf)
    n_bits = max(1, (n_cols - 1).bit_length())

    def index_step(it, m):
        trial = m + lax.shift_left(jnp.int32(1), jnp.int32(n_bits - 1) - it)
        taken = _count(jnp.logical_and(ties, col < trial))
        return jnp.where(taken <= need - 1.0, trial, m)

    def lowest_indices(_):
        return lax.fori_loop(0, n_bits, index_step, jnp.zeros((rows, 1), jnp.int32))

    def all_ties(_):
        return jnp.full((rows, 1), n_cols, jnp.int32)

    last = lax.cond(jnp.max(surplus) > 0.0, lowest_indices, all_ties, None)
    take_tie = jnp.logical_and(jnp.logical_and(ties, col <= last), need >= 1.0)
    top = jnp.logical_or(above, take_tie)
    return jnp.logical_and(finite, jnp.logical_or(few, top))


def _index_weights(kw, d_idx):
    w_scale = (d_idx ** -0.5) * (IDX_HEADS ** -0.5)
    return [kw[:, d_idx + h:d_idx + h + 1] * w_scale for h in range(IDX_HEADS)]


def _head_rows(qi, d_idx):
    return jnp.concatenate([qi[:, h * d_idx:(h + 1) * d_idx] for h in range(IDX_HEADS)], axis=0)


def _weighted_relu_sum(lg, w_cols, rows):
    score = jnp.zeros((rows, lg.shape[1]), F32)
    for h in range(IDX_HEADS):
        score = score + jnp.maximum(lg[h * rows:(h + 1) * rows], 0.0) * w_cols[h]
    return score


HEADS_PER_DOT = 2


def _prompt_attn_kernel(q_ref, qi_ref, kw_ref, kit_ref, kt_ref, vt_ref, prev_ref, o_ref,
                        *, tq, n_keys, k_sel, d_head, d_idx, q_block):
    del prev_ref
    qi = qi_ref[...]
    w_cols = _index_weights(kw_ref[...], d_idx)
    kit = kit_ref[...]
    score = jnp.zeros((tq, n_keys), F32)
    for h in range(IDX_HEADS):
        lg = jnp.dot(qi[:, h * d_idx:(h + 1) * d_idx].astype(BF16), kit,
                     preferred_element_type=F32)
        score = score + jnp.maximum(lg, 0.0) * w_cols[h]

    col = lax.broadcasted_iota(jnp.int32, (tq, n_keys), 1)
    qpos = q_block * tq + lax.broadcasted_iota(jnp.int32, (tq, n_keys), 0)
    score = jnp.where(col <= qpos, score, NEG_INF)
    bias = jnp.where(_topk_mask(score, col, k_sel, n_keys), 0.0, NEG_INF)

    q = q_ref[...] * (d_head ** -0.5)
    for n in range(N_KV_HEADS):
        kt = kt_ref[n * d_head:(n + 1) * d_head, :]
        vt = vt_ref[n * d_head:(n + 1) * d_head, :]
        for g0 in range(0, KV_GROUP, HEADS_PER_DOT):
            heads = [n * KV_GROUP + g0 + i for i in range(HEADS_PER_DOT)]
            qs = jnp.concatenate([q[:, h * d_head:(h + 1) * d_head] for h in heads], axis=0)
            s = jnp.dot(qs.astype(BF16), kt, preferred_element_type=F32)
            s = (s.reshape(HEADS_PER_DOT, tq, n_keys) + bias[None]).reshape(-1, n_keys)
            m = jnp.max(s, axis=1, keepdims=True)
            p = jnp.exp(s - m)
            l = jnp.sum(p, axis=1, keepdims=True)
            o = _dot_nt(p.astype(BF16), vt) / l
            for i, h in enumerate(heads):
                o_ref[:, h * d_head:(h + 1) * d_head] = o[i * tq:(i + 1) * tq].astype(o_ref.dtype)


def _prompt_attention(q, qiw, ki_t, k_t, v_t, batch, seq, d_head, d_idx, k_sel, tq):
    attn_w = q.shape[1]
    kv_w = N_KV_HEADS * d_head
    qi_w = IDX_HEADS * d_idx
    nq = seq // tq
    out = jnp.zeros((batch * seq, attn_w), BF16)
    for qb in range(nq):
        n_keys = (qb + 1) * tq
        row = lambda w, c, qb=qb: pl.BlockSpec((tq, w), lambda b: (b * nq + qb, c))
        keys = lambda w, n_keys=n_keys: pl.BlockSpec((None, w, n_keys), lambda b: (b, 0, 0))
        out = pl.pallas_call(
            functools.partial(_prompt_attn_kernel, tq=tq, n_keys=n_keys, k_sel=k_sel,
                              d_head=d_head, d_idx=d_idx, q_block=qb),
            grid=(batch,),
            in_specs=[row(attn_w, 0), row(qi_w, 0), row(LANES, qi_w // LANES),
                      keys(d_idx), keys(kv_w), keys(kv_w),
                      pl.BlockSpec(memory_space=pl.ANY)],
            out_specs=row(attn_w, 0),
            out_shape=jax.ShapeDtypeStruct((batch * seq, attn_w), BF16),
            input_output_aliases={6: 0},
            compiler_params=_params("arbitrary"),
            name=f"prompt_attention_q{qb}",
        )(q, qiw, qiw, ki_t, k_t, v_t, out)
    return out


def _lane_columns(cols, width):
    rows = cols[0].shape[0]
    lane = lax.broadcasted_iota(jnp.int32, (rows, width), 1)
    out = jnp.zeros((rows, width), F32)
    for j, c in enumerate(cols):
        out = jnp.where(lane == j, c, out)
    return out


def _sample_score_kernel(pt_ref, qi_ref, kw_ref, kin_ref, *rest, n_pages, page, t_new, d_idx):
    del pt_ref
    ipages, s_ref = rest[:n_pages], rest[n_pages]
    past = n_pages * page
    rows = SUBLANES
    qi_rows = _head_rows(qi_ref[...], d_idx)
    qi_bf = qi_rows.astype(BF16)
    w_cols = _index_weights(kw_ref[...], d_idx)
    for p in range(n_pages):
        lg = jnp.dot(qi_bf, ipages[p][...].astype(BF16), preferred_element_type=F32)
        s_ref[:, p * page:(p + 1) * page] = _weighted_relu_sum(lg, w_cols, rows)
    kin = kin_ref[...]
    new_cols = []
    for j in range(t_new):
        lg = jnp.sum(qi_rows * kin[j:j + 1, :], axis=1, keepdims=True)
        new_cols.append(_weighted_relu_sum(lg, w_cols, rows))
    lane = lax.broadcasted_iota(jnp.int32, (rows, LANES), 1)
    t = jnp.minimum(lax.broadcasted_iota(jnp.int32, (rows, LANES), 0), t_new - 1)
    s_ref[:, past:] = jnp.where(lane <= t, _lane_columns(new_cols, LANES), NEG_INF)


def _select_kernel(s_ref, b_ref, *, k_sel):
    score = s_ref[...]
    col = lax.broadcasted_iota(jnp.int32, score.shape, 1)
    keep = _topk_mask(score, col, k_sel, score.shape[1])
    b_ref[...] = jnp.where(keep, 0.0, NEG_INF)


def _sample_attn_kernel(pt_ref, q_ref, kn_ref, vn_ref, bias_ref, *rest,
                        n_pages, page, t_new, d_head):
    del pt_ref
    kpages = rest[:n_pages]
    vpages = rest[n_pages:2 * n_pages]
    o_ref, kt_ref, vt_ref = rest[2 * n_pages:]
    past = n_pages * page
    rows = SUBLANES
    kv_w = N_KV_HEADS * d_head

    for p in range(n_pages):
        kt_ref[:, p * page:(p + 1) * page] = kpages[p][...].astype(BF16)
        vt_ref[:, p * page:(p + 1) * page] = vpages[p][...].astype(BF16)

    bias = jnp.concatenate([bias_ref[...]] * N_HEADS, axis=0)

    q = q_ref[...] * (d_head ** -0.5)
    lane_head = lax.broadcasted_iota(jnp.int32, (rows, kv_w), 1) // d_head
    q_rows = jnp.concatenate(
        [jnp.where(lane_head == n, q[:, g * kv_w:(g + 1) * kv_w], 0.0)
         for g in range(KV_GROUP) for n in range(N_KV_HEADS)], axis=0)

    s_past = jnp.dot(q_rows.astype(BF16), kt_ref[...], preferred_element_type=F32)
    s_past = s_past + bias[:, :past]
    kn = kn_ref[...]
    s_new = _lane_columns(
        [jnp.sum(q_rows * kn[j:j + 1, :], axis=1, keepdims=True) for j in range(t_new)], LANES)
    s_new = s_new + bias[:, past:]

    m = jnp.maximum(jnp.max(s_past, axis=1, keepdims=True), jnp.max(s_new, axis=1, keepdims=True))
    p_past = jnp.exp(s_past - m)
    p_new = jnp.exp(s_new - m)
    l = jnp.sum(p_past, axis=1, keepdims=True) + jnp.sum(p_new, axis=1, keepdims=True)
    o = _dot_nt(p_past.astype(BF16), vt_ref[...])
    vn = vn_ref[...]
    for j in range(t_new):
        o = o + p_new[:, j:j + 1] * vn[j:j + 1, :]
    o = o / l

    chunks = []
    for g in range(KV_GROUP):
        acc = jnp.zeros((rows, kv_w), F32)
        for n in range(N_KV_HEADS):
            r0 = (g * N_KV_HEADS + n) * rows
            acc = acc + jnp.where(lane_head == n, o[r0:r0 + rows], 0.0)
        chunks.append(acc)
    o_ref[...] = jnp.concatenate(chunks, axis=1).astype(o_ref.dtype)


def _page_map(b, pt, *, p, n_pages):
    return (pt[b * n_pages + p], 0, 0)


def _sample_attention(q, small, cache_k, cache_v, cache_ki, page_table, t_new, d_head, d_idx):
    nb, n_pages = page_table.shape
    n_pool, page = cache_k.shape[0], cache_k.shape[1]
    attn_w = q.shape[1]
    kv_w = N_KV_HEADS * d_head
    qi_w = IDX_HEADS * d_idx
    rows = SUBLANES
    past = n_pages * page
    k_sel = min(TOPK_MAX, (past + t_new) // 4)

    def pad_rows(a):
        a = a.reshape(nb, t_new, a.shape[-1])
        return jnp.pad(a, ((0, 0), (0, rows - t_new), (0, 0)))

    q8 = pad_rows(q)
    qi8 = pad_rows(small[:, :qi_w])
    kw8 = pad_rows(small[:, qi_w + 2 * kv_w:])
    kn = pad_rows(small[:, qi_w:qi_w + kv_w])
    vn = pad_rows(small[:, qi_w + kv_w:qi_w + 2 * kv_w])
    kin = pad_rows(small[:, qi_w + 2 * kv_w:qi_w + 2 * kv_w + d_idx])

    ck = jnp.transpose(cache_k, (0, 2, 3, 1)).reshape(n_pool, kv_w, page)
    cv = jnp.transpose(cache_v, (0, 2, 3, 1)).reshape(n_pool, kv_w, page)
    ci = jnp.transpose(cache_ki, (0, 2, 1))

    def row_spec(w):
        return pl.BlockSpec((None, rows, w), lambda b, pt: (b, 0, 0))

    def page_specs(w):
        return [pl.BlockSpec((None, w, page), functools.partial(_page_map, p=p, n_pages=n_pages))
                for p in range(n_pages)]

    pt_flat = page_table.reshape(-1)
    n_keys = past + LANES

    score = pl.pallas_call(
        functools.partial(_sample_score_kernel, n_pages=n_pages, page=page, t_new=t_new,
                          d_idx=d_idx),
        grid_spec=pltpu.PrefetchScalarGridSpec(
            num_scalar_prefetch=1,
            grid=(nb,),
            in_specs=[row_spec(qi_w), row_spec(LANES), row_spec(d_idx)] + page_specs(d_idx),
            out_specs=pl.BlockSpec((None, rows, n_keys), lambda b, pt: (b, 0, 0)),
        ),
        out_shape=jax.ShapeDtypeStruct((nb, rows, n_keys), F32),
        compiler_params=_params("arbitrary"),
        name="sample_scores",
    )(pt_flat, qi8, kw8, kin, *([ci] * n_pages))

    sel_rows = math.gcd(nb * rows, 256)
    bias = pl.pallas_call(
        functools.partial(_select_kernel, k_sel=k_sel),
        grid=(nb * rows // sel_rows,),
        in_specs=[pl.BlockSpec((sel_rows, n_keys), lambda i: (i, 0))],
        out_specs=pl.BlockSpec((sel_rows, n_keys), lambda i: (i, 0)),
        out_shape=jax.ShapeDtypeStruct((nb * rows, n_keys), F32),
        compiler_params=_params("arbitrary"),
        name="sample_select",
    )(score.reshape(nb * rows, n_keys)).reshape(nb, rows, n_keys)

    out = pl.pallas_call(
        functools.partial(_sample_attn_kernel, n_pages=n_pages, page=page, t_new=t_new,
                          d_head=d_head),
        grid_spec=pltpu.PrefetchScalarGridSpec(
            num_scalar_prefetch=1,
            grid=(nb,),
            in_specs=[row_spec(attn_w), row_spec(kv_w), row_spec(kv_w), row_spec(n_keys)]
                     + page_specs(kv_w) + page_specs(kv_w),
            out_specs=pl.BlockSpec((None, rows, attn_w), lambda b, pt: (b, 0, 0)),
            scratch_shapes=[pltpu.VMEM((kv_w, past), BF16), pltpu.VMEM((kv_w, past), BF16)],
        ),
        out_shape=jax.ShapeDtypeStruct((nb, rows, attn_w), BF16),
        compiler_params=_params("arbitrary"),
        name="sample_attention",
    )(pt_flat, q8, kn, vn, bias, *([ck] * n_pages), *([cv] * n_pages))
    return out[:, :t_new].reshape(nb * t_new, attn_w)


def _group_major_heads(a, d_head):
    n = a.shape[1]
    a = a.reshape(N_KV_HEADS, KV_GROUP, d_head, n)
    return jnp.swapaxes(a, 0, 1).reshape(N_HEADS * d_head, n)


def _s5_step(xr, xi, ar, ai, bu, half):
    br, bi = bu[:, :half], bu[:, half:]
    if xr is None:
        return br, bi
    return ar * xr - ai * xi + br, ar * xi + ai * xr + bi


def _s5_local_kernel(u_ref, b_ref, ar_ref, ai_ref, z_ref, *, steps, width, n_tiles):
    sw = b_ref.shape[2]
    half = sw // 2
    for j in range(n_tiles):
        ar, ai = ar_ref[j:j + 1, :], ai_ref[j:j + 1, :]
        xr = xi = None
        for s in range(steps):
            c0 = s * width + j * LANES
            bu = jnp.dot(u_ref[:, c0:c0 + LANES].astype(BF16), b_ref[j], preferred_element_type=F32)
            xr, xi = _s5_step(xr, xi, ar, ai, bu, half)
        z_ref[:, j * sw:j * sw + half] = xr
        z_ref[:, j * sw + half:(j + 1) * sw] = xi


def _s5_carry_kernel(z_ref, ar_ref, ai_ref, xc_ref, xf_ref, *, batch, n_chunks):
    half = z_ref.shape[1] // 2
    ar, ai = ar_ref[...], ai_ref[...]

    def body(c, carry):
        new = []
        for b in range(batch):
            xr, xi = carry[b]
            row = b * n_chunks + c
            xc_ref[pl.ds(row, 1), :] = jnp.concatenate([xr, xi], axis=1)
            z = z_ref[pl.ds(row, 1), :]
            new.append((ar * xr - ai * xi + z[:, :half], ar * xi + ai * xr + z[:, half:]))
        return tuple(new)

    zero = jnp.zeros((1, half), F32)
    final = lax.fori_loop(0, n_chunks, body, tuple((zero, zero) for _ in range(batch)))
    for b in range(batch):
        xf_ref[b:b + 1, :] = jnp.concatenate(list(final[b]), axis=1)


def _s5_out_kernel(u_ref, x0_ref, b_ref, c_ref, ar_ref, ai_ref, d_ref, y_ref, *maybe_xf,
                   steps, width, n_tiles):
    sw = b_ref.shape[2]
    half = sw // 2
    for j in range(n_tiles):
        ar, ai = ar_ref[j:j + 1, :], ai_ref[j:j + 1, :]
        xr = x0_ref[:, j * sw:j * sw + half]
        xi = x0_ref[:, j * sw + half:(j + 1) * sw]
        d = d_ref[:, j * LANES:(j + 1) * LANES]
        for s in range(steps):
            c0 = s * width + j * LANES
            u = u_ref[:, c0:c0 + LANES]
            bu = jnp.dot(u.astype(BF16), b_ref[j], preferred_element_type=F32)
            xr, xi = _s5_step(xr, xi, ar, ai, bu, half)
            x = jnp.concatenate([xr, xi], axis=1).astype(BF16)
            y_ref[:, c0:c0 + LANES] = jnp.dot(x, c_ref[j], preferred_element_type=F32) + d * u
        if maybe_xf:
            maybe_xf[0][:, j * sw:j * sw + half] = xr
            maybe_xf[0][:, j * sw + half:(j + 1) * sw] = xi


def _s5_params(a_re, a_im, log_step, b_re, b_im, c_re, c_im, chunk_steps):
    g, p = a_re.shape
    nt = g // GROUPS_PER_TILE
    a_re, a_im = a_re.astype(F32), a_im.astype(F32)
    step = jnp.exp(log_step.astype(F32))[:, None]

    def discretise(n_steps):
        mag = jnp.exp(a_re * step * n_steps)
        return mag * jnp.cos(a_im * step * n_steps), mag * jnp.sin(a_im * step * n_steps)

    lr, li = discretise(1.0)
    cr, ci = discretise(float(chunk_steps))
    den = a_re * a_re + a_im * a_im
    fr = ((lr - 1.0) * a_re + li * a_im) / den
    fi = (li * a_re - (lr - 1.0) * a_im) / den
    b_re, b_im = b_re.astype(F32), b_im.astype(F32)
    bb_re = fr[..., None] * b_re - fi[..., None] * b_im
    bb_im = fr[..., None] * b_im + fi[..., None] * b_re
    eye = jnp.eye(GROUPS_PER_TILE, dtype=F32)

    bb = jnp.stack([bb_re, bb_im], axis=0).reshape(2, nt, GROUPS_PER_TILE, p, S5_GROUP)
    bb = jnp.transpose(bb, (1, 2, 4, 0, 3))
    b_blk = bb[:, :, :, :, None, :] * eye[None, :, None, None, :, None]
    b_blk = b_blk.reshape(nt, LANES, 2 * GROUPS_PER_TILE * p).astype(BF16)

    cc = jnp.stack([c_re.astype(F32), -c_im.astype(F32)], axis=0)
    cc = cc.reshape(2, nt, GROUPS_PER_TILE, S5_GROUP, p)
    cc = jnp.transpose(cc, (1, 0, 2, 4, 3))
    c_blk = cc[:, :, :, :, None, :] * eye[None, None, :, None, :, None]
    c_blk = c_blk.reshape(nt, 2 * GROUPS_PER_TILE * p, LANES).astype(BF16)

    def tiles(z):
        return z.reshape(nt, GROUPS_PER_TILE * p)

    return b_blk, c_blk, tiles(lr), tiles(li), tiles(cr), tiles(ci)


def _state_to_tiles(s_re, s_im):
    nb, g, p = s_re.shape
    nt = g // GROUPS_PER_TILE
    st = jnp.stack([s_re.reshape(nb, nt, GROUPS_PER_TILE * p),
                    s_im.reshape(nb, nt, GROUPS_PER_TILE * p)], axis=2)
    return st.reshape(nb, nt * 2 * GROUPS_PER_TILE * p)


def _tiles_to_state(x, g, p):
    nb = x.shape[0]
    nt = g // GROUPS_PER_TILE
    st = x.reshape(nb, nt, 2, GROUPS_PER_TILE, p)
    return st[:, :, 0].reshape(nb, g, p), st[:, :, 1].reshape(nb, g, p)


def _s5_outputs(u2, x0, prm, d_skip, steps, tr, want_final):
    b_blk, c_blk, ar, ai = prm
    r, uw = u2.shape
    width = uw // steps
    nt, _, sw = b_blk.shape
    kern = functools.partial(_s5_out_kernel, steps=steps, width=width, n_tiles=nt)
    full3 = lambda a: pl.BlockSpec(a.shape, lambda i: (0, 0, 0))
    full2 = lambda a: pl.BlockSpec(a.shape, lambda i: (0, 0))
    d2 = d_skip.reshape(1, width).astype(F32)
    out_shape = [jax.ShapeDtypeStruct((r, uw), F32)]
    out_specs = [pl.BlockSpec((tr, uw), lambda i: (i, 0))]
    if want_final:
        out_shape.append(jax.ShapeDtypeStruct((r, nt * sw), F32))
        out_specs.append(pl.BlockSpec((tr, nt * sw), lambda i: (i, 0)))
    return pl.pallas_call(
        kern,
        grid=(r // tr,),
        in_specs=[pl.BlockSpec((tr, uw), lambda i: (i, 0)),
                  pl.BlockSpec((tr, nt * sw), lambda i: (i, 0)),
                  full3(b_blk), full3(c_blk), full2(ar), full2(ai), full2(d2)],
        out_specs=out_specs,
        out_shape=out_shape,
        compiler_params=_params("arbitrary"),
        name="s5_outputs",
    )(u2, x0, b_blk, c_blk, ar, ai, d2)


def _s5_prompt(u, batch, seq, prm_all, d_skip, g, p):
    b_blk, c_blk, ar, ai, ar_c, ai_c = prm_all
    steps = STEP_CHUNK
    n_chunks = seq // steps
    r = batch * n_chunks
    width = u.shape[1]
    u2 = u.reshape(r, steps * width)
    nt, _, sw = b_blk.shape
    tr = 128
    z = pl.pallas_call(
        functools.partial(_s5_local_kernel, steps=steps, width=width, n_tiles=nt),
        grid=(r // tr,),
        in_specs=[pl.BlockSpec((tr, steps * width), lambda i: (i, 0)),
                  pl.BlockSpec(b_blk.shape, lambda i: (0, 0, 0)),
                  pl.BlockSpec(ar.shape, lambda i: (0, 0)),
                  pl.BlockSpec(ai.shape, lambda i: (0, 0))],
        out_specs=pl.BlockSpec((tr, nt * sw), lambda i: (i, 0)),
        out_shape=jax.ShapeDtypeStruct((r, nt * sw), F32),
        compiler_params=_params("arbitrary"),
        name="s5_chunk_states",
    )(u2, b_blk, ar, ai)
    half = sw // 2
    xc, xf = pl.pallas_call(
        functools.partial(_s5_carry_kernel, batch=batch, n_chunks=n_chunks),
        grid=(nt,),
        in_specs=[pl.BlockSpec((r, sw), lambda j: (0, j)),
                  pl.BlockSpec((None, 1, half), lambda j: (j, 0, 0)),
                  pl.BlockSpec((None, 1, half), lambda j: (j, 0, 0))],
        out_specs=[pl.BlockSpec((r, sw), lambda j: (0, j)),
                   pl.BlockSpec((batch, sw), lambda j: (0, j))],
        out_shape=[jax.ShapeDtypeStruct((r, nt * sw), F32),
                   jax.ShapeDtypeStruct((batch, nt * sw), F32)],
        compiler_params=_params("arbitrary"),
        name="s5_carry",
    )(z, ar_c.reshape(nt, 1, half), ai_c.reshape(nt, 1, half))
    (y2,) = _s5_outputs(u2, xc, (b_blk, c_blk, ar, ai), d_skip, steps, tr, False)
    s_re, s_im = _tiles_to_state(xf, g, p)
    return y2.reshape(batch * seq, width), s_re, s_im


def _s5_sample(u, nb, t_new, prm_all, d_skip, s_re, s_im):
    b_blk, c_blk, ar, ai, _, _ = prm_all
    g, p = s_re.shape[1], s_re.shape[2]
    width = u.shape[1]
    u2 = u.reshape(nb, t_new * width)
    x0 = _state_to_tiles(s_re.astype(F32), s_im.astype(F32))
    y2, xf = _s5_outputs(u2, x0, (b_blk, c_blk, ar, ai), d_skip, t_new, nb, True)
    n_re, n_im = _tiles_to_state(xf, g, p)
    return y2.reshape(nb * t_new, width), n_re, n_im


def _merge_kernel(attn_ref, y_ref, ga_ref, gb_ref, wa_ref, wg_ref, o_ref):
    d = o_ref.shape[1]
    y_a = jnp.dot(attn_ref[...], wa_ref[...], preferred_element_type=F32)
    z = _gelu_tanh(y_ref[...]).astype(BF16)
    glu = jnp.dot(z, wg_ref[...], preferred_element_type=F32)
    y_b = glu[:, :d] * _sigmoid(glu[:, d:])
    o_ref[...] = (_sigmoid(ga_ref[...]) * y_a + _sigmoid(gb_ref[...]) * y_b).astype(o_ref.dtype)


def _merge(attn, y, ga, gb, w_attn, w_glu, tm):
    m, d = ga.shape
    row = lambda w: pl.BlockSpec((tm, w), lambda i: (i, 0))
    return pl.pallas_call(
        _merge_kernel,
        grid=(m // tm,),
        in_specs=[row(attn.shape[1]), row(y.shape[1]), row(d), row(d),
                  pl.BlockSpec(w_attn.shape, lambda i: (0, 0)),
                  pl.BlockSpec(w_glu.shape, lambda i: (0, 0))],
        out_specs=row(d),
        out_shape=jax.ShapeDtypeStruct((m, d), BF16),
        compiler_params=_params("arbitrary"),
        name="merge",
    )(attn, y, ga, gb, w_attn, w_glu)


def _out_proj_kernel(m_ref, x_ref, w_ref, g_ref, x1_ref, h_ref):
    x1 = x_ref[...] + jnp.dot(m_ref[...], w_ref[...], preferred_element_type=F32)
    x1_ref[...] = x1
    ms = jnp.mean(x1 * x1, axis=-1, keepdims=True)
    h_ref[...] = (x1 * lax.rsqrt(ms + EPS) * g_ref[...]).astype(h_ref.dtype)


def _out_proj(merged, x, w_out, g, tm):
    m, d = x.shape
    row = pl.BlockSpec((tm, d), lambda i: (i, 0))
    return pl.pallas_call(
        _out_proj_kernel,
        grid=(m // tm,),
        in_specs=[row, row, pl.BlockSpec(w_out.shape, lambda i: (0, 0)),
                  pl.BlockSpec((1, d), lambda i: (0, 0))],
        out_specs=[row, row],
        out_shape=[jax.ShapeDtypeStruct((m, d), F32), jax.ShapeDtypeStruct((m, d), BF16)],
        compiler_params=_params("arbitrary"),
        name="out_proj",
    )(merged, x, w_out, g.reshape(1, d))


FF_SUB = 2


def _shift_rows(up, prev, k):
    body = pltpu.roll(up, k, axis=0)
    row = lax.broadcasted_iota(jnp.int32, prev.shape, 0)
    head = jnp.where(row < k, pltpu.roll(prev, k, axis=0), body[:SUBLANES])
    return jnp.concatenate([head, body[SUBLANES:]], axis=0)


def _conv_taps(up, u1, u2, cw, cb):
    return cb + cw[0:1, :] * u2 + cw[1:2, :] * u1 + cw[2:3, :] * up


def _ffn_finish(j, n_j, final_norm, acts, wd_ref, x1_ref, g_ref, y_ref):
    y_ref[...] += jnp.dot(jnp.concatenate(acts, axis=1), wd_ref[...], preferred_element_type=F32)

    @pl.when(j == n_j - 1)
    def _():
        x2 = x1_ref[...] + y_ref[...]
        if final_norm:
            ms = jnp.mean(x2 * x2, axis=-1, keepdims=True)
            x2 = x2 * lax.rsqrt(ms + EPS) * g_ref[...]
        y_ref[...] = x2


def _ffn_prompt_kernel(h_ref, wg_ref, wv_ref, cwg_ref, cwv_ref, cbg_ref, cbv_ref, wd_ref, x1_ref,
                       g_ref, y_ref, convg_ref, convv_ref, carry_ref,
                       *, tm, tiles_per_seq, n_j, final_norm):
    i, j = pl.program_id(0), pl.program_id(1)

    @pl.when(i % tiles_per_seq == 0)
    def _():
        carry_ref[j] = jnp.zeros(carry_ref.shape[1:], F32)

    @pl.when(j == 0)
    def _():
        y_ref[...] = jnp.zeros(y_ref.shape, F32)

    h = h_ref[...]
    prev = carry_ref[j]
    sub = FF_TILE // FF_SUB
    acts, tails = [], ([], [])
    for s in range(FF_SUB):
        cs = slice(s * sub, (s + 1) * sub)
        mixed = []
        for half, (w_ref, cw_ref, cb_ref) in enumerate(((wg_ref, cwg_ref, cbg_ref),
                                                        (wv_ref, cwv_ref, cbv_ref))):
            up = jnp.dot(h, w_ref[:, cs], preferred_element_type=F32)
            p8 = prev[:, half * FF_TILE + s * sub:half * FF_TILE + (s + 1) * sub]
            mixed.append(_conv_taps(up, _shift_rows(up, p8, 1), _shift_rows(up, p8, 2),
                                    cw_ref[:, cs], cb_ref[:, cs]))
            tails[half].append(up[tm - SUBLANES:])
        gate, val = mixed
        acts.append((gate * _sigmoid(gate) * val).astype(BF16))
    tail_g = jnp.concatenate(tails[0], axis=1)
    tail_v = jnp.concatenate(tails[1], axis=1)
    carry_ref[j] = jnp.concatenate([tail_g, tail_v], axis=1)
    convg_ref[...] = tail_g
    convv_ref[...] = tail_v
    _ffn_finish(j, n_j, final_norm, acts, wd_ref, x1_ref, g_ref, y_ref)


def _ffn_sample_kernel(h_ref, wg_ref, wv_ref, cwg_ref, cwv_ref, cbg_ref, cbv_ref, wd_ref, x1_ref,
                       g_ref, s0g_ref, s0v_ref, s1g_ref, s1v_ref, y_ref, convg_ref, convv_ref,
                       *, nb, t_new, n_j, final_norm):
    j = pl.program_id(0)

    @pl.when(j == 0)
    def _():
        y_ref[...] = jnp.zeros(y_ref.shape, F32)

    h = h_ref[...]
    sub = FF_TILE // FF_SUB
    acts = []
    for s in range(FF_SUB):
        cs = slice(s * sub, (s + 1) * sub)
        mixed = []
        for w_ref, cw_ref, cb_ref, s0_ref, s1_ref, conv_ref in (
                (wg_ref, cwg_ref, cbg_ref, s0g_ref, s1g_ref, convg_ref),
                (wv_ref, cwv_ref, cbv_ref, s0v_ref, s1v_ref, convv_ref)):
            up = jnp.dot(h, w_ref[:, cs], preferred_element_type=F32)
            s0, s1 = s0_ref[:, cs], s1_ref[:, cs]
            u1 = jnp.concatenate([s1, up[:(t_new - 1) * nb]], axis=0)
            u2 = jnp.concatenate([s0, s1, up[:(t_new - 2) * nb]], axis=0)
            mixed.append(_conv_taps(up, u1, u2, cw_ref[:, cs], cb_ref[:, cs]))
            conv_ref[0, :, cs] = up[(t_new - 2) * nb:(t_new - 1) * nb]
            conv_ref[1, :, cs] = up[(t_new - 1) * nb:]
        gate, val = mixed
        acts.append((gate * _sigmoid(gate) * val).astype(BF16))
    _ffn_finish(j, n_j, final_norm, acts, wd_ref, x1_ref, g_ref, y_ref)


def _ffn_weight_specs(d, n_j, ix):
    gate = lambda rows: pl.BlockSpec((rows, FF_TILE), ix(lambda j: (0, j)))
    val = lambda rows: pl.BlockSpec((rows, FF_TILE), ix(lambda j: (0, n_j + j)))
    return [gate(d), val(d), gate(3), val(3), gate(1), val(1),
            pl.BlockSpec((FF_TILE, d), ix(lambda j: (j, 0)))]


def _ffn_prompt(h2, x1, wu, cw, cb, wd, g, batch, seq, tm, final_norm):
    m, d = x1.shape
    d_ff = wd.shape[0]
    n_j = d_ff // FF_TILE
    tps = seq // tm
    kern = functools.partial(_ffn_prompt_kernel, tm=tm, tiles_per_seq=tps, n_j=n_j,
                             final_norm=final_norm)
    ix = lambda f: (lambda i, j: f(j))
    conv_spec = pl.BlockSpec((None, SUBLANES, FF_TILE), lambda i, j: (i, 0, j))
    conv_shape = jax.ShapeDtypeStruct((m // tm, SUBLANES, d_ff), F32)
    y, cg, cv = pl.pallas_call(
        kern,
        grid=(m // tm, n_j),
        in_specs=[pl.BlockSpec((tm, d), lambda i, j: (i, 0))] + _ffn_weight_specs(d, n_j, ix)
                 + [pl.BlockSpec((tm, d), lambda i, j: (i, 0)),
                    pl.BlockSpec((1, d), lambda i, j: (0, 0))],
        out_specs=[pl.BlockSpec((tm, d), lambda i, j: (i, 0)), conv_spec, conv_spec],
        out_shape=[jax.ShapeDtypeStruct((m, d), F32), conv_shape, conv_shape],
        scratch_shapes=[pltpu.VMEM((n_j, SUBLANES, 2 * FF_TILE), F32)],
        compiler_params=_params("arbitrary", "arbitrary"),
        name="ffn_prompt",
    )(h2, wu, wu, cw, cw, cb, cb, wd, x1, g.reshape(1, d))
    return y, jnp.concatenate([cg, cv], axis=-1)[tps - 1::tps, SUBLANES - 2:]


def _ffn_sample(h2, x1, wu, cw, cb, wd, g, state, nb, t_new, final_norm):
    m, d = x1.shape
    d_ff = wd.shape[0]
    n_j = d_ff // FF_TILE
    kern = functools.partial(_ffn_sample_kernel, nb=nb, t_new=t_new, n_j=n_j,
                             final_norm=final_norm)
    ix = lambda f: f
    s0, s1 = state[:, 0, :], state[:, 1, :]
    st_gate = pl.BlockSpec((nb, FF_TILE), lambda j: (0, j))
    st_val = pl.BlockSpec((nb, FF_TILE), lambda j: (0, n_j + j))
    conv_spec = pl.BlockSpec((2, nb, FF_TILE), lambda j: (0, 0, j))
    y, cg, cv = pl.pallas_call(
        kern,
        grid=(n_j,),
        in_specs=[pl.BlockSpec((m, d), lambda j: (0, 0))] + _ffn_weight_specs(d, n_j, ix)
                 + [pl.BlockSpec((m, d), lambda j: (0, 0)),
                    pl.BlockSpec((1, d), lambda j: (0, 0)),
                    st_gate, st_val, st_gate, st_val],
        out_specs=[pl.BlockSpec((m, d), lambda j: (0, 0)), conv_spec, conv_spec],
        out_shape=[jax.ShapeDtypeStruct((m, d), F32),
                   jax.ShapeDtypeStruct((2, nb, d_ff), F32),
                   jax.ShapeDtypeStruct((2, nb, d_ff), F32)],
        compiler_params=_params("arbitrary"),
        name="ffn_sample",
    )(h2, wu, wu, cw, cw, cb, cb, wd, x1, g.reshape(1, d), s0, s0, s1, s1)
    return y, jnp.swapaxes(jnp.concatenate([cg, cv], axis=-1), 0, 1)


def kernel(x_prompt, x_sample, cache_k, cache_v, cache_kidx, state_s5_re, state_s5_im, state_ffn_conv, page_table, norm_mix, w_in, w_attn_proj, s5_a_re, s5_a_im, s5_log_step, s5_b_re, s5_b_im, s5_c_re, s5_c_im, s5_d, w_glu, w_out, norm_ffn, w_up, conv_w, conv_b, w_down, norm_final):
    depth = w_in.shape[0]
    batch, seq, d_model = x_prompt.shape
    nb, t_new, _ = x_sample.shape
    d_head = cache_k.shape[-1]
    d_idx = cache_kidx.shape[-1]
    attn_w = N_HEADS * d_head
    kv_w = N_KV_HEADS * d_head
    qi_w = IDX_HEADS * d_idx
    groups, p_state = s5_a_re.shape[1], s5_a_re.shape[2]
    s5_w = groups * S5_GROUP

    xp = x_prompt.reshape(batch * seq, d_model)
    xs = x_sample.reshape(nb * t_new, d_model)
    tm_p = min(512, batch * seq)
    tm_s = nb * t_new

    outs = {name: [] for name in ("kp", "vp", "kip", "srp", "sip", "cp",
                                  "ks", "vs", "kis", "srs", "sis", "cs")}
    for l in range(depth):
        w_t = jnp.swapaxes(w_in[l], 0, 1)
        o = 0
        seg = {}
        for name, width in (("q", attn_w), ("k", kv_w), ("v", kv_w), ("qi", qi_w), ("ki", d_idx),
                            ("wi", IDX_HEADS), ("u", s5_w), ("ga", d_model), ("gb", d_model)):
            seg[name] = w_t[o:o + width]
            o += width
        cols = lambda rows_t: jnp.swapaxes(rows_t, 0, 1).astype(BF16)
        w_q = cols(seg["q"])
        w_q_grouped = cols(_group_major_heads(seg["q"], d_head))
        w_u = cols(seg["u"])
        w_ga = cols(seg["ga"])
        w_gb = cols(seg["gb"])
        pad = jnp.zeros((LANES - d_idx - IDX_HEADS, d_model), F32)
        w_small = cols(jnp.concatenate([seg["qi"], seg["k"], seg["v"], seg["ki"], seg["wi"], pad],
                                       axis=0))
        w_qiw = cols(jnp.concatenate([seg["qi"], seg["ki"], seg["wi"], pad], axis=0))
        w_kv_t = jnp.concatenate([seg["k"], seg["v"], seg["ki"]], axis=0).astype(BF16)
        w_attn = w_attn_proj[l].astype(BF16)
        w_attn_grouped = _group_major_heads(w_attn_proj[l], d_head).astype(BF16)
        w_g = w_glu[l].astype(BF16)
        w_o = w_out[l].astype(BF16)
        wu = w_up[l].astype(BF16)
        cw = conv_w[l].astype(F32)
        cb = conv_b[l].astype(F32).reshape(1, -1)
        wd = w_down[l].astype(BF16)
        s5p = _s5_params(s5_a_re[l], s5_a_im[l], s5_log_step[l], s5_b_re[l], s5_b_im[l],
                         s5_c_re[l], s5_c_im[l], STEP_CHUNK)
        last = l == depth - 1

        def project(h, tm, wq, ws):
            q = _matmul(h, wq, tm, attn_w, F32, "proj_q")
            u = _matmul(h, w_u, tm, s5_w, F32, "proj_u")
            ga = _matmul(h, w_ga, tm, d_model, F32, "proj_ga")
            gb = _matmul(h, w_gb, tm, d_model, F32, "proj_gb")
            small = _matmul(h, ws, tm, ws.shape[1], F32, "proj_small")
            return q, u, ga, gb, small

        h = _rmsnorm(xp, norm_mix[l], tm_p, BF16)
        q, u, ga, gb, qiw = project(h, tm_p, w_q, w_qiw)
        k_t, v_t, ki_t, kb_t, vb_t, kib_t = _kv_transposed(h, w_kv_t, batch, seq, kv_w, d_idx,
                                                            min(512, seq))
        k_sel = min(TOPK_MAX, seq // 4)
        attn = _prompt_attention(q, qiw, kib_t, kb_t, vb_t, batch, seq, d_head, d_idx, k_sel,
                                 min(256, seq))
        y5, srp, sip = _s5_prompt(u, batch, seq, s5p, s5_d[l], groups, p_state)
        merged = _merge(attn, y5, ga, gb, w_attn, w_g, min(256, batch * seq))
        x1, h2 = _out_proj(merged, xp, w_o, norm_ffn[l], tm_p)
        xp, conv_p = _ffn_prompt(h2, x1, wu, cw, cb, wd, norm_final, batch, seq, tm_p, last)

        def heads_last(a_t):
            return jnp.transpose(a_t.reshape(batch, N_KV_HEADS, d_head, seq), (0, 3, 1, 2))

        outs["kp"].append(heads_last(k_t)); outs["vp"].append(heads_last(v_t))
        outs["kip"].append(jnp.swapaxes(ki_t, 1, 2))
        outs["srp"].append(srp); outs["sip"].append(sip); outs["cp"].append(conv_p)

        h = _rmsnorm(xs, norm_mix[l], tm_s, BF16)
        q, u, ga, gb, small = project(h, tm_s, w_q_grouped, w_small)
        attn = _sample_attention(q, small, cache_k[l], cache_v[l], cache_kidx[l], page_table,
                                 t_new, d_head, d_idx)
        y5, srs, sis = _s5_sample(u, nb, t_new, s5p, s5_d[l], state_s5_re[l], state_s5_im[l])
        merged = _merge(attn, y5, ga, gb, w_attn_grouped, w_g, min(256, nb * t_new))
        x1, h2 = _out_proj(merged, xs, w_o, norm_ffn[l], tm_s)

        def time_major(a):
            return a.reshape(nb, t_new, -1).swapaxes(0, 1).reshape(nb * t_new, -1)

        y_tm, conv_s = _ffn_sample(time_major(h2), time_major(x1), wu, cw, cb, wd, norm_final,
                                   state_ffn_conv[l].astype(F32), nb, t_new, last)
        xs = y_tm.reshape(t_new, nb, d_model).swapaxes(0, 1).reshape(nb * t_new, d_model)
        k_new = small[:, qi_w:qi_w + kv_w].reshape(nb, t_new, N_KV_HEADS, d_head)
        v_new = small[:, qi_w + kv_w:qi_w + 2 * kv_w].reshape(nb, t_new, N_KV_HEADS, d_head)
        ki_new = small[:, qi_w + 2 * kv_w:qi_w + 2 * kv_w + d_idx].reshape(nb, t_new, d_idx)
        outs["ks"].append(k_new); outs["vs"].append(v_new); outs["kis"].append(ki_new)
        outs["srs"].append(srs); outs["sis"].append(sis); outs["cs"].append(conv_s)

    stk = {name: jnp.stack(v) for name, v in outs.items()}
    return (xp.reshape(batch, seq, d_model), xs.reshape(nb, t_new, d_model),
            stk["kp"], stk["vp"], stk["kip"], stk["srp"], stk["sip"], stk["cp"],
            stk["ks"], stk["vs"], stk["kis"], stk["srs"], stk["sis"], stk["cs"])
```

```python
import functools
import math

import jax
import jax.numpy as jnp
from jax import lax
from jax.experimental import pallas as pl
from jax.experimental.pallas import tpu as pltpu

F32 = jnp.float32
BF16 = jnp.bfloat16

EPS = 1e-6
LOG2E = math.log2(math.e)
TOPK_MAX = 256
N_HEADS = 16
N_KV_HEADS = 4
KV_GROUP = N_HEADS // N_KV_HEADS
IDX_HEADS = 8
S5_GROUP = 16
STEP_CHUNK = 8
LANES = 128
SUBLANES = 8
GROUPS_PER_TILE = LANES // S5_GROUP
FF_TILE = 512
VMEM_LIMIT = 48 * 1024 * 1024
NEG_INF = float("-inf")
INT_MIN = -2 ** 31


def _params(*sem):
    return pltpu.CompilerParams(dimension_semantics=sem, vmem_limit_bytes=VMEM_LIMIT)


def _sigmoid(x):
    return 1.0 / (1.0 + jnp.exp(-x))


def _gelu_tanh(x):
    c = math.sqrt(2.0 / math.pi)
    return 0.5 * x * (1.0 + jnp.tanh(c * (x + 0.044715 * (x * x * x))))


def _dot_nt(a, b):
    return lax.dot_general(a, b, (((1,), (1,)), ((), ())), preferred_element_type=F32)


def _rms_kernel(x_ref, g_ref, o_ref):
    x = x_ref[...]
    ms = jnp.mean(x * x, axis=-1, keepdims=True)
    o_ref[...] = (x * lax.rsqrt(ms + EPS) * g_ref[...]).astype(o_ref.dtype)


def _rmsnorm(x, g, tm, out_dtype):
    m, d = x.shape
    return pl.pallas_call(
        _rms_kernel,
        grid=(m // tm,),
        in_specs=[pl.BlockSpec((tm, d), lambda i: (i, 0)),
                  pl.BlockSpec((1, d), lambda i: (0, 0))],
        out_specs=pl.BlockSpec((tm, d), lambda i: (i, 0)),
        out_shape=jax.ShapeDtypeStruct((m, d), out_dtype),
        compiler_params=_params("arbitrary"),
        name="rmsnorm",
    )(x, g.reshape(1, d))


def _mm_kernel(h_ref, w_ref, o_ref):
    o_ref[...] = _dot_nt(h_ref[...], w_ref[...]).astype(o_ref.dtype)


def _matmul(h, w_t, tm, out_dtype, name):
    m, k = h.shape
    n = w_t.shape[0]
    return pl.pallas_call(
        _mm_kernel,
        grid=(m // tm,),
        in_specs=[pl.BlockSpec((tm, k), lambda i: (i, 0)),
                  pl.BlockSpec((n, k), lambda i: (0, 0))],
        out_specs=pl.BlockSpec((tm, n), lambda i: (i, 0)),
        out_shape=jax.ShapeDtypeStruct((m, n), out_dtype),
        compiler_params=_params("arbitrary"),
        name=name,
    )(h, w_t)


def _kv_t_kernel(w_ref, h_ref, k_ref, v_ref, ki_ref, kb_ref, vb_ref, kib_ref, *, kv_w):
    out = _dot_nt(w_ref[...], h_ref[...])
    for lo, hi, full_ref, half_ref in ((0, kv_w, k_ref, kb_ref), (kv_w, 2 * kv_w, v_ref, vb_ref),
                                       (2 * kv_w, out.shape[0], ki_ref, kib_ref)):
        full_ref[...] = out[lo:hi]
        half_ref[...] = out[lo:hi].astype(BF16)


def _kv_transposed(h, w_t, batch, seq, kv_w, d_idx, tn):
    d = h.shape[1]
    nt = seq // tn
    out = lambda rows: pl.BlockSpec((None, rows, tn), lambda b, i: (b, 0, i))
    shape = lambda rows, dt: jax.ShapeDtypeStruct((batch, rows, seq), dt)
    widths = (kv_w, kv_w, d_idx)
    return pl.pallas_call(
        functools.partial(_kv_t_kernel, kv_w=kv_w),
        grid=(batch, nt),
        in_specs=[pl.BlockSpec(w_t.shape, lambda b, i: (0, 0)),
                  pl.BlockSpec((tn, d), lambda b, i: (b * nt + i, 0))],
        out_specs=[out(w) for w in widths] * 2,
        out_shape=[shape(w, F32) for w in widths] + [shape(w, BF16) for w in widths],
        compiler_params=_params("arbitrary", "arbitrary"),
        name="proj_kv_transposed",
    )(w_t, h)


def _count(mask):
    return jnp.sum(jnp.where(mask, 1.0, 0.0), axis=1, keepdims=True)


def _topk_mask(score, col, k_sel, n_cols):
    kf = float(k_sel)
    rows = score.shape[0]

    def as_float(key):
        bits = jnp.where(key < 0, key ^ jnp.int32(0x7FFFFFFF), key)
        return lax.bitcast_convert_type(bits, F32)

    cand0 = jnp.where(_count(score >= 0.0) >= kf, jnp.int32(0), jnp.int32(INT_MIN))
    cand0 = jnp.broadcast_to(cand0, (rows, 1)).astype(jnp.int32)

    def value_step(it, cand):
        trial = cand + lax.shift_left(jnp.int32(1), jnp.int32(30) - it)
        return jnp.where(_count(score >= as_float(trial)) >= kf, trial, cand)

    thr = as_float(lax.fori_loop(0, 31, value_step, cand0))

    above = score > thr
    ties = score == thr
    need = kf - _count(above)
    finite = score > NEG_INF
    few = _count(finite) <= kf
    surplus = jnp.where(few, 0.0, _count(score >= thr) - kf)
    n_bits = max(1, (n_cols - 1).bit_length())

    def index_step(it, m):
        trial = m + lax.shift_left(jnp.int32(1), jnp.int32(n_bits - 1) - it)
        taken = _count(jnp.logical_and(ties, col < trial))
        return jnp.where(taken <= need - 1.0, trial, m)

    def lowest_indices(_):
        return lax.fori_loop(0, n_bits, index_step, jnp.zeros((rows, 1), jnp.int32))

    def all_ties(_):
        return jnp.full((rows, 1), n_cols, jnp.int32)

    last = lax.cond(jnp.max(surplus) > 0.0, lowest_indices, all_ties, None)
    take_tie = jnp.logical_and(jnp.logical_and(ties, col <= last), need >= 1.0)
    top = jnp.logical_or(above, take_tie)
    return jnp.logical_and(finite, jnp.logical_or(few, top))


def _index_weights(kw, d_idx):
    w_scale = (d_idx ** -0.5) * (IDX_HEADS ** -0.5)
    return [kw[:, d_idx + h:d_idx + h + 1] * w_scale for h in range(IDX_HEADS)]


def _head_rows(qi, d_idx):
    return jnp.concatenate([qi[:, h * d_idx:(h + 1) * d_idx] for h in range(IDX_HEADS)], axis=0)


def _weighted_relu_sum(lg, w_cols, rows):
    score = jnp.zeros((rows, lg.shape[1]), F32)
    for h in range(IDX_HEADS):
        score = score + jnp.maximum(lg[h * rows:(h + 1) * rows], 0.0) * w_cols[h]
    return score


HEADS_PER_DOT = 2


def _prompt_attn_kernel(q_ref, qi_ref, kw_ref, kit_ref, kt_ref, vt_ref, prev_ref, o_ref,
                        *, tq, n_keys, k_sel, d_head, d_idx, q_block):
    del prev_ref
    qi = qi_ref[...]
    w_cols = _index_weights(kw_ref[...], d_idx)
    kit = kit_ref[...]
    score = jnp.zeros((tq, n_keys), F32)
    for h in range(IDX_HEADS):
        lg = jnp.dot(qi[:, h * d_idx:(h + 1) * d_idx].astype(BF16), kit,
                     preferred_element_type=F32)
        score = score + jnp.maximum(lg, 0.0) * w_cols[h]

    col = lax.broadcasted_iota(jnp.int32, (tq, n_keys), 1)
    qpos = q_block * tq + lax.broadcasted_iota(jnp.int32, (tq, n_keys), 0)
    score = jnp.where(col <= qpos, score, NEG_INF)
    bias = jnp.where(_topk_mask(score, col, k_sel, n_keys), 0.0, NEG_INF)

    q = q_ref[...] * (d_head ** -0.5 * LOG2E)
    for n in range(N_KV_HEADS):
        kt = kt_ref[n * d_head:(n + 1) * d_head, :]
        vt = vt_ref[n * d_head:(n + 1) * d_head, :]
        for g0 in range(0, KV_GROUP, HEADS_PER_DOT):
            heads = [(g0 + i) * N_KV_HEADS + n for i in range(HEADS_PER_DOT)]
            qs = jnp.concatenate([q[:, h * d_head:(h + 1) * d_head] for h in heads], axis=0)
            s = jnp.dot(qs.astype(BF16), kt, preferred_element_type=F32)
            s = (s.reshape(HEADS_PER_DOT, tq, n_keys) + bias[None]).reshape(-1, n_keys)
            m = jnp.max(s, axis=1, keepdims=True)
            p = jnp.exp2(s - m)
            l = jnp.sum(p, axis=1, keepdims=True)
            o = _dot_nt(p.astype(BF16), vt) / l
            for i, h in enumerate(heads):
                o_ref[:, h * d_head:(h + 1) * d_head] = o[i * tq:(i + 1) * tq].astype(o_ref.dtype)


def _prompt_attention(q, qiw, ki_t, k_t, v_t, batch, seq, d_head, d_idx, k_sel, tq):
    attn_w = q.shape[1]
    kv_w = N_KV_HEADS * d_head
    qi_w = IDX_HEADS * d_idx
    nq = seq // tq
    out = jnp.zeros((batch * seq, attn_w), BF16)
    for qb in range(nq):
        n_keys = (qb + 1) * tq
        row = lambda w, c, qb=qb: pl.BlockSpec((tq, w), lambda b: (b * nq + qb, c))
        keys = lambda w, n_keys=n_keys: pl.BlockSpec((None, w, n_keys), lambda b: (b, 0, 0))
        out = pl.pallas_call(
            functools.partial(_prompt_attn_kernel, tq=tq, n_keys=n_keys, k_sel=k_sel,
                              d_head=d_head, d_idx=d_idx, q_block=qb),
            grid=(batch,),
            in_specs=[row(attn_w, 0), row(qi_w, 0), row(LANES, qi_w // LANES),
                      keys(d_idx), keys(kv_w), keys(kv_w),
                      pl.BlockSpec(memory_space=pl.ANY)],
            out_specs=row(attn_w, 0),
            out_shape=jax.ShapeDtypeStruct((batch * seq, attn_w), BF16),
            input_output_aliases={6: 0},
            compiler_params=_params("arbitrary"),
            name=f"prompt_attention_q{qb}",
        )(q, qiw, qiw, ki_t, k_t, v_t, out)
    return out


def _lane_columns(cols, width):
    rows = cols[0].shape[0]
    lane = lax.broadcasted_iota(jnp.int32, (rows, width), 1)
    out = jnp.zeros((rows, width), F32)
    for j, c in enumerate(cols):
        out = jnp.where(lane == j, c, out)
    return out


def _sample_score_kernel(pt_ref, qi_ref, kw_ref, kin_ref, *rest, n_pages, page, t_new, d_idx):
    del pt_ref
    ipages, s_ref = rest[:n_pages], rest[n_pages]
    past = n_pages * page
    rows = SUBLANES
    qi_rows = _head_rows(qi_ref[...], d_idx)
    qi_bf = qi_rows.astype(BF16)
    w_cols = _index_weights(kw_ref[...], d_idx)
    pad_row = lax.broadcasted_iota(jnp.int32, (rows, page), 0) >= t_new
    first = lax.broadcasted_iota(jnp.int32, (rows, page), 1) == 0
    for p in range(n_pages):
        lg = jnp.dot(qi_bf, ipages[p][...].astype(BF16), preferred_element_type=F32)
        pad_val = jnp.where(first, 0.0, NEG_INF) if p == 0 else NEG_INF
        s_ref[:, p * page:(p + 1) * page] = jnp.where(pad_row, pad_val,
                                                      _weighted_relu_sum(lg, w_cols, rows))
    kin = kin_ref[...]
    new_cols = []
    for j in range(t_new):
        lg = jnp.sum(qi_rows * kin[j:j + 1, :], axis=1, keepdims=True)
        new_cols.append(_weighted_relu_sum(lg, w_cols, rows))
    lane = lax.broadcasted_iota(jnp.int32, (rows, LANES), 1)
    t = lax.broadcasted_iota(jnp.int32, (rows, LANES), 0)
    causal_new = jnp.logical_and(lane <= t, t < t_new)
    s_ref[:, past:] = jnp.where(causal_new, _lane_columns(new_cols, LANES), NEG_INF)


def _select_kernel(s_ref, b_ref, *, k_sel):
    score = s_ref[...]
    col = lax.broadcasted_iota(jnp.int32, score.shape, 1)
    keep = _topk_mask(score, col, k_sel, score.shape[1])
    b_ref[...] = jnp.where(keep, 0.0, NEG_INF)


def _sample_attn_kernel(pt_ref, q_ref, kn_ref, vn_ref, bias_ref, *rest,
                        n_pages, page, t_new, d_head):
    del pt_ref
    kpages = rest[:n_pages]
    vpages = rest[n_pages:2 * n_pages]
    o_ref, kt_ref, vt_ref = rest[2 * n_pages:]
    past = n_pages * page
    rows = SUBLANES
    kv_w = N_KV_HEADS * d_head

    for p in range(n_pages):
        kt_ref[:, p * page:(p + 1) * page] = kpages[p][...].astype(BF16)
        vt_ref[:, p * page:(p + 1) * page] = vpages[p][...].astype(BF16)

    bias = jnp.concatenate([bias_ref[...]] * N_HEADS, axis=0)

    q = q_ref[...] * (d_head ** -0.5 * LOG2E)
    lane_head = lax.broadcasted_iota(jnp.int32, (rows, kv_w), 1) // d_head
    q_rows = jnp.concatenate(
        [jnp.where(lane_head == n, q[:, g * kv_w:(g + 1) * kv_w], 0.0)
         for g in range(KV_GROUP) for n in range(N_KV_HEADS)], axis=0)

    s_past = jnp.dot(q_rows.astype(BF16), kt_ref[...], preferred_element_type=F32)
    s_past = s_past + bias[:, :past]
    kn = kn_ref[...]
    s_new = _lane_columns(
        [jnp.sum(q_rows * kn[j:j + 1, :], axis=1, keepdims=True) for j in range(t_new)], LANES)
    s_new = s_new + bias[:, past:]

    m = jnp.maximum(jnp.max(s_past, axis=1, keepdims=True), jnp.max(s_new, axis=1, keepdims=True))
    p_past = jnp.exp2(s_past - m)
    p_new = jnp.exp2(s_new - m)
    l = jnp.sum(p_past, axis=1, keepdims=True) + jnp.sum(p_new, axis=1, keepdims=True)
    o = _dot_nt(p_past.astype(BF16), vt_ref[...])
    vn = vn_ref[...]
    for j in range(t_new):
        o = o + p_new[:, j:j + 1] * vn[j:j + 1, :]
    o = o / l

    chunks = []
    for g in range(KV_GROUP):
        acc = jnp.zeros((rows, kv_w), F32)
        for n in range(N_KV_HEADS):
            r0 = (g * N_KV_HEADS + n) * rows
            acc = acc + jnp.where(lane_head == n, o[r0:r0 + rows], 0.0)
        chunks.append(acc)
    o_ref[...] = jnp.concatenate(chunks, axis=1).astype(o_ref.dtype)


def _page_map(b, pt, *, p, n_pages):
    return (pt[b * n_pages + p], 0, 0)


def _sample_attention(q, small, cache_k, cache_v, cache_ki, page_table, t_new, d_head, d_idx):
    nb, n_pages = page_table.shape
    n_pool, page = cache_k.shape[0], cache_k.shape[1]
    attn_w = q.shape[1]
    kv_w = N_KV_HEADS * d_head
    qi_w = IDX_HEADS * d_idx
    rows = SUBLANES
    past = n_pages * page
    k_sel = min(TOPK_MAX, (past + t_new) // 4)

    def pad_rows(a):
        a = a.reshape(nb, t_new, a.shape[-1])
        return jnp.pad(a, ((0, 0), (0, rows - t_new), (0, 0)))

    q8 = pad_rows(q)
    qi8 = pad_rows(small[:, :qi_w])
    kw8 = pad_rows(small[:, qi_w + 2 * kv_w:])
    kn = pad_rows(small[:, qi_w:qi_w + kv_w])
    vn = pad_rows(small[:, qi_w + kv_w:qi_w + 2 * kv_w])
    kin = pad_rows(small[:, qi_w + 2 * kv_w:qi_w + 2 * kv_w + d_idx])

    ck = jnp.transpose(cache_k, (0, 2, 3, 1)).reshape(n_pool, kv_w, page)
    cv = jnp.transpose(cache_v, (0, 2, 3, 1)).reshape(n_pool, kv_w, page)
    ci = jnp.transpose(cache_ki, (0, 2, 1))

    def row_spec(w):
        return pl.BlockSpec((None, rows, w), lambda b, pt: (b, 0, 0))

    def page_specs(w):
        return [pl.BlockSpec((None, w, page), functools.partial(_page_map, p=p, n_pages=n_pages))
                for p in range(n_pages)]

    pt_flat = page_table.reshape(-1)
    n_keys = past + LANES

    score = pl.pallas_call(
        functools.partial(_sample_score_kernel, n_pages=n_pages, page=page, t_new=t_new,
                          d_idx=d_idx),
        grid_spec=pltpu.PrefetchScalarGridSpec(
            num_scalar_prefetch=1,
            grid=(nb,),
            in_specs=[row_spec(qi_w), row_spec(LANES), row_spec(d_idx)] + page_specs(d_idx),
            out_specs=pl.BlockSpec((None, rows, n_keys), lambda b, pt: (b, 0, 0)),
        ),
        out_shape=jax.ShapeDtypeStruct((nb, rows, n_keys), F32),
        compiler_params=_params("arbitrary"),
        name="sample_scores",
    )(pt_flat, qi8, kw8, kin, *([ci] * n_pages))

    sel_rows = math.gcd(nb * rows, 256)
    bias = pl.pallas_call(
        functools.partial(_select_kernel, k_sel=k_sel),
        grid=(nb * rows // sel_rows,),
        in_specs=[pl.BlockSpec((sel_rows, n_keys), lambda i: (i, 0))],
        out_specs=pl.BlockSpec((sel_rows, n_keys), lambda i: (i, 0)),
        out_shape=jax.ShapeDtypeStruct((nb * rows, n_keys), F32),
        compiler_params=_params("arbitrary"),
        name="sample_select",
    )(score.reshape(nb * rows, n_keys)).reshape(nb, rows, n_keys)

    out = pl.pallas_call(
        functools.partial(_sample_attn_kernel, n_pages=n_pages, page=page, t_new=t_new,
                          d_head=d_head),
        grid_spec=pltpu.PrefetchScalarGridSpec(
            num_scalar_prefetch=1,
            grid=(nb,),
            in_specs=[row_spec(attn_w), row_spec(kv_w), row_spec(kv_w), row_spec(n_keys)]
                     + page_specs(kv_w) + page_specs(kv_w),
            out_specs=pl.BlockSpec((None, rows, attn_w), lambda b, pt: (b, 0, 0)),
            scratch_shapes=[pltpu.VMEM((kv_w, past), BF16), pltpu.VMEM((kv_w, past), BF16)],
        ),
        out_shape=jax.ShapeDtypeStruct((nb, rows, attn_w), BF16),
        compiler_params=_params("arbitrary"),
        name="sample_attention",
    )(pt_flat, q8, kn, vn, bias, *([ck] * n_pages), *([cv] * n_pages))
    return out[:, :t_new].reshape(nb * t_new, attn_w)


def _group_major_heads(a, d_head):
    n = a.shape[1]
    a = a.reshape(N_KV_HEADS, KV_GROUP, d_head, n)
    return jnp.swapaxes(a, 0, 1).reshape(N_HEADS * d_head, n)


def _s5_step(xr, xi, ar, ai, bu, half):
    br, bi = bu[:, :half], bu[:, half:]
    if xr is None:
        return br, bi
    return ar * xr - ai * xi + br, ar * xi + ai * xr + bi


def _s5_local_kernel(u_ref, b_ref, ar_ref, ai_ref, z_ref, *, steps, width, n_tiles):
    sw = b_ref.shape[2]
    half = sw // 2
    for j in range(n_tiles):
        ar, ai = ar_ref[j:j + 1, :], ai_ref[j:j + 1, :]
        xr = xi = None
        for s in range(steps):
            c0 = s * width + j * LANES
            bu = jnp.dot(u_ref[:, c0:c0 + LANES].astype(BF16), b_ref[j], preferred_element_type=F32)
            xr, xi = _s5_step(xr, xi, ar, ai, bu, half)
        z_ref[:, j * sw:j * sw + half] = xr
        z_ref[:, j * sw + half:(j + 1) * sw] = xi


def _s5_carry_kernel(z_ref, ar_ref, ai_ref, xc_ref, xf_ref, *, batch, n_chunks):
    half = z_ref.shape[1] // 2
    ar, ai = ar_ref[...], ai_ref[...]

    def body(c, carry):
        new = []
        for b in range(batch):
            xr, xi = carry[b]
            row = b * n_chunks + c
            xc_ref[pl.ds(row, 1), :] = jnp.concatenate([xr, xi], axis=1)
            z = z_ref[pl.ds(row, 1), :]
            new.append((ar * xr - ai * xi + z[:, :half], ar * xi + ai * xr + z[:, half:]))
        return tuple(new)

    zero = jnp.zeros((1, half), F32)
    final = lax.fori_loop(0, n_chunks, body, tuple((zero, zero) for _ in range(batch)))
    for b in range(batch):
        xf_ref[b:b + 1, :] = jnp.concatenate(list(final[b]), axis=1)


def _s5_out_kernel(u_ref, x0_ref, b_ref, c_ref, ar_ref, ai_ref, d_ref, y_ref, *maybe_xf,
                   steps, width, n_tiles):
    sw = b_ref.shape[2]
    half = sw // 2
    for j in range(n_tiles):
        ar, ai = ar_ref[j:j + 1, :], ai_ref[j:j + 1, :]
        xr = x0_ref[:, j * sw:j * sw + half]
        xi = x0_ref[:, j * sw + half:(j + 1) * sw]
        d = d_ref[:, j * LANES:(j + 1) * LANES]
        for s in range(steps):
            c0 = s * width + j * LANES
            u = u_ref[:, c0:c0 + LANES]
            bu = jnp.dot(u.astype(BF16), b_ref[j], preferred_element_type=F32)
            xr, xi = _s5_step(xr, xi, ar, ai, bu, half)
            x = jnp.concatenate([xr, xi], axis=1).astype(BF16)
            y_ref[:, c0:c0 + LANES] = jnp.dot(x, c_ref[j], preferred_element_type=F32) + d * u
        if maybe_xf:
            maybe_xf[0][:, j * sw:j * sw + half] = xr
            maybe_xf[0][:, j * sw + half:(j + 1) * sw] = xi


def _s5_params(a_re, a_im, log_step, b_re, b_im, c_re, c_im, chunk_steps):
    g, p = a_re.shape
    nt = g // GROUPS_PER_TILE
    a_re, a_im = a_re.astype(F32), a_im.astype(F32)
    step = jnp.exp(log_step.astype(F32))[:, None]

    def discretise(n_steps):
        mag = jnp.exp(a_re * step * n_steps)
        return mag * jnp.cos(a_im * step * n_steps), mag * jnp.sin(a_im * step * n_steps)

    lr, li = discretise(1.0)
    cr, ci = discretise(float(chunk_steps))
    den = a_re * a_re + a_im * a_im
    fr = ((lr - 1.0) * a_re + li * a_im) / den
    fi = (li * a_re - (lr - 1.0) * a_im) / den
    b_re, b_im = b_re.astype(F32), b_im.astype(F32)
    bb_re = fr[..., None] * b_re - fi[..., None] * b_im
    bb_im = fr[..., None] * b_im + fi[..., None] * b_re
    eye = jnp.eye(GROUPS_PER_TILE, dtype=F32)

    bb = jnp.stack([bb_re, bb_im], axis=0).reshape(2, nt, GROUPS_PER_TILE, p, S5_GROUP)
    bb = jnp.transpose(bb, (1, 2, 4, 0, 3))
    b_blk = bb[:, :, :, :, None, :] * eye[None, :, None, None, :, None]
    b_blk = b_blk.reshape(nt, LANES, 2 * GROUPS_PER_TILE * p).astype(BF16)

    cc = jnp.stack([c_re.astype(F32), -c_im.astype(F32)], axis=0)
    cc = cc.reshape(2, nt, GROUPS_PER_TILE, S5_GROUP, p)
    cc = jnp.transpose(cc, (1, 0, 2, 4, 3))
    c_blk = cc[:, :, :, :, None, :] * eye[None, None, :, None, :, None]
    c_blk = c_blk.reshape(nt, 2 * GROUPS_PER_TILE * p, LANES).astype(BF16)

    def tiles(z):
        return z.reshape(nt, GROUPS_PER_TILE * p)

    return b_blk, c_blk, tiles(lr), tiles(li), tiles(cr), tiles(ci)


def _state_to_tiles(s_re, s_im):
    nb, g, p = s_re.shape
    nt = g // GROUPS_PER_TILE
    st = jnp.stack([s_re.reshape(nb, nt, GROUPS_PER_TILE * p),
                    s_im.reshape(nb, nt, GROUPS_PER_TILE * p)], axis=2)
    return st.reshape(nb, nt * 2 * GROUPS_PER_TILE * p)


def _tiles_to_state(x, g, p):
    nb = x.shape[0]
    nt = g // GROUPS_PER_TILE
    st = x.reshape(nb, nt, 2, GROUPS_PER_TILE, p)
    return st[:, :, 0].reshape(nb, g, p), st[:, :, 1].reshape(nb, g, p)


def _s5_outputs(u2, x0, prm, d_skip, steps, tr, want_final):
    b_blk, c_blk, ar, ai = prm
    r, uw = u2.shape
    width = uw // steps
    nt, _, sw = b_blk.shape
    kern = functools.partial(_s5_out_kernel, steps=steps, width=width, n_tiles=nt)
    full3 = lambda a: pl.BlockSpec(a.shape, lambda i: (0, 0, 0))
    full2 = lambda a: pl.BlockSpec(a.shape, lambda i: (0, 0))
    d2 = d_skip.reshape(1, width).astype(F32)
    out_shape = [jax.ShapeDtypeStruct((r, uw), F32)]
    out_specs = [pl.BlockSpec((tr, uw), lambda i: (i, 0))]
    if want_final:
        out_shape.append(jax.ShapeDtypeStruct((r, nt * sw), F32))
        out_specs.append(pl.BlockSpec((tr, nt * sw), lambda i: (i, 0)))
    return pl.pallas_call(
        kern,
        grid=(r // tr,),
        in_specs=[pl.BlockSpec((tr, uw), lambda i: (i, 0)),
                  pl.BlockSpec((tr, nt * sw), lambda i: (i, 0)),
                  full3(b_blk), full3(c_blk), full2(ar), full2(ai), full2(d2)],
        out_specs=out_specs,
        out_shape=out_shape,
        compiler_params=_params("arbitrary"),
        name="s5_outputs",
    )(u2, x0, b_blk, c_blk, ar, ai, d2)


def _s5_prompt(u, batch, seq, prm_all, d_skip, g, p):
    b_blk, c_blk, ar, ai, ar_c, ai_c = prm_all
    steps = STEP_CHUNK
    n_chunks = seq // steps
    r = batch * n_chunks
    width = u.shape[1]
    u2 = u.reshape(r, steps * width)
    nt, _, sw = b_blk.shape
    tr = 128
    z = pl.pallas_call(
        functools.partial(_s5_local_kernel, steps=steps, width=width, n_tiles=nt),
        grid=(r // tr,),
        in_specs=[pl.BlockSpec((tr, steps * width), lambda i: (i, 0)),
                  pl.BlockSpec(b_blk.shape, lambda i: (0, 0, 0)),
                  pl.BlockSpec(ar.shape, lambda i: (0, 0)),
                  pl.BlockSpec(ai.shape, lambda i: (0, 0))],
        out_specs=pl.BlockSpec((tr, nt * sw), lambda i: (i, 0)),
        out_shape=jax.ShapeDtypeStruct((r, nt * sw), F32),
        compiler_params=_params("arbitrary"),
        name="s5_chunk_states",
    )(u2, b_blk, ar, ai)
    half = sw // 2
    xc, xf = pl.pallas_call(
        functools.partial(_s5_carry_kernel, batch=batch, n_chunks=n_chunks),
        grid=(nt,),
        in_specs=[pl.BlockSpec((r, sw), lambda j: (0, j)),
                  pl.BlockSpec((None, 1, half), lambda j: (j, 0, 0)),
                  pl.BlockSpec((None, 1, half), lambda j: (j, 0, 0))],
        out_specs=[pl.BlockSpec((r, sw), lambda j: (0, j)),
                   pl.BlockSpec((batch, sw), lambda j: (0, j))],
        out_shape=[jax.ShapeDtypeStruct((r, nt * sw), F32),
                   jax.ShapeDtypeStruct((batch, nt * sw), F32)],
        compiler_params=_params("arbitrary"),
        name="s5_carry",
    )(z, ar_c.reshape(nt, 1, half), ai_c.reshape(nt, 1, half))
    (y2,) = _s5_outputs(u2, xc, (b_blk, c_blk, ar, ai), d_skip, steps, tr, False)
    s_re, s_im = _tiles_to_state(xf, g, p)
    return y2.reshape(batch * seq, width), s_re, s_im


def _s5_sample(u, nb, t_new, prm_all, d_skip, s_re, s_im):
    b_blk, c_blk, ar, ai, _, _ = prm_all
    g, p = s_re.shape[1], s_re.shape[2]
    width = u.shape[1]
    u2 = u.reshape(nb, t_new * width)
    x0 = _state_to_tiles(s_re.astype(F32), s_im.astype(F32))
    y2, xf = _s5_outputs(u2, x0, (b_blk, c_blk, ar, ai), d_skip, t_new, nb, True)
    n_re, n_im = _tiles_to_state(xf, g, p)
    return y2.reshape(nb * t_new, width), n_re, n_im


def _merge_kernel(attn_ref, y_ref, ga_ref, gb_ref, wa_ref, wg_ref, o_ref):
    d = o_ref.shape[1]
    y_a = jnp.dot(attn_ref[...], wa_ref[...], preferred_element_type=F32)
    z = _gelu_tanh(y_ref[...]).astype(BF16)
    glu = jnp.dot(z, wg_ref[...], preferred_element_type=F32)
    y_b = glu[:, :d] * _sigmoid(glu[:, d:])
    o_ref[...] = (_sigmoid(ga_ref[...]) * y_a + _sigmoid(gb_ref[...]) * y_b).astype(o_ref.dtype)


def _merge(attn, y, ga, gb, w_attn, w_glu, tm):
    m, d = ga.shape
    row = lambda w: pl.BlockSpec((tm, w), lambda i: (i, 0))
    return pl.pallas_call(
        _merge_kernel,
        grid=(m // tm,),
        in_specs=[row(attn.shape[1]), row(y.shape[1]), row(d), row(d),
                  pl.BlockSpec(w_attn.shape, lambda i: (0, 0)),
                  pl.BlockSpec(w_glu.shape, lambda i: (0, 0))],
        out_specs=row(d),
        out_shape=jax.ShapeDtypeStruct((m, d), BF16),
        compiler_params=_params("arbitrary"),
        name="merge",
    )(attn, y, ga, gb, w_attn, w_glu)


def _out_proj_kernel(m_ref, x_ref, w_ref, g_ref, x1_ref, h_ref):
    x1 = x_ref[...] + jnp.dot(m_ref[...], w_ref[...], preferred_element_type=F32)
    x1_ref[...] = x1
    ms = jnp.mean(x1 * x1, axis=-1, keepdims=True)
    h_ref[...] = (x1 * lax.rsqrt(ms + EPS) * g_ref[...]).astype(h_ref.dtype)


def _out_proj(merged, x, w_out, g, tm):
    m, d = x.shape
    row = pl.BlockSpec((tm, d), lambda i: (i, 0))
    return pl.pallas_call(
        _out_proj_kernel,
        grid=(m // tm,),
        in_specs=[row, row, pl.BlockSpec(w_out.shape, lambda i: (0, 0)),
                  pl.BlockSpec((1, d), lambda i: (0, 0))],
        out_specs=[row, row],
        out_shape=[jax.ShapeDtypeStruct((m, d), F32), jax.ShapeDtypeStruct((m, d), BF16)],
        compiler_params=_params("arbitrary"),
        name="out_proj",
    )(merged, x, w_out, g.reshape(1, d))


FF_SUB = 2


def _shift_rows(up, prev, k):
    body = pltpu.roll(up, k, axis=0)
    row = lax.broadcasted_iota(jnp.int32, prev.shape, 0)
    head = jnp.where(row < k, pltpu.roll(prev, k, axis=0), body[:SUBLANES])
    return jnp.concatenate([head, body[SUBLANES:]], axis=0)


def _conv_taps(up, u1, u2, cw, cb):
    return cb + cw[0:1, :] * u2 + cw[1:2, :] * u1 + cw[2:3, :] * up


def _ffn_finish(j, n_j, final_norm, acts, wd_ref, x1_ref, g_ref, y_ref):
    y_ref[...] += jnp.dot(jnp.concatenate(acts, axis=1), wd_ref[...], preferred_element_type=F32)

    @pl.when(j == n_j - 1)
    def _():
        x2 = x1_ref[...] + y_ref[...]
        if final_norm:
            ms = jnp.mean(x2 * x2, axis=-1, keepdims=True)
            x2 = x2 * lax.rsqrt(ms + EPS) * g_ref[...]
        y_ref[...] = x2


def _ffn_prompt_kernel(h_ref, wg_ref, wv_ref, cwg_ref, cwv_ref, cbg_ref, cbv_ref, wd_ref, x1_ref,
                       g_ref, y_ref, convg_ref, convv_ref, carry_ref,
                       *, tm, tiles_per_seq, n_j, final_norm):
    i, j = pl.program_id(0), pl.program_id(1)

    @pl.when(i % tiles_per_seq == 0)
    def _():
        carry_ref[j] = jnp.zeros(carry_ref.shape[1:], F32)

    @pl.when(j == 0)
    def _():
        y_ref[...] = jnp.zeros(y_ref.shape, F32)

    h = h_ref[...]
    prev = carry_ref[j]
    sub = FF_TILE // FF_SUB
    acts, tails = [], ([], [])
    for s in range(FF_SUB):
        cs = slice(s * sub, (s + 1) * sub)
        mixed = []
        for half, (w_ref, cw_ref, cb_ref) in enumerate(((wg_ref, cwg_ref, cbg_ref),
                                                        (wv_ref, cwv_ref, cbv_ref))):
            up = jnp.dot(h, w_ref[:, cs], preferred_element_type=F32)
            p8 = prev[:, half * FF_TILE + s * sub:half * FF_TILE + (s + 1) * sub]
            mixed.append(_conv_taps(up, _shift_rows(up, p8, 1), _shift_rows(up, p8, 2),
                                    cw_ref[:, cs], cb_ref[:, cs]))
            tails[half].append(up[tm - SUBLANES:])
        gate, val = mixed
        acts.append((gate * _sigmoid(gate) * val).astype(BF16))
    tail_g = jnp.concatenate(tails[0], axis=1)
    tail_v = jnp.concatenate(tails[1], axis=1)
    carry_ref[j] = jnp.concatenate([tail_g, tail_v], axis=1)
    convg_ref[...] = tail_g
    convv_ref[...] = tail_v
    _ffn_finish(j, n_j, final_norm, acts, wd_ref, x1_ref, g_ref, y_ref)


def _ffn_sample_kernel(h_ref, wg_ref, wv_ref, cwg_ref, cwv_ref, cbg_ref, cbv_ref, wd_ref, x1_ref,
                       g_ref, s0g_ref, s0v_ref, s1g_ref, s1v_ref, y_ref, convg_ref, convv_ref,
                       *, nb, t_new, n_j, final_norm):
    j = pl.program_id(0)

    @pl.when(j == 0)
    def _():
        y_ref[...] = jnp.zeros(y_ref.shape, F32)

    h = h_ref[...]
    sub = FF_TILE // FF_SUB
    acts = []
    for s in range(FF_SUB):
        cs = slice(s * sub, (s + 1) * sub)
        mixed = []
        for w_ref, cw_ref, cb_ref, s0_ref, s1_ref, conv_ref in (
                (wg_ref, cwg_ref, cbg_ref, s0g_ref, s1g_ref, convg_ref),
                (wv_ref, cwv_ref, cbv_ref, s0v_ref, s1v_ref, convv_ref)):
            up = jnp.dot(h, w_ref[:, cs], preferred_element_type=F32)
            s0, s1 = s0_ref[:, cs], s1_ref[:, cs]
            u1 = jnp.concatenate([s1, up[:(t_new - 1) * nb]], axis=0)
            u2 = jnp.concatenate([s0, s1, up[:(t_new - 2) * nb]], axis=0)
            mixed.append(_conv_taps(up, u1, u2, cw_ref[:, cs], cb_ref[:, cs]))
            conv_ref[0, :, cs] = up[(t_new - 2) * nb:(t_new - 1) * nb]
            conv_ref[1, :, cs] = up[(t_new - 1) * nb:]
        gate, val = mixed
        acts.append((gate * _sigmoid(gate) * val).astype(BF16))
    _ffn_finish(j, n_j, final_norm, acts, wd_ref, x1_ref, g_ref, y_ref)


def _ffn_weight_specs(d, n_j, ix):
    gate = lambda rows: pl.BlockSpec((rows, FF_TILE), ix(lambda j: (0, j)))
    val = lambda rows: pl.BlockSpec((rows, FF_TILE), ix(lambda j: (0, n_j + j)))
    return [gate(d), val(d), gate(3), val(3), gate(1), val(1),
            pl.BlockSpec((FF_TILE, d), ix(lambda j: (j, 0)))]


def _ffn_prompt(h2, x1, wu, cw, cb, wd, g, batch, seq, tm, final_norm):
    m, d = x1.shape
    d_ff = wd.shape[0]
    n_j = d_ff // FF_TILE
    tps = seq // tm
    kern = functools.partial(_ffn_prompt_kernel, tm=tm, tiles_per_seq=tps, n_j=n_j,
                             final_norm=final_norm)
    ix = lambda f: (lambda i, j: f(j))
    conv_spec = pl.BlockSpec((None, SUBLANES, FF_TILE), lambda i, j: (i, 0, j))
    conv_shape = jax.ShapeDtypeStruct((m // tm, SUBLANES, d_ff), F32)
    y, cg, cv = pl.pallas_call(
        kern,
        grid=(m // tm, n_j),
        in_specs=[pl.BlockSpec((tm, d), lambda i, j: (i, 0))] + _ffn_weight_specs(d, n_j, ix)
                 + [pl.BlockSpec((tm, d), lambda i, j: (i, 0)),
                    pl.BlockSpec((1, d), lambda i, j: (0, 0))],
        out_specs=[pl.BlockSpec((tm, d), lambda i, j: (i, 0)), conv_spec, conv_spec],
        out_shape=[jax.ShapeDtypeStruct((m, d), F32), conv_shape, conv_shape],
        scratch_shapes=[pltpu.VMEM((n_j, SUBLANES, 2 * FF_TILE), F32)],
        compiler_params=_params("arbitrary", "arbitrary"),
        name="ffn_prompt",
    )(h2, wu, wu, cw, cw, cb, cb, wd, x1, g.reshape(1, d))
    return y, jnp.concatenate([cg, cv], axis=-1)[tps - 1::tps, SUBLANES - 2:]


def _ffn_sample(h2, x1, wu, cw, cb, wd, g, state, nb, t_new, final_norm):
    m, d = x1.shape
    d_ff = wd.shape[0]
    n_j = d_ff // FF_TILE
    kern = functools.partial(_ffn_sample_kernel, nb=nb, t_new=t_new, n_j=n_j,
                             final_norm=final_norm)
    ix = lambda f: f
    s0, s1 = state[:, 0, :], state[:, 1, :]
    st_gate = pl.BlockSpec((nb, FF_TILE), lambda j: (0, j))
    st_val = pl.BlockSpec((nb, FF_TILE), lambda j: (0, n_j + j))
    conv_spec = pl.BlockSpec((2, nb, FF_TILE), lambda j: (0, 0, j))
    y, cg, cv = pl.pallas_call(
        kern,
        grid=(n_j,),
        in_specs=[pl.BlockSpec((m, d), lambda j: (0, 0))] + _ffn_weight_specs(d, n_j, ix)
                 + [pl.BlockSpec((m, d), lambda j: (0, 0)),
                    pl.BlockSpec((1, d), lambda j: (0, 0)),
                    st_gate, st_val, st_gate, st_val],
        out_specs=[pl.BlockSpec((m, d), lambda j: (0, 0)), conv_spec, conv_spec],
        out_shape=[jax.ShapeDtypeStruct((m, d), F32),
                   jax.ShapeDtypeStruct((2, nb, d_ff), F32),
                   jax.ShapeDtypeStruct((2, nb, d_ff), F32)],
        compiler_params=_params("arbitrary"),
        name="ffn_sample",
    )(h2, wu, wu, cw, cw, cb, cb, wd, x1, g.reshape(1, d), s0, s0, s1, s1)
    return y, jnp.swapaxes(jnp.concatenate([cg, cv], axis=-1), 0, 1)


def kernel(x_prompt, x_sample, cache_k, cache_v, cache_kidx, state_s5_re, state_s5_im, state_ffn_conv, page_table, norm_mix, w_in, w_attn_proj, s5_a_re, s5_a_im, s5_log_step, s5_b_re, s5_b_im, s5_c_re, s5_c_im, s5_d, w_glu, w_out, norm_ffn, w_up, conv_w, conv_b, w_down, norm_final):
    depth = w_in.shape[0]
    batch, seq, d_model = x_prompt.shape
    nb, t_new, _ = x_sample.shape
    d_head = cache_k.shape[-1]
    d_idx = cache_kidx.shape[-1]
    attn_w = N_HEADS * d_head
    kv_w = N_KV_HEADS * d_head
    qi_w = IDX_HEADS * d_idx
    groups, p_state = s5_a_re.shape[1], s5_a_re.shape[2]
    s5_w = groups * S5_GROUP

    xp = x_prompt.reshape(batch * seq, d_model)
    xs = x_sample.reshape(nb * t_new, d_model)
    tm_p = min(512, batch * seq)
    tm_s = nb * t_new

    outs = {name: [] for name in ("kp", "vp", "kip", "srp", "sip", "cp",
                                  "ks", "vs", "kis", "srs", "sis", "cs")}
    for l in range(depth):
        w_t = jnp.swapaxes(w_in[l], 0, 1).astype(BF16)
        o = 0
        seg = {}
        for name, width in (("q", attn_w), ("k", kv_w), ("v", kv_w), ("qi", qi_w), ("ki", d_idx),
                            ("wi", IDX_HEADS), ("u", s5_w), ("ga", d_model), ("gb", d_model)):
            seg[name] = w_t[o:o + width]
            o += width
        w_q_grouped = _group_major_heads(seg["q"], d_head)
        w_u, w_ga, w_gb = seg["u"], seg["ga"], seg["gb"]
        pad = jnp.zeros((LANES - d_idx - IDX_HEADS, d_model), BF16)
        w_small = jnp.concatenate([seg["qi"], seg["k"], seg["v"], seg["ki"], seg["wi"], pad], axis=0)
        w_qiw = jnp.concatenate([seg["qi"], seg["ki"], seg["wi"], pad], axis=0)
        w_kv_t = jnp.concatenate([seg["k"], seg["v"], seg["ki"]], axis=0)
        w_attn_grouped = _group_major_heads(w_attn_proj[l].astype(BF16), d_head)
        w_g = w_glu[l].astype(BF16)
        w_o = w_out[l].astype(BF16)
        wu = w_up[l].astype(BF16)
        cw = conv_w[l].astype(F32)
        cb = conv_b[l].astype(F32).reshape(1, -1)
        wd = w_down[l].astype(BF16)
        s5p = _s5_params(s5_a_re[l], s5_a_im[l], s5_log_step[l], s5_b_re[l], s5_b_im[l],
                         s5_c_re[l], s5_c_im[l], STEP_CHUNK)
        last = l == depth - 1

        def project(h, tm, ws):
            q = _matmul(h, w_q_grouped, tm, F32, "proj_q")
            u = _matmul(h, w_u, tm, F32, "proj_u")
            ga = _matmul(h, w_ga, tm, F32, "proj_ga")
            gb = _matmul(h, w_gb, tm, F32, "proj_gb")
            small = _matmul(h, ws, tm, F32, "proj_small")
            return q, u, ga, gb, small

        h = _rmsnorm(xp, norm_mix[l], tm_p, BF16)
        q, u, ga, gb, qiw = project(h, tm_p, w_qiw)
        k_t, v_t, ki_t, kb_t, vb_t, kib_t = _kv_transposed(h, w_kv_t, batch, seq, kv_w, d_idx,
                                                            min(512, seq))
        k_sel = min(TOPK_MAX, seq // 4)
        attn = _prompt_attention(q, qiw, kib_t, kb_t, vb_t, batch, seq, d_head, d_idx, k_sel,
                                 min(256, seq))
        y5, srp, sip = _s5_prompt(u, batch, seq, s5p, s5_d[l], groups, p_state)
        merged = _merge(attn, y5, ga, gb, w_attn_grouped, w_g, min(256, batch * seq))
        x1, h2 = _out_proj(merged, xp, w_o, norm_ffn[l], tm_p)
        xp, conv_p = _ffn_prompt(h2, x1, wu, cw, cb, wd, norm_final, batch, seq, tm_p, last)

        def heads_last(a_t):
            return jnp.transpose(a_t.reshape(batch, N_KV_HEADS, d_head, seq), (0, 3, 1, 2))

        outs["kp"].append(heads_last(k_t)); outs["vp"].append(heads_last(v_t))
        outs["kip"].append(jnp.swapaxes(ki_t, 1, 2))
        outs["srp"].append(srp); outs["sip"].append(sip); outs["cp"].append(conv_p)

        h = _rmsnorm(xs, norm_mix[l], tm_s, BF16)
        q, u, ga, gb, small = project(h, tm_s, w_small)
        attn = _sample_attention(q, small, cache_k[l], cache_v[l], cache_kidx[l], page_table,
                                 t_new, d_head, d_idx)
        y5, srs, sis = _s5_sample(u, nb, t_new, s5p, s5_d[l], state_s5_re[l], state_s5_im[l])
        merged = _merge(attn, y5, ga, gb, w_attn_grouped, w_g, min(256, nb * t_new))
        x1, h2 = _out_proj(merged, xs, w_o, norm_ffn[l], tm_s)

        def time_major(a):
            return a.reshape(nb, t_new, -1).swapaxes(0, 1).reshape(nb * t_new, -1)

        y_tm, conv_s = _ffn_sample(time_major(h2), time_major(x1), wu, cw, cb, wd, norm_final,
                                   state_ffn_conv[l].astype(F32), nb, t_new, last)
        xs = y_tm.reshape(t_new, nb, d_model).swapaxes(0, 1).reshape(nb * t_new, d_model)
        k_new = small[:, qi_w:qi_w + kv_w].reshape(nb, t_new, N_KV_HEADS, d_head)
        v_new = small[:, qi_w + kv_w:qi_w + 2 * kv_w].reshape(nb, t_new, N_KV_HEADS, d_head)
        ki_new = small[:, qi_w + 2 * kv_w:qi_w + 2 * kv_w + d_idx].reshape(nb, t_new, d_idx)
        outs["ks"].append(k_new); outs["vs"].append(v_new); outs["kis"].append(ki_new)
        outs["srs"].append(srs); outs["sis"].append(sis); outs["cs"].append(conv_s)

    stk = {name: jnp.stack(v) for name, v in outs.items()}
    return (xp.reshape(batch, seq, d_model), xs.reshape(nb, t_new, d_model),
            stk["kp"], stk["vp"], stk["kip"], stk["srp"], stk["sip"], stk["cp"],
            stk["ks"], stk["vs"], stk["kis"], stk["srs"], stk["sis"], stk["cs"])
```

```python
import functools
import math

import jax
import jax.numpy as jnp
from jax import lax
from jax.experimental import pallas as pl
from jax.experimental.pallas import tpu as pltpu

F32 = jnp.float32
BF16 = jnp.bfloat16

EPS = 1e-6
LOG2E = math.log2(math.e)
TOPK_MAX = 256
N_HEADS = 16
N_KV_HEADS = 4
KV_GROUP = N_HEADS // N_KV_HEADS
IDX_HEADS = 8
S5_GROUP = 16
STEP_CHUNK = 8
LANES = 128
SUBLANES = 8
GROUPS_PER_TILE = LANES // S5_GROUP
FF_TILE = 512
VMEM_LIMIT = 48 * 1024 * 1024
NEG_INF = float("-inf")
INT_MIN = -2 ** 31


def _params(*sem):
    return pltpu.CompilerParams(dimension_semantics=sem, vmem_limit_bytes=VMEM_LIMIT)


def _sigmoid(x):
    return 1.0 / (1.0 + jnp.exp(-x))


def _gelu_tanh(x):
    c = math.sqrt(2.0 / math.pi)
    return 0.5 * x * (1.0 + jnp.tanh(c * (x + 0.044715 * (x * x * x))))


def _dot_nt(a, b):
    return lax.dot_general(a, b, (((1,), (1,)), ((), ())), preferred_element_type=F32)


def _rms_kernel(x_ref, g_ref, o_ref):
    x = x_ref[...]
    ms = jnp.mean(x * x, axis=-1, keepdims=True)
    o_ref[...] = (x * lax.rsqrt(ms + EPS) * g_ref[...]).astype(o_ref.dtype)


def _rmsnorm(x, g, tm, out_dtype):
    m, d = x.shape
    return pl.pallas_call(
        _rms_kernel,
        grid=(m // tm,),
        in_specs=[pl.BlockSpec((tm, d), lambda i: (i, 0)),
                  pl.BlockSpec((1, d), lambda i: (0, 0))],
        out_specs=pl.BlockSpec((tm, d), lambda i: (i, 0)),
        out_shape=jax.ShapeDtypeStruct((m, d), out_dtype),
        compiler_params=_params("arbitrary"),
        name="rmsnorm",
    )(x, g.reshape(1, d))


def _mm_kernel(h_ref, w_ref, o_ref):
    o_ref[...] = _dot_nt(h_ref[...], w_ref[...]).astype(o_ref.dtype)


def _matmul(h, w_t, tm, out_dtype, name):
    m, k = h.shape
    n = w_t.shape[0]
    return pl.pallas_call(
        _mm_kernel,
        grid=(m // tm,),
        in_specs=[pl.BlockSpec((tm, k), lambda i: (i, 0)),
                  pl.BlockSpec((n, k), lambda i: (0, 0))],
        out_specs=pl.BlockSpec((tm, n), lambda i: (i, 0)),
        out_shape=jax.ShapeDtypeStruct((m, n), out_dtype),
        compiler_params=_params("arbitrary"),
        name=name,
    )(h, w_t)


def _kv_t_kernel(w_ref, h_ref, k_ref, v_ref, ki_ref, kb_ref, vb_ref, kib_ref, *, kv_w):
    out = _dot_nt(w_ref[...], h_ref[...])
    for lo, hi, full_ref, half_ref in ((0, kv_w, k_ref, kb_ref), (kv_w, 2 * kv_w, v_ref, vb_ref),
                                       (2 * kv_w, out.shape[0], ki_ref, kib_ref)):
        full_ref[...] = out[lo:hi]
        half_ref[...] = out[lo:hi].astype(BF16)


def _kv_transposed(h, w_t, batch, seq, kv_w, d_idx, tn):
    d = h.shape[1]
    nt = seq // tn
    out = lambda rows: pl.BlockSpec((None, rows, tn), lambda b, i: (b, 0, i))
    shape = lambda rows, dt: jax.ShapeDtypeStruct((batch, rows, seq), dt)
    widths = (kv_w, kv_w, d_idx)
    return pl.pallas_call(
        functools.partial(_kv_t_kernel, kv_w=kv_w),
        grid=(batch, nt),
        in_specs=[pl.BlockSpec(w_t.shape, lambda b, i: (0, 0)),
                  pl.BlockSpec((tn, d), lambda b, i: (b * nt + i, 0))],
        out_specs=[out(w) for w in widths] * 2,
        out_shape=[shape(w, F32) for w in widths] + [shape(w, BF16) for w in widths],
        compiler_params=_params("arbitrary", "arbitrary"),
        name="proj_kv_transposed",
    )(w_t, h)


def _count(mask):
    return jnp.sum(jnp.where(mask, 1.0, 0.0), axis=1, keepdims=True)


def _topk_mask(score, col, k_sel, n_cols):
    kf = float(k_sel)
    rows = score.shape[0]

    def as_float(key):
        bits = jnp.where(key < 0, key ^ jnp.int32(0x7FFFFFFF), key)
        return lax.bitcast_convert_type(bits, F32)

    cand0 = jnp.where(_count(score >= 0.0) >= kf, jnp.int32(0), jnp.int32(INT_MIN))
    cand0 = jnp.broadcast_to(cand0, (rows, 1)).astype(jnp.int32)

    def value_step(it, cand):
        trial = cand + lax.shift_left(jnp.int32(1), jnp.int32(30) - it)
        return jnp.where(_count(score >= as_float(trial)) >= kf, trial, cand)

    thr = as_float(lax.fori_loop(0, 31, value_step, cand0))

    above = score > thr
    ties = score == thr
    need = kf - _count(above)
    finite = score > NEG_INF
    few = _count(finite) <= kf
    surplus = jnp.where(few, 0.0, _count(score >= thr) - kf)
    n_bits = max(1, (n_cols - 1).bit_length())

    def index_step(it, m):
        trial = m + lax.shift_left(jnp.int32(1), jnp.int32(n_bits - 1) - it)
        taken = _count(jnp.logical_and(ties, col < trial))
        return jnp.where(taken <= need - 1.0, trial, m)

    def lowest_indices(_):
        return lax.fori_loop(0, n_bits, index_step, jnp.zeros((rows, 1), jnp.int32))

    def all_ties(_):
        return jnp.full((rows, 1), n_cols, jnp.int32)

    last = lax.cond(jnp.max(surplus) > 0.0, lowest_indices, all_ties, None)
    take_tie = jnp.logical_and(jnp.logical_and(ties, col <= last), need >= 1.0)
    top = jnp.logical_or(above, take_tie)
    return jnp.logical_and(finite, jnp.logical_or(few, top))


def _index_weights(kw, d_idx):
    w_scale = (d_idx ** -0.5) * (IDX_HEADS ** -0.5)
    return [kw[:, d_idx + h:d_idx + h + 1] * w_scale for h in range(IDX_HEADS)]


def _head_rows(qi, d_idx):
    return jnp.concatenate([qi[:, h * d_idx:(h + 1) * d_idx] for h in range(IDX_HEADS)], axis=0)


def _weighted_relu_sum(lg, w_cols, rows):
    score = jnp.zeros((rows, lg.shape[1]), F32)
    for h in range(IDX_HEADS):
        score = score + jnp.maximum(lg[h * rows:(h + 1) * rows], 0.0) * w_cols[h]
    return score


HEADS_PER_DOT = 2


def _prompt_attn_kernel(q_ref, qi_ref, kw_ref, kit_ref, kt_ref, vt_ref, prev_ref, o_ref,
                        *, tq, n_keys, k_sel, d_head, d_idx, q_block):
    del prev_ref
    qi = qi_ref[...]
    w_cols = _index_weights(kw_ref[...], d_idx)
    kit = kit_ref[...]
    score = jnp.zeros((tq, n_keys), F32)
    for h in range(IDX_HEADS):
        lg = jnp.dot(qi[:, h * d_idx:(h + 1) * d_idx].astype(BF16), kit,
                     preferred_element_type=F32)
        score = score + jnp.maximum(lg, 0.0) * w_cols[h]

    col = lax.broadcasted_iota(jnp.int32, (tq, n_keys), 1)
    qpos = q_block * tq + lax.broadcasted_iota(jnp.int32, (tq, n_keys), 0)
    score = jnp.where(col <= qpos, score, NEG_INF)
    bias = jnp.where(_topk_mask(score, col, k_sel, n_keys), 0.0, NEG_INF)

    q = q_ref[...] * (d_head ** -0.5 * LOG2E)
    for n in range(N_KV_HEADS):
        kt = kt_ref[n * d_head:(n + 1) * d_head, :]
        vt = vt_ref[n * d_head:(n + 1) * d_head, :]
        for g0 in range(0, KV_GROUP, HEADS_PER_DOT):
            heads = [(g0 + i) * N_KV_HEADS + n for i in range(HEADS_PER_DOT)]
            qs = jnp.concatenate([q[:, h * d_head:(h + 1) * d_head] for h in heads], axis=0)
            s = jnp.dot(qs.astype(BF16), kt, preferred_element_type=F32)
            s = (s.reshape(HEADS_PER_DOT, tq, n_keys) + bias[None]).reshape(-1, n_keys)
            m = jnp.max(s, axis=1, keepdims=True)
            p = jnp.exp2(s - m)
            l = jnp.sum(p, axis=1, keepdims=True)
            o = _dot_nt(p.astype(BF16), vt) / l
            for i, h in enumerate(heads):
                o_ref[:, h * d_head:(h + 1) * d_head] = o[i * tq:(i + 1) * tq].astype(o_ref.dtype)


def _prompt_attention(q, qiw, ki_t, k_t, v_t, batch, seq, d_head, d_idx, k_sel, tq):
    attn_w = q.shape[1]
    kv_w = N_KV_HEADS * d_head
    qi_w = IDX_HEADS * d_idx
    nq = seq // tq
    out = jnp.zeros((batch * seq, attn_w), BF16)
    for qb in range(nq):
        n_keys = (qb + 1) * tq
        row = lambda w, c, qb=qb: pl.BlockSpec((tq, w), lambda b: (b * nq + qb, c))
        keys = lambda w, n_keys=n_keys: pl.BlockSpec((None, w, n_keys), lambda b: (b, 0, 0))
        out = pl.pallas_call(
            functools.partial(_prompt_attn_kernel, tq=tq, n_keys=n_keys, k_sel=k_sel,
                              d_head=d_head, d_idx=d_idx, q_block=qb),
            grid=(batch,),
            in_specs=[row(attn_w, 0), row(qi_w, 0), row(LANES, qi_w // LANES),
                      keys(d_idx), keys(kv_w), keys(kv_w),
                      pl.BlockSpec(memory_space=pl.ANY)],
            out_specs=row(attn_w, 0),
            out_shape=jax.ShapeDtypeStruct((batch * seq, attn_w), BF16),
            input_output_aliases={6: 0},
            compiler_params=_params("arbitrary"),
            name=f"prompt_attention_q{qb}",
        )(q, qiw, qiw, ki_t, k_t, v_t, out)
    return out


def _lane_columns(cols, width):
    rows = cols[0].shape[0]
    lane = lax.broadcasted_iota(jnp.int32, (rows, width), 1)
    out = jnp.zeros((rows, width), F32)
    for j, c in enumerate(cols):
        out = jnp.where(lane == j, c, out)
    return out


def _sample_score_kernel(pt_ref, qi_ref, kw_ref, kin_ref, *rest, n_pages, page, t_new, d_idx):
    del pt_ref
    ipages, s_ref = rest[:n_pages], rest[n_pages]
    past = n_pages * page
    rows = SUBLANES
    qi_rows = _head_rows(qi_ref[...], d_idx)
    qi_bf = qi_rows.astype(BF16)
    w_cols = _index_weights(kw_ref[...], d_idx)
    pad_row = lax.broadcasted_iota(jnp.int32, (rows, page), 0) >= t_new
    first = lax.broadcasted_iota(jnp.int32, (rows, page), 1) == 0
    for p in range(n_pages):
        lg = jnp.dot(qi_bf, ipages[p][...].astype(BF16), preferred_element_type=F32)
        pad_val = jnp.where(first, 0.0, NEG_INF) if p == 0 else NEG_INF
        s_ref[:, p * page:(p + 1) * page] = jnp.where(pad_row, pad_val,
                                                      _weighted_relu_sum(lg, w_cols, rows))
    kin = kin_ref[...]
    new_cols = []
    for j in range(t_new):
        lg = jnp.sum(qi_rows * kin[j:j + 1, :], axis=1, keepdims=True)
        new_cols.append(_weighted_relu_sum(lg, w_cols, rows))
    lane = lax.broadcasted_iota(jnp.int32, (rows, LANES), 1)
    t = lax.broadcasted_iota(jnp.int32, (rows, LANES), 0)
    causal_new = jnp.logical_and(lane <= t, t < t_new)
    s_ref[:, past:] = jnp.where(causal_new, _lane_columns(new_cols, LANES), NEG_INF)


def _select_kernel(s_ref, b_ref, *, k_sel):
    score = s_ref[...]
    col = lax.broadcasted_iota(jnp.int32, score.shape, 1)
    keep = _topk_mask(score, col, k_sel, score.shape[1])
    b_ref[...] = jnp.where(keep, 0.0, NEG_INF)


def _sample_attn_kernel(pt_ref, q_ref, kn_ref, vn_ref, bias_ref, *rest,
                        n_pages, page, t_new, d_head):
    del pt_ref
    kpages = rest[:n_pages]
    vpages = rest[n_pages:2 * n_pages]
    o_ref, kt_ref, vt_ref = rest[2 * n_pages:]
    past = n_pages * page
    rows = SUBLANES
    kv_w = N_KV_HEADS * d_head

    for p in range(n_pages):
        kt_ref[:, p * page:(p + 1) * page] = kpages[p][...].astype(BF16)
        vt_ref[:, p * page:(p + 1) * page] = vpages[p][...].astype(BF16)

    bias = jnp.concatenate([bias_ref[...]] * N_HEADS, axis=0)

    q = q_ref[...] * (d_head ** -0.5 * LOG2E)
    lane_head = lax.broadcasted_iota(jnp.int32, (rows, kv_w), 1) // d_head
    q_rows = jnp.concatenate(
        [jnp.where(lane_head == n, q[:, g * kv_w:(g + 1) * kv_w], 0.0)
         for g in range(KV_GROUP) for n in range(N_KV_HEADS)], axis=0)

    s_past = jnp.dot(q_rows.astype(BF16), kt_ref[...], preferred_element_type=F32)
    s_past = s_past + bias[:, :past]
    kn = kn_ref[...]
    s_new = _lane_columns(
        [jnp.sum(q_rows * kn[j:j + 1, :], axis=1, keepdims=True) for j in range(t_new)], LANES)
    s_new = s_new + bias[:, past:]

    m = jnp.maximum(jnp.max(s_past, axis=1, keepdims=True), jnp.max(s_new, axis=1, keepdims=True))
    p_past = jnp.exp2(s_past - m)
    p_new = jnp.exp2(s_new - m)
    l = jnp.sum(p_past, axis=1, keepdims=True) + jnp.sum(p_new, axis=1, keepdims=True)
    o = _dot_nt(p_past.astype(BF16), vt_ref[...])
    vn = vn_ref[...]
    for j in range(t_new):
        o = o + p_new[:, j:j + 1] * vn[j:j + 1, :]
    o = o / l

    chunks = []
    for g in range(KV_GROUP):
        acc = jnp.zeros((rows, kv_w), F32)
        for n in range(N_KV_HEADS):
            r0 = (g * N_KV_HEADS + n) * rows
            acc = acc + jnp.where(lane_head == n, o[r0:r0 + rows], 0.0)
        chunks.append(acc)
    o_ref[...] = jnp.concatenate(chunks, axis=1).astype(o_ref.dtype)


def _page_map(b, pt, *, p, n_pages):
    return (pt[b * n_pages + p], 0, 0)


def _sample_attention(q, small, cache_k, cache_v, cache_ki, page_table, t_new, d_head, d_idx):
    nb, n_pages = page_table.shape
    n_pool, page = cache_k.shape[0], cache_k.shape[1]
    attn_w = q.shape[1]
    kv_w = N_KV_HEADS * d_head
    qi_w = IDX_HEADS * d_idx
    rows = SUBLANES
    past = n_pages * page
    k_sel = min(TOPK_MAX, (past + t_new) // 4)

    def pad_rows(a):
        a = a.reshape(nb, t_new, a.shape[-1])
        return jnp.pad(a, ((0, 0), (0, rows - t_new), (0, 0)))

    q8 = pad_rows(q)
    qi8 = pad_rows(small[:, :qi_w])
    kw8 = pad_rows(small[:, qi_w + 2 * kv_w:])
    kn = pad_rows(small[:, qi_w:qi_w + kv_w])
    vn = pad_rows(small[:, qi_w + kv_w:qi_w + 2 * kv_w])
    kin = pad_rows(small[:, qi_w + 2 * kv_w:qi_w + 2 * kv_w + d_idx])

    ck = jnp.transpose(cache_k, (0, 2, 3, 1)).reshape(n_pool, kv_w, page)
    cv = jnp.transpose(cache_v, (0, 2, 3, 1)).reshape(n_pool, kv_w, page)
    ci = jnp.transpose(cache_ki, (0, 2, 1))

    def row_spec(w):
        return pl.BlockSpec((None, rows, w), lambda b, pt: (b, 0, 0))

    def page_specs(w):
        return [pl.BlockSpec((None, w, page), functools.partial(_page_map, p=p, n_pages=n_pages))
                for p in range(n_pages)]

    pt_flat = page_table.reshape(-1)
    n_keys = past + LANES

    score = pl.pallas_call(
        functools.partial(_sample_score_kernel, n_pages=n_pages, page=page, t_new=t_new,
                          d_idx=d_idx),
        grid_spec=pltpu.PrefetchScalarGridSpec(
            num_scalar_prefetch=1,
            grid=(nb,),
            in_specs=[row_spec(qi_w), row_spec(LANES), row_spec(d_idx)] + page_specs(d_idx),
            out_specs=pl.BlockSpec((None, rows, n_keys), lambda b, pt: (b, 0, 0)),
        ),
        out_shape=jax.ShapeDtypeStruct((nb, rows, n_keys), F32),
        compiler_params=_params("arbitrary"),
        name="sample_scores",
    )(pt_flat, qi8, kw8, kin, *([ci] * n_pages))

    sel_rows = math.gcd(nb * rows, 256)
    bias = pl.pallas_call(
        functools.partial(_select_kernel, k_sel=k_sel),
        grid=(nb * rows // sel_rows,),
        in_specs=[pl.BlockSpec((sel_rows, n_keys), lambda i: (i, 0))],
        out_specs=pl.BlockSpec((sel_rows, n_keys), lambda i: (i, 0)),
        out_shape=jax.ShapeDtypeStruct((nb * rows, n_keys), F32),
        compiler_params=_params("arbitrary"),
        name="sample_select",
    )(score.reshape(nb * rows, n_keys)).reshape(nb, rows, n_keys)

    out = pl.pallas_call(
        functools.partial(_sample_attn_kernel, n_pages=n_pages, page=page, t_new=t_new,
                          d_head=d_head),
        grid_spec=pltpu.PrefetchScalarGridSpec(
            num_scalar_prefetch=1,
            grid=(nb,),
            in_specs=[row_spec(attn_w), row_spec(kv_w), row_spec(kv_w), row_spec(n_keys)]
                     + page_specs(kv_w) + page_specs(kv_w),
            out_specs=pl.BlockSpec((None, rows, attn_w), lambda b, pt: (b, 0, 0)),
            scratch_shapes=[pltpu.VMEM((kv_w, past), BF16), pltpu.VMEM((kv_w, past), BF16)],
        ),
        out_shape=jax.ShapeDtypeStruct((nb, rows, attn_w), BF16),
        compiler_params=_params("arbitrary"),
        name="sample_attention",
    )(pt_flat, q8, kn, vn, bias, *([ck] * n_pages), *([cv] * n_pages))
    return out[:, :t_new].reshape(nb * t_new, attn_w)


def _group_major_heads(a, d_head):
    n = a.shape[1]
    a = a.reshape(N_KV_HEADS, KV_GROUP, d_head, n)
    return jnp.swapaxes(a, 0, 1).reshape(N_HEADS * d_head, n)


def _s5_step(xr, xi, ar, ai, bu, half):
    br, bi = bu[:, :half], bu[:, half:]
    if xr is None:
        return br, bi
    return ar * xr - ai * xi + br, ar * xi + ai * xr + bi


def _s5_inputs(u_ref, b_ref, j):
    tr, steps, _ = u_ref.shape
    x = jnp.transpose(u_ref[:, :, j * LANES:(j + 1) * LANES], (1, 0, 2))
    us = [x[s] for s in range(steps)]
    bu = jnp.dot(jnp.concatenate(us, axis=0).astype(BF16), b_ref[j], preferred_element_type=F32)
    return us, [bu[s * tr:(s + 1) * tr] for s in range(steps)]


def _s5_local_kernel(u_ref, b_ref, ar_ref, ai_ref, z_ref, *, n_tiles):
    sw = b_ref.shape[2]
    half = sw // 2
    for j in range(n_tiles):
        ar, ai = ar_ref[j:j + 1, :], ai_ref[j:j + 1, :]
        xr = xi = None
        for bu in _s5_inputs(u_ref, b_ref, j)[1]:
            xr, xi = _s5_step(xr, xi, ar, ai, bu, half)
        z_ref[:, j * sw:j * sw + half] = xr
        z_ref[:, j * sw + half:(j + 1) * sw] = xi


def _s5_carry_kernel(z_ref, ar_ref, ai_ref, xc_ref, xf_ref, *, batch, n_chunks):
    half = z_ref.shape[1] // 2
    ar, ai = ar_ref[...], ai_ref[...]

    def body(c, carry):
        new = []
        for b in range(batch):
            xr, xi = carry[b]
            row = b * n_chunks + c
            xc_ref[pl.ds(row, 1), :] = jnp.concatenate([xr, xi], axis=1)
            z = z_ref[pl.ds(row, 1), :]
            new.append((ar * xr - ai * xi + z[:, :half], ar * xi + ai * xr + z[:, half:]))
        return tuple(new)

    zero = jnp.zeros((1, half), F32)
    final = lax.fori_loop(0, n_chunks, body, tuple((zero, zero) for _ in range(batch)))
    for b in range(batch):
        xf_ref[b:b + 1, :] = jnp.concatenate(list(final[b]), axis=1)


def _s5_out_kernel(u_ref, x0_ref, b_ref, c_ref, ar_ref, ai_ref, d_ref, y_ref, *maybe_xf,
                   n_tiles):
    tr, steps, _ = u_ref.shape
    sw = b_ref.shape[2]
    half = sw // 2
    for j in range(n_tiles):
        ar, ai = ar_ref[j:j + 1, :], ai_ref[j:j + 1, :]
        xr = x0_ref[:, j * sw:j * sw + half]
        xi = x0_ref[:, j * sw + half:(j + 1) * sw]
        d = d_ref[:, j * LANES:(j + 1) * LANES]
        us, bus = _s5_inputs(u_ref, b_ref, j)
        xs = []
        for bu in bus:
            xr, xi = _s5_step(xr, xi, ar, ai, bu, half)
            xs.append(jnp.concatenate([xr, xi], axis=1).astype(BF16))
        y = jnp.dot(jnp.concatenate(xs, axis=0), c_ref[j], preferred_element_type=F32)
        ys = jnp.stack([y[s * tr:(s + 1) * tr] + d * us[s] for s in range(steps)], axis=0)
        y_ref[:, :, j * LANES:(j + 1) * LANES] = jnp.transpose(ys, (1, 0, 2))
        if maybe_xf:
            maybe_xf[0][:, j * sw:j * sw + half] = xr
            maybe_xf[0][:, j * sw + half:(j + 1) * sw] = xi


def _s5_params(a_re, a_im, log_step, b_re, b_im, c_re, c_im, chunk_steps):
    g, p = a_re.shape
    nt = g // GROUPS_PER_TILE
    a_re, a_im = a_re.astype(F32), a_im.astype(F32)
    step = jnp.exp(log_step.astype(F32))[:, None]

    def discretise(n_steps):
        mag = jnp.exp(a_re * step * n_steps)
        return mag * jnp.cos(a_im * step * n_steps), mag * jnp.sin(a_im * step * n_steps)

    lr, li = discretise(1.0)
    cr, ci = discretise(float(chunk_steps))
    den = a_re * a_re + a_im * a_im
    fr = ((lr - 1.0) * a_re + li * a_im) / den
    fi = (li * a_re - (lr - 1.0) * a_im) / den
    b_re, b_im = b_re.astype(F32), b_im.astype(F32)
    bb_re = fr[..., None] * b_re - fi[..., None] * b_im
    bb_im = fr[..., None] * b_im + fi[..., None] * b_re
    eye = jnp.eye(GROUPS_PER_TILE, dtype=F32)

    bb = jnp.stack([bb_re, bb_im], axis=0).reshape(2, nt, GROUPS_PER_TILE, p, S5_GROUP)
    bb = jnp.transpose(bb, (1, 2, 4, 0, 3))
    b_blk = bb[:, :, :, :, None, :] * eye[None, :, None, None, :, None]
    b_blk = b_blk.reshape(nt, LANES, 2 * GROUPS_PER_TILE * p).astype(BF16)

    cc = jnp.stack([c_re.astype(F32), -c_im.astype(F32)], axis=0)
    cc = cc.reshape(2, nt, GROUPS_PER_TILE, S5_GROUP, p)
    cc = jnp.transpose(cc, (1, 0, 2, 4, 3))
    c_blk = cc[:, :, :, :, None, :] * eye[None, None, :, None, :, None]
    c_blk = c_blk.reshape(nt, 2 * GROUPS_PER_TILE * p, LANES).astype(BF16)

    def tiles(z):
        return z.reshape(nt, GROUPS_PER_TILE * p)

    return b_blk, c_blk, tiles(lr), tiles(li), tiles(cr), tiles(ci)


def _state_to_tiles(s_re, s_im):
    nb, g, p = s_re.shape
    nt = g // GROUPS_PER_TILE
    st = jnp.stack([s_re.reshape(nb, nt, GROUPS_PER_TILE * p),
                    s_im.reshape(nb, nt, GROUPS_PER_TILE * p)], axis=2)
    return st.reshape(nb, nt * 2 * GROUPS_PER_TILE * p)


def _tiles_to_state(x, g, p):
    nb = x.shape[0]
    nt = g // GROUPS_PER_TILE
    st = x.reshape(nb, nt, 2, GROUPS_PER_TILE, p)
    return st[:, :, 0].reshape(nb, g, p), st[:, :, 1].reshape(nb, g, p)


def _s5_outputs(u3, x0, prm, d_skip, tr, want_final):
    b_blk, c_blk, ar, ai = prm
    r, steps, width = u3.shape
    nt, _, sw = b_blk.shape
    kern = functools.partial(_s5_out_kernel, n_tiles=nt)
    full3 = lambda a: pl.BlockSpec(a.shape, lambda i: (0, 0, 0))
    full2 = lambda a: pl.BlockSpec(a.shape, lambda i: (0, 0))
    d2 = d_skip.reshape(1, width).astype(F32)
    tokens = pl.BlockSpec((tr, steps, width), lambda i: (i, 0, 0))
    out_shape = [jax.ShapeDtypeStruct(u3.shape, F32)]
    out_specs = [tokens]
    if want_final:
        out_shape.append(jax.ShapeDtypeStruct((r, nt * sw), F32))
        out_specs.append(pl.BlockSpec((tr, nt * sw), lambda i: (i, 0)))
    return pl.pallas_call(
        kern,
        grid=(r // tr,),
        in_specs=[tokens, pl.BlockSpec((tr, nt * sw), lambda i: (i, 0)),
                  full3(b_blk), full3(c_blk), full2(ar), full2(ai), full2(d2)],
        out_specs=out_specs,
        out_shape=out_shape,
        compiler_params=_params("arbitrary"),
        name="s5_outputs",
    )(u3, x0, b_blk, c_blk, ar, ai, d2)


def _s5_prompt(u, batch, seq, prm_all, d_skip, g, p):
    b_blk, c_blk, ar, ai, ar_c, ai_c = prm_all
    steps = STEP_CHUNK
    n_chunks = seq // steps
    r = batch * n_chunks
    width = u.shape[1]
    u3 = u.reshape(r, steps, width)
    nt, _, sw = b_blk.shape
    tr = 128
    z = pl.pallas_call(
        functools.partial(_s5_local_kernel, n_tiles=nt),
        grid=(r // tr,),
        in_specs=[pl.BlockSpec((tr, steps, width), lambda i: (i, 0, 0)),
                  pl.BlockSpec(b_blk.shape, lambda i: (0, 0, 0)),
                  pl.BlockSpec(ar.shape, lambda i: (0, 0)),
                  pl.BlockSpec(ai.shape, lambda i: (0, 0))],
        out_specs=pl.BlockSpec((tr, nt * sw), lambda i: (i, 0)),
        out_shape=jax.ShapeDtypeStruct((r, nt * sw), F32),
        compiler_params=_params("arbitrary"),
        name="s5_chunk_states",
    )(u3, b_blk, ar, ai)
    half = sw // 2
    xc, xf = pl.pallas_call(
        functools.partial(_s5_carry_kernel, batch=batch, n_chunks=n_chunks),
        grid=(nt,),
        in_specs=[pl.BlockSpec((r, sw), lambda j: (0, j)),
                  pl.BlockSpec((None, 1, half), lambda j: (j, 0, 0)),
                  pl.BlockSpec((None, 1, half), lambda j: (j, 0, 0))],
        out_specs=[pl.BlockSpec((r, sw), lambda j: (0, j)),
                   pl.BlockSpec((batch, sw), lambda j: (0, j))],
        out_shape=[jax.ShapeDtypeStruct((r, nt * sw), F32),
                   jax.ShapeDtypeStruct((batch, nt * sw), F32)],
        compiler_params=_params("arbitrary"),
        name="s5_carry",
    )(z, ar_c.reshape(nt, 1, half), ai_c.reshape(nt, 1, half))
    (y3,) = _s5_outputs(u3, xc, (b_blk, c_blk, ar, ai), d_skip, tr, False)
    s_re, s_im = _tiles_to_state(xf, g, p)
    return y3.reshape(batch * seq, width), s_re, s_im


def _s5_sample(u, nb, t_new, prm_all, d_skip, s_re, s_im):
    b_blk, c_blk, ar, ai, _, _ = prm_all
    g, p = s_re.shape[1], s_re.shape[2]
    width = u.shape[1]
    x0 = _state_to_tiles(s_re.astype(F32), s_im.astype(F32))
    y3, xf = _s5_outputs(u.reshape(nb, t_new, width), x0, (b_blk, c_blk, ar, ai), d_skip, nb, True)
    n_re, n_im = _tiles_to_state(xf, g, p)
    return y3.reshape(nb * t_new, width), n_re, n_im


def _merge_kernel(attn_ref, y_ref, ga_ref, gb_ref, wa_ref, wg_ref, o_ref):
    d = o_ref.shape[1]
    y_a = jnp.dot(attn_ref[...], wa_ref[...], preferred_element_type=F32)
    z = _gelu_tanh(y_ref[...]).astype(BF16)
    glu = jnp.dot(z, wg_ref[...], preferred_element_type=F32)
    y_b = glu[:, :d] * _sigmoid(glu[:, d:])
    o_ref[...] = (_sigmoid(ga_ref[...]) * y_a + _sigmoid(gb_ref[...]) * y_b).astype(o_ref.dtype)


def _merge(attn, y, ga, gb, w_attn, w_glu, tm):
    m, d = ga.shape
    row = lambda w: pl.BlockSpec((tm, w), lambda i: (i, 0))
    return pl.pallas_call(
        _merge_kernel,
        grid=(m // tm,),
        in_specs=[row(attn.shape[1]), row(y.shape[1]), row(d), row(d),
                  pl.BlockSpec(w_attn.shape, lambda i: (0, 0)),
                  pl.BlockSpec(w_glu.shape, lambda i: (0, 0))],
        out_specs=row(d),
        out_shape=jax.ShapeDtypeStruct((m, d), BF16),
        compiler_params=_params("arbitrary"),
        name="merge",
    )(attn, y, ga, gb, w_attn, w_glu)


def _out_proj_kernel(m_ref, x_ref, w_ref, g_ref, x1_ref, h_ref):
    x1 = x_ref[...] + jnp.dot(m_ref[...], w_ref[...], preferred_element_type=F32)
    x1_ref[...] = x1
    ms = jnp.mean(x1 * x1, axis=-1, keepdims=True)
    h_ref[...] = (x1 * lax.rsqrt(ms + EPS) * g_ref[...]).astype(h_ref.dtype)


def _out_proj(merged, x, w_out, g, tm):
    m, d = x.shape
    row = pl.BlockSpec((tm, d), lambda i: (i, 0))
    return pl.pallas_call(
        _out_proj_kernel,
        grid=(m // tm,),
        in_specs=[row, row, pl.BlockSpec(w_out.shape, lambda i: (0, 0)),
                  pl.BlockSpec((1, d), lambda i: (0, 0))],
        out_specs=[row, row],
        out_shape=[jax.ShapeDtypeStruct((m, d), F32), jax.ShapeDtypeStruct((m, d), BF16)],
        compiler_params=_params("arbitrary"),
        name="out_proj",
    )(merged, x, w_out, g.reshape(1, d))


FF_SUB = 2


def _shift_rows(up, prev, k):
    body = pltpu.roll(up, k, axis=0)
    row = lax.broadcasted_iota(jnp.int32, prev.shape, 0)
    head = jnp.where(row < k, pltpu.roll(prev, k, axis=0), body[:SUBLANES])
    return jnp.concatenate([head, body[SUBLANES:]], axis=0)


def _conv_taps(up, u1, u2, cw, cb):
    return cb + cw[0:1, :] * u2 + cw[1:2, :] * u1 + cw[2:3, :] * up


def _ffn_finish(j, n_j, final_norm, acts, wd_ref, x1_ref, g_ref, y_ref):
    y_ref[...] += jnp.dot(jnp.concatenate(acts, axis=1), wd_ref[...], preferred_element_type=F32)

    @pl.when(j == n_j - 1)
    def _():
        x2 = x1_ref[...] + y_ref[...]
        if final_norm:
            ms = jnp.mean(x2 * x2, axis=-1, keepdims=True)
            x2 = x2 * lax.rsqrt(ms + EPS) * g_ref[...]
        y_ref[...] = x2


def _ffn_prompt_kernel(h_ref, wg_ref, wv_ref, cwg_ref, cwv_ref, cbg_ref, cbv_ref, wd_ref, x1_ref,
                       g_ref, y_ref, convg_ref, convv_ref, carry_ref,
                       *, tm, tiles_per_seq, n_j, final_norm):
    i, j = pl.program_id(0), pl.program_id(1)

    @pl.when(i % tiles_per_seq == 0)
    def _():
        carry_ref[j] = jnp.zeros(carry_ref.shape[1:], F32)

    @pl.when(j == 0)
    def _():
        y_ref[...] = jnp.zeros(y_ref.shape, F32)

    h = h_ref[...]
    prev = carry_ref[j]
    sub = FF_TILE // FF_SUB
    acts, tails = [], ([], [])
    for s in range(FF_SUB):
        cs = slice(s * sub, (s + 1) * sub)
        mixed = []
        for half, (w_ref, cw_ref, cb_ref) in enumerate(((wg_ref, cwg_ref, cbg_ref),
                                                        (wv_ref, cwv_ref, cbv_ref))):
            up = jnp.dot(h, w_ref[:, cs], preferred_element_type=F32)
            p8 = prev[:, half * FF_TILE + s * sub:half * FF_TILE + (s + 1) * sub]
            mixed.append(_conv_taps(up, _shift_rows(up, p8, 1), _shift_rows(up, p8, 2),
                                    cw_ref[:, cs], cb_ref[:, cs]))
            tails[half].append(up[tm - SUBLANES:])
        gate, val = mixed
        acts.append((gate * _sigmoid(gate) * val).astype(BF16))
    tail_g = jnp.concatenate(tails[0], axis=1)
    tail_v = jnp.concatenate(tails[1], axis=1)
    carry_ref[j] = jnp.concatenate([tail_g, tail_v], axis=1)
    convg_ref[...] = tail_g
    convv_ref[...] = tail_v
    _ffn_finish(j, n_j, final_norm, acts, wd_ref, x1_ref, g_ref, y_ref)


def _ffn_sample_kernel(h_ref, wg_ref, wv_ref, cwg_ref, cwv_ref, cbg_ref, cbv_ref, wd_ref, x1_ref,
                       g_ref, s0g_ref, s0v_ref, s1g_ref, s1v_ref, y_ref, convg_ref, convv_ref,
                       *, nb, t_new, n_j, final_norm):
    j = pl.program_id(0)

    @pl.when(j == 0)
    def _():
        y_ref[...] = jnp.zeros(y_ref.shape, F32)

    h = h_ref[...]
    sub = FF_TILE // FF_SUB
    acts = []
    for s in range(FF_SUB):
        cs = slice(s * sub, (s + 1) * sub)
        mixed = []
        for w_ref, cw_ref, cb_ref, s0_ref, s1_ref, conv_ref in (
                (wg_ref, cwg_ref, cbg_ref, s0g_ref, s1g_ref, convg_ref),
                (wv_ref, cwv_ref, cbv_ref, s0v_ref, s1v_ref, convv_ref)):
            up = jnp.dot(h, w_ref[:, cs], preferred_element_type=F32)
            s0, s1 = s0_ref[:, cs], s1_ref[:, cs]
            u1 = jnp.concatenate([s1, up[:(t_new - 1) * nb]], axis=0)
            u2 = jnp.concatenate([s0, s1, up[:(t_new - 2) * nb]], axis=0)
            mixed.append(_conv_taps(up, u1, u2, cw_ref[:, cs], cb_ref[:, cs]))
            conv_ref[0, :, cs] = up[(t_new - 2) * nb:(t_new - 1) * nb]
            conv_ref[1, :, cs] = up[(t_new - 1) * nb:]
        gate, val = mixed
        acts.append((gate * _sigmoid(gate) * val).astype(BF16))
    _ffn_finish(j, n_j, final_norm, acts, wd_ref, x1_ref, g_ref, y_ref)


def _ffn_weight_specs(d, n_j, ix):
    gate = lambda rows: pl.BlockSpec((rows, FF_TILE), ix(lambda j: (0, j)))
    val = lambda rows: pl.BlockSpec((rows, FF_TILE), ix(lambda j: (0, n_j + j)))
    return [gate(d), val(d), gate(3), val(3), gate(1), val(1),
            pl.BlockSpec((FF_TILE, d), ix(lambda j: (j, 0)))]


def _ffn_prompt(h2, x1, wu, cw, cb, wd, g, batch, seq, tm, final_norm):
    m, d = x1.shape
    d_ff = wd.shape[0]
    n_j = d_ff // FF_TILE
    tps = seq // tm
    kern = functools.partial(_ffn_prompt_kernel, tm=tm, tiles_per_seq=tps, n_j=n_j,
                             final_norm=final_norm)
    ix = lambda f: (lambda i, j: f(j))
    conv_spec = pl.BlockSpec((None, SUBLANES, FF_TILE), lambda i, j: (i, 0, j))
    conv_shape = jax.ShapeDtypeStruct((m // tm, SUBLANES, d_ff), F32)
    y, cg, cv = pl.pallas_call(
        kern,
        grid=(m // tm, n_j),
        in_specs=[pl.BlockSpec((tm, d), lambda i, j: (i, 0))] + _ffn_weight_specs(d, n_j, ix)
                 + [pl.BlockSpec((tm, d), lambda i, j: (i, 0)),
                    pl.BlockSpec((1, d), lambda i, j: (0, 0))],
        out_specs=[pl.BlockSpec((tm, d), lambda i, j: (i, 0)), conv_spec, conv_spec],
        out_shape=[jax.ShapeDtypeStruct((m, d), F32), conv_shape, conv_shape],
        scratch_shapes=[pltpu.VMEM((n_j, SUBLANES, 2 * FF_TILE), F32)],
        compiler_params=_params("arbitrary", "arbitrary"),
        name="ffn_prompt",
    )(h2, wu, wu, cw, cw, cb, cb, wd, x1, g.reshape(1, d))
    return y, jnp.concatenate([cg, cv], axis=-1)[tps - 1::tps, SUBLANES - 2:]


def _ffn_sample(h2, x1, wu, cw, cb, wd, g, state, nb, t_new, final_norm):
    m, d = x1.shape
    d_ff = wd.shape[0]
    n_j = d_ff // FF_TILE
    kern = functools.partial(_ffn_sample_kernel, nb=nb, t_new=t_new, n_j=n_j,
                             final_norm=final_norm)
    ix = lambda f: f
    s0, s1 = state[:, 0, :], state[:, 1, :]
    st_gate = pl.BlockSpec((nb, FF_TILE), lambda j: (0, j))
    st_val = pl.BlockSpec((nb, FF_TILE), lambda j: (0, n_j + j))
    conv_spec = pl.BlockSpec((2, nb, FF_TILE), lambda j: (0, 0, j))
    y, cg, cv = pl.pallas_call(
        kern,
        grid=(n_j,),
        in_specs=[pl.BlockSpec((m, d), lambda j: (0, 0))] + _ffn_weight_specs(d, n_j, ix)
                 + [pl.BlockSpec((m, d), lambda j: (0, 0)),
                    pl.BlockSpec((1, d), lambda j: (0, 0)),
                    st_gate, st_val, st_gate, st_val],
        out_specs=[pl.BlockSpec((m, d), lambda j: (0, 0)), conv_spec, conv_spec],
        out_shape=[jax.ShapeDtypeStruct((m, d), F32),
                   jax.ShapeDtypeStruct((2, nb, d_ff), F32),
                   jax.ShapeDtypeStruct((2, nb, d_ff), F32)],
        compiler_params=_params("arbitrary"),
        name="ffn_sample",
    )(h2, wu, wu, cw, cw, cb, cb, wd, x1, g.reshape(1, d), s0, s0, s1, s1)
    return y, jnp.swapaxes(jnp.concatenate([cg, cv], axis=-1), 0, 1)


def kernel(x_prompt, x_sample, cache_k, cache_v, cache_kidx, state_s5_re, state_s5_im, state_ffn_conv, page_table, norm_mix, w_in, w_attn_proj, s5_a_re, s5_a_im, s5_log_step, s5_b_re, s5_b_im, s5_c_re, s5_c_im, s5_d, w_glu, w_out, norm_ffn, w_up, conv_w, conv_b, w_down, norm_final):
    depth = w_in.shape[0]
    batch, seq, d_model = x_prompt.shape
    nb, t_new, _ = x_sample.shape
    d_head = cache_k.shape[-1]
    d_idx = cache_kidx.shape[-1]
    attn_w = N_HEADS * d_head
    kv_w = N_KV_HEADS * d_head
    qi_w = IDX_HEADS * d_idx
    groups, p_state = s5_a_re.shape[1], s5_a_re.shape[2]
    s5_w = groups * S5_GROUP

    xp = x_prompt.reshape(batch * seq, d_model)
    xs = x_sample.reshape(nb * t_new, d_model)
    tm_p = min(512, batch * seq)
    tm_s = nb * t_new

    outs = {name: [] for name in ("kp", "vp", "kip", "srp", "sip", "cp",
                                  "ks", "vs", "kis", "srs", "sis", "cs")}
    for l in range(depth):
        w_t = jnp.swapaxes(w_in[l], 0, 1).astype(BF16)
        o = 0
        seg = {}
        for name, width in (("q", attn_w), ("k", kv_w), ("v", kv_w), ("qi", qi_w), ("ki", d_idx),
                            ("wi", IDX_HEADS), ("u", s5_w), ("ga", d_model), ("gb", d_model)):
            seg[name] = w_t[o:o + width]
            o += width
        w_q_grouped = _group_major_heads(seg["q"], d_head)
        w_u, w_ga, w_gb = seg["u"], seg["ga"], seg["gb"]
        pad = jnp.zeros((LANES - d_idx - IDX_HEADS, d_model), BF16)
        w_small = jnp.concatenate([seg["qi"], seg["k"], seg["v"], seg["ki"], seg["wi"], pad], axis=0)
        w_qiw = jnp.concatenate([seg["qi"], seg["ki"], seg["wi"], pad], axis=0)
        w_kv_t = jnp.concatenate([seg["k"], seg["v"], seg["ki"]], axis=0)
        w_attn_grouped = _group_major_heads(w_attn_proj[l].astype(BF16), d_head)
        w_g = w_glu[l].astype(BF16)
        w_o = w_out[l].astype(BF16)
        wu = w_up[l].astype(BF16)
        cw = conv_w[l].astype(F32)
        cb = conv_b[l].astype(F32).reshape(1, -1)
        wd = w_down[l].astype(BF16)
        s5p = _s5_params(s5_a_re[l], s5_a_im[l], s5_log_step[l], s5_b_re[l], s5_b_im[l],
                         s5_c_re[l], s5_c_im[l], STEP_CHUNK)
        last = l == depth - 1

        def project(h, tm, ws):
            q = _matmul(h, w_q_grouped, tm, F32, "proj_q")
            u = _matmul(h, w_u, tm, F32, "proj_u")
            ga = _matmul(h, w_ga, tm, F32, "proj_ga")
            gb = _matmul(h, w_gb, tm, F32, "proj_gb")
            small = _matmul(h, ws, tm, F32, "proj_small")
            return q, u, ga, gb, small

        h = _rmsnorm(xp, norm_mix[l], tm_p, BF16)
        q, u, ga, gb, qiw = project(h, tm_p, w_qiw)
        k_t, v_t, ki_t, kb_t, vb_t, kib_t = _kv_transposed(h, w_kv_t, batch, seq, kv_w, d_idx,
                                                            min(512, seq))
        k_sel = min(TOPK_MAX, seq // 4)
        attn = _prompt_attention(q, qiw, kib_t, kb_t, vb_t, batch, seq, d_head, d_idx, k_sel,
                                 min(256, seq))
        y5, srp, sip = _s5_prompt(u, batch, seq, s5p, s5_d[l], groups, p_state)
        merged = _merge(attn, y5, ga, gb, w_attn_grouped, w_g, min(256, batch * seq))
        x1, h2 = _out_proj(merged, xp, w_o, norm_ffn[l], tm_p)
        xp, conv_p = _ffn_prompt(h2, x1, wu, cw, cb, wd, norm_final, batch, seq, tm_p, last)

        def heads_last(a_t):
            return jnp.transpose(a_t.reshape(batch, N_KV_HEADS, d_head, seq), (0, 3, 1, 2))

        outs["kp"].append(heads_last(k_t)); outs["vp"].append(heads_last(v_t))
        outs["kip"].append(jnp.swapaxes(ki_t, 1, 2))
        outs["srp"].append(srp); outs["sip"].append(sip); outs["cp"].append(conv_p)

        h = _rmsnorm(xs, norm_mix[l], tm_s, BF16)
        q, u, ga, gb, small = project(h, tm_s, w_small)
        attn = _sample_attention(q, small, cache_k[l], cache_v[l], cache_kidx[l], page_table,
                                 t_new, d_head, d_idx)
        y5, srs, sis = _s5_sample(u, nb, t_new, s5p, s5_d[l], state_s5_re[l], state_s5_im[l])
        merged = _merge(attn, y5, ga, gb, w_attn_grouped, w_g, min(256, nb * t_new))
        x1, h2 = _out_proj(merged, xs, w_o, norm_ffn[l], tm_s)

        def time_major(a):
            return a.reshape(nb, t_new, -1).swapaxes(0, 1).reshape(nb * t_new, -1)

        y_tm, conv_s = _ffn_sample(time_major(h2), time_major(x1), wu, cw, cb, wd, norm_final,
                                   state_ffn_conv[l].astype(F32), nb, t_new, last)
        xs = y_tm.reshape(t_new, nb, d_model).swapaxes(0, 1).reshape(nb * t_new, d_model)
        k_new = small[:, qi_w:qi_w + kv_w].reshape(nb, t_new, N_KV_HEADS, d_head)
        v_new = small[:, qi_w + kv_w:qi_w + 2 * kv_w].reshape(nb, t_new, N_KV_HEADS, d_head)
        ki_new = small[:, qi_w + 2 * kv_w:qi_w + 2 * kv_w + d_idx].reshape(nb, t_new, d_idx)
        outs["ks"].append(k_new); outs["vs"].append(v_new); outs["kis"].append(ki_new)
        outs["srs"].append(srs); outs["sis"].append(sis); outs["cs"].append(conv_s)

    stk = {name: jnp.stack(v) for name, v in outs.items()}
    return (xp.reshape(batch, seq, d_model), xs.reshape(nb, t_new, d_model),
            stk["kp"], stk["vp"], stk["kip"], stk["srp"], stk["sip"], stk["cp"],
            stk["ks"], stk["vs"], stk["kis"], stk["srs"], stk["sis"], stk["cs"])
```

```python
import functools
import math

import jax
import jax.numpy as jnp
from jax import lax
from jax.experimental import pallas as pl
from jax.experimental.pallas import tpu as pltpu

F32 = jnp.float32
BF16 = jnp.bfloat16

EPS = 1e-6
LOG2E = math.log2(math.e)
TOPK_MAX = 256
N_HEADS = 16
N_KV_HEADS = 4
KV_GROUP = N_HEADS // N_KV_HEADS
IDX_HEADS = 8
S5_GROUP = 16
STEP_CHUNK = 8
LANES = 128
SUBLANES = 8
GROUPS_PER_TILE = LANES // S5_GROUP
FF_TILE = 512
VMEM_LIMIT = 48 * 1024 * 1024
NEG_INF = float("-inf")
INT_MIN = -2 ** 31


def _params(*sem):
    return pltpu.CompilerParams(dimension_semantics=sem, vmem_limit_bytes=VMEM_LIMIT)


def _sigmoid(x):
    return 0.5 * jnp.tanh(0.5 * x) + 0.5


def _gelu_tanh(x):
    c = math.sqrt(2.0 / math.pi)
    return 0.5 * x * (1.0 + jnp.tanh(c * (x + 0.044715 * (x * x * x))))


def _dot_nt(a, b):
    return lax.dot_general(a, b, (((1,), (1,)), ((), ())), preferred_element_type=F32)


def _rms_kernel(x_ref, g_ref, o_ref):
    x = x_ref[...]
    ms = jnp.mean(x * x, axis=-1, keepdims=True)
    o_ref[...] = (x * lax.rsqrt(ms + EPS) * g_ref[...]).astype(o_ref.dtype)


def _rmsnorm(x, g, tm, out_dtype):
    m, d = x.shape
    return pl.pallas_call(
        _rms_kernel,
        grid=(m // tm,),
        in_specs=[pl.BlockSpec((tm, d), lambda i: (i, 0)),
                  pl.BlockSpec((1, d), lambda i: (0, 0))],
        out_specs=pl.BlockSpec((tm, d), lambda i: (i, 0)),
        out_shape=jax.ShapeDtypeStruct((m, d), out_dtype),
        compiler_params=_params("arbitrary"),
        name="rmsnorm",
    )(x, g.reshape(1, d))


def _mm_kernel(h_ref, w_ref, o_ref):
    o_ref[...] = _dot_nt(h_ref[...], w_ref[...]).astype(o_ref.dtype)


def _matmul(h, w_t, tm, out_dtype, name):
    m, k = h.shape
    n = w_t.shape[0]
    return pl.pallas_call(
        _mm_kernel,
        grid=(m // tm,),
        in_specs=[pl.BlockSpec((tm, k), lambda i: (i, 0)),
                  pl.BlockSpec((n, k), lambda i: (0, 0))],
        out_specs=pl.BlockSpec((tm, n), lambda i: (i, 0)),
        out_shape=jax.ShapeDtypeStruct((m, n), out_dtype),
        compiler_params=_params("arbitrary"),
        name=name,
    )(h, w_t)


def _kv_t_kernel(w_ref, h_ref, k_ref, v_ref, ki_ref, kb_ref, vb_ref, kib_ref, *, kv_w):
    out = _dot_nt(w_ref[...], h_ref[...])
    for lo, hi, full_ref, half_ref in ((0, kv_w, k_ref, kb_ref), (kv_w, 2 * kv_w, v_ref, vb_ref),
                                       (2 * kv_w, out.shape[0], ki_ref, kib_ref)):
        full_ref[...] = out[lo:hi]
        half_ref[...] = out[lo:hi].astype(BF16)


def _kv_transposed(h, w_t, batch, seq, kv_w, d_idx, tn):
    d = h.shape[1]
    nt = seq // tn
    out = lambda rows: pl.BlockSpec((None, rows, tn), lambda b, i: (b, 0, i))
    shape = lambda rows, dt: jax.ShapeDtypeStruct((batch, rows, seq), dt)
    widths = (kv_w, kv_w, d_idx)
    return pl.pallas_call(
        functools.partial(_kv_t_kernel, kv_w=kv_w),
        grid=(batch, nt),
        in_specs=[pl.BlockSpec(w_t.shape, lambda b, i: (0, 0)),
                  pl.BlockSpec((tn, d), lambda b, i: (b * nt + i, 0))],
        out_specs=[out(w) for w in widths] * 2,
        out_shape=[shape(w, F32) for w in widths] + [shape(w, BF16) for w in widths],
        compiler_params=_params("arbitrary", "arbitrary"),
        name="proj_kv_transposed",
    )(w_t, h)


def _count(mask):
    return jnp.sum(jnp.where(mask, 1.0, 0.0), axis=1, keepdims=True)


def _topk_mask(score, col, k_sel, n_cols):
    kf = float(k_sel)
    rows = score.shape[0]

    def as_float(key):
        bits = jnp.where(key < 0, key ^ jnp.int32(0x7FFFFFFF), key)
        return lax.bitcast_convert_type(bits, F32)

    cand0 = jnp.where(_count(score >= 0.0) >= kf, jnp.int32(0), jnp.int32(INT_MIN))
    cand0 = jnp.broadcast_to(cand0, (rows, 1)).astype(jnp.int32)

    def value_step(it, cand):
        trial = cand + lax.shift_left(jnp.int32(1), jnp.int32(30) - it)
        return jnp.where(_count(score >= as_float(trial)) >= kf, trial, cand)

    thr = as_float(lax.fori_loop(0, 31, value_step, cand0))

    above = score > thr
    ties = score == thr
    need = kf - _count(above)
    finite = score > NEG_INF
    few = _count(finite) <= kf
    surplus = jnp.where(few, 0.0, _count(score >= thr) - kf)
    n_bits = max(1, (n_cols - 1).bit_length())

    def index_step(it, m):
        trial = m + lax.shift_left(jnp.int32(1), jnp.int32(n_bits - 1) - it)
        taken = _count(jnp.logical_and(ties, col < trial))
        return jnp.where(taken <= need - 1.0, trial, m)

    def lowest_indices(_):
        return lax.fori_loop(0, n_bits, index_step, jnp.zeros((rows, 1), jnp.int32))

    def all_ties(_):
        return jnp.full((rows, 1), n_cols, jnp.int32)

    last = lax.cond(jnp.max(surplus) > 0.0, lowest_indices, all_ties, None)
    take_tie = jnp.logical_and(jnp.logical_and(ties, col <= last), need >= 1.0)
    top = jnp.logical_or(above, take_tie)
    return jnp.logical_and(finite, jnp.logical_or(few, top))


def _index_weights(kw, d_idx):
    w_scale = (d_idx ** -0.5) * (IDX_HEADS ** -0.5)
    return [kw[:, d_idx + h:d_idx + h + 1] * w_scale for h in range(IDX_HEADS)]


def _head_rows(qi, d_idx):
    return jnp.concatenate([qi[:, h * d_idx:(h + 1) * d_idx] for h in range(IDX_HEADS)], axis=0)


def _weighted_relu_sum(lg, w_cols, rows):
    score = jnp.zeros((rows, lg.shape[1]), F32)
    for h in range(IDX_HEADS):
        score = score + jnp.maximum(lg[h * rows:(h + 1) * rows], 0.0) * w_cols[h]
    return score


HEADS_PER_DOT = 2


def _prompt_attn_kernel(q_ref, qi_ref, kw_ref, kit_ref, kt_ref, vt_ref, prev_ref, o_ref,
                        *, tq, n_keys, k_sel, d_head, d_idx, q_block):
    del prev_ref
    qi = qi_ref[...]
    w_cols = _index_weights(kw_ref[...], d_idx)
    kit = kit_ref[...]
    score = jnp.zeros((tq, n_keys), F32)
    for h in range(IDX_HEADS):
        lg = jnp.dot(qi[:, h * d_idx:(h + 1) * d_idx].astype(BF16), kit,
                     preferred_element_type=F32)
        score = score + jnp.maximum(lg, 0.0) * w_cols[h]

    col = lax.broadcasted_iota(jnp.int32, (tq, n_keys), 1)
    qpos = q_block * tq + lax.broadcasted_iota(jnp.int32, (tq, n_keys), 0)
    score = jnp.where(col <= qpos, score, NEG_INF)
    bias = jnp.where(_topk_mask(score, col, k_sel, n_keys), 0.0, NEG_INF)

    q = q_ref[...] * (d_head ** -0.5 * LOG2E)
    for n in range(N_KV_HEADS):
        kt = kt_ref[n * d_head:(n + 1) * d_head, :]
        vt = vt_ref[n * d_head:(n + 1) * d_head, :]
        for g0 in range(0, KV_GROUP, HEADS_PER_DOT):
            heads = [(g0 + i) * N_KV_HEADS + n for i in range(HEADS_PER_DOT)]
            qs = jnp.concatenate([q[:, h * d_head:(h + 1) * d_head] for h in heads], axis=0)
            s = jnp.dot(qs.astype(BF16), kt, preferred_element_type=F32)
            s = (s.reshape(HEADS_PER_DOT, tq, n_keys) + bias[None]).reshape(-1, n_keys)
            m = jnp.max(s, axis=1, keepdims=True)
            p = jnp.exp2(s - m)
            l = jnp.sum(p, axis=1, keepdims=True)
            o = _dot_nt(p.astype(BF16), vt) / l
            for i, h in enumerate(heads):
                o_ref[:, h * d_head:(h + 1) * d_head] = o[i * tq:(i + 1) * tq].astype(o_ref.dtype)


def _prompt_attention(q, qiw, ki_t, k_t, v_t, batch, seq, d_head, d_idx, k_sel, tq):
    attn_w = q.shape[1]
    kv_w = N_KV_HEADS * d_head
    qi_w = IDX_HEADS * d_idx
    nq = seq // tq
    out = jnp.zeros((batch * seq, attn_w), BF16)
    for qb in range(nq):
        n_keys = (qb + 1) * tq
        row = lambda w, c, qb=qb: pl.BlockSpec((tq, w), lambda b: (b * nq + qb, c))
        keys = lambda w, n_keys=n_keys: pl.BlockSpec((None, w, n_keys), lambda b: (b, 0, 0))
        out = pl.pallas_call(
            functools.partial(_prompt_attn_kernel, tq=tq, n_keys=n_keys, k_sel=k_sel,
                              d_head=d_head, d_idx=d_idx, q_block=qb),
            grid=(batch,),
            in_specs=[row(attn_w, 0), row(qi_w, 0), row(LANES, qi_w // LANES),
                      keys(d_idx), keys(kv_w), keys(kv_w),
                      pl.BlockSpec(memory_space=pl.ANY)],
            out_specs=row(attn_w, 0),
            out_shape=jax.ShapeDtypeStruct((batch * seq, attn_w), BF16),
            input_output_aliases={6: 0},
            compiler_params=_params("arbitrary"),
            name=f"prompt_attention_q{qb}",
        )(q, qiw, qiw, ki_t, k_t, v_t, out)
    return out


def _lane_columns(cols, width):
    rows = cols[0].shape[0]
    lane = lax.broadcasted_iota(jnp.int32, (rows, width), 1)
    out = jnp.zeros((rows, width), F32)
    for j, c in enumerate(cols):
        out = jnp.where(lane == j, c, out)
    return out


def _sample_score_kernel(pt_ref, qi_ref, kw_ref, kin_ref, *rest, n_pages, page, t_new, d_idx):
    del pt_ref
    ipages, s_ref = rest[:n_pages], rest[n_pages]
    past = n_pages * page
    rows = SUBLANES
    qi_rows = _head_rows(qi_ref[...], d_idx)
    qi_bf = qi_rows.astype(BF16)
    w_cols = _index_weights(kw_ref[...], d_idx)
    pad_row = lax.broadcasted_iota(jnp.int32, (rows, page), 0) >= t_new
    first = lax.broadcasted_iota(jnp.int32, (rows, page), 1) == 0
    for p in range(n_pages):
        lg = jnp.dot(qi_bf, ipages[p][...].astype(BF16), preferred_element_type=F32)
        pad_val = jnp.where(first, 0.0, NEG_INF) if p == 0 else NEG_INF
        s_ref[:, p * page:(p + 1) * page] = jnp.where(pad_row, pad_val,
                                                      _weighted_relu_sum(lg, w_cols, rows))
    kin = kin_ref[...]
    new_cols = []
    for j in range(t_new):
        lg = jnp.sum(qi_rows * kin[j:j + 1, :], axis=1, keepdims=True)
        new_cols.append(_weighted_relu_sum(lg, w_cols, rows))
    lane = lax.broadcasted_iota(jnp.int32, (rows, LANES), 1)
    t = lax.broadcasted_iota(jnp.int32, (rows, LANES), 0)
    causal_new = jnp.logical_and(lane <= t, t < t_new)
    s_ref[:, past:] = jnp.where(causal_new, _lane_columns(new_cols, LANES), NEG_INF)


def _select_kernel(s_ref, b_ref, *, k_sel):
    score = s_ref[...]
    col = lax.broadcasted_iota(jnp.int32, score.shape, 1)
    keep = _topk_mask(score, col, k_sel, score.shape[1])
    b_ref[...] = jnp.where(keep, 0.0, NEG_INF)


SAMPLE_SEQS_PER_STEP = 2


def _sample_attn_kernel(pt_ref, q_ref, kn_ref, vn_ref, bias_ref, *rest,
                        n_seqs, n_pages, page, t_new, d_head):
    del pt_ref
    pages = rest[:2 * n_seqs * n_pages]
    o_ref, kt_ref, vt_ref = rest[2 * n_seqs * n_pages:]
    for i in range(n_seqs):
        kpages = pages[i * n_pages:(i + 1) * n_pages]
        vpages = pages[(n_seqs + i) * n_pages:(n_seqs + i + 1) * n_pages]
        o_ref[i] = _sample_attend(q_ref[i], kn_ref[i], vn_ref[i], bias_ref[i], kpages, vpages,
                                  kt_ref.at[i], vt_ref.at[i], page, t_new, d_head
                                  ).astype(o_ref.dtype)


def _sample_attend(q, kn, vn, bias, kpages, vpages, kt_ref, vt_ref, page, t_new, d_head):
    n_pages = len(kpages)
    past = n_pages * page
    rows = SUBLANES
    kv_w = N_KV_HEADS * d_head

    for p in range(n_pages):
        kt_ref[:, p * page:(p + 1) * page] = kpages[p][...].astype(BF16)
        vt_ref[:, p * page:(p + 1) * page] = vpages[p][...].astype(BF16)

    bias = jnp.concatenate([bias] * N_HEADS, axis=0)

    q = q * (d_head ** -0.5 * LOG2E)
    lane_head = lax.broadcasted_iota(jnp.int32, (rows, kv_w), 1) // d_head
    q_rows = jnp.concatenate(
        [jnp.where(lane_head == n, q[:, g * kv_w:(g + 1) * kv_w], 0.0)
         for g in range(KV_GROUP) for n in range(N_KV_HEADS)], axis=0)

    s_past = jnp.dot(q_rows.astype(BF16), kt_ref[...], preferred_element_type=F32)
    s_past = s_past + bias[:, :past]
    s_new = _lane_columns(
        [jnp.sum(q_rows * kn[j:j + 1, :], axis=1, keepdims=True) for j in range(t_new)], LANES)
    s_new = s_new + bias[:, past:]

    m = jnp.maximum(jnp.max(s_past, axis=1, keepdims=True), jnp.max(s_new, axis=1, keepdims=True))
    p_past = jnp.exp2(s_past - m)
    p_new = jnp.exp2(s_new - m)
    l = jnp.sum(p_past, axis=1, keepdims=True) + jnp.sum(p_new, axis=1, keepdims=True)
    o = _dot_nt(p_past.astype(BF16), vt_ref[...])
    for j in range(t_new):
        o = o + p_new[:, j:j + 1] * vn[j:j + 1, :]
    o = o / l

    chunks = []
    for g in range(KV_GROUP):
        acc = jnp.zeros((rows, kv_w), F32)
        for n in range(N_KV_HEADS):
            r0 = (g * N_KV_HEADS + n) * rows
            acc = acc + jnp.where(lane_head == n, o[r0:r0 + rows], 0.0)
        chunks.append(acc)
    return jnp.concatenate(chunks, axis=1)


def _page_map(b, pt, *, p, n_pages):
    return (pt[b * n_pages + p], 0, 0)


def _seq_page_map(b, pt, *, i, p, n_seqs, n_pages):
    return (pt[(b * n_seqs + i) * n_pages + p], 0, 0)


def _sample_attention(q, small, cache_k, cache_v, cache_ki, page_table, t_new, d_head, d_idx):
    nb, n_pages = page_table.shape
    n_pool, page = cache_k.shape[0], cache_k.shape[1]
    attn_w = q.shape[1]
    kv_w = N_KV_HEADS * d_head
    qi_w = IDX_HEADS * d_idx
    rows = SUBLANES
    past = n_pages * page
    k_sel = min(TOPK_MAX, (past + t_new) // 4)

    def pad_rows(a):
        a = a.reshape(nb, t_new, a.shape[-1])
        return jnp.pad(a, ((0, 0), (0, rows - t_new), (0, 0)))

    q8 = pad_rows(q)
    qi8 = pad_rows(small[:, :qi_w])
    kw8 = pad_rows(small[:, qi_w + 2 * kv_w:])
    kn = pad_rows(small[:, qi_w:qi_w + kv_w])
    vn = pad_rows(small[:, qi_w + kv_w:qi_w + 2 * kv_w])
    kin = pad_rows(small[:, qi_w + 2 * kv_w:qi_w + 2 * kv_w + d_idx])

    ck = jnp.transpose(cache_k, (0, 2, 3, 1)).reshape(n_pool, kv_w, page)
    cv = jnp.transpose(cache_v, (0, 2, 3, 1)).reshape(n_pool, kv_w, page)
    ci = jnp.transpose(cache_ki, (0, 2, 1))

    def row_spec(w):
        return pl.BlockSpec((None, rows, w), lambda b, pt: (b, 0, 0))

    def page_specs(w):
        return [pl.BlockSpec((None, w, page), functools.partial(_page_map, p=p, n_pages=n_pages))
                for p in range(n_pages)]

    pt_flat = page_table.reshape(-1)
    n_keys = past + LANES

    score = pl.pallas_call(
        functools.partial(_sample_score_kernel, n_pages=n_pages, page=page, t_new=t_new,
                          d_idx=d_idx),
        grid_spec=pltpu.PrefetchScalarGridSpec(
            num_scalar_prefetch=1,
            grid=(nb,),
            in_specs=[row_spec(qi_w), row_spec(LANES), row_spec(d_idx)] + page_specs(d_idx),
            out_specs=pl.BlockSpec((None, rows, n_keys), lambda b, pt: (b, 0, 0)),
        ),
        out_shape=jax.ShapeDtypeStruct((nb, rows, n_keys), F32),
        compiler_params=_params("arbitrary"),
        name="sample_scores",
    )(pt_flat, qi8, kw8, kin, *([ci] * n_pages))

    sel_rows = math.gcd(nb * rows, 256)
    bias = pl.pallas_call(
        functools.partial(_select_kernel, k_sel=k_sel),
        grid=(nb * rows // sel_rows,),
        in_specs=[pl.BlockSpec((sel_rows, n_keys), lambda i: (i, 0))],
        out_specs=pl.BlockSpec((sel_rows, n_keys), lambda i: (i, 0)),
        out_shape=jax.ShapeDtypeStruct((nb * rows, n_keys), F32),
        compiler_params=_params("arbitrary"),
        name="sample_select",
    )(score.reshape(nb * rows, n_keys)).reshape(nb, rows, n_keys)

    n_seqs = math.gcd(nb, SAMPLE_SEQS_PER_STEP)

    def seqs_spec(w):
        return pl.BlockSpec((n_seqs, rows, w), lambda b, pt: (b, 0, 0))

    def seq_page_specs(w):
        return [pl.BlockSpec((None, w, page),
                             functools.partial(_seq_page_map, i=i, p=p, n_seqs=n_seqs,
                                               n_pages=n_pages))
                for i in range(n_seqs) for p in range(n_pages)]

    out = pl.pallas_call(
        functools.partial(_sample_attn_kernel, n_seqs=n_seqs, n_pages=n_pages, page=page,
                          t_new=t_new, d_head=d_head),
        grid_spec=pltpu.PrefetchScalarGridSpec(
            num_scalar_prefetch=1,
            grid=(nb // n_seqs,),
            in_specs=[seqs_spec(attn_w), seqs_spec(kv_w), seqs_spec(kv_w), seqs_spec(n_keys)]
                     + seq_page_specs(kv_w) + seq_page_specs(kv_w),
            out_specs=seqs_spec(attn_w),
            scratch_shapes=[pltpu.VMEM((n_seqs, kv_w, past), BF16),
                            pltpu.VMEM((n_seqs, kv_w, past), BF16)],
        ),
        out_shape=jax.ShapeDtypeStruct((nb, rows, attn_w), BF16),
        compiler_params=_params("arbitrary"),
        name="sample_attention",
    )(pt_flat, q8, kn, vn, bias, *([ck] * (n_seqs * n_pages)), *([cv] * (n_seqs * n_pages)))
    return out[:, :t_new].reshape(nb * t_new, attn_w)


def _group_major_heads(a, d_head):
    n = a.shape[1]
    a = a.reshape(N_KV_HEADS, KV_GROUP, d_head, n)
    return jnp.swapaxes(a, 0, 1).reshape(N_HEADS * d_head, n)


def _s5_step(xr, xi, ar, ai, bu, half):
    br, bi = bu[:, :half], bu[:, half:]
    if xr is None:
        return br, bi
    return ar * xr - ai * xi + br, ar * xi + ai * xr + bi


def _s5_inputs(u_ref, b_ref, j):
    tr, steps, _ = u_ref.shape
    x = jnp.transpose(u_ref[:, :, j * LANES:(j + 1) * LANES], (1, 0, 2))
    us = [x[s] for s in range(steps)]
    bu = jnp.dot(jnp.concatenate(us, axis=0).astype(BF16), b_ref[j], preferred_element_type=F32)
    return us, [bu[s * tr:(s + 1) * tr] for s in range(steps)]


def _s5_local_kernel(u_ref, b_ref, ar_ref, ai_ref, z_ref, *, n_tiles):
    sw = b_ref.shape[2]
    half = sw // 2
    for j in range(n_tiles):
        ar, ai = ar_ref[j:j + 1, :], ai_ref[j:j + 1, :]
        xr = xi = None
        for bu in _s5_inputs(u_ref, b_ref, j)[1]:
            xr, xi = _s5_step(xr, xi, ar, ai, bu, half)
        z_ref[:, j * sw:j * sw + half] = xr
        z_ref[:, j * sw + half:(j + 1) * sw] = xi


def _s5_carry_kernel(z_ref, ar_ref, ai_ref, xc_ref, xf_ref, *, batch, n_chunks):
    half = z_ref.shape[1] // 2
    ar, ai = ar_ref[...], ai_ref[...]

    def body(c, carry):
        new = []
        for b in range(batch):
            xr, xi = carry[b]
            row = b * n_chunks + c
            xc_ref[pl.ds(row, 1), :] = jnp.concatenate([xr, xi], axis=1)
            z = z_ref[pl.ds(row, 1), :]
            new.append((ar * xr - ai * xi + z[:, :half], ar * xi + ai * xr + z[:, half:]))
        return tuple(new)

    zero = jnp.zeros((1, half), F32)
    final = lax.fori_loop(0, n_chunks, body, tuple((zero, zero) for _ in range(batch)),
                          unroll=math.gcd(n_chunks, 4))
    for b in range(batch):
        xf_ref[b:b + 1, :] = jnp.concatenate(list(final[b]), axis=1)


def _s5_out_kernel(u_ref, x0_ref, b_ref, c_ref, ar_ref, ai_ref, d_ref, y_ref, *maybe_xf,
                   n_tiles):
    tr, steps, _ = u_ref.shape
    sw = b_ref.shape[2]
    half = sw // 2
    for j in range(n_tiles):
        ar, ai = ar_ref[j:j + 1, :], ai_ref[j:j + 1, :]
        xr = x0_ref[:, j * sw:j * sw + half]
        xi = x0_ref[:, j * sw + half:(j + 1) * sw]
        d = d_ref[:, j * LANES:(j + 1) * LANES]
        us, bus = _s5_inputs(u_ref, b_ref, j)
        xs = []
        for bu in bus:
            xr, xi = _s5_step(xr, xi, ar, ai, bu, half)
            xs.append(jnp.concatenate([xr, xi], axis=1).astype(BF16))
        y = jnp.dot(jnp.concatenate(xs, axis=0), c_ref[j], preferred_element_type=F32)
        ys = jnp.stack([y[s * tr:(s + 1) * tr] + d * us[s] for s in range(steps)], axis=0)
        y_ref[:, :, j * LANES:(j + 1) * LANES] = jnp.transpose(ys, (1, 0, 2))
        if maybe_xf:
            maybe_xf[0][:, j * sw:j * sw + half] = xr
            maybe_xf[0][:, j * sw + half:(j + 1) * sw] = xi


def _s5_params(a_re, a_im, log_step, b_re, b_im, c_re, c_im, chunk_steps):
    g, p = a_re.shape
    nt = g // GROUPS_PER_TILE
    a_re, a_im = a_re.astype(F32), a_im.astype(F32)
    step = jnp.exp(log_step.astype(F32))[:, None]

    def discretise(n_steps):
        mag = jnp.exp(a_re * step * n_steps)
        return mag * jnp.cos(a_im * step * n_steps), mag * jnp.sin(a_im * step * n_steps)

    lr, li = discretise(1.0)
    cr, ci = discretise(float(chunk_steps))
    den = a_re * a_re + a_im * a_im
    fr = ((lr - 1.0) * a_re + li * a_im) / den
    fi = (li * a_re - (lr - 1.0) * a_im) / den
    b_re, b_im = b_re.astype(F32), b_im.astype(F32)
    bb_re = fr[..., None] * b_re - fi[..., None] * b_im
    bb_im = fr[..., None] * b_im + fi[..., None] * b_re
    eye = jnp.eye(GROUPS_PER_TILE, dtype=F32)

    bb = jnp.stack([bb_re, bb_im], axis=0).reshape(2, nt, GROUPS_PER_TILE, p, S5_GROUP)
    bb = jnp.transpose(bb, (1, 2, 4, 0, 3))
    b_blk = bb[:, :, :, :, None, :] * eye[None, :, None, None, :, None]
    b_blk = b_blk.reshape(nt, LANES, 2 * GROUPS_PER_TILE * p).astype(BF16)

    cc = jnp.stack([c_re.astype(F32), -c_im.astype(F32)], axis=0)
    cc = cc.reshape(2, nt, GROUPS_PER_TILE, S5_GROUP, p)
    cc = jnp.transpose(cc, (1, 0, 2, 4, 3))
    c_blk = cc[:, :, :, :, None, :] * eye[None, None, :, None, :, None]
    c_blk = c_blk.reshape(nt, 2 * GROUPS_PER_TILE * p, LANES).astype(BF16)

    def tiles(z):
        return z.reshape(nt, GROUPS_PER_TILE * p)

    return b_blk, c_blk, tiles(lr), tiles(li), tiles(cr), tiles(ci)


def _state_to_tiles(s_re, s_im):
    nb, g, p = s_re.shape
    nt = g // GROUPS_PER_TILE
    st = jnp.stack([s_re.reshape(nb, nt, GROUPS_PER_TILE * p),
                    s_im.reshape(nb, nt, GROUPS_PER_TILE * p)], axis=2)
    return st.reshape(nb, nt * 2 * GROUPS_PER_TILE * p)


def _tiles_to_state(x, g, p):
    nb = x.shape[0]
    nt = g // GROUPS_PER_TILE
    st = x.reshape(nb, nt, 2, GROUPS_PER_TILE, p)
    return st[:, :, 0].reshape(nb, g, p), st[:, :, 1].reshape(nb, g, p)


def _s5_outputs(u3, x0, prm, d_skip, tr, want_final):
    b_blk, c_blk, ar, ai = prm
    r, steps, width = u3.shape
    nt, _, sw = b_blk.shape
    kern = functools.partial(_s5_out_kernel, n_tiles=nt)
    full3 = lambda a: pl.BlockSpec(a.shape, lambda i: (0, 0, 0))
    full2 = lambda a: pl.BlockSpec(a.shape, lambda i: (0, 0))
    d2 = d_skip.reshape(1, width).astype(F32)
    tokens = pl.BlockSpec((tr, steps, width), lambda i: (i, 0, 0))
    out_shape = [jax.ShapeDtypeStruct(u3.shape, F32)]
    out_specs = [tokens]
    if want_final:
        out_shape.append(jax.ShapeDtypeStruct((r, nt * sw), F32))
        out_specs.append(pl.BlockSpec((tr, nt * sw), lambda i: (i, 0)))
    return pl.pallas_call(
        kern,
        grid=(r // tr,),
        in_specs=[tokens, pl.BlockSpec((tr, nt * sw), lambda i: (i, 0)),
                  full3(b_blk), full3(c_blk), full2(ar), full2(ai), full2(d2)],
        out_specs=out_specs,
        out_shape=out_shape,
        compiler_params=_params("arbitrary"),
        name="s5_outputs",
    )(u3, x0, b_blk, c_blk, ar, ai, d2)


def _s5_prompt(u, batch, seq, prm_all, d_skip, g, p):
    b_blk, c_blk, ar, ai, ar_c, ai_c = prm_all
    steps = STEP_CHUNK
    n_chunks = seq // steps
    r = batch * n_chunks
    width = u.shape[1]
    u3 = u.reshape(r, steps, width)
    nt, _, sw = b_blk.shape
    tr = 128
    z = pl.pallas_call(
        functools.partial(_s5_local_kernel, n_tiles=nt),
        grid=(r // tr,),
        in_specs=[pl.BlockSpec((tr, steps, width), lambda i: (i, 0, 0)),
                  pl.BlockSpec(b_blk.shape, lambda i: (0, 0, 0)),
                  pl.BlockSpec(ar.shape, lambda i: (0, 0)),
                  pl.BlockSpec(ai.shape, lambda i: (0, 0))],
        out_specs=pl.BlockSpec((tr, nt * sw), lambda i: (i, 0)),
        out_shape=jax.ShapeDtypeStruct((r, nt * sw), F32),
        compiler_params=_params("arbitrary"),
        name="s5_chunk_states",
    )(u3, b_blk, ar, ai)
    half = sw // 2
    xc, xf = pl.pallas_call(
        functools.partial(_s5_carry_kernel, batch=batch, n_chunks=n_chunks),
        grid=(nt,),
        in_specs=[pl.BlockSpec((r, sw), lambda j: (0, j)),
                  pl.BlockSpec((None, 1, half), lambda j: (j, 0, 0)),
                  pl.BlockSpec((None, 1, half), lambda j: (j, 0, 0))],
        out_specs=[pl.BlockSpec((r, sw), lambda j: (0, j)),
                   pl.BlockSpec((batch, sw), lambda j: (0, j))],
        out_shape=[jax.ShapeDtypeStruct((r, nt * sw), F32),
                   jax.ShapeDtypeStruct((batch, nt * sw), F32)],
        compiler_params=_params("arbitrary"),
        name="s5_carry",
    )(z, ar_c.reshape(nt, 1, half), ai_c.reshape(nt, 1, half))
    (y3,) = _s5_outputs(u3, xc, (b_blk, c_blk, ar, ai), d_skip, tr, False)
    s_re, s_im = _tiles_to_state(xf, g, p)
    return y3.reshape(batch * seq, width), s_re, s_im


def _s5_sample(u, nb, t_new, prm_all, d_skip, s_re, s_im):
    b_blk, c_blk, ar, ai, _, _ = prm_all
    g, p = s_re.shape[1], s_re.shape[2]
    width = u.shape[1]
    x0 = _state_to_tiles(s_re.astype(F32), s_im.astype(F32))
    y3, xf = _s5_outputs(u.reshape(nb, t_new, width), x0, (b_blk, c_blk, ar, ai), d_skip, nb, True)
    n_re, n_im = _tiles_to_state(xf, g, p)
    return y3.reshape(nb * t_new, width), n_re, n_im


def _merge_kernel(attn_ref, y_ref, ga_ref, gb_ref, wa_ref, wg_ref, o_ref):
    d = o_ref.shape[1]
    y_a = jnp.dot(attn_ref[...], wa_ref[...], preferred_element_type=F32)
    z = _gelu_tanh(y_ref[...]).astype(BF16)
    glu = jnp.dot(z, wg_ref[...], preferred_element_type=F32)
    y_b = glu[:, :d] * _sigmoid(glu[:, d:])
    o_ref[...] = (_sigmoid(ga_ref[...]) * y_a + _sigmoid(gb_ref[...]) * y_b).astype(o_ref.dtype)


def _merge(attn, y, ga, gb, w_attn, w_glu, tm):
    m, d = ga.shape
    row = lambda w: pl.BlockSpec((tm, w), lambda i: (i, 0))
    return pl.pallas_call(
        _merge_kernel,
        grid=(m // tm,),
        in_specs=[row(attn.shape[1]), row(y.shape[1]), row(d), row(d),
                  pl.BlockSpec(w_attn.shape, lambda i: (0, 0)),
                  pl.BlockSpec(w_glu.shape, lambda i: (0, 0))],
        out_specs=row(d),
        out_shape=jax.ShapeDtypeStruct((m, d), BF16),
        compiler_params=_params("arbitrary"),
        name="merge",
    )(attn, y, ga, gb, w_attn, w_glu)


def _out_proj_kernel(m_ref, x_ref, w_ref, g_ref, x1_ref, h_ref):
    x1 = x_ref[...] + jnp.dot(m_ref[...], w_ref[...], preferred_element_type=F32)
    x1_ref[...] = x1
    ms = jnp.mean(x1 * x1, axis=-1, keepdims=True)
    h_ref[...] = (x1 * lax.rsqrt(ms + EPS) * g_ref[...]).astype(h_ref.dtype)


def _out_proj(merged, x, w_out, g, tm):
    m, d = x.shape
    row = pl.BlockSpec((tm, d), lambda i: (i, 0))
    return pl.pallas_call(
        _out_proj_kernel,
        grid=(m // tm,),
        in_specs=[row, row, pl.BlockSpec(w_out.shape, lambda i: (0, 0)),
                  pl.BlockSpec((1, d), lambda i: (0, 0))],
        out_specs=[row, row],
        out_shape=[jax.ShapeDtypeStruct((m, d), F32), jax.ShapeDtypeStruct((m, d), BF16)],
        compiler_params=_params("arbitrary"),
        name="out_proj",
    )(merged, x, w_out, g.reshape(1, d))


FF_SUB = 2


def _shift_rows(up, prev, k):
    body = pltpu.roll(up, k, axis=0)
    row = lax.broadcasted_iota(jnp.int32, prev.shape, 0)
    head = jnp.where(row < k, pltpu.roll(prev, k, axis=0), body[:SUBLANES])
    return jnp.concatenate([head, body[SUBLANES:]], axis=0)


def _conv_taps(up, u1, u2, cw, cb):
    return cb + cw[0:1, :] * u2 + cw[1:2, :] * u1 + cw[2:3, :] * up


def _ffn_finish(j, n_j, final_norm, acts, wd_ref, x1_ref, g_ref, y_ref):
    y_ref[...] += jnp.dot(jnp.concatenate(acts, axis=1), wd_ref[...], preferred_element_type=F32)

    @pl.when(j == n_j - 1)
    def _():
        x2 = x1_ref[...] + y_ref[...]
        if final_norm:
            ms = jnp.mean(x2 * x2, axis=-1, keepdims=True)
            x2 = x2 * lax.rsqrt(ms + EPS) * g_ref[...]
        y_ref[...] = x2


def _ffn_prompt_kernel(h_ref, wg_ref, wv_ref, cwg_ref, cwv_ref, cbg_ref, cbv_ref, wd_ref, x1_ref,
                       g_ref, y_ref, convg_ref, convv_ref, carry_ref,
                       *, tm, tiles_per_seq, n_j, final_norm):
    i, j = pl.program_id(0), pl.program_id(1)

    @pl.when(i % tiles_per_seq == 0)
    def _():
        carry_ref[j] = jnp.zeros(carry_ref.shape[1:], F32)

    @pl.when(j == 0)
    def _():
        y_ref[...] = jnp.zeros(y_ref.shape, F32)

    h = h_ref[...]
    prev = carry_ref[j]
    sub = FF_TILE // FF_SUB
    acts, tails = [], ([], [])
    for s in range(FF_SUB):
        cs = slice(s * sub, (s + 1) * sub)
        mixed = []
        for half, (w_ref, cw_ref, cb_ref) in enumerate(((wg_ref, cwg_ref, cbg_ref),
                                                        (wv_ref, cwv_ref, cbv_ref))):
            up = jnp.dot(h, w_ref[:, cs], preferred_element_type=F32)
            p8 = prev[:, half * FF_TILE + s * sub:half * FF_TILE + (s + 1) * sub]
            mixed.append(_conv_taps(up, _shift_rows(up, p8, 1), _shift_rows(up, p8, 2),
                                    cw_ref[:, cs], cb_ref[:, cs]))
            tails[half].append(up[tm - SUBLANES:])
        gate, val = mixed
        acts.append((gate * _sigmoid(gate) * val).astype(BF16))
    tail_g = jnp.concatenate(tails[0], axis=1)
    tail_v = jnp.concatenate(tails[1], axis=1)
    carry_ref[j] = jnp.concatenate([tail_g, tail_v], axis=1)
    convg_ref[...] = tail_g
    convv_ref[...] = tail_v
    _ffn_finish(j, n_j, final_norm, acts, wd_ref, x1_ref, g_ref, y_ref)


def _ffn_sample_kernel(h_ref, wg_ref, wv_ref, cwg_ref, cwv_ref, cbg_ref, cbv_ref, wd_ref, x1_ref,
                       g_ref, s0g_ref, s0v_ref, s1g_ref, s1v_ref, y_ref, convg_ref, convv_ref,
                       *, nb, t_new, n_j, final_norm):
    j = pl.program_id(0)

    @pl.when(j == 0)
    def _():
        y_ref[...] = jnp.zeros(y_ref.shape, F32)

    h = h_ref[...]
    sub = FF_TILE // FF_SUB
    acts = []
    for s in range(FF_SUB):
        cs = slice(s * sub, (s + 1) * sub)
        mixed = []
        for w_ref, cw_ref, cb_ref, s0_ref, s1_ref, conv_ref in (
                (wg_ref, cwg_ref, cbg_ref, s0g_ref, s1g_ref, convg_ref),
                (wv_ref, cwv_ref, cbv_ref, s0v_ref, s1v_ref, convv_ref)):
            up = jnp.dot(h, w_ref[:, cs], preferred_element_type=F32)
            s0, s1 = s0_ref[:, cs], s1_ref[:, cs]
            u1 = jnp.concatenate([s1, up[:(t_new - 1) * nb]], axis=0)
            u2 = jnp.concatenate([s0, s1, up[:(t_new - 2) * nb]], axis=0)
            mixed.append(_conv_taps(up, u1, u2, cw_ref[:, cs], cb_ref[:, cs]))
            conv_ref[0, :, cs] = up[(t_new - 2) * nb:(t_new - 1) * nb]
            conv_ref[1, :, cs] = up[(t_new - 1) * nb:]
        gate, val = mixed
        acts.append((gate * _sigmoid(gate) * val).astype(BF16))
    _ffn_finish(j, n_j, final_norm, acts, wd_ref, x1_ref, g_ref, y_ref)


def _ffn_weight_specs(d, n_j, ix):
    gate = lambda rows: pl.BlockSpec((rows, FF_TILE), ix(lambda j: (0, j)))
    val = lambda rows: pl.BlockSpec((rows, FF_TILE), ix(lambda j: (0, n_j + j)))
    return [gate(d), val(d), gate(3), val(3), gate(1), val(1),
            pl.BlockSpec((FF_TILE, d), ix(lambda j: (j, 0)))]


def _ffn_prompt(h2, x1, wu, cw, cb, wd, g, batch, seq, tm, final_norm):
    m, d = x1.shape
    d_ff = wd.shape[0]
    n_j = d_ff // FF_TILE
    tps = seq // tm
    kern = functools.partial(_ffn_prompt_kernel, tm=tm, tiles_per_seq=tps, n_j=n_j,
                             final_norm=final_norm)
    ix = lambda f: (lambda i, j: f(j))
    conv_spec = pl.BlockSpec((None, SUBLANES, FF_TILE), lambda i, j: (i, 0, j))
    conv_shape = jax.ShapeDtypeStruct((m // tm, SUBLANES, d_ff), F32)
    y, cg, cv = pl.pallas_call(
        kern,
        grid=(m // tm, n_j),
        in_specs=[pl.BlockSpec((tm, d), lambda i, j: (i, 0))] + _ffn_weight_specs(d, n_j, ix)
                 + [pl.BlockSpec((tm, d), lambda i, j: (i, 0)),
                    pl.BlockSpec((1, d), lambda i, j: (0, 0))],
        out_specs=[pl.BlockSpec((tm, d), lambda i, j: (i, 0)), conv_spec, conv_spec],
        out_shape=[jax.ShapeDtypeStruct((m, d), F32), conv_shape, conv_shape],
        scratch_shapes=[pltpu.VMEM((n_j, SUBLANES, 2 * FF_TILE), F32)],
        compiler_params=_params("arbitrary", "arbitrary"),
        name="ffn_prompt",
    )(h2, wu, wu, cw, cw, cb, cb, wd, x1, g.reshape(1, d))
    return y, jnp.concatenate([cg, cv], axis=-1)[tps - 1::tps, SUBLANES - 2:]


def _ffn_sample(h2, x1, wu, cw, cb, wd, g, state, nb, t_new, final_norm):
    m, d = x1.shape
    d_ff = wd.shape[0]
    n_j = d_ff // FF_TILE
    kern = functools.partial(_ffn_sample_kernel, nb=nb, t_new=t_new, n_j=n_j,
                             final_norm=final_norm)
    ix = lambda f: f
    s0, s1 = state[:, 0, :], state[:, 1, :]
    st_gate = pl.BlockSpec((nb, FF_TILE), lambda j: (0, j))
    st_val = pl.BlockSpec((nb, FF_TILE), lambda j: (0, n_j + j))
    conv_spec = pl.BlockSpec((2, nb, FF_TILE), lambda j: (0, 0, j))
    y, cg, cv = pl.pallas_call(
        kern,
        grid=(n_j,),
        in_specs=[pl.BlockSpec((m, d), lambda j: (0, 0))] + _ffn_weight_specs(d, n_j, ix)
                 + [pl.BlockSpec((m, d), lambda j: (0, 0)),
                    pl.BlockSpec((1, d), lambda j: (0, 0)),
                    st_gate, st_val, st_gate, st_val],
        out_specs=[pl.BlockSpec((m, d), lambda j: (0, 0)), conv_spec, conv_spec],
        out_shape=[jax.ShapeDtypeStruct((m, d), F32),
                   jax.ShapeDtypeStruct((2, nb, d_ff), F32),
                   jax.ShapeDtypeStruct((2, nb, d_ff), F32)],
        compiler_params=_params("arbitrary"),
        name="ffn_sample",
    )(h2, wu, wu, cw, cw, cb, cb, wd, x1, g.reshape(1, d), s0, s0, s1, s1)
    return y, jnp.swapaxes(jnp.concatenate([cg, cv], axis=-1), 0, 1)


def kernel(x_prompt, x_sample, cache_k, cache_v, cache_kidx, state_s5_re, state_s5_im, state_ffn_conv, page_table, norm_mix, w_in, w_attn_proj, s5_a_re, s5_a_im, s5_log_step, s5_b_re, s5_b_im, s5_c_re, s5_c_im, s5_d, w_glu, w_out, norm_ffn, w_up, conv_w, conv_b, w_down, norm_final):
    depth = w_in.shape[0]
    batch, seq, d_model = x_prompt.shape
    nb, t_new, _ = x_sample.shape
    d_head = cache_k.shape[-1]
    d_idx = cache_kidx.shape[-1]
    attn_w = N_HEADS * d_head
    kv_w = N_KV_HEADS * d_head
    qi_w = IDX_HEADS * d_idx
    groups, p_state = s5_a_re.shape[1], s5_a_re.shape[2]
    s5_w = groups * S5_GROUP

    xp = x_prompt.reshape(batch * seq, d_model)
    xs = x_sample.reshape(nb * t_new, d_model)
    tm_p = min(512, batch * seq)
    tm_s = nb * t_new

    outs = {name: [] for name in ("kp", "vp", "kip", "srp", "sip", "cp",
                                  "ks", "vs", "kis", "srs", "sis", "cs")}
    for l in range(depth):
        w_t = jnp.swapaxes(w_in[l], 0, 1).astype(BF16)
        o = 0
        seg = {}
        for name, width in (("q", attn_w), ("k", kv_w), ("v", kv_w), ("qi", qi_w), ("ki", d_idx),
                            ("wi", IDX_HEADS), ("u", s5_w), ("ga", d_model), ("gb", d_model)):
            seg[name] = w_t[o:o + width]
            o += width
        w_q_grouped = _group_major_heads(seg["q"], d_head)
        w_u, w_ga, w_gb = seg["u"], seg["ga"], seg["gb"]
        pad = jnp.zeros((LANES - d_idx - IDX_HEADS, d_model), BF16)
        w_small = jnp.concatenate([seg["qi"], seg["k"], seg["v"], seg["ki"], seg["wi"], pad], axis=0)
        w_qiw = jnp.concatenate([seg["qi"], seg["ki"], seg["wi"], pad], axis=0)
        w_kv_t = jnp.concatenate([seg["k"], seg["v"], seg["ki"]], axis=0)
        w_attn_grouped = _group_major_heads(w_attn_proj[l].astype(BF16), d_head)
        w_g = w_glu[l].astype(BF16)
        w_o = w_out[l].astype(BF16)
        wu = w_up[l].astype(BF16)
        cw = conv_w[l].astype(F32)
        cb = conv_b[l].astype(F32).reshape(1, -1)
        wd = w_down[l].astype(BF16)
        s5p = _s5_params(s5_a_re[l], s5_a_im[l], s5_log_step[l], s5_b_re[l], s5_b_im[l],
                         s5_c_re[l], s5_c_im[l], STEP_CHUNK)
        last = l == depth - 1

        def project(h, tm, ws):
            q = _matmul(h, w_q_grouped, tm, F32, "proj_q")
            u = _matmul(h, w_u, tm, F32, "proj_u")
            ga = _matmul(h, w_ga, tm, F32, "proj_ga")
            gb = _matmul(h, w_gb, tm, F32, "proj_gb")
            small = _matmul(h, ws, tm, F32, "proj_small")
            return q, u, ga, gb, small

        h = _rmsnorm(xp, norm_mix[l], tm_p, BF16)
        q, u, ga, gb, qiw = project(h, tm_p, w_qiw)
        k_t, v_t, ki_t, kb_t, vb_t, kib_t = _kv_transposed(h, w_kv_t, batch, seq, kv_w, d_idx,
                                                            min(512, seq))
        k_sel = min(TOPK_MAX, seq // 4)
        attn = _prompt_attention(q, qiw, kib_t, kb_t, vb_t, batch, seq, d_head, d_idx, k_sel,
                                 min(256, seq))
        y5, srp, sip = _s5_prompt(u, batch, seq, s5p, s5_d[l], groups, p_state)
        merged = _merge(attn, y5, ga, gb, w_attn_grouped, w_g, min(256, batch * seq))
        x1, h2 = _out_proj(merged, xp, w_o, norm_ffn[l], tm_p)
        xp, conv_p = _ffn_prompt(h2, x1, wu, cw, cb, wd, norm_final, batch, seq, tm_p, last)

        def heads_last(a_t):
            return jnp.transpose(a_t.reshape(batch, N_KV_HEADS, d_head, seq), (0, 3, 1, 2))

        outs["kp"].append(heads_last(k_t)); outs["vp"].append(heads_last(v_t))
        outs["kip"].append(jnp.swapaxes(ki_t, 1, 2))
        outs["srp"].append(srp); outs["sip"].append(sip); outs["cp"].append(conv_p)

        h = _rmsnorm(xs, norm_mix[l], tm_s, BF16)
        q, u, ga, gb, small = project(h, tm_s, w_small)
        attn = _sample_attention(q, small, cache_k[l], cache_v[l], cache_kidx[l], page_table,
                                 t_new, d_head, d_idx)
        y5, srs, sis = _s5_sample(u, nb, t_new, s5p, s5_d[l], state_s5_re[l], state_s5_im[l])
        merged = _merge(attn, y5, ga, gb, w_attn_grouped, w_g, min(256, nb * t_new))
        x1, h2 = _out_proj(merged, xs, w_o, norm_ffn[l], tm_s)

        def time_major(a):
            return a.reshape(nb, t_new, -1).swapaxes(0, 1).reshape(nb * t_new, -1)

        y_tm, conv_s = _ffn_sample(time_major(h2), time_major(x1), wu, cw, cb, wd, norm_final,
                                   state_ffn_conv[l].astype(F32), nb, t_new, last)
        xs = y_tm.reshape(t_new, nb, d_model).swapaxes(0, 1).reshape(nb * t_new, d_model)
        k_new = small[:, qi_w:qi_w + kv_w].reshape(nb, t_new, N_KV_HEADS, d_head)
        v_new = small[:, qi_w + kv_w:qi_w + 2 * kv_w].reshape(nb, t_new, N_KV_HEADS, d_head)
        ki_new = small[:, qi_w + 2 * kv_w:qi_w + 2 * kv_w + d_idx].reshape(nb, t_new, d_idx)
        outs["ks"].append(k_new); outs["vs"].append(v_new); outs["kis"].append(ki_new)
        outs["srs"].append(srs); outs["sis"].append(sis); outs["cs"].append(conv_s)

    stk = {name: jnp.stack(v) for name, v in outs.items()}
    return (xp.reshape(batch, seq, d_model), xs.reshape(nb, t_new, d_model),
            stk["kp"], stk["vp"], stk["kip"], stk["srp"], stk["sip"], stk["cp"],
            stk["ks"], stk["vs"], stk["kis"], stk["srs"], stk["sis"], stk["cs"])
```

```python
import functools
import math

import jax
import jax.numpy as jnp
from jax import lax
from jax.experimental import pallas as pl
from jax.experimental.pallas import tpu as pltpu

F32 = jnp.float32
BF16 = jnp.bfloat16

EPS = 1e-6
LOG2E = math.log2(math.e)
TOPK_MAX = 256
N_HEADS = 16
N_KV_HEADS = 4
KV_GROUP = N_HEADS // N_KV_HEADS
IDX_HEADS = 8
S5_GROUP = 16
STEP_CHUNK = 8
LANES = 128
SUBLANES = 8
GROUPS_PER_TILE = LANES // S5_GROUP
FF_TILE = 512
VMEM_LIMIT = 48 * 1024 * 1024
NEG_INF = float("-inf")
INT_MIN = -2 ** 31


def _params(*sem):
    return pltpu.CompilerParams(dimension_semantics=sem, vmem_limit_bytes=VMEM_LIMIT)


def _sigmoid(x):
    return 0.5 * jnp.tanh(0.5 * x) + 0.5


def _gelu_tanh(x):
    c = math.sqrt(2.0 / math.pi)
    return 0.5 * x * (1.0 + jnp.tanh(c * (x + 0.044715 * (x * x * x))))


def _dot_nt(a, b):
    return lax.dot_general(a, b, (((1,), (1,)), ((), ())), preferred_element_type=F32)


def _rms_kernel(x_ref, g_ref, o_ref):
    x = x_ref[...]
    ms = jnp.mean(x * x, axis=-1, keepdims=True)
    o_ref[...] = (x * lax.rsqrt(ms + EPS) * g_ref[...]).astype(o_ref.dtype)


def _rmsnorm(x, g, tm, out_dtype):
    m, d = x.shape
    return pl.pallas_call(
        _rms_kernel,
        grid=(m // tm,),
        in_specs=[pl.BlockSpec((tm, d), lambda i: (i, 0)),
                  pl.BlockSpec((1, d), lambda i: (0, 0))],
        out_specs=pl.BlockSpec((tm, d), lambda i: (i, 0)),
        out_shape=jax.ShapeDtypeStruct((m, d), out_dtype),
        compiler_params=_params("arbitrary"),
        name="rmsnorm",
    )(x, g.reshape(1, d))


def _mm_kernel(h_ref, w_ref, o_ref):
    o_ref[...] = _dot_nt(h_ref[...], w_ref[...]).astype(o_ref.dtype)


def _matmul(h, w_t, tm, out_dtype, name):
    m, k = h.shape
    n = w_t.shape[0]
    return pl.pallas_call(
        _mm_kernel,
        grid=(m // tm,),
        in_specs=[pl.BlockSpec((tm, k), lambda i: (i, 0)),
                  pl.BlockSpec((n, k), lambda i: (0, 0))],
        out_specs=pl.BlockSpec((tm, n), lambda i: (i, 0)),
        out_shape=jax.ShapeDtypeStruct((m, n), out_dtype),
        compiler_params=_params("arbitrary"),
        name=name,
    )(h, w_t)


def _kv_t_kernel(w_ref, h_ref, k_ref, v_ref, ki_ref, kb_ref, vb_ref, kib_ref, *, kv_w):
    out = _dot_nt(w_ref[...], h_ref[...])
    for lo, hi, full_ref, half_ref in ((0, kv_w, k_ref, kb_ref), (kv_w, 2 * kv_w, v_ref, vb_ref),
                                       (2 * kv_w, out.shape[0], ki_ref, kib_ref)):
        full_ref[...] = out[lo:hi]
        half_ref[...] = out[lo:hi].astype(BF16)


def _kv_transposed(h, w_t, batch, seq, kv_w, d_idx, tn):
    d = h.shape[1]
    nt = seq // tn
    out = lambda rows: pl.BlockSpec((None, rows, tn), lambda b, i: (b, 0, i))
    shape = lambda rows, dt: jax.ShapeDtypeStruct((batch, rows, seq), dt)
    widths = (kv_w, kv_w, d_idx)
    return pl.pallas_call(
        functools.partial(_kv_t_kernel, kv_w=kv_w),
        grid=(batch, nt),
        in_specs=[pl.BlockSpec(w_t.shape, lambda b, i: (0, 0)),
                  pl.BlockSpec((tn, d), lambda b, i: (b * nt + i, 0))],
        out_specs=[out(w) for w in widths] * 2,
        out_shape=[shape(w, F32) for w in widths] + [shape(w, BF16) for w in widths],
        compiler_params=_params("arbitrary", "arbitrary"),
        name="proj_kv_transposed",
    )(w_t, h)


SEARCH_GROUP = 4


def _count(mask):
    return jnp.sum(jnp.where(mask, 1.0, 0.0), axis=1, keepdims=True)


def _topk_mask(score, col, k_sel, n_cols):
    kf = float(k_sel)
    rows = score.shape[0]

    def as_float(key):
        bits = jnp.where(key < 0, key ^ jnp.int32(0x7FFFFFFF), key)
        return lax.bitcast_convert_type(bits, F32)

    finite = score > NEG_INF
    few = _count(finite) <= kf

    cnt0 = _count(score >= 0.0)
    cand0 = jnp.where(cnt0 >= kf, jnp.int32(0), jnp.int32(INT_MIN))
    cand0 = jnp.broadcast_to(cand0, (rows, 1)).astype(jnp.int32)
    cnt0 = jnp.where(cnt0 >= kf, cnt0, float(n_cols))

    def unresolved(cnt):
        return jnp.max(jnp.where(few, 0.0, cnt - kf)) > 0.0

    def search_cond(carry):
        it, _, cnt = carry
        return jnp.logical_and(it < 31, unresolved(cnt))

    def search_body(carry):
        it, cand, cnt = carry
        for g in range(SEARCH_GROUP):
            shift = jnp.int32(30) - (it + g)
            bit = jnp.where(shift >= 0, lax.shift_left(jnp.int32(1), jnp.maximum(shift, 0)), 0)
            trial = cand + bit
            c_trial = _count(score >= as_float(trial))
            ok = c_trial >= kf
            cand = jnp.where(ok, trial, cand)
            cnt = jnp.where(ok, c_trial, cnt)
        return it + SEARCH_GROUP, cand, cnt

    _, cand, cnt = lax.while_loop(search_cond, search_body, (jnp.int32(0), cand0, cnt0))
    thr = as_float(cand)

    above = score > thr
    ties = score == thr
    need = kf - _count(above)
    surplus = jnp.where(few, 0.0, cnt - kf)
    n_bits = max(1, (n_cols - 1).bit_length())

    def index_step(it, m):
        trial = m + lax.shift_left(jnp.int32(1), jnp.int32(n_bits - 1) - it)
        taken = _count(jnp.logical_and(ties, col < trial))
        return jnp.where(taken <= need - 1.0, trial, m)

    def lowest_indices(_):
        return lax.fori_loop(0, n_bits, index_step, jnp.zeros((rows, 1), jnp.int32))

    def all_ties(_):
        return jnp.full((rows, 1), n_cols, jnp.int32)

    last = lax.cond(jnp.max(surplus) > 0.0, lowest_indices, all_ties, None)
    take_tie = jnp.logical_and(jnp.logical_and(ties, col <= last), need >= 1.0)
    top = jnp.logical_or(above, take_tie)
    return jnp.logical_and(finite, jnp.logical_or(few, top))


def _index_weights(kw, d_idx):
    w_scale = (d_idx ** -0.5) * (IDX_HEADS ** -0.5)
    return [kw[:, d_idx + h:d_idx + h + 1] * w_scale for h in range(IDX_HEADS)]


def _head_rows(qi, d_idx):
    return jnp.concatenate([qi[:, h * d_idx:(h + 1) * d_idx] for h in range(IDX_HEADS)], axis=0)


def _weighted_relu_sum(lg, w_cols, rows):
    score = jnp.zeros((rows, lg.shape[1]), F32)
    for h in range(IDX_HEADS):
        score = score + jnp.maximum(lg[h * rows:(h + 1) * rows], 0.0) * w_cols[h]
    return score


HEADS_PER_DOT = 2


PROMPT_SCORE_ELEMS = 512 * 1024


def _prompt_attn_kernel(q_ref, qi_ref, kw_ref, kit_ref, kt_ref, vt_ref, prev_ref, o_ref,
                        *, tq, n_keys, k_sel, d_head, d_idx, q_block):
    del prev_ref
    n_seqs = q_ref.shape[0]
    scores = []
    for b in range(n_seqs):
        qi = qi_ref[b]
        w_cols = _index_weights(kw_ref[b], d_idx)
        kit = kit_ref[b]
        score = jnp.zeros((tq, n_keys), F32)
        for h in range(IDX_HEADS):
            lg = jnp.dot(qi[:, h * d_idx:(h + 1) * d_idx].astype(BF16), kit,
                         preferred_element_type=F32)
            score = score + jnp.maximum(lg, 0.0) * w_cols[h]
        scores.append(score)
    score = jnp.concatenate(scores, axis=0)

    col = lax.broadcasted_iota(jnp.int32, score.shape, 1)
    row = lax.broadcasted_iota(jnp.int32, (n_seqs, tq, n_keys), 1).reshape(score.shape)
    score = jnp.where(col <= q_block * tq + row, score, NEG_INF)
    bias_all = jnp.where(_topk_mask(score, col, k_sel, n_keys), 0.0, NEG_INF)

    for b in range(n_seqs):
        bias = bias_all[b * tq:(b + 1) * tq]
        q = q_ref[b] * (d_head ** -0.5 * LOG2E)
        for n in range(N_KV_HEADS):
            kt = kt_ref[b, n * d_head:(n + 1) * d_head, :]
            vt = vt_ref[b, n * d_head:(n + 1) * d_head, :]
            for g0 in range(0, KV_GROUP, HEADS_PER_DOT):
                heads = [(g0 + i) * N_KV_HEADS + n for i in range(HEADS_PER_DOT)]
                qs = jnp.concatenate([q[:, h * d_head:(h + 1) * d_head] for h in heads], axis=0)
                s = jnp.dot(qs.astype(BF16), kt, preferred_element_type=F32)
                s = (s.reshape(HEADS_PER_DOT, tq, n_keys) + bias[None]).reshape(-1, n_keys)
                m = jnp.max(s, axis=1, keepdims=True)
                p = jnp.exp2(s - m)
                l = jnp.sum(p, axis=1, keepdims=True)
                o = _dot_nt(p.astype(BF16), vt) / l
                for i, h in enumerate(heads):
                    o_ref[b, :, h * d_head:(h + 1) * d_head] = (
                        o[i * tq:(i + 1) * tq].astype(o_ref.dtype))


def _prompt_attention(q, qiw, ki_t, k_t, v_t, batch, seq, d_head, d_idx, k_sel, tq):
    attn_w = q.shape[1]
    kv_w = N_KV_HEADS * d_head
    qi_w = IDX_HEADS * d_idx
    nq = seq // tq
    q3 = q.reshape(batch, seq, attn_w)
    qiw3 = qiw.reshape(batch, seq, qiw.shape[1])
    out = jnp.zeros((batch, seq, attn_w), BF16)
    for qb in range(nq):
        n_keys = (qb + 1) * tq
        n_seqs = max(d for d in range(1, batch + 1)
                     if batch % d == 0 and (d == 1 or d * tq * n_keys <= PROMPT_SCORE_ELEMS))
        row = lambda w, c, qb=qb, n_seqs=n_seqs: pl.BlockSpec((n_seqs, tq, w), lambda b: (b, qb, c))
        keys = lambda w, n_keys=n_keys, n_seqs=n_seqs: pl.BlockSpec((n_seqs, w, n_keys),
                                                                     lambda b: (b, 0, 0))
        out = pl.pallas_call(
            functools.partial(_prompt_attn_kernel, tq=tq, n_keys=n_keys, k_sel=k_sel,
                              d_head=d_head, d_idx=d_idx, q_block=qb),
            grid=(batch // n_seqs,),
            in_specs=[row(attn_w, 0), row(qi_w, 0), row(LANES, qi_w // LANES),
                      keys(d_idx), keys(kv_w), keys(kv_w),
                      pl.BlockSpec(memory_space=pl.ANY)],
            out_specs=row(attn_w, 0),
            out_shape=jax.ShapeDtypeStruct((batch, seq, attn_w), BF16),
            input_output_aliases={6: 0},
            compiler_params=_params("arbitrary"),
            name=f"prompt_attention_q{qb}",
        )(q3, qiw3, qiw3, ki_t, k_t, v_t, out)
    return out.reshape(batch * seq, attn_w)


def _lane_columns(cols, width):
    rows = cols[0].shape[0]
    lane = lax.broadcasted_iota(jnp.int32, (rows, width), 1)
    out = jnp.zeros((rows, width), F32)
    for j, c in enumerate(cols):
        out = jnp.where(lane == j, c, out)
    return out


def _sample_score_kernel(pt_ref, qi_ref, kw_ref, kin_ref, *rest, n_seqs, n_pages, page, t_new,
                         d_idx):
    del pt_ref
    s_ref = rest[n_seqs * n_pages]
    past = n_pages * page
    rows = SUBLANES
    pad_row = lax.broadcasted_iota(jnp.int32, (rows, page), 0) >= t_new
    first = lax.broadcasted_iota(jnp.int32, (rows, page), 1) == 0
    lane = lax.broadcasted_iota(jnp.int32, (rows, LANES), 1)
    t = lax.broadcasted_iota(jnp.int32, (rows, LANES), 0)
    causal_new = jnp.logical_and(lane <= t, t < t_new)
    for i in range(n_seqs):
        ipages = rest[i * n_pages:(i + 1) * n_pages]
        qi_rows = _head_rows(qi_ref[i], d_idx)
        qi_bf = qi_rows.astype(BF16)
        w_cols = _index_weights(kw_ref[i], d_idx)
        for p in range(n_pages):
            lg = jnp.dot(qi_bf, ipages[p][...].astype(BF16), preferred_element_type=F32)
            pad_val = jnp.where(first, 0.0, NEG_INF) if p == 0 else NEG_INF
            s_ref[i, :, p * page:(p + 1) * page] = jnp.where(
                pad_row, pad_val, _weighted_relu_sum(lg, w_cols, rows))
        kin = kin_ref[i]
        new_cols = []
        for j in range(t_new):
            lg = jnp.sum(qi_rows * kin[j:j + 1, :], axis=1, keepdims=True)
            new_cols.append(_weighted_relu_sum(lg, w_cols, rows))
        s_ref[i, :, past:] = jnp.where(causal_new, _lane_columns(new_cols, LANES), NEG_INF)


def _select_kernel(s_ref, b_ref, *, k_sel):
    score = s_ref[...]
    col = lax.broadcasted_iota(jnp.int32, score.shape, 1)
    keep = _topk_mask(score, col, k_sel, score.shape[1])
    b_ref[...] = jnp.where(keep, 0.0, NEG_INF)


SAMPLE_SEQS_PER_STEP = 4


def _sample_attn_kernel(pt_ref, q_ref, kn_ref, vn_ref, bias_ref, *rest,
                        n_seqs, n_pages, page, t_new, d_head):
    del pt_ref
    pages = rest[:2 * n_seqs * n_pages]
    o_ref, kt_ref, vt_ref = rest[2 * n_seqs * n_pages:]
    for i in range(n_seqs):
        kpages = pages[i * n_pages:(i + 1) * n_pages]
        vpages = pages[(n_seqs + i) * n_pages:(n_seqs + i + 1) * n_pages]
        o_ref[i] = _sample_attend(q_ref[i], kn_ref[i], vn_ref[i], bias_ref[i], kpages, vpages,
                                  kt_ref.at[i], vt_ref.at[i], page, t_new, d_head
                                  ).astype(o_ref.dtype)


def _sample_attend(q, kn, vn, bias, kpages, vpages, kt_ref, vt_ref, page, t_new, d_head):
    n_pages = len(kpages)
    past = n_pages * page
    rows = SUBLANES
    kv_w = N_KV_HEADS * d_head

    for p in range(n_pages):
        kt_ref[:, p * page:(p + 1) * page] = kpages[p][...].astype(BF16)
        vt_ref[:, p * page:(p + 1) * page] = vpages[p][...].astype(BF16)

    bias = jnp.concatenate([bias] * N_HEADS, axis=0)

    q = q * (d_head ** -0.5 * LOG2E)
    lane_head = lax.broadcasted_iota(jnp.int32, (rows, kv_w), 1) // d_head
    q_rows = jnp.concatenate(
        [jnp.where(lane_head == n, q[:, g * kv_w:(g + 1) * kv_w], 0.0)
         for g in range(KV_GROUP) for n in range(N_KV_HEADS)], axis=0)

    s_past = jnp.dot(q_rows.astype(BF16), kt_ref[...], preferred_element_type=F32)
    s_past = s_past + bias[:, :past]
    s_new = _lane_columns(
        [jnp.sum(q_rows * kn[j:j + 1, :], axis=1, keepdims=True) for j in range(t_new)], LANES)
    s_new = s_new + bias[:, past:]

    m = jnp.maximum(jnp.max(s_past, axis=1, keepdims=True), jnp.max(s_new, axis=1, keepdims=True))
    p_past = jnp.exp2(s_past - m)
    p_new = jnp.exp2(s_new - m)
    l = jnp.sum(p_past, axis=1, keepdims=True) + jnp.sum(p_new, axis=1, keepdims=True)
    o = _dot_nt(p_past.astype(BF16), vt_ref[...])
    for j in range(t_new):
        o = o + p_new[:, j:j + 1] * vn[j:j + 1, :]
    o = o / l

    chunks = []
    for g in range(KV_GROUP):
        acc = jnp.zeros((rows, kv_w), F32)
        for n in range(N_KV_HEADS):
            r0 = (g * N_KV_HEADS + n) * rows
            acc = acc + jnp.where(lane_head == n, o[r0:r0 + rows], 0.0)
        chunks.append(acc)
    return jnp.concatenate(chunks, axis=1)


def _seq_page_map(b, pt, *, i, p, n_seqs, n_pages):
    return (pt[(b * n_seqs + i) * n_pages + p], 0, 0)


def _sample_attention(q, small, cache_k, cache_v, cache_ki, page_table, t_new, d_head, d_idx):
    nb, n_pages = page_table.shape
    n_pool, page = cache_k.shape[0], cache_k.shape[1]
    attn_w = q.shape[1]
    kv_w = N_KV_HEADS * d_head
    qi_w = IDX_HEADS * d_idx
    rows = SUBLANES
    past = n_pages * page
    k_sel = min(TOPK_MAX, (past + t_new) // 4)

    def pad_rows(a):
        a = a.reshape(nb, t_new, a.shape[-1])
        return jnp.pad(a, ((0, 0), (0, rows - t_new), (0, 0)))

    q8 = pad_rows(q)
    qi8 = pad_rows(small[:, :qi_w])
    kw8 = pad_rows(small[:, qi_w + 2 * kv_w:])
    kn = pad_rows(small[:, qi_w:qi_w + kv_w])
    vn = pad_rows(small[:, qi_w + kv_w:qi_w + 2 * kv_w])
    kin = pad_rows(small[:, qi_w + 2 * kv_w:qi_w + 2 * kv_w + d_idx])

    ck = jnp.transpose(cache_k, (0, 2, 3, 1)).reshape(n_pool, kv_w, page)
    cv = jnp.transpose(cache_v, (0, 2, 3, 1)).reshape(n_pool, kv_w, page)
    ci = jnp.transpose(cache_ki, (0, 2, 1))

    pt_flat = page_table.reshape(-1)
    n_keys = past + LANES
    n_seqs = math.gcd(nb, SAMPLE_SEQS_PER_STEP)

    def seqs_spec(w):
        return pl.BlockSpec((n_seqs, rows, w), lambda b, pt: (b, 0, 0))

    def seq_page_specs(w):
        return [pl.BlockSpec((None, w, page),
                             functools.partial(_seq_page_map, i=i, p=p, n_seqs=n_seqs,
                                               n_pages=n_pages))
                for i in range(n_seqs) for p in range(n_pages)]

    score = pl.pallas_call(
        functools.partial(_sample_score_kernel, n_seqs=n_seqs, n_pages=n_pages, page=page,
                          t_new=t_new, d_idx=d_idx),
        grid_spec=pltpu.PrefetchScalarGridSpec(
            num_scalar_prefetch=1,
            grid=(nb // n_seqs,),
            in_specs=[seqs_spec(qi_w), seqs_spec(LANES), seqs_spec(d_idx)]
                     + seq_page_specs(d_idx),
            out_specs=seqs_spec(n_keys),
        ),
        out_shape=jax.ShapeDtypeStruct((nb, rows, n_keys), F32),
        compiler_params=_params("arbitrary"),
        name="sample_scores",
    )(pt_flat, qi8, kw8, kin, *([ci] * (n_seqs * n_pages)))

    sel_rows = math.gcd(nb * rows, 256)
    bias = pl.pallas_call(
        functools.partial(_select_kernel, k_sel=k_sel),
        grid=(nb * rows // sel_rows,),
        in_specs=[pl.BlockSpec((sel_rows, n_keys), lambda i: (i, 0))],
        out_specs=pl.BlockSpec((sel_rows, n_keys), lambda i: (i, 0)),
        out_shape=jax.ShapeDtypeStruct((nb * rows, n_keys), F32),
        compiler_params=_params("arbitrary"),
        name="sample_select",
    )(score.reshape(nb * rows, n_keys)).reshape(nb, rows, n_keys)

    out = pl.pallas_call(
        functools.partial(_sample_attn_kernel, n_seqs=n_seqs, n_pages=n_pages, page=page,
                          t_new=t_new, d_head=d_head),
        grid_spec=pltpu.PrefetchScalarGridSpec(
            num_scalar_prefetch=1,
            grid=(nb // n_seqs,),
            in_specs=[seqs_spec(attn_w), seqs_spec(kv_w), seqs_spec(kv_w), seqs_spec(n_keys)]
                     + seq_page_specs(kv_w) + seq_page_specs(kv_w),
            out_specs=seqs_spec(attn_w),
            scratch_shapes=[pltpu.VMEM((n_seqs, kv_w, past), BF16),
                            pltpu.VMEM((n_seqs, kv_w, past), BF16)],
        ),
        out_shape=jax.ShapeDtypeStruct((nb, rows, attn_w), BF16),
        compiler_params=_params("arbitrary"),
        name="sample_attention",
    )(pt_flat, q8, kn, vn, bias, *([ck] * (n_seqs * n_pages)), *([cv] * (n_seqs * n_pages)))
    return out[:, :t_new].reshape(nb * t_new, attn_w)


def _group_major_heads(a, d_head):
    n = a.shape[1]
    a = a.reshape(N_KV_HEADS, KV_GROUP, d_head, n)
    return jnp.swapaxes(a, 0, 1).reshape(N_HEADS * d_head, n)


def _s5_step(xr, xi, ar, ai, bu, half):
    br, bi = bu[:, :half], bu[:, half:]
    if xr is None:
        return br, bi
    return ar * xr - ai * xi + br, ar * xi + ai * xr + bi


def _s5_inputs(u_ref, b_ref, j):
    tr, steps, _ = u_ref.shape
    x = jnp.transpose(u_ref[:, :, j * LANES:(j + 1) * LANES], (1, 0, 2))
    us = [x[s] for s in range(steps)]
    bu = jnp.dot(jnp.concatenate(us, axis=0).astype(BF16), b_ref[j], preferred_element_type=F32)
    return us, [bu[s * tr:(s + 1) * tr] for s in range(steps)]


def _s5_local_kernel(u_ref, b_ref, ar_ref, ai_ref, z_ref, *, n_tiles):
    sw = b_ref.shape[2]
    half = sw // 2
    for j in range(n_tiles):
        ar, ai = ar_ref[j:j + 1, :], ai_ref[j:j + 1, :]
        xr = xi = None
        for bu in _s5_inputs(u_ref, b_ref, j)[1]:
            xr, xi = _s5_step(xr, xi, ar, ai, bu, half)
        z_ref[:, j * sw:j * sw + half] = xr
        z_ref[:, j * sw + half:(j + 1) * sw] = xi


def _s5_carry_kernel(z_ref, ar_ref, ai_ref, xc_ref, xf_ref, *, batch, n_chunks):
    half = z_ref.shape[1] // 2
    ar, ai = ar_ref[...], ai_ref[...]

    def body(c, carry):
        new = []
        for b in range(batch):
            xr, xi = carry[b]
            row = b * n_chunks + c
            xc_ref[pl.ds(row, 1), :] = jnp.concatenate([xr, xi], axis=1)
            z = z_ref[pl.ds(row, 1), :]
            new.append((ar * xr - ai * xi + z[:, :half], ar * xi + ai * xr + z[:, half:]))
        return tuple(new)

    zero = jnp.zeros((1, half), F32)
    final = lax.fori_loop(0, n_chunks, body, tuple((zero, zero) for _ in range(batch)),
                          unroll=math.gcd(n_chunks, 4))
    for b in range(batch):
        xf_ref[b:b + 1, :] = jnp.concatenate(list(final[b]), axis=1)


def _s5_out_kernel(u_ref, x0_ref, b_ref, c_ref, ar_ref, ai_ref, d_ref, y_ref, *maybe_xf,
                   n_tiles):
    tr, steps, _ = u_ref.shape
    sw = b_ref.shape[2]
    half = sw // 2
    for j in range(n_tiles):
        ar, ai = ar_ref[j:j + 1, :], ai_ref[j:j + 1, :]
        xr = x0_ref[:, j * sw:j * sw + half]
        xi = x0_ref[:, j * sw + half:(j + 1) * sw]
        d = d_ref[:, j * LANES:(j + 1) * LANES]
        us, bus = _s5_inputs(u_ref, b_ref, j)
        xs = []
        for bu in bus:
            xr, xi = _s5_step(xr, xi, ar, ai, bu, half)
            xs.append(jnp.concatenate([xr, xi], axis=1).astype(BF16))
        y = jnp.dot(jnp.concatenate(xs, axis=0), c_ref[j], preferred_element_type=F32)
        ys = jnp.stack([y[s * tr:(s + 1) * tr] + d * us[s] for s in range(steps)], axis=0)
        y_ref[:, :, j * LANES:(j + 1) * LANES] = jnp.transpose(ys, (1, 0, 2))
        if maybe_xf:
            maybe_xf[0][:, j * sw:j * sw + half] = xr
            maybe_xf[0][:, j * sw + half:(j + 1) * sw] = xi


def _s5_params(a_re, a_im, log_step, b_re, b_im, c_re, c_im, chunk_steps):
    g, p = a_re.shape
    nt = g // GROUPS_PER_TILE
    a_re, a_im = a_re.astype(F32), a_im.astype(F32)
    step = jnp.exp(log_step.astype(F32))[:, None]

    def discretise(n_steps):
        mag = jnp.exp(a_re * step * n_steps)
        return mag * jnp.cos(a_im * step * n_steps), mag * jnp.sin(a_im * step * n_steps)

    lr, li = discretise(1.0)
    cr, ci = discretise(float(chunk_steps))
    den = a_re * a_re + a_im * a_im
    fr = ((lr - 1.0) * a_re + li * a_im) / den
    fi = (li * a_re - (lr - 1.0) * a_im) / den
    b_re, b_im = b_re.astype(F32), b_im.astype(F32)
    bb_re = fr[..., None] * b_re - fi[..., None] * b_im
    bb_im = fr[..., None] * b_im + fi[..., None] * b_re
    eye = jnp.eye(GROUPS_PER_TILE, dtype=F32)

    bb = jnp.stack([bb_re, bb_im], axis=0).reshape(2, nt, GROUPS_PER_TILE, p, S5_GROUP)
    bb = jnp.transpose(bb, (1, 2, 4, 0, 3))
    b_blk = bb[:, :, :, :, None, :] * eye[None, :, None, None, :, None]
    b_blk = b_blk.reshape(nt, LANES, 2 * GROUPS_PER_TILE * p).astype(BF16)

    cc = jnp.stack([c_re.astype(F32), -c_im.astype(F32)], axis=0)
    cc = cc.reshape(2, nt, GROUPS_PER_TILE, S5_GROUP, p)
    cc = jnp.transpose(cc, (1, 0, 2, 4, 3))
    c_blk = cc[:, :, :, :, None, :] * eye[None, None, :, None, :, None]
    c_blk = c_blk.reshape(nt, 2 * GROUPS_PER_TILE * p, LANES).astype(BF16)

    def tiles(z):
        return z.reshape(nt, GROUPS_PER_TILE * p)

    return b_blk, c_blk, tiles(lr), tiles(li), tiles(cr), tiles(ci)


def _state_to_tiles(s_re, s_im):
    nb, g, p = s_re.shape
    nt = g // GROUPS_PER_TILE
    st = jnp.stack([s_re.reshape(nb, nt, GROUPS_PER_TILE * p),
                    s_im.reshape(nb, nt, GROUPS_PER_TILE * p)], axis=2)
    return st.reshape(nb, nt * 2 * GROUPS_PER_TILE * p)


def _tiles_to_state(x, g, p):
    nb = x.shape[0]
    nt = g // GROUPS_PER_TILE
    st = x.reshape(nb, nt, 2, GROUPS_PER_TILE, p)
    return st[:, :, 0].reshape(nb, g, p), st[:, :, 1].reshape(nb, g, p)


def _s5_outputs(u3, x0, prm, d_skip, tr, want_final):
    b_blk, c_blk, ar, ai = prm
    r, steps, width = u3.shape
    nt, _, sw = b_blk.shape
    kern = functools.partial(_s5_out_kernel, n_tiles=nt)
    full3 = lambda a: pl.BlockSpec(a.shape, lambda i: (0, 0, 0))
    full2 = lambda a: pl.BlockSpec(a.shape, lambda i: (0, 0))
    d2 = d_skip.reshape(1, width).astype(F32)
    tokens = pl.BlockSpec((tr, steps, width), lambda i: (i, 0, 0))
    out_shape = [jax.ShapeDtypeStruct(u3.shape, F32)]
    out_specs = [tokens]
    if want_final:
        out_shape.append(jax.ShapeDtypeStruct((r, nt * sw), F32))
        out_specs.append(pl.BlockSpec((tr, nt * sw), lambda i: (i, 0)))
    return pl.pallas_call(
        kern,
        grid=(r // tr,),
        in_specs=[tokens, pl.BlockSpec((tr, nt * sw), lambda i: (i, 0)),
                  full3(b_blk), full3(c_blk), full2(ar), full2(ai), full2(d2)],
        out_specs=out_specs,
        out_shape=out_shape,
        compiler_params=_params("arbitrary"),
        name="s5_outputs",
    )(u3, x0, b_blk, c_blk, ar, ai, d2)


def _s5_prompt(u, batch, seq, prm_all, d_skip, g, p):
    b_blk, c_blk, ar, ai, ar_c, ai_c = prm_all
    steps = STEP_CHUNK
    n_chunks = seq // steps
    r = batch * n_chunks
    width = u.shape[1]
    u3 = u.reshape(r, steps, width)
    nt, _, sw = b_blk.shape
    tr = 128
    z = pl.pallas_call(
        functools.partial(_s5_local_kernel, n_tiles=nt),
        grid=(r // tr,),
        in_specs=[pl.BlockSpec((tr, steps, width), lambda i: (i, 0, 0)),
                  pl.BlockSpec(b_blk.shape, lambda i: (0, 0, 0)),
                  pl.BlockSpec(ar.shape, lambda i: (0, 0)),
                  pl.BlockSpec(ai.shape, lambda i: (0, 0))],
        out_specs=pl.BlockSpec((tr, nt * sw), lambda i: (i, 0)),
        out_shape=jax.ShapeDtypeStruct((r, nt * sw), F32),
        compiler_params=_params("arbitrary"),
        name="s5_chunk_states",
    )(u3, b_blk, ar, ai)
    half = sw // 2
    xc, xf = pl.pallas_call(
        functools.partial(_s5_carry_kernel, batch=batch, n_chunks=n_chunks),
        grid=(nt,),
        in_specs=[pl.BlockSpec((r, sw), lambda j: (0, j)),
                  pl.BlockSpec((None, 1, half), lambda j: (j, 0, 0)),
                  pl.BlockSpec((None, 1, half), lambda j: (j, 0, 0))],
        out_specs=[pl.BlockSpec((r, sw), lambda j: (0, j)),
                   pl.BlockSpec((batch, sw), lambda j: (0, j))],
        out_shape=[jax.ShapeDtypeStruct((r, nt * sw), F32),
                   jax.ShapeDtypeStruct((batch, nt * sw), F32)],
        compiler_params=_params("arbitrary"),
        name="s5_carry",
    )(z, ar_c.reshape(nt, 1, half), ai_c.reshape(nt, 1, half))
    (y3,) = _s5_outputs(u3, xc, (b_blk, c_blk, ar, ai), d_skip, tr, False)
    s_re, s_im = _tiles_to_state(xf, g, p)
    return y3.reshape(batch * seq, width), s_re, s_im


def _s5_sample(u, nb, t_new, prm_all, d_skip, s_re, s_im):
    b_blk, c_blk, ar, ai, _, _ = prm_all
    g, p = s_re.shape[1], s_re.shape[2]
    width = u.shape[1]
    x0 = _state_to_tiles(s_re.astype(F32), s_im.astype(F32))
    y3, xf = _s5_outputs(u.reshape(nb, t_new, width), x0, (b_blk, c_blk, ar, ai), d_skip, nb, True)
    n_re, n_im = _tiles_to_state(xf, g, p)
    return y3.reshape(nb * t_new, width), n_re, n_im


def _merge_kernel(attn_ref, y_ref, ga_ref, gb_ref, wa_ref, wg_ref, o_ref):
    d = o_ref.shape[1]
    y_a = jnp.dot(attn_ref[...], wa_ref[...], preferred_element_type=F32)
    z = _gelu_tanh(y_ref[...]).astype(BF16)
    glu = jnp.dot(z, wg_ref[...], preferred_element_type=F32)
    y_b = glu[:, :d] * _sigmoid(glu[:, d:])
    o_ref[...] = (_sigmoid(ga_ref[...]) * y_a + _sigmoid(gb_ref[...]) * y_b).astype(o_ref.dtype)


def _merge(attn, y, ga, gb, w_attn, w_glu, tm):
    m, d = ga.shape
    row = lambda w: pl.BlockSpec((tm, w), lambda i: (i, 0))
    return pl.pallas_call(
        _merge_kernel,
        grid=(m // tm,),
        in_specs=[row(attn.shape[1]), row(y.shape[1]), row(d), row(d),
                  pl.BlockSpec(w_attn.shape, lambda i: (0, 0)),
                  pl.BlockSpec(w_glu.shape, lambda i: (0, 0))],
        out_specs=row(d),
        out_shape=jax.ShapeDtypeStruct((m, d), BF16),
        compiler_params=_params("arbitrary"),
        name="merge",
    )(attn, y, ga, gb, w_attn, w_glu)


def _out_proj_kernel(m_ref, x_ref, w_ref, g_ref, x1_ref, h_ref):
    x1 = x_ref[...] + jnp.dot(m_ref[...], w_ref[...], preferred_element_type=F32)
    x1_ref[...] = x1
    ms = jnp.mean(x1 * x1, axis=-1, keepdims=True)
    h_ref[...] = (x1 * lax.rsqrt(ms + EPS) * g_ref[...]).astype(h_ref.dtype)


def _out_proj(merged, x, w_out, g, tm):
    m, d = x.shape
    row = pl.BlockSpec((tm, d), lambda i: (i, 0))
    return pl.pallas_call(
        _out_proj_kernel,
        grid=(m // tm,),
        in_specs=[row, row, pl.BlockSpec(w_out.shape, lambda i: (0, 0)),
                  pl.BlockSpec((1, d), lambda i: (0, 0))],
        out_specs=[row, row],
        out_shape=[jax.ShapeDtypeStruct((m, d), F32), jax.ShapeDtypeStruct((m, d), BF16)],
        compiler_params=_params("arbitrary"),
        name="out_proj",
    )(merged, x, w_out, g.reshape(1, d))


FF_SUB = 2


def _shift_rows(up, prev, k):
    body = pltpu.roll(up, k, axis=0)
    row = lax.broadcasted_iota(jnp.int32, prev.shape, 0)
    head = jnp.where(row < k, pltpu.roll(prev, k, axis=0), body[:SUBLANES])
    return jnp.concatenate([head, body[SUBLANES:]], axis=0)


def _conv_taps(up, u1, u2, cw, cb):
    return cb + cw[0:1, :] * u2 + cw[1:2, :] * u1 + cw[2:3, :] * up


def _ffn_finish(j, n_j, final_norm, acts, wd_ref, x1_ref, g_ref, y_ref):
    y_ref[...] += jnp.dot(jnp.concatenate(acts, axis=1), wd_ref[...], preferred_element_type=F32)

    @pl.when(j == n_j - 1)
    def _():
        x2 = x1_ref[...] + y_ref[...]
        if final_norm:
            ms = jnp.mean(x2 * x2, axis=-1, keepdims=True)
            x2 = x2 * lax.rsqrt(ms + EPS) * g_ref[...]
        y_ref[...] = x2


def _ffn_prompt_kernel(h_ref, wg_ref, wv_ref, cwg_ref, cwv_ref, cbg_ref, cbv_ref, wd_ref, x1_ref,
                       g_ref, y_ref, convg_ref, convv_ref, carry_ref,
                       *, tm, tiles_per_seq, n_j, final_norm):
    i, j = pl.program_id(0), pl.program_id(1)

    @pl.when(i % tiles_per_seq == 0)
    def _():
        carry_ref[j] = jnp.zeros(carry_ref.shape[1:], F32)

    @pl.when(j == 0)
    def _():
        y_ref[...] = jnp.zeros(y_ref.shape, F32)

    h = h_ref[...]
    prev = carry_ref[j]
    sub = FF_TILE // FF_SUB
    acts, tails = [], ([], [])
    for s in range(FF_SUB):
        cs = slice(s * sub, (s + 1) * sub)
        mixed = []
        for half, (w_ref, cw_ref, cb_ref) in enumerate(((wg_ref, cwg_ref, cbg_ref),
                                                        (wv_ref, cwv_ref, cbv_ref))):
            up = jnp.dot(h, w_ref[:, cs], preferred_element_type=F32)
            p8 = prev[:, half * FF_TILE + s * sub:half * FF_TILE + (s + 1) * sub]
            mixed.append(_conv_taps(up, _shift_rows(up, p8, 1), _shift_rows(up, p8, 2),
                                    cw_ref[:, cs], cb_ref[:, cs]))
            tails[half].append(up[tm - SUBLANES:])
        gate, val = mixed
        acts.append((gate * _sigmoid(gate) * val).astype(BF16))
    tail_g = jnp.concatenate(tails[0], axis=1)
    tail_v = jnp.concatenate(tails[1], axis=1)
    carry_ref[j] = jnp.concatenate([tail_g, tail_v], axis=1)
    convg_ref[...] = tail_g
    convv_ref[...] = tail_v
    _ffn_finish(j, n_j, final_norm, acts, wd_ref, x1_ref, g_ref, y_ref)


def _ffn_sample_kernel(h_ref, wg_ref, wv_ref, cwg_ref, cwv_ref, cbg_ref, cbv_ref, wd_ref, x1_ref,
                       g_ref, s0g_ref, s0v_ref, s1g_ref, s1v_ref, y_ref, convg_ref, convv_ref,
                       *, nb, t_new, n_j, final_norm):
    j = pl.program_id(0)

    @pl.when(j == 0)
    def _():
        y_ref[...] = jnp.zeros(y_ref.shape, F32)

    h = h_ref[...]
    sub = FF_TILE // FF_SUB
    acts = []
    for s in range(FF_SUB):
        cs = slice(s * sub, (s + 1) * sub)
        mixed = []
        for w_ref, cw_ref, cb_ref, s0_ref, s1_ref, conv_ref in (
                (wg_ref, cwg_ref, cbg_ref, s0g_ref, s1g_ref, convg_ref),
                (wv_ref, cwv_ref, cbv_ref, s0v_ref, s1v_ref, convv_ref)):
            up = jnp.dot(h, w_ref[:, cs], preferred_element_type=F32)
            s0, s1 = s0_ref[:, cs], s1_ref[:, cs]
            u1 = jnp.concatenate([s1, up[:(t_new - 1) * nb]], axis=0)
            u2 = jnp.concatenate([s0, s1, up[:(t_new - 2) * nb]], axis=0)
            mixed.append(_conv_taps(up, u1, u2, cw_ref[:, cs], cb_ref[:, cs]))
            conv_ref[0, :, cs] = up[(t_new - 2) * nb:(t_new - 1) * nb]
            conv_ref[1, :, cs] = up[(t_new - 1) * nb:]
        gate, val = mixed
        acts.append((gate * _sigmoid(gate) * val).astype(BF16))
    _ffn_finish(j, n_j, final_norm, acts, wd_ref, x1_ref, g_ref, y_ref)


def _ffn_weight_specs(d, n_j, ix):
    gate = lambda rows: pl.BlockSpec((rows, FF_TILE), ix(lambda j: (0, j)))
    val = lambda rows: pl.BlockSpec((rows, FF_TILE), ix(lambda j: (0, n_j + j)))
    return [gate(d), val(d), gate(3), val(3), gate(1), val(1),
            pl.BlockSpec((FF_TILE, d), ix(lambda j: (j, 0)))]


def _ffn_prompt(h2, x1, wu, cw, cb, wd, g, batch, seq, tm, final_norm):
    m, d = x1.shape
    d_ff = wd.shape[0]
    n_j = d_ff // FF_TILE
    tps = seq // tm
    kern = functools.partial(_ffn_prompt_kernel, tm=tm, tiles_per_seq=tps, n_j=n_j,
                             final_norm=final_norm)
    ix = lambda f: (lambda i, j: f(j))
    conv_spec = pl.BlockSpec((None, SUBLANES, FF_TILE), lambda i, j: (i, 0, j))
    conv_shape = jax.ShapeDtypeStruct((m // tm, SUBLANES, d_ff), F32)
    y, cg, cv = pl.pallas_call(
        kern,
        grid=(m // tm, n_j),
        in_specs=[pl.BlockSpec((tm, d), lambda i, j: (i, 0))] + _ffn_weight_specs(d, n_j, ix)
                 + [pl.BlockSpec((tm, d), lambda i, j: (i, 0)),
                    pl.BlockSpec((1, d), lambda i, j: (0, 0))],
        out_specs=[pl.BlockSpec((tm, d), lambda i, j: (i, 0)), conv_spec, conv_spec],
        out_shape=[jax.ShapeDtypeStruct((m, d), F32), conv_shape, conv_shape],
        scratch_shapes=[pltpu.VMEM((n_j, SUBLANES, 2 * FF_TILE), F32)],
        compiler_params=_params("arbitrary", "arbitrary"),
        name="ffn_prompt",
    )(h2, wu, wu, cw, cw, cb, cb, wd, x1, g.reshape(1, d))
    return y, jnp.concatenate([cg, cv], axis=-1)[tps - 1::tps, SUBLANES - 2:]


def _ffn_sample(h2, x1, wu, cw, cb, wd, g, state, nb, t_new, final_norm):
    m, d = x1.shape
    d_ff = wd.shape[0]
    n_j = d_ff // FF_TILE
    kern = functools.partial(_ffn_sample_kernel, nb=nb, t_new=t_new, n_j=n_j,
                             final_norm=final_norm)
    ix = lambda f: f
    s0, s1 = state[:, 0, :], state[:, 1, :]
    st_gate = pl.BlockSpec((nb, FF_TILE), lambda j: (0, j))
    st_val = pl.BlockSpec((nb, FF_TILE), lambda j: (0, n_j + j))
    conv_spec = pl.BlockSpec((2, nb, FF_TILE), lambda j: (0, 0, j))
    y, cg, cv = pl.pallas_call(
        kern,
        grid=(n_j,),
        in_specs=[pl.BlockSpec((m, d), lambda j: (0, 0))] + _ffn_weight_specs(d, n_j, ix)
                 + [pl.BlockSpec((m, d), lambda j: (0, 0)),
                    pl.BlockSpec((1, d), lambda j: (0, 0)),
                    st_gate, st_val, st_gate, st_val],
        out_specs=[pl.BlockSpec((m, d), lambda j: (0, 0)), conv_spec, conv_spec],
        out_shape=[jax.ShapeDtypeStruct((m, d), F32),
                   jax.ShapeDtypeStruct((2, nb, d_ff), F32),
                   jax.ShapeDtypeStruct((2, nb, d_ff), F32)],
        compiler_params=_params("arbitrary"),
        name="ffn_sample",
    )(h2, wu, wu, cw, cw, cb, cb, wd, x1, g.reshape(1, d), s0, s0, s1, s1)
    return y, jnp.swapaxes(jnp.concatenate([cg, cv], axis=-1), 0, 1)


def kernel(x_prompt, x_sample, cache_k, cache_v, cache_kidx, state_s5_re, state_s5_im, state_ffn_conv, page_table, norm_mix, w_in, w_attn_proj, s5_a_re, s5_a_im, s5_log_step, s5_b_re, s5_b_im, s5_c_re, s5_c_im, s5_d, w_glu, w_out, norm_ffn, w_up, conv_w, conv_b, w_down, norm_final):
    depth = w_in.shape[0]
    batch, seq, d_model = x_prompt.shape
    nb, t_new, _ = x_sample.shape
    d_head = cache_k.shape[-1]
    d_idx = cache_kidx.shape[-1]
    attn_w = N_HEADS * d_head
    kv_w = N_KV_HEADS * d_head
    qi_w = IDX_HEADS * d_idx
    groups, p_state = s5_a_re.shape[1], s5_a_re.shape[2]
    s5_w = groups * S5_GROUP

    xp = x_prompt.reshape(batch * seq, d_model)
    xs = x_sample.reshape(nb * t_new, d_model)
    tm_p = min(512, batch * seq)
    tm_s = nb * t_new

    outs = {name: [] for name in ("kp", "vp", "kip", "srp", "sip", "cp",
                                  "ks", "vs", "kis", "srs", "sis", "cs")}
    for l in range(depth):
        w_t = jnp.swapaxes(w_in[l], 0, 1).astype(BF16)
        o = 0
        seg = {}
        for name, width in (("q", attn_w), ("k", kv_w), ("v", kv_w), ("qi", qi_w), ("ki", d_idx),
                            ("wi", IDX_HEADS), ("u", s5_w), ("ga", d_model), ("gb", d_model)):
            seg[name] = w_t[o:o + width]
            o += width
        w_q_grouped = _group_major_heads(seg["q"], d_head)
        w_u, w_ga, w_gb = seg["u"], seg["ga"], seg["gb"]
        pad = jnp.zeros((LANES - d_idx - IDX_HEADS, d_model), BF16)
        w_small = jnp.concatenate([seg["qi"], seg["k"], seg["v"], seg["ki"], seg["wi"], pad], axis=0)
        w_qiw = jnp.concatenate([seg["qi"], seg["ki"], seg["wi"], pad], axis=0)
        w_kv_t = jnp.concatenate([seg["k"], seg["v"], seg["ki"]], axis=0)
        w_attn_grouped = _group_major_heads(w_attn_proj[l].astype(BF16), d_head)
        w_g = w_glu[l].astype(BF16)
        w_o = w_out[l].astype(BF16)
        wu = w_up[l].astype(BF16)
        cw = conv_w[l].astype(F32)
        cb = conv_b[l].astype(F32).reshape(1, -1)
        wd = w_down[l].astype(BF16)
        s5p = _s5_params(s5_a_re[l], s5_a_im[l], s5_log_step[l], s5_b_re[l], s5_b_im[l],
                         s5_c_re[l], s5_c_im[l], STEP_CHUNK)
        last = l == depth - 1

        def project(h, tm, ws):
            q = _matmul(h, w_q_grouped, tm, F32, "proj_q")
            u = _matmul(h, w_u, tm, F32, "proj_u")
            ga = _matmul(h, w_ga, tm, F32, "proj_ga")
            gb = _matmul(h, w_gb, tm, F32, "proj_gb")
            small = _matmul(h, ws, tm, F32, "proj_small")
            return q, u, ga, gb, small

        h = _rmsnorm(xp, norm_mix[l], tm_p, BF16)
        q, u, ga, gb, qiw = project(h, tm_p, w_qiw)
        k_t, v_t, ki_t, kb_t, vb_t, kib_t = _kv_transposed(h, w_kv_t, batch, seq, kv_w, d_idx,
                                                            min(512, seq))
        k_sel = min(TOPK_MAX, seq // 4)
        attn = _prompt_attention(q, qiw, kib_t, kb_t, vb_t, batch, seq, d_head, d_idx, k_sel,
                                 min(256, seq))
        y5, srp, sip = _s5_prompt(u, batch, seq, s5p, s5_d[l], groups, p_state)
        merged = _merge(attn, y5, ga, gb, w_attn_grouped, w_g, min(256, batch * seq))
        x1, h2 = _out_proj(merged, xp, w_o, norm_ffn[l], tm_p)
        xp, conv_p = _ffn_prompt(h2, x1, wu, cw, cb, wd, norm_final, batch, seq, tm_p, last)

        def heads_last(a_t):
            return jnp.transpose(a_t.reshape(batch, N_KV_HEADS, d_head, seq), (0, 3, 1, 2))

        outs["kp"].append(heads_last(k_t)); outs["vp"].append(heads_last(v_t))
        outs["kip"].append(jnp.swapaxes(ki_t, 1, 2))
        outs["srp"].append(srp); outs["sip"].append(sip); outs["cp"].append(conv_p)

        h = _rmsnorm(xs, norm_mix[l], tm_s, BF16)
        q, u, ga, gb, small = project(h, tm_s, w_small)
        attn = _sample_attention(q, small, cache_k[l], cache_v[l], cache_kidx[l], page_table,
                                 t_new, d_head, d_idx)
        y5, srs, sis = _s5_sample(u, nb, t_new, s5p, s5_d[l], state_s5_re[l], state_s5_im[l])
        merged = _merge(attn, y5, ga, gb, w_attn_grouped, w_g, min(256, nb * t_new))
        x1, h2 = _out_proj(merged, xs, w_o, norm_ffn[l], tm_s)

        def time_major(a):
            return a.reshape(nb, t_new, -1).swapaxes(0, 1).reshape(nb * t_new, -1)

        y_tm, conv_s = _ffn_sample(time_major(h2), time_major(x1), wu, cw, cb, wd, norm_final,
                                   state_ffn_conv[l].astype(F32), nb, t_new, last)
        xs = y_tm.reshape(t_new, nb, d_model).swapaxes(0, 1).reshape(nb * t_new, d_model)
        k_new = small[:, qi_w:qi_w + kv_w].reshape(nb, t_new, N_KV_HEADS, d_head)
        v_new = small[:, qi_w + kv_w:qi_w + 2 * kv_w].reshape(nb, t_new, N_KV_HEADS, d_head)
        ki_new = small[:, qi_w + 2 * kv_w:qi_w + 2 * kv_w + d_idx].reshape(nb, t_new, d_idx)
        outs["ks"].append(k_new); outs["vs"].append(v_new); outs["kis"].append(ki_new)
        outs["srs"].append(srs); outs["sis"].append(sis); outs["cs"].append(conv_s)

    stk = {name: jnp.stack(v) for name, v in outs.items()}
    return (xp.reshape(batch, seq, d_model), xs.reshape(nb, t_new, d_model),
            stk["kp"], stk["vp"], stk["kip"], stk["srp"], stk["sip"], stk["cp"],
            stk["ks"], stk["vs"], stk["kis"], stk["srs"], stk["sis"], stk["cs"])
```

```python
import functools
import math

import jax
import jax.numpy as jnp
from jax import lax
from jax.experimental import pallas as pl
from jax.experimental.pallas import tpu as pltpu

F32 = jnp.float32
BF16 = jnp.bfloat16

EPS = 1e-6
LOG2E = math.log2(math.e)
TOPK_MAX = 256
N_HEADS = 16
N_KV_HEADS = 4
KV_GROUP = N_HEADS // N_KV_HEADS
IDX_HEADS = 8
S5_GROUP = 16
STEP_CHUNK = 8
LANES = 128
SUBLANES = 8
GROUPS_PER_TILE = LANES // S5_GROUP
FF_TILE = 512
VMEM_LIMIT = 48 * 1024 * 1024
NEG_INF = float("-inf")
INT_MIN = -2 ** 31


def _params(*sem):
    return pltpu.CompilerParams(dimension_semantics=sem, vmem_limit_bytes=VMEM_LIMIT)


def _sigmoid(x):
    return 0.5 * jnp.tanh(0.5 * x) + 0.5


def _gelu_tanh(x):
    c = math.sqrt(2.0 / math.pi)
    return 0.5 * x * (1.0 + jnp.tanh(c * (x + 0.044715 * (x * x * x))))


def _dot_nt(a, b):
    return lax.dot_general(a, b, (((1,), (1,)), ((), ())), preferred_element_type=F32)


def _rms_kernel(x_ref, g_ref, o_ref):
    x = x_ref[...]
    ms = jnp.mean(x * x, axis=-1, keepdims=True)
    o_ref[...] = (x * lax.rsqrt(ms + EPS) * g_ref[...]).astype(o_ref.dtype)


def _rmsnorm(x, g, tm, out_dtype):
    m, d = x.shape
    return pl.pallas_call(
        _rms_kernel,
        grid=(m // tm,),
        in_specs=[pl.BlockSpec((tm, d), lambda i: (i, 0)),
                  pl.BlockSpec((1, d), lambda i: (0, 0))],
        out_specs=pl.BlockSpec((tm, d), lambda i: (i, 0)),
        out_shape=jax.ShapeDtypeStruct((m, d), out_dtype),
        compiler_params=_params("arbitrary"),
        name="rmsnorm",
    )(x, g.reshape(1, d))


def _mm_kernel(h_ref, *refs):
    n = len(refs) // 2
    h = h_ref[...]
    for w_ref, o_ref in zip(refs[:n], refs[n:]):
        o_ref[...] = _dot_nt(h, w_ref[...]).astype(o_ref.dtype)


def _matmul(h, weights_t, tm, out_dtype, name):
    m, k = h.shape
    return pl.pallas_call(
        _mm_kernel,
        grid=(m // tm,),
        in_specs=[pl.BlockSpec((tm, k), lambda i: (i, 0))]
                 + [pl.BlockSpec(w.shape, lambda i: (0, 0)) for w in weights_t],
        out_specs=[pl.BlockSpec((tm, w.shape[0]), lambda i: (i, 0)) for w in weights_t],
        out_shape=[jax.ShapeDtypeStruct((m, w.shape[0]), out_dtype) for w in weights_t],
        compiler_params=_params("arbitrary"),
        name=name,
    )(h, *weights_t)


def _kv_t_kernel(w_ref, h_ref, k_ref, v_ref, ki_ref, kb_ref, vb_ref, kib_ref, *, kv_w):
    out = _dot_nt(w_ref[...], h_ref[...])
    for lo, hi, full_ref, half_ref in ((0, kv_w, k_ref, kb_ref), (kv_w, 2 * kv_w, v_ref, vb_ref),
                                       (2 * kv_w, out.shape[0], ki_ref, kib_ref)):
        full_ref[...] = out[lo:hi]
        half_ref[...] = out[lo:hi].astype(BF16)


def _kv_transposed(h, w_t, batch, seq, kv_w, d_idx, tn):
    d = h.shape[1]
    nt = seq // tn
    out = lambda rows: pl.BlockSpec((None, rows, tn), lambda b, i: (b, 0, i))
    shape = lambda rows, dt: jax.ShapeDtypeStruct((batch, rows, seq), dt)
    widths = (kv_w, kv_w, d_idx)
    return pl.pallas_call(
        functools.partial(_kv_t_kernel, kv_w=kv_w),
        grid=(batch, nt),
        in_specs=[pl.BlockSpec(w_t.shape, lambda b, i: (0, 0)),
                  pl.BlockSpec((tn, d), lambda b, i: (b * nt + i, 0))],
        out_specs=[out(w) for w in widths] * 2,
        out_shape=[shape(w, F32) for w in widths] + [shape(w, BF16) for w in widths],
        compiler_params=_params("arbitrary", "arbitrary"),
        name="proj_kv_transposed",
    )(w_t, h)


SEARCH_GROUP = 4


def _count(mask):
    return jnp.sum(jnp.where(mask, 1.0, 0.0), axis=1, keepdims=True)


def _topk_mask(score, col, k_sel, n_cols):
    kf = float(k_sel)
    rows = score.shape[0]

    def as_float(key):
        bits = jnp.where(key < 0, key ^ jnp.int32(0x7FFFFFFF), key)
        return lax.bitcast_convert_type(bits, F32)

    finite = score > NEG_INF
    few = _count(finite) <= kf

    cnt0 = _count(score >= 0.0)
    cand0 = jnp.where(cnt0 >= kf, jnp.int32(0), jnp.int32(INT_MIN))
    cand0 = jnp.broadcast_to(cand0, (rows, 1)).astype(jnp.int32)
    cnt0 = jnp.where(cnt0 >= kf, cnt0, float(n_cols))

    def unresolved(cnt):
        return jnp.max(jnp.where(few, 0.0, cnt - kf)) > 0.0

    def search_cond(carry):
        it, _, cnt = carry
        return jnp.logical_and(it < 31, unresolved(cnt))

    def search_body(carry):
        it, cand, cnt = carry
        for g in range(SEARCH_GROUP):
            shift = jnp.int32(30) - (it + g)
            bit = jnp.where(shift >= 0, lax.shift_left(jnp.int32(1), jnp.maximum(shift, 0)), 0)
            trial = cand + bit
            c_trial = _count(score >= as_float(trial))
            ok = c_trial >= kf
            cand = jnp.where(ok, trial, cand)
            cnt = jnp.where(ok, c_trial, cnt)
        return it + SEARCH_GROUP, cand, cnt

    _, cand, cnt = lax.while_loop(search_cond, search_body, (jnp.int32(0), cand0, cnt0))
    thr = as_float(cand)

    above = score > thr
    ties = score == thr
    need = kf - _count(above)
    surplus = jnp.where(few, 0.0, cnt - kf)
    n_bits = max(1, (n_cols - 1).bit_length())

    def index_step(it, m):
        trial = m + lax.shift_left(jnp.int32(1), jnp.int32(n_bits - 1) - it)
        taken = _count(jnp.logical_and(ties, col < trial))
        return jnp.where(taken <= need - 1.0, trial, m)

    def lowest_indices(_):
        return lax.fori_loop(0, n_bits, index_step, jnp.zeros((rows, 1), jnp.int32))

    def all_ties(_):
        return jnp.full((rows, 1), n_cols, jnp.int32)

    last = lax.cond(jnp.max(surplus) > 0.0, lowest_indices, all_ties, None)
    take_tie = jnp.logical_and(jnp.logical_and(ties, col <= last), need >= 1.0)
    top = jnp.logical_or(above, take_tie)
    return jnp.logical_and(finite, jnp.logical_or(few, top))


def _index_weights(kw, d_idx):
    w_scale = (d_idx ** -0.5) * (IDX_HEADS ** -0.5)
    return [kw[:, d_idx + h:d_idx + h + 1] * w_scale for h in range(IDX_HEADS)]


def _head_rows(qi, d_idx):
    return jnp.concatenate([qi[:, h * d_idx:(h + 1) * d_idx] for h in range(IDX_HEADS)], axis=0)


def _weighted_relu_sum(lg, w_cols, rows):
    score = jnp.zeros((rows, lg.shape[1]), F32)
    for h in range(IDX_HEADS):
        score = score + jnp.maximum(lg[h * rows:(h + 1) * rows], 0.0) * w_cols[h]
    return score


HEADS_PER_DOT = 2


PROMPT_SCORE_ELEMS = 512 * 1024


def _prompt_attn_kernel(q_ref, qi_ref, kw_ref, kit_ref, kt_ref, vt_ref, prev_ref, o_ref,
                        *, tq, n_keys, k_sel, d_head, d_idx, q_block):
    del prev_ref
    n_seqs = q_ref.shape[0]
    scores = []
    for b in range(n_seqs):
        qi = qi_ref[b]
        w_cols = _index_weights(kw_ref[b], d_idx)
        kit = kit_ref[b]
        score = jnp.zeros((tq, n_keys), F32)
        for h in range(IDX_HEADS):
            lg = jnp.dot(qi[:, h * d_idx:(h + 1) * d_idx].astype(BF16), kit,
                         preferred_element_type=F32)
            score = score + jnp.maximum(lg, 0.0) * w_cols[h]
        scores.append(score)
    score = jnp.concatenate(scores, axis=0)

    col = lax.broadcasted_iota(jnp.int32, score.shape, 1)
    row = lax.broadcasted_iota(jnp.int32, (n_seqs, tq, n_keys), 1).reshape(score.shape)
    score = jnp.where(col <= q_block * tq + row, score, NEG_INF)
    bias_all = jnp.where(_topk_mask(score, col, k_sel, n_keys), 0.0, NEG_INF)

    for b in range(n_seqs):
        bias = bias_all[b * tq:(b + 1) * tq]
        q = q_ref[b] * (d_head ** -0.5 * LOG2E)
        for n in range(N_KV_HEADS):
            kt = kt_ref[b, n * d_head:(n + 1) * d_head, :]
            vt = vt_ref[b, n * d_head:(n + 1) * d_head, :]
            for g0 in range(0, KV_GROUP, HEADS_PER_DOT):
                heads = [(g0 + i) * N_KV_HEADS + n for i in range(HEADS_PER_DOT)]
                qs = jnp.concatenate([q[:, h * d_head:(h + 1) * d_head] for h in heads], axis=0)
                s = jnp.dot(qs.astype(BF16), kt, preferred_element_type=F32)
                s = (s.reshape(HEADS_PER_DOT, tq, n_keys) + bias[None]).reshape(-1, n_keys)
                m = jnp.max(s, axis=1, keepdims=True)
                p = jnp.exp2(s - m)
                l = jnp.sum(p, axis=1, keepdims=True)
                o = _dot_nt(p.astype(BF16), vt) / l
                for i, h in enumerate(heads):
                    o_ref[b, :, h * d_head:(h + 1) * d_head] = (
                        o[i * tq:(i + 1) * tq].astype(o_ref.dtype))


def _prompt_attention(q, qiw, ki_t, k_t, v_t, batch, seq, d_head, d_idx, k_sel, tq):
    attn_w = q.shape[1]
    kv_w = N_KV_HEADS * d_head
    qi_w = IDX_HEADS * d_idx
    nq = seq // tq
    q3 = q.reshape(batch, seq, attn_w)
    qiw3 = qiw.reshape(batch, seq, qiw.shape[1])
    out = jnp.zeros((batch, seq, attn_w), BF16)
    for qb in range(nq):
        n_keys = (qb + 1) * tq
        n_seqs = max(d for d in range(1, batch + 1)
                     if batch % d == 0 and (d == 1 or d * tq * n_keys <= PROMPT_SCORE_ELEMS))
        row = lambda w, c, qb=qb, n_seqs=n_seqs: pl.BlockSpec((n_seqs, tq, w), lambda b: (b, qb, c))
        keys = lambda w, n_keys=n_keys, n_seqs=n_seqs: pl.BlockSpec((n_seqs, w, n_keys),
                                                                     lambda b: (b, 0, 0))
        out = pl.pallas_call(
            functools.partial(_prompt_attn_kernel, tq=tq, n_keys=n_keys, k_sel=k_sel,
                              d_head=d_head, d_idx=d_idx, q_block=qb),
            grid=(batch // n_seqs,),
            in_specs=[row(attn_w, 0), row(qi_w, 0), row(LANES, qi_w // LANES),
                      keys(d_idx), keys(kv_w), keys(kv_w),
                      pl.BlockSpec(memory_space=pl.ANY)],
            out_specs=row(attn_w, 0),
            out_shape=jax.ShapeDtypeStruct((batch, seq, attn_w), BF16),
            input_output_aliases={6: 0},
            compiler_params=_params("arbitrary"),
            name=f"prompt_attention_q{qb}",
        )(q3, qiw3, qiw3, ki_t, k_t, v_t, out)
    return out.reshape(batch * seq, attn_w)


def _lane_columns(cols, width):
    rows = cols[0].shape[0]
    lane = lax.broadcasted_iota(jnp.int32, (rows, width), 1)
    out = jnp.zeros((rows, width), F32)
    for j, c in enumerate(cols):
        out = jnp.where(lane == j, c, out)
    return out


def _sample_score_kernel(pt_ref, qi_ref, kw_ref, kin_ref, *rest, n_seqs, n_pages, page, t_new,
                         d_idx):
    del pt_ref
    s_ref = rest[n_seqs * n_pages]
    past = n_pages * page
    rows = SUBLANES
    pad_row = lax.broadcasted_iota(jnp.int32, (rows, page), 0) >= t_new
    first = lax.broadcasted_iota(jnp.int32, (rows, page), 1) == 0
    lane = lax.broadcasted_iota(jnp.int32, (rows, LANES), 1)
    t = lax.broadcasted_iota(jnp.int32, (rows, LANES), 0)
    causal_new = jnp.logical_and(lane <= t, t < t_new)
    for i in range(n_seqs):
        ipages = rest[i * n_pages:(i + 1) * n_pages]
        qi_rows = _head_rows(qi_ref[i], d_idx)
        qi_bf = qi_rows.astype(BF16)
        w_cols = _index_weights(kw_ref[i], d_idx)
        for p in range(n_pages):
            lg = jnp.dot(qi_bf, ipages[p][...].astype(BF16), preferred_element_type=F32)
            pad_val = jnp.where(first, 0.0, NEG_INF) if p == 0 else NEG_INF
            s_ref[i, :, p * page:(p + 1) * page] = jnp.where(
                pad_row, pad_val, _weighted_relu_sum(lg, w_cols, rows))
        kin = kin_ref[i]
        new_cols = []
        for j in range(t_new):
            lg = jnp.sum(qi_rows * kin[j:j + 1, :], axis=1, keepdims=True)
            new_cols.append(_weighted_relu_sum(lg, w_cols, rows))
        s_ref[i, :, past:] = jnp.where(causal_new, _lane_columns(new_cols, LANES), NEG_INF)


def _select_kernel(s_ref, b_ref, *, k_sel):
    score = s_ref[...]
    col = lax.broadcasted_iota(jnp.int32, score.shape, 1)
    keep = _topk_mask(score, col, k_sel, score.shape[1])
    b_ref[...] = jnp.where(keep, 0.0, NEG_INF)


SCORE_SEQS_PER_STEP = 4
ATTEND_SEQS_PER_STEP = 2


def _sample_attn_kernel(pt_ref, q_ref, kn_ref, vn_ref, bias_ref, *rest,
                        n_seqs, n_pages, page, t_new, d_head):
    del pt_ref
    pages = rest[:2 * n_seqs * n_pages]
    o_ref, kt_ref, vt_ref = rest[2 * n_seqs * n_pages:]
    for i in range(n_seqs):
        kpages = pages[i * n_pages:(i + 1) * n_pages]
        vpages = pages[(n_seqs + i) * n_pages:(n_seqs + i + 1) * n_pages]
        o_ref[i] = _sample_attend(q_ref[i], kn_ref[i], vn_ref[i], bias_ref[i], kpages, vpages,
                                  kt_ref.at[i], vt_ref.at[i], page, t_new, d_head
                                  ).astype(o_ref.dtype)


def _sample_attend(q, kn, vn, bias, kpages, vpages, kt_ref, vt_ref, page, t_new, d_head):
    n_pages = len(kpages)
    past = n_pages * page
    rows = SUBLANES
    kv_w = N_KV_HEADS * d_head

    for p in range(n_pages):
        kt_ref[:, p * page:(p + 1) * page] = kpages[p][...].astype(BF16)
        vt_ref[:, p * page:(p + 1) * page] = vpages[p][...].astype(BF16)

    bias = jnp.concatenate([bias] * N_HEADS, axis=0)

    q = q * (d_head ** -0.5 * LOG2E)
    lane_head = lax.broadcasted_iota(jnp.int32, (rows, kv_w), 1) // d_head
    q_rows = jnp.concatenate(
        [jnp.where(lane_head == n, q[:, g * kv_w:(g + 1) * kv_w], 0.0)
         for g in range(KV_GROUP) for n in range(N_KV_HEADS)], axis=0)

    s_past = jnp.dot(q_rows.astype(BF16), kt_ref[...], preferred_element_type=F32)
    s_past = s_past + bias[:, :past]
    s_new = _lane_columns(
        [jnp.sum(q_rows * kn[j:j + 1, :], axis=1, keepdims=True) for j in range(t_new)], LANES)
    s_new = s_new + bias[:, past:]

    m = jnp.maximum(jnp.max(s_past, axis=1, keepdims=True), jnp.max(s_new, axis=1, keepdims=True))
    p_past = jnp.exp2(s_past - m)
    p_new = jnp.exp2(s_new - m)
    l = jnp.sum(p_past, axis=1, keepdims=True) + jnp.sum(p_new, axis=1, keepdims=True)
    o = _dot_nt(p_past.astype(BF16), vt_ref[...])
    for j in range(t_new):
        o = o + p_new[:, j:j + 1] * vn[j:j + 1, :]
    o = o / l

    chunks = []
    for g in range(KV_GROUP):
        acc = jnp.zeros((rows, kv_w), F32)
        for n in range(N_KV_HEADS):
            r0 = (g * N_KV_HEADS + n) * rows
            acc = acc + jnp.where(lane_head == n, o[r0:r0 + rows], 0.0)
        chunks.append(acc)
    return jnp.concatenate(chunks, axis=1)


def _seq_page_map(b, pt, *, i, p, n_seqs, n_pages):
    return (pt[(b * n_seqs + i) * n_pages + p], 0, 0)


def _sample_attention(q, small, cache_k, cache_v, cache_ki, page_table, t_new, d_head, d_idx):
    nb, n_pages = page_table.shape
    n_pool, page = cache_k.shape[0], cache_k.shape[1]
    attn_w = q.shape[1]
    kv_w = N_KV_HEADS * d_head
    qi_w = IDX_HEADS * d_idx
    rows = SUBLANES
    past = n_pages * page
    k_sel = min(TOPK_MAX, (past + t_new) // 4)

    def pad_rows(a):
        a = a.reshape(nb, t_new, a.shape[-1])
        return jnp.pad(a, ((0, 0), (0, rows - t_new), (0, 0)))

    q8 = pad_rows(q)
    qi8 = pad_rows(small[:, :qi_w])
    kw8 = pad_rows(small[:, qi_w + 2 * kv_w:])
    kn = pad_rows(small[:, qi_w:qi_w + kv_w])
    vn = pad_rows(small[:, qi_w + kv_w:qi_w + 2 * kv_w])
    kin = pad_rows(small[:, qi_w + 2 * kv_w:qi_w + 2 * kv_w + d_idx])

    ck = jnp.transpose(cache_k, (0, 2, 3, 1)).reshape(n_pool, kv_w, page)
    cv = jnp.transpose(cache_v, (0, 2, 3, 1)).reshape(n_pool, kv_w, page)
    ci = jnp.transpose(cache_ki, (0, 2, 1))

    pt_flat = page_table.reshape(-1)
    n_keys = past + LANES

    def seqs_spec(w, n_seqs):
        return pl.BlockSpec((n_seqs, rows, w), lambda b, pt: (b, 0, 0))

    def seq_page_specs(w, n_seqs):
        return [pl.BlockSpec((None, w, page),
                             functools.partial(_seq_page_map, i=i, p=p, n_seqs=n_seqs,
                                               n_pages=n_pages))
                for i in range(n_seqs) for p in range(n_pages)]

    n_seqs = math.gcd(nb, SCORE_SEQS_PER_STEP)
    score = pl.pallas_call(
        functools.partial(_sample_score_kernel, n_seqs=n_seqs, n_pages=n_pages, page=page,
                          t_new=t_new, d_idx=d_idx),
        grid_spec=pltpu.PrefetchScalarGridSpec(
            num_scalar_prefetch=1,
            grid=(nb // n_seqs,),
            in_specs=[seqs_spec(qi_w, n_seqs), seqs_spec(LANES, n_seqs), seqs_spec(d_idx, n_seqs)]
                     + seq_page_specs(d_idx, n_seqs),
            out_specs=seqs_spec(n_keys, n_seqs),
        ),
        out_shape=jax.ShapeDtypeStruct((nb, rows, n_keys), F32),
        compiler_params=_params("arbitrary"),
        name="sample_scores",
    )(pt_flat, qi8, kw8, kin, *([ci] * (n_seqs * n_pages)))

    sel_rows = math.gcd(nb * rows, 256)
    bias = pl.pallas_call(
        functools.partial(_select_kernel, k_sel=k_sel),
        grid=(nb * rows // sel_rows,),
        in_specs=[pl.BlockSpec((sel_rows, n_keys), lambda i: (i, 0))],
        out_specs=pl.BlockSpec((sel_rows, n_keys), lambda i: (i, 0)),
        out_shape=jax.ShapeDtypeStruct((nb * rows, n_keys), F32),
        compiler_params=_params("arbitrary"),
        name="sample_select",
    )(score.reshape(nb * rows, n_keys)).reshape(nb, rows, n_keys)

    n_seqs = math.gcd(nb, ATTEND_SEQS_PER_STEP)
    out = pl.pallas_call(
        functools.partial(_sample_attn_kernel, n_seqs=n_seqs, n_pages=n_pages, page=page,
                          t_new=t_new, d_head=d_head),
        grid_spec=pltpu.PrefetchScalarGridSpec(
            num_scalar_prefetch=1,
            grid=(nb // n_seqs,),
            in_specs=[seqs_spec(attn_w, n_seqs), seqs_spec(kv_w, n_seqs), seqs_spec(kv_w, n_seqs),
                      seqs_spec(n_keys, n_seqs)]
                     + seq_page_specs(kv_w, n_seqs) + seq_page_specs(kv_w, n_seqs),
            out_specs=seqs_spec(attn_w, n_seqs),
            scratch_shapes=[pltpu.VMEM((n_seqs, kv_w, past), BF16),
                            pltpu.VMEM((n_seqs, kv_w, past), BF16)],
        ),
        out_shape=jax.ShapeDtypeStruct((nb, rows, attn_w), BF16),
        compiler_params=_params("arbitrary"),
        name="sample_attention",
    )(pt_flat, q8, kn, vn, bias, *([ck] * (n_seqs * n_pages)), *([cv] * (n_seqs * n_pages)))
    return out[:, :t_new].reshape(nb * t_new, attn_w)


def _group_major_heads(a, d_head):
    n = a.shape[1]
    a = a.reshape(N_KV_HEADS, KV_GROUP, d_head, n)
    return jnp.swapaxes(a, 0, 1).reshape(N_HEADS * d_head, n)


def _s5_step(xr, xi, ar, ai, bu, half):
    br, bi = bu[:, :half], bu[:, half:]
    if xr is None:
        return br, bi
    return ar * xr - ai * xi + br, ar * xi + ai * xr + bi


def _s5_inputs(u_ref, b_ref, j):
    tr, steps, _ = u_ref.shape
    x = jnp.transpose(u_ref[:, :, j * LANES:(j + 1) * LANES], (1, 0, 2))
    us = [x[s] for s in range(steps)]
    bu = jnp.dot(jnp.concatenate(us, axis=0).astype(BF16), b_ref[j], preferred_element_type=F32)
    return us, [bu[s * tr:(s + 1) * tr] for s in range(steps)]


def _s5_local_kernel(u_ref, b_ref, ar_ref, ai_ref, z_ref, *, n_tiles):
    sw = b_ref.shape[2]
    half = sw // 2
    for j in range(n_tiles):
        ar, ai = ar_ref[j:j + 1, :], ai_ref[j:j + 1, :]
        xr = xi = None
        for bu in _s5_inputs(u_ref, b_ref, j)[1]:
            xr, xi = _s5_step(xr, xi, ar, ai, bu, half)
        z_ref[:, j * sw:j * sw + half] = xr
        z_ref[:, j * sw + half:(j + 1) * sw] = xi


def _s5_carry_kernel(z_ref, ar_ref, ai_ref, xc_ref, xf_ref, *, batch, n_chunks):
    half = z_ref.shape[1] // 2
    ar, ai = ar_ref[...], ai_ref[...]

    def body(c, carry):
        new = []
        for b in range(batch):
            xr, xi = carry[b]
            row = b * n_chunks + c
            xc_ref[pl.ds(row, 1), :] = jnp.concatenate([xr, xi], axis=1)
            z = z_ref[pl.ds(row, 1), :]
            new.append((ar * xr - ai * xi + z[:, :half], ar * xi + ai * xr + z[:, half:]))
        return tuple(new)

    zero = jnp.zeros((1, half), F32)
    final = lax.fori_loop(0, n_chunks, body, tuple((zero, zero) for _ in range(batch)),
                          unroll=math.gcd(n_chunks, 4))
    for b in range(batch):
        xf_ref[b:b + 1, :] = jnp.concatenate(list(final[b]), axis=1)


def _s5_out_kernel(u_ref, x0_ref, b_ref, c_ref, ar_ref, ai_ref, d_ref, y_ref, *maybe_xf,
                   n_tiles):
    tr, steps, _ = u_ref.shape
    sw = b_ref.shape[2]
    half = sw // 2
    for j in range(n_tiles):
        ar, ai = ar_ref[j:j + 1, :], ai_ref[j:j + 1, :]
        xr = x0_ref[:, j * sw:j * sw + half]
        xi = x0_ref[:, j * sw + half:(j + 1) * sw]
        d = d_ref[:, j * LANES:(j + 1) * LANES]
        us, bus = _s5_inputs(u_ref, b_ref, j)
        xs = []
        for bu in bus:
            xr, xi = _s5_step(xr, xi, ar, ai, bu, half)
            xs.append(jnp.concatenate([xr, xi], axis=1).astype(BF16))
        y = jnp.dot(jnp.concatenate(xs, axis=0), c_ref[j], preferred_element_type=F32)
        ys = jnp.stack([y[s * tr:(s + 1) * tr] + d * us[s] for s in range(steps)], axis=0)
        y_ref[:, :, j * LANES:(j + 1) * LANES] = jnp.transpose(ys, (1, 0, 2))
        if maybe_xf:
            maybe_xf[0][:, j * sw:j * sw + half] = xr
            maybe_xf[0][:, j * sw + half:(j + 1) * sw] = xi


def _s5_params(a_re, a_im, log_step, b_re, b_im, c_re, c_im, chunk_steps):
    g, p = a_re.shape
    nt = g // GROUPS_PER_TILE
    a_re, a_im = a_re.astype(F32), a_im.astype(F32)
    step = jnp.exp(log_step.astype(F32))[:, None]

    def discretise(n_steps):
        mag = jnp.exp(a_re * step * n_steps)
        return mag * jnp.cos(a_im * step * n_steps), mag * jnp.sin(a_im * step * n_steps)

    lr, li = discretise(1.0)
    cr, ci = discretise(float(chunk_steps))
    den = a_re * a_re + a_im * a_im
    fr = ((lr - 1.0) * a_re + li * a_im) / den
    fi = (li * a_re - (lr - 1.0) * a_im) / den
    b_re, b_im = b_re.astype(F32), b_im.astype(F32)
    bb_re = fr[..., None] * b_re - fi[..., None] * b_im
    bb_im = fr[..., None] * b_im + fi[..., None] * b_re
    eye = jnp.eye(GROUPS_PER_TILE, dtype=F32)

    bb = jnp.stack([bb_re, bb_im], axis=0).reshape(2, nt, GROUPS_PER_TILE, p, S5_GROUP)
    bb = jnp.transpose(bb, (1, 2, 4, 0, 3))
    b_blk = bb[:, :, :, :, None, :] * eye[None, :, None, None, :, None]
    b_blk = b_blk.reshape(nt, LANES, 2 * GROUPS_PER_TILE * p).astype(BF16)

    cc = jnp.stack([c_re.astype(F32), -c_im.astype(F32)], axis=0)
    cc = cc.reshape(2, nt, GROUPS_PER_TILE, S5_GROUP, p)
    cc = jnp.transpose(cc, (1, 0, 2, 4, 3))
    c_blk = cc[:, :, :, :, None, :] * eye[None, None, :, None, :, None]
    c_blk = c_blk.reshape(nt, 2 * GROUPS_PER_TILE * p, LANES).astype(BF16)

    def tiles(z):
        return z.reshape(nt, GROUPS_PER_TILE * p)

    return b_blk, c_blk, tiles(lr), tiles(li), tiles(cr), tiles(ci)


def _state_to_tiles(s_re, s_im):
    nb, g, p = s_re.shape
    nt = g // GROUPS_PER_TILE
    st = jnp.stack([s_re.reshape(nb, nt, GROUPS_PER_TILE * p),
                    s_im.reshape(nb, nt, GROUPS_PER_TILE * p)], axis=2)
    return st.reshape(nb, nt * 2 * GROUPS_PER_TILE * p)


def _tiles_to_state(x, g, p):
    nb = x.shape[0]
    nt = g // GROUPS_PER_TILE
    st = x.reshape(nb, nt, 2, GROUPS_PER_TILE, p)
    return st[:, :, 0].reshape(nb, g, p), st[:, :, 1].reshape(nb, g, p)


def _s5_outputs(u3, x0, prm, d_skip, tr, want_final):
    b_blk, c_blk, ar, ai = prm
    r, steps, width = u3.shape
    nt, _, sw = b_blk.shape
    kern = functools.partial(_s5_out_kernel, n_tiles=nt)
    full3 = lambda a: pl.BlockSpec(a.shape, lambda i: (0, 0, 0))
    full2 = lambda a: pl.BlockSpec(a.shape, lambda i: (0, 0))
    d2 = d_skip.reshape(1, width).astype(F32)
    tokens = pl.BlockSpec((tr, steps, width), lambda i: (i, 0, 0))
    out_shape = [jax.ShapeDtypeStruct(u3.shape, F32)]
    out_specs = [tokens]
    if want_final:
        out_shape.append(jax.ShapeDtypeStruct((r, nt * sw), F32))
        out_specs.append(pl.BlockSpec((tr, nt * sw), lambda i: (i, 0)))
    return pl.pallas_call(
        kern,
        grid=(r // tr,),
        in_specs=[tokens, pl.BlockSpec((tr, nt * sw), lambda i: (i, 0)),
                  full3(b_blk), full3(c_blk), full2(ar), full2(ai), full2(d2)],
        out_specs=out_specs,
        out_shape=out_shape,
        compiler_params=_params("arbitrary"),
        name="s5_outputs",
    )(u3, x0, b_blk, c_blk, ar, ai, d2)


def _s5_prompt(u, batch, seq, prm_all, d_skip, g, p):
    b_blk, c_blk, ar, ai, ar_c, ai_c = prm_all
    steps = STEP_CHUNK
    n_chunks = seq // steps
    r = batch * n_chunks
    width = u.shape[1]
    u3 = u.reshape(r, steps, width)
    nt, _, sw = b_blk.shape
    tr = 128
    z = pl.pallas_call(
        functools.partial(_s5_local_kernel, n_tiles=nt),
        grid=(r // tr,),
        in_specs=[pl.BlockSpec((tr, steps, width), lambda i: (i, 0, 0)),
                  pl.BlockSpec(b_blk.shape, lambda i: (0, 0, 0)),
                  pl.BlockSpec(ar.shape, lambda i: (0, 0)),
                  pl.BlockSpec(ai.shape, lambda i: (0, 0))],
        out_specs=pl.BlockSpec((tr, nt * sw), lambda i: (i, 0)),
        out_shape=jax.ShapeDtypeStruct((r, nt * sw), F32),
        compiler_params=_params("arbitrary"),
        name="s5_chunk_states",
    )(u3, b_blk, ar, ai)
    half = sw // 2
    xc, xf = pl.pallas_call(
        functools.partial(_s5_carry_kernel, batch=batch, n_chunks=n_chunks),
        grid=(nt,),
        in_specs=[pl.BlockSpec((r, sw), lambda j: (0, j)),
                  pl.BlockSpec((None, 1, half), lambda j: (j, 0, 0)),
                  pl.BlockSpec((None, 1, half), lambda j: (j, 0, 0))],
        out_specs=[pl.BlockSpec((r, sw), lambda j: (0, j)),
                   pl.BlockSpec((batch, sw), lambda j: (0, j))],
        out_shape=[jax.ShapeDtypeStruct((r, nt * sw), F32),
                   jax.ShapeDtypeStruct((batch, nt * sw), F32)],
        compiler_params=_params("arbitrary"),
        name="s5_carry",
    )(z, ar_c.reshape(nt, 1, half), ai_c.reshape(nt, 1, half))
    (y3,) = _s5_outputs(u3, xc, (b_blk, c_blk, ar, ai), d_skip, tr, False)
    s_re, s_im = _tiles_to_state(xf, g, p)
    return y3.reshape(batch * seq, width), s_re, s_im


def _s5_sample(u, nb, t_new, prm_all, d_skip, s_re, s_im):
    b_blk, c_blk, ar, ai, _, _ = prm_all
    g, p = s_re.shape[1], s_re.shape[2]
    width = u.shape[1]
    x0 = _state_to_tiles(s_re.astype(F32), s_im.astype(F32))
    y3, xf = _s5_outputs(u.reshape(nb, t_new, width), x0, (b_blk, c_blk, ar, ai), d_skip, nb, True)
    n_re, n_im = _tiles_to_state(xf, g, p)
    return y3.reshape(nb * t_new, width), n_re, n_im


def _merge_kernel(attn_ref, y_ref, ga_ref, gb_ref, wa_ref, wg_ref, o_ref):
    d = o_ref.shape[1]
    y_a = jnp.dot(attn_ref[...], wa_ref[...], preferred_element_type=F32)
    z = _gelu_tanh(y_ref[...]).astype(BF16)
    glu = jnp.dot(z, wg_ref[...], preferred_element_type=F32)
    y_b = glu[:, :d] * _sigmoid(glu[:, d:])
    o_ref[...] = (_sigmoid(ga_ref[...]) * y_a + _sigmoid(gb_ref[...]) * y_b).astype(o_ref.dtype)


def _merge(attn, y, ga, gb, w_attn, w_glu, tm):
    m, d = ga.shape
    row = lambda w: pl.BlockSpec((tm, w), lambda i: (i, 0))
    return pl.pallas_call(
        _merge_kernel,
        grid=(m // tm,),
        in_specs=[row(attn.shape[1]), row(y.shape[1]), row(d), row(d),
                  pl.BlockSpec(w_attn.shape, lambda i: (0, 0)),
                  pl.BlockSpec(w_glu.shape, lambda i: (0, 0))],
        out_specs=row(d),
        out_shape=jax.ShapeDtypeStruct((m, d), BF16),
        compiler_params=_params("arbitrary"),
        name="merge",
    )(attn, y, ga, gb, w_attn, w_glu)


def _out_proj_kernel(m_ref, x_ref, w_ref, g_ref, x1_ref, h_ref):
    x1 = x_ref[...] + jnp.dot(m_ref[...], w_ref[...], preferred_element_type=F32)
    x1_ref[...] = x1
    ms = jnp.mean(x1 * x1, axis=-1, keepdims=True)
    h_ref[...] = (x1 * lax.rsqrt(ms + EPS) * g_ref[...]).astype(h_ref.dtype)


def _out_proj(merged, x, w_out, g, tm):
    m, d = x.shape
    row = pl.BlockSpec((tm, d), lambda i: (i, 0))
    return pl.pallas_call(
        _out_proj_kernel,
        grid=(m // tm,),
        in_specs=[row, row, pl.BlockSpec(w_out.shape, lambda i: (0, 0)),
                  pl.BlockSpec((1, d), lambda i: (0, 0))],
        out_specs=[row, row],
        out_shape=[jax.ShapeDtypeStruct((m, d), F32), jax.ShapeDtypeStruct((m, d), BF16)],
        compiler_params=_params("arbitrary"),
        name="out_proj",
    )(merged, x, w_out, g.reshape(1, d))


FF_SUB = 2


def _shift_rows(up, prev, k):
    body = pltpu.roll(up, k, axis=0)
    row = lax.broadcasted_iota(jnp.int32, prev.shape, 0)
    head = jnp.where(row < k, pltpu.roll(prev, k, axis=0), body[:SUBLANES])
    return jnp.concatenate([head, body[SUBLANES:]], axis=0)


def _conv_taps(up, u1, u2, cw, cb):
    return cb + cw[0:1, :] * u2 + cw[1:2, :] * u1 + cw[2:3, :] * up


def _ffn_finish(j, n_j, final_norm, acts, wd_ref, x1_ref, g_ref, y_ref):
    y_ref[...] += jnp.dot(jnp.concatenate(acts, axis=1), wd_ref[...], preferred_element_type=F32)

    @pl.when(j == n_j - 1)
    def _():
        x2 = x1_ref[...] + y_ref[...]
        if final_norm:
            ms = jnp.mean(x2 * x2, axis=-1, keepdims=True)
            x2 = x2 * lax.rsqrt(ms + EPS) * g_ref[...]
        y_ref[...] = x2


def _ffn_prompt_kernel(h_ref, wg_ref, wv_ref, cwg_ref, cwv_ref, cbg_ref, cbv_ref, wd_ref, x1_ref,
                       g_ref, y_ref, convg_ref, convv_ref, carry_ref,
                       *, tm, tiles_per_seq, n_j, final_norm):
    i, j = pl.program_id(0), pl.program_id(1)

    @pl.when(i % tiles_per_seq == 0)
    def _():
        carry_ref[j] = jnp.zeros(carry_ref.shape[1:], F32)

    @pl.when(j == 0)
    def _():
        y_ref[...] = jnp.zeros(y_ref.shape, F32)

    h = h_ref[...]
    prev = carry_ref[j]
    sub = FF_TILE // FF_SUB
    acts, tails = [], ([], [])
    for s in range(FF_SUB):
        cs = slice(s * sub, (s + 1) * sub)
        mixed = []
        for half, (w_ref, cw_ref, cb_ref) in enumerate(((wg_ref, cwg_ref, cbg_ref),
                                                        (wv_ref, cwv_ref, cbv_ref))):
            up = jnp.dot(h, w_ref[:, cs], preferred_element_type=F32)
            p8 = prev[:, half * FF_TILE + s * sub:half * FF_TILE + (s + 1) * sub]
            mixed.append(_conv_taps(up, _shift_rows(up, p8, 1), _shift_rows(up, p8, 2),
                                    cw_ref[:, cs], cb_ref[:, cs]))
            tails[half].append(up[tm - SUBLANES:])
        gate, val = mixed
        acts.append((gate * _sigmoid(gate) * val).astype(BF16))
    tail_g = jnp.concatenate(tails[0], axis=1)
    tail_v = jnp.concatenate(tails[1], axis=1)
    carry_ref[j] = jnp.concatenate([tail_g, tail_v], axis=1)
    convg_ref[...] = tail_g
    convv_ref[...] = tail_v
    _ffn_finish(j, n_j, final_norm, acts, wd_ref, x1_ref, g_ref, y_ref)


def _ffn_sample_kernel(h_ref, wg_ref, wv_ref, cwg_ref, cwv_ref, cbg_ref, cbv_ref, wd_ref, x1_ref,
                       g_ref, s0g_ref, s0v_ref, s1g_ref, s1v_ref, y_ref, convg_ref, convv_ref,
                       *, nb, t_new, n_j, final_norm):
    j = pl.program_id(0)

    @pl.when(j == 0)
    def _():
        y_ref[...] = jnp.zeros(y_ref.shape, F32)

    h = h_ref[...]
    sub = FF_TILE // FF_SUB
    acts = []
    for s in range(FF_SUB):
        cs = slice(s * sub, (s + 1) * sub)
        mixed = []
        for w_ref, cw_ref, cb_ref, s0_ref, s1_ref, conv_ref in (
                (wg_ref, cwg_ref, cbg_ref, s0g_ref, s1g_ref, convg_ref),
                (wv_ref, cwv_ref, cbv_ref, s0v_ref, s1v_ref, convv_ref)):
            up = jnp.dot(h, w_ref[:, cs], preferred_element_type=F32)
            s0, s1 = s0_ref[:, cs], s1_ref[:, cs]
            u1 = jnp.concatenate([s1, up[:(t_new - 1) * nb]], axis=0)
            u2 = jnp.concatenate([s0, s1, up[:(t_new - 2) * nb]], axis=0)
            mixed.append(_conv_taps(up, u1, u2, cw_ref[:, cs], cb_ref[:, cs]))
            conv_ref[0, :, cs] = up[(t_new - 2) * nb:(t_new - 1) * nb]
            conv_ref[1, :, cs] = up[(t_new - 1) * nb:]
        gate, val = mixed
        acts.append((gate * _sigmoid(gate) * val).astype(BF16))
    _ffn_finish(j, n_j, final_norm, acts, wd_ref, x1_ref, g_ref, y_ref)


def _ffn_weight_specs(d, n_j, ix):
    gate = lambda rows: pl.BlockSpec((rows, FF_TILE), ix(lambda j: (0, j)))
    val = lambda rows: pl.BlockSpec((rows, FF_TILE), ix(lambda j: (0, n_j + j)))
    return [gate(d), val(d), gate(3), val(3), gate(1), val(1),
            pl.BlockSpec((FF_TILE, d), ix(lambda j: (j, 0)))]


def _ffn_prompt(h2, x1, wu, cw, cb, wd, g, batch, seq, tm, final_norm):
    m, d = x1.shape
    d_ff = wd.shape[0]
    n_j = d_ff // FF_TILE
    tps = seq // tm
    kern = functools.partial(_ffn_prompt_kernel, tm=tm, tiles_per_seq=tps, n_j=n_j,
                             final_norm=final_norm)
    ix = lambda f: (lambda i, j: f(j))
    conv_spec = pl.BlockSpec((None, SUBLANES, FF_TILE), lambda i, j: (i, 0, j))
    conv_shape = jax.ShapeDtypeStruct((m // tm, SUBLANES, d_ff), F32)
    y, cg, cv = pl.pallas_call(
        kern,
        grid=(m // tm, n_j),
        in_specs=[pl.BlockSpec((tm, d), lambda i, j: (i, 0))] + _ffn_weight_specs(d, n_j, ix)
                 + [pl.BlockSpec((tm, d), lambda i, j: (i, 0)),
                    pl.BlockSpec((1, d), lambda i, j: (0, 0))],
        out_specs=[pl.BlockSpec((tm, d), lambda i, j: (i, 0)), conv_spec, conv_spec],
        out_shape=[jax.ShapeDtypeStruct((m, d), F32), conv_shape, conv_shape],
        scratch_shapes=[pltpu.VMEM((n_j, SUBLANES, 2 * FF_TILE), F32)],
        compiler_params=_params("arbitrary", "arbitrary"),
        name="ffn_prompt",
    )(h2, wu, wu, cw, cw, cb, cb, wd, x1, g.reshape(1, d))
    return y, jnp.concatenate([cg, cv], axis=-1)[tps - 1::tps, SUBLANES - 2:]


def _ffn_sample(h2, x1, wu, cw, cb, wd, g, state, nb, t_new, final_norm):
    m, d = x1.shape
    d_ff = wd.shape[0]
    n_j = d_ff // FF_TILE
    kern = functools.partial(_ffn_sample_kernel, nb=nb, t_new=t_new, n_j=n_j,
                             final_norm=final_norm)
    ix = lambda f: f
    s0, s1 = state[:, 0, :], state[:, 1, :]
    st_gate = pl.BlockSpec((nb, FF_TILE), lambda j: (0, j))
    st_val = pl.BlockSpec((nb, FF_TILE), lambda j: (0, n_j + j))
    conv_spec = pl.BlockSpec((2, nb, FF_TILE), lambda j: (0, 0, j))
    y, cg, cv = pl.pallas_call(
        kern,
        grid=(n_j,),
        in_specs=[pl.BlockSpec((m, d), lambda j: (0, 0))] + _ffn_weight_specs(d, n_j, ix)
                 + [pl.BlockSpec((m, d), lambda j: (0, 0)),
                    pl.BlockSpec((1, d), lambda j: (0, 0)),
                    st_gate, st_val, st_gate, st_val],
        out_specs=[pl.BlockSpec((m, d), lambda j: (0, 0)), conv_spec, conv_spec],
        out_shape=[jax.ShapeDtypeStruct((m, d), F32),
                   jax.ShapeDtypeStruct((2, nb, d_ff), F32),
                   jax.ShapeDtypeStruct((2, nb, d_ff), F32)],
        compiler_params=_params("arbitrary"),
        name="ffn_sample",
    )(h2, wu, wu, cw, cw, cb, cb, wd, x1, g.reshape(1, d), s0, s0, s1, s1)
    return y, jnp.swapaxes(jnp.concatenate([cg, cv], axis=-1), 0, 1)


def kernel(x_prompt, x_sample, cache_k, cache_v, cache_kidx, state_s5_re, state_s5_im, state_ffn_conv, page_table, norm_mix, w_in, w_attn_proj, s5_a_re, s5_a_im, s5_log_step, s5_b_re, s5_b_im, s5_c_re, s5_c_im, s5_d, w_glu, w_out, norm_ffn, w_up, conv_w, conv_b, w_down, norm_final):
    depth = w_in.shape[0]
    batch, seq, d_model = x_prompt.shape
    nb, t_new, _ = x_sample.shape
    d_head = cache_k.shape[-1]
    d_idx = cache_kidx.shape[-1]
    attn_w = N_HEADS * d_head
    kv_w = N_KV_HEADS * d_head
    qi_w = IDX_HEADS * d_idx
    groups, p_state = s5_a_re.shape[1], s5_a_re.shape[2]
    s5_w = groups * S5_GROUP

    xp = x_prompt.reshape(batch * seq, d_model)
    xs = x_sample.reshape(nb * t_new, d_model)
    tm_p = min(512, batch * seq)
    tm_s = nb * t_new

    outs = {name: [] for name in ("kp", "vp", "kip", "srp", "sip", "cp",
                                  "ks", "vs", "kis", "srs", "sis", "cs")}
    for l in range(depth):
        w_t = jnp.swapaxes(w_in[l], 0, 1).astype(BF16)
        o = 0
        seg = {}
        for name, width in (("q", attn_w), ("k", kv_w), ("v", kv_w), ("qi", qi_w), ("ki", d_idx),
                            ("wi", IDX_HEADS), ("u", s5_w), ("ga", d_model), ("gb", d_model)):
            seg[name] = w_t[o:o + width]
            o += width
        w_q_grouped = _group_major_heads(seg["q"], d_head)
        w_u, w_ga, w_gb = seg["u"], seg["ga"], seg["gb"]
        pad = jnp.zeros((LANES - d_idx - IDX_HEADS, d_model), BF16)
        w_small = jnp.concatenate([seg["qi"], seg["k"], seg["v"], seg["ki"], seg["wi"], pad], axis=0)
        w_qiw = jnp.concatenate([seg["qi"], seg["ki"], seg["wi"], pad], axis=0)
        w_kv_t = jnp.concatenate([seg["k"], seg["v"], seg["ki"]], axis=0)
        w_attn_grouped = _group_major_heads(w_attn_proj[l].astype(BF16), d_head)
        w_g = w_glu[l].astype(BF16)
        w_o = w_out[l].astype(BF16)
        wu = w_up[l].astype(BF16)
        cw = conv_w[l].astype(F32)
        cb = conv_b[l].astype(F32).reshape(1, -1)
        wd = w_down[l].astype(BF16)
        s5p = _s5_params(s5_a_re[l], s5_a_im[l], s5_log_step[l], s5_b_re[l], s5_b_im[l],
                         s5_c_re[l], s5_c_im[l], STEP_CHUNK)
        last = l == depth - 1

        def project(h, tm, ws):
            q, u, small = _matmul(h, [w_q_grouped, w_u, ws], tm, F32, "proj_q_u_small")
            (ga,) = _matmul(h, [w_ga], tm, F32, "proj_ga")
            (gb,) = _matmul(h, [w_gb], tm, F32, "proj_gb")
            return q, u, ga, gb, small

        h = _rmsnorm(xp, norm_mix[l], tm_p, BF16)
        q, u, ga, gb, qiw = project(h, tm_p, w_qiw)
        k_t, v_t, ki_t, kb_t, vb_t, kib_t = _kv_transposed(h, w_kv_t, batch, seq, kv_w, d_idx,
                                                            min(512, seq))
        k_sel = min(TOPK_MAX, seq // 4)
        attn = _prompt_attention(q, qiw, kib_t, kb_t, vb_t, batch, seq, d_head, d_idx, k_sel,
                                 min(256, seq))
        y5, srp, sip = _s5_prompt(u, batch, seq, s5p, s5_d[l], groups, p_state)
        merged = _merge(attn, y5, ga, gb, w_attn_grouped, w_g, min(256, batch * seq))
        x1, h2 = _out_proj(merged, xp, w_o, norm_ffn[l], tm_p)
        xp, conv_p = _ffn_prompt(h2, x1, wu, cw, cb, wd, norm_final, batch, seq, tm_p, last)

        def heads_last(a_t):
            return jnp.transpose(a_t.reshape(batch, N_KV_HEADS, d_head, seq), (0, 3, 1, 2))

        outs["kp"].append(heads_last(k_t)); outs["vp"].append(heads_last(v_t))
        outs["kip"].append(jnp.swapaxes(ki_t, 1, 2))
        outs["srp"].append(srp); outs["sip"].append(sip); outs["cp"].append(conv_p)

        h = _rmsnorm(xs, norm_mix[l], tm_s, BF16)
        q, u, ga, gb, small = project(h, tm_s, w_small)
        attn = _sample_attention(q, small, cache_k[l], cache_v[l], cache_kidx[l], page_table,
                                 t_new, d_head, d_idx)
        y5, srs, sis = _s5_sample(u, nb, t_new, s5p, s5_d[l], state_s5_re[l], state_s5_im[l])
        merged = _merge(attn, y5, ga, gb, w_attn_grouped, w_g, min(256, nb * t_new))
        x1, h2 = _out_proj(merged, xs, w_o, norm_ffn[l], tm_s)

        def time_major(a):
            return a.reshape(nb, t_new, -1).swapaxes(0, 1).reshape(nb * t_new, -1)

        y_tm, conv_s = _ffn_sample(time_major(h2), time_major(x1), wu, cw, cb, wd, norm_final,
                                   state_ffn_conv[l].astype(F32), nb, t_new, last)
        xs = y_tm.reshape(t_new, nb, d_model).swapaxes(0, 1).reshape(nb * t_new, d_model)
        k_new = small[:, qi_w:qi_w + kv_w].reshape(nb, t_new, N_KV_HEADS, d_head)
        v_new = small[:, qi_w + kv_w:qi_w + 2 * kv_w].reshape(nb, t_new, N_KV_HEADS, d_head)
        ki_new = small[:, qi_w + 2 * kv_w:qi_w + 2 * kv_w + d_idx].reshape(nb, t_new, d_idx)
        outs["ks"].append(k_new); outs["vs"].append(v_new); outs["kis"].append(ki_new)
        outs["srs"].append(srs); outs["sis"].append(sis); outs["cs"].append(conv_s)

    stk = {name: jnp.stack(v) for name, v in outs.items()}
    return (xp.reshape(batch, seq, d_model), xs.reshape(nb, t_new, d_model),
            stk["kp"], stk["vp"], stk["kip"], stk["srp"], stk["sip"], stk["cp"],
            stk["ks"], stk["vs"], stk["kis"], stk["srs"], stk["sis"], stk["cs"])
```

```python
import functools
import math

import jax
import jax.numpy as jnp
from jax import lax
from jax.experimental import pallas as pl
from jax.experimental.pallas import tpu as pltpu

F32 = jnp.float32
BF16 = jnp.bfloat16

EPS = 1e-6
LOG2E = math.log2(math.e)
TOPK_MAX = 256
N_HEADS = 16
N_KV_HEADS = 4
KV_GROUP = N_HEADS // N_KV_HEADS
IDX_HEADS = 8
S5_GROUP = 16
STEP_CHUNK = 8
LANES = 128
SUBLANES = 8
GROUPS_PER_TILE = LANES // S5_GROUP
FF_TILE = 512
VMEM_LIMIT = 48 * 1024 * 1024
NEG_INF = float("-inf")
INT_MIN = -2 ** 31


def _params(*sem):
    return pltpu.CompilerParams(dimension_semantics=sem, vmem_limit_bytes=VMEM_LIMIT)


def _sigmoid(x):
    return 0.5 * jnp.tanh(0.5 * x) + 0.5


def _gelu_tanh(x):
    c = math.sqrt(2.0 / math.pi)
    return 0.5 * x * (1.0 + jnp.tanh(c * (x + 0.044715 * (x * x * x))))


def _dot_nt(a, b):
    return lax.dot_general(a, b, (((1,), (1,)), ((), ())), preferred_element_type=F32)


def _rms_kernel(x_ref, g_ref, o_ref):
    x = x_ref[...]
    ms = jnp.mean(x * x, axis=-1, keepdims=True)
    o_ref[...] = (x * lax.rsqrt(ms + EPS) * g_ref[...]).astype(o_ref.dtype)


def _rmsnorm(x, g, tm, out_dtype):
    m, d = x.shape
    return pl.pallas_call(
        _rms_kernel,
        grid=(m // tm,),
        in_specs=[pl.BlockSpec((tm, d), lambda i: (i, 0)),
                  pl.BlockSpec((1, d), lambda i: (0, 0))],
        out_specs=pl.BlockSpec((tm, d), lambda i: (i, 0)),
        out_shape=jax.ShapeDtypeStruct((m, d), out_dtype),
        compiler_params=_params("arbitrary"),
        name="rmsnorm",
    )(x, g.reshape(1, d))


def _mm_kernel(h_ref, *refs):
    n = len(refs) // 2
    h = h_ref[...]
    for w_ref, o_ref in zip(refs[:n], refs[n:]):
        o_ref[...] = _dot_nt(h, w_ref[...]).astype(o_ref.dtype)


def _matmul(h, weights_t, tm, out_dtype, name):
    m, k = h.shape
    return pl.pallas_call(
        _mm_kernel,
        grid=(m // tm,),
        in_specs=[pl.BlockSpec((tm, k), lambda i: (i, 0))]
                 + [pl.BlockSpec(w.shape, lambda i: (0, 0)) for w in weights_t],
        out_specs=[pl.BlockSpec((tm, w.shape[0]), lambda i: (i, 0)) for w in weights_t],
        out_shape=[jax.ShapeDtypeStruct((m, w.shape[0]), out_dtype) for w in weights_t],
        compiler_params=_params("arbitrary"),
        name=name,
    )(h, *weights_t)


def _kv_t_kernel(w_ref, h_ref, k_ref, v_ref, ki_ref, kb_ref, vb_ref, kib_ref, *, kv_w):
    out = _dot_nt(w_ref[...], h_ref[...])
    for lo, hi, full_ref, half_ref in ((0, kv_w, k_ref, kb_ref), (kv_w, 2 * kv_w, v_ref, vb_ref),
                                       (2 * kv_w, out.shape[0], ki_ref, kib_ref)):
        full_ref[...] = out[lo:hi]
        half_ref[...] = out[lo:hi].astype(BF16)


def _kv_transposed(h, w_t, batch, seq, kv_w, d_idx, tn):
    d = h.shape[1]
    nt = seq // tn
    out = lambda rows: pl.BlockSpec((None, rows, tn), lambda b, i: (b, 0, i))
    shape = lambda rows, dt: jax.ShapeDtypeStruct((batch, rows, seq), dt)
    widths = (kv_w, kv_w, d_idx)
    return pl.pallas_call(
        functools.partial(_kv_t_kernel, kv_w=kv_w),
        grid=(batch, nt),
        in_specs=[pl.BlockSpec(w_t.shape, lambda b, i: (0, 0)),
                  pl.BlockSpec((tn, d), lambda b, i: (b * nt + i, 0))],
        out_specs=[out(w) for w in widths] * 2,
        out_shape=[shape(w, F32) for w in widths] + [shape(w, BF16) for w in widths],
        compiler_params=_params("arbitrary", "arbitrary"),
        name="proj_kv_transposed",
    )(w_t, h)


SEARCH_GROUP = 4


def _count(mask):
    return jnp.sum(jnp.where(mask, 1.0, 0.0), axis=1, keepdims=True)


def _topk_mask(score, col, k_sel, n_cols):
    kf = float(k_sel)
    rows = score.shape[0]

    def as_float(key):
        bits = jnp.where(key < 0, key ^ jnp.int32(0x7FFFFFFF), key)
        return lax.bitcast_convert_type(bits, F32)

    finite = score > NEG_INF
    few = _count(finite) <= kf

    cnt0 = _count(score >= 0.0)
    cand0 = jnp.where(cnt0 >= kf, jnp.int32(0), jnp.int32(INT_MIN))
    cand0 = jnp.broadcast_to(cand0, (rows, 1)).astype(jnp.int32)
    cnt0 = jnp.where(cnt0 >= kf, cnt0, float(n_cols))

    def search_body(group, carry):
        cand, cnt = carry
        for g in range(SEARCH_GROUP):
            shift = jnp.int32(30) - (group * SEARCH_GROUP + g)
            bit = jnp.where(shift >= 0, lax.shift_left(jnp.int32(1), jnp.maximum(shift, 0)), 0)
            trial = cand + bit
            c_trial = _count(score >= as_float(trial))
            ok = c_trial >= kf
            cand = jnp.where(ok, trial, cand)
            cnt = jnp.where(ok, c_trial, cnt)
        return cand, cnt

    cand, cnt = lax.fori_loop(0, -(-31 // SEARCH_GROUP), search_body, (cand0, cnt0))
    thr = as_float(cand)

    above = score > thr
    ties = score == thr
    need = kf - _count(above)
    surplus = jnp.where(few, 0.0, cnt - kf)
    n_bits = max(1, (n_cols - 1).bit_length())

    def index_step(it, m):
        trial = m + lax.shift_left(jnp.int32(1), jnp.int32(n_bits - 1) - it)
        taken = _count(jnp.logical_and(ties, col < trial))
        return jnp.where(taken <= need - 1.0, trial, m)

    def lowest_indices(_):
        return lax.fori_loop(0, n_bits, index_step, jnp.zeros((rows, 1), jnp.int32))

    def all_ties(_):
        return jnp.full((rows, 1), n_cols, jnp.int32)

    last = lax.cond(jnp.max(surplus) > 0.0, lowest_indices, all_ties, None)
    take_tie = jnp.logical_and(jnp.logical_and(ties, col <= last), need >= 1.0)
    top = jnp.logical_or(above, take_tie)
    return jnp.logical_and(finite, jnp.logical_or(few, top))


def _index_weights(kw, d_idx):
    w_scale = (d_idx ** -0.5) * (IDX_HEADS ** -0.5)
    return [kw[:, d_idx + h:d_idx + h + 1] * w_scale for h in range(IDX_HEADS)]


def _head_rows(qi, d_idx):
    return jnp.concatenate([qi[:, h * d_idx:(h + 1) * d_idx] for h in range(IDX_HEADS)], axis=0)


def _weighted_relu_sum(lg, w_cols, rows):
    score = jnp.zeros((rows, lg.shape[1]), F32)
    for h in range(IDX_HEADS):
        score = score + jnp.maximum(lg[h * rows:(h + 1) * rows], 0.0) * w_cols[h]
    return score


HEADS_PER_DOT = 2


PROMPT_SCORE_ELEMS = 512 * 1024


def _prompt_attn_kernel(q_ref, qi_ref, kw_ref, kit_ref, kt_ref, vt_ref, prev_ref, o_ref,
                        *, tq, n_keys, k_sel, d_head, d_idx, q_block):
    del prev_ref
    n_seqs = q_ref.shape[0]
    scores = []
    for b in range(n_seqs):
        qi = qi_ref[b]
        w_cols = _index_weights(kw_ref[b], d_idx)
        kit = kit_ref[b]
        score = jnp.zeros((tq, n_keys), F32)
        for h in range(IDX_HEADS):
            lg = jnp.dot(qi[:, h * d_idx:(h + 1) * d_idx].astype(BF16), kit,
                         preferred_element_type=F32)
            score = score + jnp.maximum(lg, 0.0) * w_cols[h]
        scores.append(score)
    score = jnp.concatenate(scores, axis=0)

    col = lax.broadcasted_iota(jnp.int32, score.shape, 1)
    row = lax.broadcasted_iota(jnp.int32, (n_seqs, tq, n_keys), 1).reshape(score.shape)
    score = jnp.where(col <= q_block * tq + row, score, NEG_INF)
    bias_all = jnp.where(_topk_mask(score, col, k_sel, n_keys), 0.0, NEG_INF)

    for b in range(n_seqs):
        bias = bias_all[b * tq:(b + 1) * tq]
        q = q_ref[b] * (d_head ** -0.5 * LOG2E)
        for n in range(N_KV_HEADS):
            kt = kt_ref[b, n * d_head:(n + 1) * d_head, :]
            vt = vt_ref[b, n * d_head:(n + 1) * d_head, :]
            for g0 in range(0, KV_GROUP, HEADS_PER_DOT):
                heads = [(g0 + i) * N_KV_HEADS + n for i in range(HEADS_PER_DOT)]
                qs = jnp.concatenate([q[:, h * d_head:(h + 1) * d_head] for h in heads], axis=0)
                s = jnp.dot(qs.astype(BF16), kt, preferred_element_type=F32)
                s = (s.reshape(HEADS_PER_DOT, tq, n_keys) + bias[None]).reshape(-1, n_keys)
                m = jnp.max(s, axis=1, keepdims=True)
                p = jnp.exp2(s - m)
                l = jnp.sum(p, axis=1, keepdims=True)
                o = _dot_nt(p.astype(BF16), vt) / l
                for i, h in enumerate(heads):
                    o_ref[b, :, h * d_head:(h + 1) * d_head] = (
                        o[i * tq:(i + 1) * tq].astype(o_ref.dtype))


def _prompt_attention(q, qiw, ki_t, k_t, v_t, batch, seq, d_head, d_idx, k_sel, tq):
    attn_w = q.shape[1]
    kv_w = N_KV_HEADS * d_head
    qi_w = IDX_HEADS * d_idx
    nq = seq // tq
    q3 = q.reshape(batch, seq, attn_w)
    qiw3 = qiw.reshape(batch, seq, qiw.shape[1])
    out = jnp.zeros((batch, seq, attn_w), BF16)
    for qb in range(nq):
        n_keys = (qb + 1) * tq
        n_seqs = max(d for d in range(1, batch + 1)
                     if batch % d == 0 and (d == 1 or d * tq * n_keys <= PROMPT_SCORE_ELEMS))
        row = lambda w, c, qb=qb, n_seqs=n_seqs: pl.BlockSpec((n_seqs, tq, w), lambda b: (b, qb, c))
        keys = lambda w, n_keys=n_keys, n_seqs=n_seqs: pl.BlockSpec((n_seqs, w, n_keys),
                                                                     lambda b: (b, 0, 0))
        out = pl.pallas_call(
            functools.partial(_prompt_attn_kernel, tq=tq, n_keys=n_keys, k_sel=k_sel,
                              d_head=d_head, d_idx=d_idx, q_block=qb),
            grid=(batch // n_seqs,),
            in_specs=[row(attn_w, 0), row(qi_w, 0), row(LANES, qi_w // LANES),
                      keys(d_idx), keys(kv_w), keys(kv_w),
                      pl.BlockSpec(memory_space=pl.ANY)],
            out_specs=row(attn_w, 0),
            out_shape=jax.ShapeDtypeStruct((batch, seq, attn_w), BF16),
            input_output_aliases={6: 0},
            compiler_params=_params("arbitrary"),
            name=f"prompt_attention_q{qb}",
        )(q3, qiw3, qiw3, ki_t, k_t, v_t, out)
    return out.reshape(batch * seq, attn_w)


def _lane_columns(cols, width):
    rows = cols[0].shape[0]
    lane = lax.broadcasted_iota(jnp.int32, (rows, width), 1)
    out = jnp.zeros((rows, width), F32)
    for j, c in enumerate(cols):
        out = jnp.where(lane == j, c, out)
    return out


def _sample_score_kernel(pt_ref, qi_ref, kw_ref, kin_ref, *rest, n_seqs, n_pages, page, t_new,
                         d_idx):
    del pt_ref
    s_ref = rest[n_seqs * n_pages]
    past = n_pages * page
    rows = SUBLANES
    pad_row = lax.broadcasted_iota(jnp.int32, (rows, page), 0) >= t_new
    first = lax.broadcasted_iota(jnp.int32, (rows, page), 1) == 0
    lane = lax.broadcasted_iota(jnp.int32, (rows, LANES), 1)
    t = lax.broadcasted_iota(jnp.int32, (rows, LANES), 0)
    causal_new = jnp.logical_and(lane <= t, t < t_new)
    for i in range(n_seqs):
        ipages = rest[i * n_pages:(i + 1) * n_pages]
        qi_rows = _head_rows(qi_ref[i], d_idx)
        qi_bf = qi_rows.astype(BF16)
        w_cols = _index_weights(kw_ref[i], d_idx)
        for p in range(n_pages):
            lg = jnp.dot(qi_bf, ipages[p][...].astype(BF16), preferred_element_type=F32)
            pad_val = jnp.where(first, 0.0, NEG_INF) if p == 0 else NEG_INF
            s_ref[i, :, p * page:(p + 1) * page] = jnp.where(
                pad_row, pad_val, _weighted_relu_sum(lg, w_cols, rows))
        kin = kin_ref[i]
        new_cols = []
        for j in range(t_new):
            lg = jnp.sum(qi_rows * kin[j:j + 1, :], axis=1, keepdims=True)
            new_cols.append(_weighted_relu_sum(lg, w_cols, rows))
        s_ref[i, :, past:] = jnp.where(causal_new, _lane_columns(new_cols, LANES), NEG_INF)


def _select_kernel(s_ref, b_ref, *, k_sel):
    score = s_ref[...]
    col = lax.broadcasted_iota(jnp.int32, score.shape, 1)
    keep = _topk_mask(score, col, k_sel, score.shape[1])
    b_ref[...] = jnp.where(keep, 0.0, NEG_INF)


SCORE_SEQS_PER_STEP = 4
ATTEND_SEQS_PER_STEP = 2


def _sample_attn_kernel(pt_ref, q_ref, kn_ref, vn_ref, bias_ref, *rest,
                        n_seqs, n_pages, page, t_new, d_head):
    del pt_ref
    pages = rest[:2 * n_seqs * n_pages]
    o_ref, kt_ref, vt_ref = rest[2 * n_seqs * n_pages:]
    for i in range(n_seqs):
        kpages = pages[i * n_pages:(i + 1) * n_pages]
        vpages = pages[(n_seqs + i) * n_pages:(n_seqs + i + 1) * n_pages]
        o_ref[i] = _sample_attend(q_ref[i], kn_ref[i], vn_ref[i], bias_ref[i], kpages, vpages,
                                  kt_ref.at[i], vt_ref.at[i], page, t_new, d_head
                                  ).astype(o_ref.dtype)


def _sample_attend(q, kn, vn, bias, kpages, vpages, kt_ref, vt_ref, page, t_new, d_head):
    n_pages = len(kpages)
    past = n_pages * page
    rows = SUBLANES
    kv_w = N_KV_HEADS * d_head

    for p in range(n_pages):
        kt_ref[:, p * page:(p + 1) * page] = kpages[p][...].astype(BF16)
        vt_ref[:, p * page:(p + 1) * page] = vpages[p][...].astype(BF16)

    bias = jnp.concatenate([bias] * N_HEADS, axis=0)

    q = q * (d_head ** -0.5 * LOG2E)
    lane_head = lax.broadcasted_iota(jnp.int32, (rows, kv_w), 1) // d_head
    q_rows = jnp.concatenate(
        [jnp.where(lane_head == n, q[:, g * kv_w:(g + 1) * kv_w], 0.0)
         for g in range(KV_GROUP) for n in range(N_KV_HEADS)], axis=0)

    s_past = jnp.dot(q_rows.astype(BF16), kt_ref[...], preferred_element_type=F32)
    s_past = s_past + bias[:, :past]
    s_new = _lane_columns(
        [jnp.sum(q_rows * kn[j:j + 1, :], axis=1, keepdims=True) for j in range(t_new)], LANES)
    s_new = s_new + bias[:, past:]

    m = jnp.maximum(jnp.max(s_past, axis=1, keepdims=True), jnp.max(s_new, axis=1, keepdims=True))
    p_past = jnp.exp2(s_past - m)
    p_new = jnp.exp2(s_new - m)
    l = jnp.sum(p_past, axis=1, keepdims=True) + jnp.sum(p_new, axis=1, keepdims=True)
    o = _dot_nt(p_past.astype(BF16), vt_ref[...])
    for j in range(t_new):
        o = o + p_new[:, j:j + 1] * vn[j:j + 1, :]
    o = o / l

    chunks = []
    for g in range(KV_GROUP):
        acc = jnp.zeros((rows, kv_w), F32)
        for n in range(N_KV_HEADS):
            r0 = (g * N_KV_HEADS + n) * rows
            acc = acc + jnp.where(lane_head == n, o[r0:r0 + rows], 0.0)
        chunks.append(acc)
    return jnp.concatenate(chunks, axis=1)


def _seq_page_map(b, pt, *, i, p, n_seqs, n_pages):
    return (pt[(b * n_seqs + i) * n_pages + p], 0, 0)


def _sample_attention(q, small, cache_k, cache_v, cache_ki, page_table, t_new, d_head, d_idx):
    nb, n_pages = page_table.shape
    n_pool, page = cache_k.shape[0], cache_k.shape[1]
    attn_w = q.shape[1]
    kv_w = N_KV_HEADS * d_head
    qi_w = IDX_HEADS * d_idx
    rows = SUBLANES
    past = n_pages * page
    k_sel = min(TOPK_MAX, (past + t_new) // 4)

    def pad_rows(a):
        a = a.reshape(nb, t_new, a.shape[-1])
        return jnp.pad(a, ((0, 0), (0, rows - t_new), (0, 0)))

    q8 = pad_rows(q)
    qi8 = pad_rows(small[:, :qi_w])
    kw8 = pad_rows(small[:, qi_w + 2 * kv_w:])
    kn = pad_rows(small[:, qi_w:qi_w + kv_w])
    vn = pad_rows(small[:, qi_w + kv_w:qi_w + 2 * kv_w])
    kin = pad_rows(small[:, qi_w + 2 * kv_w:qi_w + 2 * kv_w + d_idx])

    ck = jnp.transpose(cache_k, (0, 2, 3, 1)).reshape(n_pool, kv_w, page)
    cv = jnp.transpose(cache_v, (0, 2, 3, 1)).reshape(n_pool, kv_w, page)
    ci = jnp.transpose(cache_ki, (0, 2, 1))

    pt_flat = page_table.reshape(-1)
    n_keys = past + LANES

    def seqs_spec(w, n_seqs):
        return pl.BlockSpec((n_seqs, rows, w), lambda b, pt: (b, 0, 0))

    def seq_page_specs(w, n_seqs):
        return [pl.BlockSpec((None, w, page),
                             functools.partial(_seq_page_map, i=i, p=p, n_seqs=n_seqs,
                                               n_pages=n_pages))
                for i in range(n_seqs) for p in range(n_pages)]

    n_seqs = math.gcd(nb, SCORE_SEQS_PER_STEP)
    score = pl.pallas_call(
        functools.partial(_sample_score_kernel, n_seqs=n_seqs, n_pages=n_pages, page=page,
                          t_new=t_new, d_idx=d_idx),
        grid_spec=pltpu.PrefetchScalarGridSpec(
            num_scalar_prefetch=1,
            grid=(nb // n_seqs,),
            in_specs=[seqs_spec(qi_w, n_seqs), seqs_spec(LANES, n_seqs), seqs_spec(d_idx, n_seqs)]
                     + seq_page_specs(d_idx, n_seqs),
            out_specs=seqs_spec(n_keys, n_seqs),
        ),
        out_shape=jax.ShapeDtypeStruct((nb, rows, n_keys), F32),
        compiler_params=_params("arbitrary"),
        name="sample_scores",
    )(pt_flat, qi8, kw8, kin, *([ci] * (n_seqs * n_pages)))

    sel_rows = math.gcd(nb * rows, 256)
    bias = pl.pallas_call(
        functools.partial(_select_kernel, k_sel=k_sel),
        grid=(nb * rows // sel_rows,),
        in_specs=[pl.BlockSpec((sel_rows, n_keys), lambda i: (i, 0))],
        out_specs=pl.BlockSpec((sel_rows, n_keys), lambda i: (i, 0)),
        out_shape=jax.ShapeDtypeStruct((nb * rows, n_keys), F32),
        compiler_params=_params("arbitrary"),
        name="sample_select",
    )(score.reshape(nb * rows, n_keys)).reshape(nb, rows, n_keys)

    n_seqs = math.gcd(nb, ATTEND_SEQS_PER_STEP)
    out = pl.pallas_call(
        functools.partial(_sample_attn_kernel, n_seqs=n_seqs, n_pages=n_pages, page=page,
                          t_new=t_new, d_head=d_head),
        grid_spec=pltpu.PrefetchScalarGridSpec(
            num_scalar_prefetch=1,
            grid=(nb // n_seqs,),
            in_specs=[seqs_spec(attn_w, n_seqs), seqs_spec(kv_w, n_seqs), seqs_spec(kv_w, n_seqs),
                      seqs_spec(n_keys, n_seqs)]
                     + seq_page_specs(kv_w, n_seqs) + seq_page_specs(kv_w, n_seqs),
            out_specs=seqs_spec(attn_w, n_seqs),
            scratch_shapes=[pltpu.VMEM((n_seqs, kv_w, past), BF16),
                            pltpu.VMEM((n_seqs, kv_w, past), BF16)],
        ),
        out_shape=jax.ShapeDtypeStruct((nb, rows, attn_w), BF16),
        compiler_params=_params("arbitrary"),
        name="sample_attention",
    )(pt_flat, q8, kn, vn, bias, *([ck] * (n_seqs * n_pages)), *([cv] * (n_seqs * n_pages)))
    return out[:, :t_new].reshape(nb * t_new, attn_w)


def _group_major_heads(a, d_head):
    n = a.shape[1]
    a = a.reshape(N_KV_HEADS, KV_GROUP, d_head, n)
    return jnp.swapaxes(a, 0, 1).reshape(N_HEADS * d_head, n)


def _s5_step(xr, xi, ar, ai, bu, half):
    br, bi = bu[:, :half], bu[:, half:]
    if xr is None:
        return br, bi
    return ar * xr - ai * xi + br, ar * xi + ai * xr + bi


def _s5_inputs(u_ref, b_ref, j):
    tr, steps, _ = u_ref.shape
    x = jnp.transpose(u_ref[:, :, j * LANES:(j + 1) * LANES], (1, 0, 2))
    us = [x[s] for s in range(steps)]
    bu = jnp.dot(jnp.concatenate(us, axis=0).astype(BF16), b_ref[j], preferred_element_type=F32)
    return us, [bu[s * tr:(s + 1) * tr] for s in range(steps)]


def _s5_local_kernel(u_ref, b_ref, ar_ref, ai_ref, z_ref, *, n_tiles):
    sw = b_ref.shape[2]
    half = sw // 2
    for j in range(n_tiles):
        ar, ai = ar_ref[j:j + 1, :], ai_ref[j:j + 1, :]
        xr = xi = None
        for bu in _s5_inputs(u_ref, b_ref, j)[1]:
            xr, xi = _s5_step(xr, xi, ar, ai, bu, half)
        z_ref[:, j * sw:j * sw + half] = xr
        z_ref[:, j * sw + half:(j + 1) * sw] = xi


def _s5_carry_kernel(z_ref, ar_ref, ai_ref, xc_ref, xf_ref, *, batch, n_chunks):
    half = z_ref.shape[1] // 2
    ar, ai = ar_ref[...], ai_ref[...]

    def body(c, carry):
        new = []
        for b in range(batch):
            xr, xi = carry[b]
            row = b * n_chunks + c
            xc_ref[pl.ds(row, 1), :] = jnp.concatenate([xr, xi], axis=1)
            z = z_ref[pl.ds(row, 1), :]
            new.append((ar * xr - ai * xi + z[:, :half], ar * xi + ai * xr + z[:, half:]))
        return tuple(new)

    zero = jnp.zeros((1, half), F32)
    final = lax.fori_loop(0, n_chunks, body, tuple((zero, zero) for _ in range(batch)),
                          unroll=math.gcd(n_chunks, 4))
    for b in range(batch):
        xf_ref[b:b + 1, :] = jnp.concatenate(list(final[b]), axis=1)


def _s5_out_kernel(u_ref, x0_ref, b_ref, c_ref, ar_ref, ai_ref, d_ref, y_ref, *maybe_xf,
                   n_tiles):
    tr, steps, _ = u_ref.shape
    sw = b_ref.shape[2]
    half = sw // 2
    for j in range(n_tiles):
        ar, ai = ar_ref[j:j + 1, :], ai_ref[j:j + 1, :]
        xr = x0_ref[:, j * sw:j * sw + half]
        xi = x0_ref[:, j * sw + half:(j + 1) * sw]
        d = d_ref[:, j * LANES:(j + 1) * LANES]
        us, bus = _s5_inputs(u_ref, b_ref, j)
        xs = []
        for bu in bus:
            xr, xi = _s5_step(xr, xi, ar, ai, bu, half)
            xs.append(jnp.concatenate([xr, xi], axis=1).astype(BF16))
        y = jnp.dot(jnp.concatenate(xs, axis=0), c_ref[j], preferred_element_type=F32)
        ys = jnp.stack([y[s * tr:(s + 1) * tr] + d * us[s] for s in range(steps)], axis=0)
        y_ref[:, :, j * LANES:(j + 1) * LANES] = jnp.transpose(ys, (1, 0, 2))
        if maybe_xf:
            maybe_xf[0][:, j * sw:j * sw + half] = xr
            maybe_xf[0][:, j * sw + half:(j + 1) * sw] = xi


def _s5_params(a_re, a_im, log_step, b_re, b_im, c_re, c_im, chunk_steps):
    g, p = a_re.shape
    nt = g // GROUPS_PER_TILE
    a_re, a_im = a_re.astype(F32), a_im.astype(F32)
    step = jnp.exp(log_step.astype(F32))[:, None]

    def discretise(n_steps):
        mag = jnp.exp(a_re * step * n_steps)
        return mag * jnp.cos(a_im * step * n_steps), mag * jnp.sin(a_im * step * n_steps)

    lr, li = discretise(1.0)
    cr, ci = discretise(float(chunk_steps))
    den = a_re * a_re + a_im * a_im
    fr = ((lr - 1.0) * a_re + li * a_im) / den
    fi = (li * a_re - (lr - 1.0) * a_im) / den
    b_re, b_im = b_re.astype(F32), b_im.astype(F32)
    bb_re = fr[..., None] * b_re - fi[..., None] * b_im
    bb_im = fr[..., None] * b_im + fi[..., None] * b_re
    eye = jnp.eye(GROUPS_PER_TILE, dtype=BF16)

    bb = jnp.stack([bb_re, bb_im], axis=0).reshape(2, nt, GROUPS_PER_TILE, p, S5_GROUP)
    bb = jnp.transpose(bb, (1, 2, 4, 0, 3)).astype(BF16)
    b_blk = bb[:, :, :, :, None, :] * eye[None, :, None, None, :, None]
    b_blk = b_blk.reshape(nt, LANES, 2 * GROUPS_PER_TILE * p)

    cc = jnp.stack([c_re.astype(F32), -c_im.astype(F32)], axis=0)
    cc = cc.reshape(2, nt, GROUPS_PER_TILE, S5_GROUP, p)
    cc = jnp.transpose(cc, (1, 0, 2, 4, 3)).astype(BF16)
    c_blk = cc[:, :, :, :, None, :] * eye[None, None, :, None, :, None]
    c_blk = c_blk.reshape(nt, 2 * GROUPS_PER_TILE * p, LANES)

    def tiles(z):
        return z.reshape(nt, GROUPS_PER_TILE * p)

    return b_blk, c_blk, tiles(lr), tiles(li), tiles(cr), tiles(ci)


def _state_to_tiles(s_re, s_im):
    nb, g, p = s_re.shape
    nt = g // GROUPS_PER_TILE
    st = jnp.stack([s_re.reshape(nb, nt, GROUPS_PER_TILE * p),
                    s_im.reshape(nb, nt, GROUPS_PER_TILE * p)], axis=2)
    return st.reshape(nb, nt * 2 * GROUPS_PER_TILE * p)


def _tiles_to_state(x, g, p):
    nb = x.shape[0]
    nt = g // GROUPS_PER_TILE
    st = x.reshape(nb, nt, 2, GROUPS_PER_TILE, p)
    return st[:, :, 0].reshape(nb, g, p), st[:, :, 1].reshape(nb, g, p)


def _s5_outputs(u3, x0, prm, d_skip, tr, want_final):
    b_blk, c_blk, ar, ai = prm
    r, steps, width = u3.shape
    nt, _, sw = b_blk.shape
    kern = functools.partial(_s5_out_kernel, n_tiles=nt)
    full3 = lambda a: pl.BlockSpec(a.shape, lambda i: (0, 0, 0))
    full2 = lambda a: pl.BlockSpec(a.shape, lambda i: (0, 0))
    d2 = d_skip.reshape(1, width).astype(F32)
    tokens = pl.BlockSpec((tr, steps, width), lambda i: (i, 0, 0))
    out_shape = [jax.ShapeDtypeStruct(u3.shape, F32)]
    out_specs = [tokens]
    if want_final:
        out_shape.append(jax.ShapeDtypeStruct((r, nt * sw), F32))
        out_specs.append(pl.BlockSpec((tr, nt * sw), lambda i: (i, 0)))
    return pl.pallas_call(
        kern,
        grid=(r // tr,),
        in_specs=[tokens, pl.BlockSpec((tr, nt * sw), lambda i: (i, 0)),
                  full3(b_blk), full3(c_blk), full2(ar), full2(ai), full2(d2)],
        out_specs=out_specs,
        out_shape=out_shape,
        compiler_params=_params("arbitrary"),
        name="s5_outputs",
    )(u3, x0, b_blk, c_blk, ar, ai, d2)


def _s5_prompt(u, batch, seq, prm_all, d_skip, g, p):
    b_blk, c_blk, ar, ai, ar_c, ai_c = prm_all
    steps = STEP_CHUNK
    n_chunks = seq // steps
    r = batch * n_chunks
    width = u.shape[1]
    u3 = u.reshape(r, steps, width)
    nt, _, sw = b_blk.shape
    tr = 128
    z = pl.pallas_call(
        functools.partial(_s5_local_kernel, n_tiles=nt),
        grid=(r // tr,),
        in_specs=[pl.BlockSpec((tr, steps, width), lambda i: (i, 0, 0)),
                  pl.BlockSpec(b_blk.shape, lambda i: (0, 0, 0)),
                  pl.BlockSpec(ar.shape, lambda i: (0, 0)),
                  pl.BlockSpec(ai.shape, lambda i: (0, 0))],
        out_specs=pl.BlockSpec((tr, nt * sw), lambda i: (i, 0)),
        out_shape=jax.ShapeDtypeStruct((r, nt * sw), F32),
        compiler_params=_params("arbitrary"),
        name="s5_chunk_states",
    )(u3, b_blk, ar, ai)
    half = sw // 2
    xc, xf = pl.pallas_call(
        functools.partial(_s5_carry_kernel, batch=batch, n_chunks=n_chunks),
        grid=(nt,),
        in_specs=[pl.BlockSpec((r, sw), lambda j: (0, j)),
                  pl.BlockSpec((None, 1, half), lambda j: (j, 0, 0)),
                  pl.BlockSpec((None, 1, half), lambda j: (j, 0, 0))],
        out_specs=[pl.BlockSpec((r, sw), lambda j: (0, j)),
                   pl.BlockSpec((batch, sw), lambda j: (0, j))],
        out_shape=[jax.ShapeDtypeStruct((r, nt * sw), F32),
                   jax.ShapeDtypeStruct((batch, nt * sw), F32)],
        compiler_params=_params("arbitrary"),
        name="s5_carry",
    )(z, ar_c.reshape(nt, 1, half), ai_c.reshape(nt, 1, half))
    (y3,) = _s5_outputs(u3, xc, (b_blk, c_blk, ar, ai), d_skip, tr, False)
    s_re, s_im = _tiles_to_state(xf, g, p)
    return y3.reshape(batch * seq, width), s_re, s_im


def _s5_sample(u, nb, t_new, prm_all, d_skip, s_re, s_im):
    b_blk, c_blk, ar, ai, _, _ = prm_all
    g, p = s_re.shape[1], s_re.shape[2]
    width = u.shape[1]
    x0 = _state_to_tiles(s_re.astype(F32), s_im.astype(F32))
    y3, xf = _s5_outputs(u.reshape(nb, t_new, width), x0, (b_blk, c_blk, ar, ai), d_skip, nb, True)
    n_re, n_im = _tiles_to_state(xf, g, p)
    return y3.reshape(nb * t_new, width), n_re, n_im


def _merge_kernel(attn_ref, y_ref, ga_ref, gb_ref, wa_ref, wg_ref, o_ref):
    d = o_ref.shape[1]
    y_a = jnp.dot(attn_ref[...], wa_ref[...], preferred_element_type=F32)
    z = _gelu_tanh(y_ref[...]).astype(BF16)
    glu = jnp.dot(z, wg_ref[...], preferred_element_type=F32)
    y_b = glu[:, :d] * _sigmoid(glu[:, d:])
    o_ref[...] = (_sigmoid(ga_ref[...]) * y_a + _sigmoid(gb_ref[...]) * y_b).astype(o_ref.dtype)


def _merge(attn, y, ga, gb, w_attn, w_glu, tm):
    m, d = ga.shape
    row = lambda w: pl.BlockSpec((tm, w), lambda i: (i, 0))
    return pl.pallas_call(
        _merge_kernel,
        grid=(m // tm,),
        in_specs=[row(attn.shape[1]), row(y.shape[1]), row(d), row(d),
                  pl.BlockSpec(w_attn.shape, lambda i: (0, 0)),
                  pl.BlockSpec(w_glu.shape, lambda i: (0, 0))],
        out_specs=row(d),
        out_shape=jax.ShapeDtypeStruct((m, d), BF16),
        compiler_params=_params("arbitrary"),
        name="merge",
    )(attn, y, ga, gb, w_attn, w_glu)


def _out_proj_kernel(m_ref, x_ref, w_ref, g_ref, x1_ref, h_ref):
    x1 = x_ref[...] + jnp.dot(m_ref[...], w_ref[...], preferred_element_type=F32)
    x1_ref[...] = x1
    ms = jnp.mean(x1 * x1, axis=-1, keepdims=True)
    h_ref[...] = (x1 * lax.rsqrt(ms + EPS) * g_ref[...]).astype(h_ref.dtype)


def _out_proj(merged, x, w_out, g, tm):
    m, d = x.shape
    row = pl.BlockSpec((tm, d), lambda i: (i, 0))
    return pl.pallas_call(
        _out_proj_kernel,
        grid=(m // tm,),
        in_specs=[row, row, pl.BlockSpec(w_out.shape, lambda i: (0, 0)),
                  pl.BlockSpec((1, d), lambda i: (0, 0))],
        out_specs=[row, row],
        out_shape=[jax.ShapeDtypeStruct((m, d), F32), jax.ShapeDtypeStruct((m, d), BF16)],
        compiler_params=_params("arbitrary"),
        name="out_proj",
    )(merged, x, w_out, g.reshape(1, d))


FF_SUB = 2


def _shift_rows(up, prev, k):
    body = pltpu.roll(up, k, axis=0)
    row = lax.broadcasted_iota(jnp.int32, prev.shape, 0)
    head = jnp.where(row < k, pltpu.roll(prev, k, axis=0), body[:SUBLANES])
    return jnp.concatenate([head, body[SUBLANES:]], axis=0)


def _conv_taps(up, u1, u2, cw, cb):
    return cb + cw[0:1, :] * u2 + cw[1:2, :] * u1 + cw[2:3, :] * up


def _ffn_finish(j, n_j, final_norm, acts, wd_ref, x1_ref, g_ref, y_ref):
    y_ref[...] += jnp.dot(jnp.concatenate(acts, axis=1), wd_ref[...], preferred_element_type=F32)

    @pl.when(j == n_j - 1)
    def _():
        x2 = x1_ref[...] + y_ref[...]
        if final_norm:
            ms = jnp.mean(x2 * x2, axis=-1, keepdims=True)
            x2 = x2 * lax.rsqrt(ms + EPS) * g_ref[...]
        y_ref[...] = x2


def _ffn_prompt_kernel(h_ref, wg_ref, wv_ref, cwg_ref, cwv_ref, cbg_ref, cbv_ref, wd_ref, x1_ref,
                       g_ref, y_ref, convg_ref, convv_ref, carry_ref,
                       *, tm, tiles_per_seq, n_j, final_norm):
    i, j = pl.program_id(0), pl.program_id(1)

    @pl.when(i % tiles_per_seq == 0)
    def _():
        carry_ref[j] = jnp.zeros(carry_ref.shape[1:], F32)

    @pl.when(j == 0)
    def _():
        y_ref[...] = jnp.zeros(y_ref.shape, F32)

    h = h_ref[...]
    prev = carry_ref[j]
    sub = FF_TILE // FF_SUB
    acts, tails = [], ([], [])
    for s in range(FF_SUB):
        cs = slice(s * sub, (s + 1) * sub)
        mixed = []
        for half, (w_ref, cw_ref, cb_ref) in enumerate(((wg_ref, cwg_ref, cbg_ref),
                                                        (wv_ref, cwv_ref, cbv_ref))):
            up = jnp.dot(h, w_ref[:, cs], preferred_element_type=F32)
            p8 = prev[:, half * FF_TILE + s * sub:half * FF_TILE + (s + 1) * sub]
            mixed.append(_conv_taps(up, _shift_rows(up, p8, 1), _shift_rows(up, p8, 2),
                                    cw_ref[:, cs], cb_ref[:, cs]))
            tails[half].append(up[tm - SUBLANES:])
        gate, val = mixed
        acts.append((gate * _sigmoid(gate) * val).astype(BF16))
    tail_g = jnp.concatenate(tails[0], axis=1)
    tail_v = jnp.concatenate(tails[1], axis=1)
    carry_ref[j] = jnp.concatenate([tail_g, tail_v], axis=1)
    convg_ref[...] = tail_g
    convv_ref[...] = tail_v
    _ffn_finish(j, n_j, final_norm, acts, wd_ref, x1_ref, g_ref, y_ref)


def _ffn_sample_kernel(h_ref, wg_ref, wv_ref, cwg_ref, cwv_ref, cbg_ref, cbv_ref, wd_ref, x1_ref,
                       g_ref, s0g_ref, s0v_ref, s1g_ref, s1v_ref, y_ref, convg_ref, convv_ref,
                       *, nb, t_new, n_j, final_norm):
    j = pl.program_id(0)

    @pl.when(j == 0)
    def _():
        y_ref[...] = jnp.zeros(y_ref.shape, F32)

    h = h_ref[...]
    sub = FF_TILE // FF_SUB
    acts = []
    for s in range(FF_SUB):
        cs = slice(s * sub, (s + 1) * sub)
        mixed = []
        for w_ref, cw_ref, cb_ref, s0_ref, s1_ref, conv_ref in (
                (wg_ref, cwg_ref, cbg_ref, s0g_ref, s1g_ref, convg_ref),
                (wv_ref, cwv_ref, cbv_ref, s0v_ref, s1v_ref, convv_ref)):
            up = jnp.dot(h, w_ref[:, cs], preferred_element_type=F32)
            s0, s1 = s0_ref[:, cs], s1_ref[:, cs]
            u1 = jnp.concatenate([s1, up[:(t_new - 1) * nb]], axis=0)
            u2 = jnp.concatenate([s0, s1, up[:(t_new - 2) * nb]], axis=0)
            mixed.append(_conv_taps(up, u1, u2, cw_ref[:, cs], cb_ref[:, cs]))
            conv_ref[0, :, cs] = up[(t_new - 2) * nb:(t_new - 1) * nb]
            conv_ref[1, :, cs] = up[(t_new - 1) * nb:]
        gate, val = mixed
        acts.append((gate * _sigmoid(gate) * val).astype(BF16))
    _ffn_finish(j, n_j, final_norm, acts, wd_ref, x1_ref, g_ref, y_ref)


def _ffn_weight_specs(d, n_j, ix):
    gate = lambda rows: pl.BlockSpec((rows, FF_TILE), ix(lambda j: (0, j)))
    val = lambda rows: pl.BlockSpec((rows, FF_TILE), ix(lambda j: (0, n_j + j)))
    return [gate(d), val(d), gate(3), val(3), gate(1), val(1),
            pl.BlockSpec((FF_TILE, d), ix(lambda j: (j, 0)))]


def _ffn_prompt(h2, x1, wu, cw, cb, wd, g, batch, seq, tm, final_norm):
    m, d = x1.shape
    d_ff = wd.shape[0]
    n_j = d_ff // FF_TILE
    tps = seq // tm
    kern = functools.partial(_ffn_prompt_kernel, tm=tm, tiles_per_seq=tps, n_j=n_j,
                             final_norm=final_norm)
    ix = lambda f: (lambda i, j: f(j))
    conv_spec = pl.BlockSpec((None, SUBLANES, FF_TILE), lambda i, j: (i, 0, j))
    conv_shape = jax.ShapeDtypeStruct((m // tm, SUBLANES, d_ff), F32)
    y, cg, cv = pl.pallas_call(
        kern,
        grid=(m // tm, n_j),
        in_specs=[pl.BlockSpec((tm, d), lambda i, j: (i, 0))] + _ffn_weight_specs(d, n_j, ix)
                 + [pl.BlockSpec((tm, d), lambda i, j: (i, 0)),
                    pl.BlockSpec((1, d), lambda i, j: (0, 0))],
        out_specs=[pl.BlockSpec((tm, d), lambda i, j: (i, 0)), conv_spec, conv_spec],
        out_shape=[jax.ShapeDtypeStruct((m, d), F32), conv_shape, conv_shape],
        scratch_shapes=[pltpu.VMEM((n_j, SUBLANES, 2 * FF_TILE), F32)],
        compiler_params=_params("arbitrary", "arbitrary"),
        name="ffn_prompt",
    )(h2, wu, wu, cw, cw, cb, cb, wd, x1, g.reshape(1, d))
    return y, jnp.concatenate([cg, cv], axis=-1)[tps - 1::tps, SUBLANES - 2:]


def _ffn_sample(h2, x1, wu, cw, cb, wd, g, state, nb, t_new, final_norm):
    m, d = x1.shape
    d_ff = wd.shape[0]
    n_j = d_ff // FF_TILE
    kern = functools.partial(_ffn_sample_kernel, nb=nb, t_new=t_new, n_j=n_j,
                             final_norm=final_norm)
    ix = lambda f: f
    s0, s1 = state[:, 0, :], state[:, 1, :]
    st_gate = pl.BlockSpec((nb, FF_TILE), lambda j: (0, j))
    st_val = pl.BlockSpec((nb, FF_TILE), lambda j: (0, n_j + j))
    conv_spec = pl.BlockSpec((2, nb, FF_TILE), lambda j: (0, 0, j))
    y, cg, cv = pl.pallas_call(
        kern,
        grid=(n_j,),
        in_specs=[pl.BlockSpec((m, d), lambda j: (0, 0))] + _ffn_weight_specs(d, n_j, ix)
                 + [pl.BlockSpec((m, d), lambda j: (0, 0)),
                    pl.BlockSpec((1, d), lambda j: (0, 0)),
                    st_gate, st_val, st_gate, st_val],
        out_specs=[pl.BlockSpec((m, d), lambda j: (0, 0)), conv_spec, conv_spec],
        out_shape=[jax.ShapeDtypeStruct((m, d), F32),
                   jax.ShapeDtypeStruct((2, nb, d_ff), F32),
                   jax.ShapeDtypeStruct((2, nb, d_ff), F32)],
        compiler_params=_params("arbitrary"),
        name="ffn_sample",
    )(h2, wu, wu, cw, cw, cb, cb, wd, x1, g.reshape(1, d), s0, s0, s1, s1)
    return y, jnp.swapaxes(jnp.concatenate([cg, cv], axis=-1), 0, 1)


def kernel(x_prompt, x_sample, cache_k, cache_v, cache_kidx, state_s5_re, state_s5_im, state_ffn_conv, page_table, norm_mix, w_in, w_attn_proj, s5_a_re, s5_a_im, s5_log_step, s5_b_re, s5_b_im, s5_c_re, s5_c_im, s5_d, w_glu, w_out, norm_ffn, w_up, conv_w, conv_b, w_down, norm_final):
    depth = w_in.shape[0]
    batch, seq, d_model = x_prompt.shape
    nb, t_new, _ = x_sample.shape
    d_head = cache_k.shape[-1]
    d_idx = cache_kidx.shape[-1]
    attn_w = N_HEADS * d_head
    kv_w = N_KV_HEADS * d_head
    qi_w = IDX_HEADS * d_idx
    groups, p_state = s5_a_re.shape[1], s5_a_re.shape[2]
    s5_w = groups * S5_GROUP

    xp = x_prompt.reshape(batch * seq, d_model)
    xs = x_sample.reshape(nb * t_new, d_model)
    tm_p = min(512, batch * seq)
    tm_s = nb * t_new

    outs = {name: [] for name in ("kp", "vp", "kip", "srp", "sip", "cp",
                                  "ks", "vs", "kis", "srs", "sis", "cs")}
    for l in range(depth):
        w_t = jnp.swapaxes(w_in[l], 0, 1).astype(BF16)
        o = 0
        seg = {}
        for name, width in (("q", attn_w), ("k", kv_w), ("v", kv_w), ("qi", qi_w), ("ki", d_idx),
                            ("wi", IDX_HEADS), ("u", s5_w), ("ga", d_model), ("gb", d_model)):
            seg[name] = w_t[o:o + width]
            o += width
        w_q_grouped = _group_major_heads(seg["q"], d_head)
        w_u, w_ga, w_gb = seg["u"], seg["ga"], seg["gb"]
        pad = jnp.zeros((LANES - d_idx - IDX_HEADS, d_model), BF16)
        w_small = jnp.concatenate([seg["qi"], seg["k"], seg["v"], seg["ki"], seg["wi"], pad], axis=0)
        w_qiw = jnp.concatenate([seg["qi"], seg["ki"], seg["wi"], pad], axis=0)
        w_kv_t = jnp.concatenate([seg["k"], seg["v"], seg["ki"]], axis=0)
        w_attn_grouped = _group_major_heads(w_attn_proj[l].astype(BF16), d_head)
        w_g = w_glu[l].astype(BF16)
        w_o = w_out[l].astype(BF16)
        wu = w_up[l].astype(BF16)
        cw = conv_w[l].astype(F32)
        cb = conv_b[l].astype(F32).reshape(1, -1)
        wd = w_down[l].astype(BF16)
        s5p = _s5_params(s5_a_re[l], s5_a_im[l], s5_log_step[l], s5_b_re[l], s5_b_im[l],
                         s5_c_re[l], s5_c_im[l], STEP_CHUNK)
        last = l == depth - 1

        def project(h, tm, ws):
            q, u, small = _matmul(h, [w_q_grouped, w_u, ws], tm, F32, "proj_q_u_small")
            (ga,) = _matmul(h, [w_ga], tm, F32, "proj_ga")
            (gb,) = _matmul(h, [w_gb], tm, F32, "proj_gb")
            return q, u, ga, gb, small

        h = _rmsnorm(xp, norm_mix[l], tm_p, BF16)
        q, u, ga, gb, qiw = project(h, tm_p, w_qiw)
        k_t, v_t, ki_t, kb_t, vb_t, kib_t = _kv_transposed(h, w_kv_t, batch, seq, kv_w, d_idx,
                                                            min(512, seq))
        k_sel = min(TOPK_MAX, seq // 4)
        attn = _prompt_attention(q, qiw, kib_t, kb_t, vb_t, batch, seq, d_head, d_idx, k_sel,
                                 min(256, seq))
        y5, srp, sip = _s5_prompt(u, batch, seq, s5p, s5_d[l], groups, p_state)
        merged = _merge(attn, y5, ga, gb, w_attn_grouped, w_g, min(256, batch * seq))
        x1, h2 = _out_proj(merged, xp, w_o, norm_ffn[l], tm_p)
        xp, conv_p = _ffn_prompt(h2, x1, wu, cw, cb, wd, norm_final, batch, seq, tm_p, last)

        def heads_last(a_t):
            return jnp.transpose(a_t.reshape(batch, N_KV_HEADS, d_head, seq), (0, 3, 1, 2))

        outs["kp"].append(heads_last(k_t)); outs["vp"].append(heads_last(v_t))
        outs["kip"].append(jnp.swapaxes(ki_t, 1, 2))
        outs["srp"].append(srp); outs["sip"].append(sip); outs["cp"].append(conv_p)

        h = _rmsnorm(xs, norm_mix[l], tm_s, BF16)
        q, u, ga, gb, small = project(h, tm_s, w_small)
        attn = _sample_attention(q, small, cache_k[l], cache_v[l], cache_kidx[l], page_table,
                                 t_new, d_head, d_idx)
        y5, srs, sis = _s5_sample(u, nb, t_new, s5p, s5_d[l], state_s5_re[l], state_s5_im[l])
        merged = _merge(attn, y5, ga, gb, w_attn_grouped, w_g, min(256, nb * t_new))
        x1, h2 = _out_proj(merged, xs, w_o, norm_ffn[l], tm_s)

        def time_major(a):
            return a.reshape(nb, t_new, -1).swapaxes(0, 1).reshape(nb * t_new, -1)

        y_tm, conv_s = _ffn_sample(time_major(h2), time_major(x1), wu, cw, cb, wd, norm_final,
                                   state_ffn_conv[l].astype(F32), nb, t_new, last)
        xs = y_tm.reshape(t_new, nb, d_model).swapaxes(0, 1).reshape(nb * t_new, d_model)
        k_new = small[:, qi_w:qi_w + kv_w].reshape(nb, t_new, N_KV_HEADS, d_head)
        v_new = small[:, qi_w + kv_w:qi_w + 2 * kv_w].reshape(nb, t_new, N_KV_HEADS, d_head)
        ki_new = small[:, qi_w + 2 * kv_w:qi_w + 2 * kv_w + d_idx].reshape(nb, t_new, d_idx)
        outs["ks"].append(k_new); outs["vs"].append(v_new); outs["kis"].append(ki_new)
        outs["srs"].append(srs); outs["sis"].append(sis); outs["cs"].append(conv_s)

    stk = {name: jnp.stack(v) for name, v in outs.items()}
    return (xp.reshape(batch, seq, d_model), xs.reshape(nb, t_new, d_model),
            stk["kp"], stk["vp"], stk["kip"], stk["srp"], stk["sip"], stk["cp"],
            stk["ks"], stk["vs"], stk["kis"], stk["srs"], stk["sis"], stk["cs"])
```

```python
import functools
import math

import jax
import jax.numpy as jnp
from jax import lax
from jax.experimental import pallas as pl
from jax.experimental.pallas import tpu as pltpu

F32 = jnp.float32
BF16 = jnp.bfloat16

EPS = 1e-6
LOG2E = math.log2(math.e)
TOPK_MAX = 256
N_HEADS = 16
N_KV_HEADS = 4
KV_GROUP = N_HEADS // N_KV_HEADS
IDX_HEADS = 8
S5_GROUP = 16
STEP_CHUNK = 8
LANES = 128
SUBLANES = 8
GROUPS_PER_TILE = LANES // S5_GROUP
FF_TILE = 512
VMEM_LIMIT = 48 * 1024 * 1024
NEG_INF = float("-inf")
INT_MIN = -2 ** 31


def _params(*sem):
    return pltpu.CompilerParams(dimension_semantics=sem, vmem_limit_bytes=VMEM_LIMIT)


def _sigmoid(x):
    return 0.5 * jnp.tanh(0.5 * x) + 0.5


def _gelu_tanh(x):
    c = math.sqrt(2.0 / math.pi)
    return 0.5 * x * (1.0 + jnp.tanh(c * (x + 0.044715 * (x * x * x))))


def _dot_nt(a, b):
    return lax.dot_general(a, b, (((1,), (1,)), ((), ())), preferred_element_type=F32)


def _rms_kernel(x_ref, g_ref, o_ref):
    x = x_ref[...]
    ms = jnp.mean(x * x, axis=-1, keepdims=True)
    o_ref[...] = (x * lax.rsqrt(ms + EPS) * g_ref[...]).astype(o_ref.dtype)


def _rmsnorm(x, g, tm, out_dtype):
    m, d = x.shape
    return pl.pallas_call(
        _rms_kernel,
        grid=(m // tm,),
        in_specs=[pl.BlockSpec((tm, d), lambda i: (i, 0)),
                  pl.BlockSpec((1, d), lambda i: (0, 0))],
        out_specs=pl.BlockSpec((tm, d), lambda i: (i, 0)),
        out_shape=jax.ShapeDtypeStruct((m, d), out_dtype),
        compiler_params=_params("arbitrary"),
        name="rmsnorm",
    )(x, g.reshape(1, d))


def _mm_kernel(h_ref, *refs):
    n = len(refs) // 2
    h = h_ref[...]
    for w_ref, o_ref in zip(refs[:n], refs[n:]):
        o_ref[...] = _dot_nt(h, w_ref[...]).astype(o_ref.dtype)


def _matmul(h, weights_t, tm, out_dtype, name):
    m, k = h.shape
    return pl.pallas_call(
        _mm_kernel,
        grid=(m // tm,),
        in_specs=[pl.BlockSpec((tm, k), lambda i: (i, 0))]
                 + [pl.BlockSpec(w.shape, lambda i: (0, 0)) for w in weights_t],
        out_specs=[pl.BlockSpec((tm, w.shape[0]), lambda i: (i, 0)) for w in weights_t],
        out_shape=[jax.ShapeDtypeStruct((m, w.shape[0]), out_dtype) for w in weights_t],
        compiler_params=_params("arbitrary"),
        name=name,
    )(h, *weights_t)


def _kv_t_kernel(w_ref, h_ref, k_ref, v_ref, ki_ref, kb_ref, vb_ref, kib_ref, *, kv_w):
    out = _dot_nt(w_ref[...], h_ref[...])
    for lo, hi, full_ref, half_ref in ((0, kv_w, k_ref, kb_ref), (kv_w, 2 * kv_w, v_ref, vb_ref),
                                       (2 * kv_w, out.shape[0], ki_ref, kib_ref)):
        full_ref[...] = out[lo:hi]
        half_ref[...] = out[lo:hi].astype(BF16)


def _kv_transposed(h, w_t, batch, seq, kv_w, d_idx, tn):
    d = h.shape[1]
    nt = seq // tn
    out = lambda rows: pl.BlockSpec((None, rows, tn), lambda b, i: (b, 0, i))
    shape = lambda rows, dt: jax.ShapeDtypeStruct((batch, rows, seq), dt)
    widths = (kv_w, kv_w, d_idx)
    return pl.pallas_call(
        functools.partial(_kv_t_kernel, kv_w=kv_w),
        grid=(batch, nt),
        in_specs=[pl.BlockSpec(w_t.shape, lambda b, i: (0, 0)),
                  pl.BlockSpec((tn, d), lambda b, i: (b * nt + i, 0))],
        out_specs=[out(w) for w in widths] * 2,
        out_shape=[shape(w, F32) for w in widths] + [shape(w, BF16) for w in widths],
        compiler_params=_params("arbitrary", "arbitrary"),
        name="proj_kv_transposed",
    )(w_t, h)


SEARCH_GROUP = 4


def _count(mask):
    return jnp.sum(jnp.where(mask, 1.0, 0.0), axis=1, keepdims=True)


def _topk_mask(score, col, k_sel, n_cols):
    kf = float(k_sel)
    rows = score.shape[0]

    def as_float(key):
        bits = jnp.where(key < 0, key ^ jnp.int32(0x7FFFFFFF), key)
        return lax.bitcast_convert_type(bits, F32)

    finite = score > NEG_INF
    few = _count(finite) <= kf

    cnt0 = _count(score >= 0.0)
    cand0 = jnp.where(cnt0 >= kf, jnp.int32(0), jnp.int32(INT_MIN))
    cand0 = jnp.broadcast_to(cand0, (rows, 1)).astype(jnp.int32)
    cnt0 = jnp.where(cnt0 >= kf, cnt0, float(n_cols))

    def search_body(group, carry):
        cand, cnt = carry
        for g in range(SEARCH_GROUP):
            shift = jnp.int32(30) - (group * SEARCH_GROUP + g)
            bit = jnp.where(shift >= 0, lax.shift_left(jnp.int32(1), jnp.maximum(shift, 0)), 0)
            trial = cand + bit
            c_trial = _count(score >= as_float(trial))
            ok = c_trial >= kf
            cand = jnp.where(ok, trial, cand)
            cnt = jnp.where(ok, c_trial, cnt)
        return cand, cnt

    cand, cnt = lax.fori_loop(0, -(-31 // SEARCH_GROUP), search_body, (cand0, cnt0))
    thr = as_float(cand)

    above = score > thr
    ties = score == thr
    need = kf - _count(above)
    surplus = jnp.where(few, 0.0, cnt - kf)
    n_bits = max(1, (n_cols - 1).bit_length())

    def index_step(it, m):
        trial = m + lax.shift_left(jnp.int32(1), jnp.int32(n_bits - 1) - it)
        taken = _count(jnp.logical_and(ties, col < trial))
        return jnp.where(taken <= need - 1.0, trial, m)

    def lowest_indices(_):
        return lax.fori_loop(0, n_bits, index_step, jnp.zeros((rows, 1), jnp.int32))

    def all_ties(_):
        return jnp.full((rows, 1), n_cols, jnp.int32)

    last = lax.cond(jnp.max(surplus) > 0.0, lowest_indices, all_ties, None)
    take_tie = jnp.logical_and(jnp.logical_and(ties, col <= last), need >= 1.0)
    top = jnp.logical_or(above, take_tie)
    return jnp.logical_and(finite, jnp.logical_or(few, top))


def _index_weights(kw, d_idx):
    w_scale = (d_idx ** -0.5) * (IDX_HEADS ** -0.5)
    return [kw[:, d_idx + h:d_idx + h + 1] * w_scale for h in range(IDX_HEADS)]


def _head_rows(qi, d_idx):
    return jnp.concatenate([qi[:, h * d_idx:(h + 1) * d_idx] for h in range(IDX_HEADS)], axis=0)


def _weighted_relu_sum(lg, w_cols, rows):
    score = jnp.zeros((rows, lg.shape[1]), F32)
    for h in range(IDX_HEADS):
        score = score + jnp.maximum(lg[h * rows:(h + 1) * rows], 0.0) * w_cols[h]
    return score


HEADS_PER_DOT = 2


PROMPT_SCORE_ELEMS = 512 * 1024


def _prompt_attn_kernel(q_ref, qi_ref, kw_ref, kit_ref, kt_ref, vt_ref, prev_ref, o_ref,
                        *, tq, n_keys, k_sel, d_head, d_idx, q_block):
    del prev_ref
    n_seqs = q_ref.shape[0]
    scores = []
    for b in range(n_seqs):
        qi = qi_ref[b]
        w_cols = _index_weights(kw_ref[b], d_idx)
        kit = kit_ref[b]
        score = jnp.zeros((tq, n_keys), F32)
        for h in range(IDX_HEADS):
            lg = jnp.dot(qi[:, h * d_idx:(h + 1) * d_idx].astype(BF16), kit,
                         preferred_element_type=F32)
            score = score + jnp.maximum(lg, 0.0) * w_cols[h]
        scores.append(score)
    score = jnp.concatenate(scores, axis=0)

    col = lax.broadcasted_iota(jnp.int32, score.shape, 1)
    row = lax.broadcasted_iota(jnp.int32, (n_seqs, tq, n_keys), 1).reshape(score.shape)
    score = jnp.where(col <= q_block * tq + row, score, NEG_INF)
    bias_all = jnp.where(_topk_mask(score, col, k_sel, n_keys), 0.0, NEG_INF)

    for b in range(n_seqs):
        bias = bias_all[b * tq:(b + 1) * tq]
        q = q_ref[b] * (d_head ** -0.5 * LOG2E)
        for n in range(N_KV_HEADS):
            kt = kt_ref[b, n * d_head:(n + 1) * d_head, :]
            vt = vt_ref[b, n * d_head:(n + 1) * d_head, :]
            for g0 in range(0, KV_GROUP, HEADS_PER_DOT):
                heads = [(g0 + i) * N_KV_HEADS + n for i in range(HEADS_PER_DOT)]
                qs = jnp.concatenate([q[:, h * d_head:(h + 1) * d_head] for h in heads], axis=0)
                s = jnp.dot(qs.astype(BF16), kt, preferred_element_type=F32)
                s = (s.reshape(HEADS_PER_DOT, tq, n_keys) + bias[None]).reshape(-1, n_keys)
                m = jnp.max(s, axis=1, keepdims=True)
                p = jnp.exp2(s - m)
                l = jnp.sum(p, axis=1, keepdims=True)
                o = _dot_nt(p.astype(BF16), vt) / l
                for i, h in enumerate(heads):
                    o_ref[b, :, h * d_head:(h + 1) * d_head] = (
                        o[i * tq:(i + 1) * tq].astype(o_ref.dtype))


def _prompt_attention(q, qiw, ki_t, k_t, v_t, batch, seq, d_head, d_idx, k_sel, tq):
    attn_w = q.shape[1]
    kv_w = N_KV_HEADS * d_head
    qi_w = IDX_HEADS * d_idx
    nq = seq // tq
    q3 = q.reshape(batch, seq, attn_w)
    qiw3 = qiw.reshape(batch, seq, qiw.shape[1])
    out = jnp.zeros((batch, seq, attn_w), BF16)
    for qb in range(nq):
        n_keys = (qb + 1) * tq
        n_seqs = max(d for d in range(1, batch + 1)
                     if batch % d == 0 and (d == 1 or d * tq * n_keys <= PROMPT_SCORE_ELEMS))
        row = lambda w, c, qb=qb, n_seqs=n_seqs: pl.BlockSpec((n_seqs, tq, w), lambda b: (b, qb, c))
        keys = lambda w, n_keys=n_keys, n_seqs=n_seqs: pl.BlockSpec((n_seqs, w, n_keys),
                                                                     lambda b: (b, 0, 0))
        out = pl.pallas_call(
            functools.partial(_prompt_attn_kernel, tq=tq, n_keys=n_keys, k_sel=k_sel,
                              d_head=d_head, d_idx=d_idx, q_block=qb),
            grid=(batch // n_seqs,),
            in_specs=[row(attn_w, 0), row(qi_w, 0), row(LANES, qi_w // LANES),
                      keys(d_idx), keys(kv_w), keys(kv_w),
                      pl.BlockSpec(memory_space=pl.ANY)],
            out_specs=row(attn_w, 0),
            out_shape=jax.ShapeDtypeStruct((batch, seq, attn_w), BF16),
            input_output_aliases={6: 0},
            compiler_params=_params("arbitrary"),
            name=f"prompt_attention_q{qb}",
        )(q3, qiw3, qiw3, ki_t, k_t, v_t, out)
    return out.reshape(batch * seq, attn_w)


def _lane_columns(cols, width):
    rows = cols[0].shape[0]
    lane = lax.broadcasted_iota(jnp.int32, (rows, width), 1)
    out = jnp.zeros((rows, width), F32)
    for j, c in enumerate(cols):
        out = jnp.where(lane == j, c, out)
    return out


def _sample_score_kernel(pt_ref, qi_ref, kw_ref, kin_ref, *rest, n_seqs, n_pages, page, t_new,
                         d_idx):
    del pt_ref
    s_ref = rest[n_seqs * n_pages]
    past = n_pages * page
    rows = SUBLANES
    pad_row = lax.broadcasted_iota(jnp.int32, (rows, page), 0) >= t_new
    first = lax.broadcasted_iota(jnp.int32, (rows, page), 1) == 0
    lane = lax.broadcasted_iota(jnp.int32, (rows, LANES), 1)
    t = lax.broadcasted_iota(jnp.int32, (rows, LANES), 0)
    causal_new = jnp.logical_and(lane <= t, t < t_new)
    for i in range(n_seqs):
        ipages = rest[i * n_pages:(i + 1) * n_pages]
        qi_rows = _head_rows(qi_ref[i], d_idx)
        qi_bf = qi_rows.astype(BF16)
        w_cols = _index_weights(kw_ref[i], d_idx)
        for p in range(n_pages):
            lg = jnp.dot(qi_bf, ipages[p][...].astype(BF16), preferred_element_type=F32)
            pad_val = jnp.where(first, 0.0, NEG_INF) if p == 0 else NEG_INF
            s_ref[i, :, p * page:(p + 1) * page] = jnp.where(
                pad_row, pad_val, _weighted_relu_sum(lg, w_cols, rows))
        kin = kin_ref[i]
        new_cols = []
        for j in range(t_new):
            lg = jnp.sum(qi_rows * kin[j:j + 1, :], axis=1, keepdims=True)
            new_cols.append(_weighted_relu_sum(lg, w_cols, rows))
        s_ref[i, :, past:] = jnp.where(causal_new, _lane_columns(new_cols, LANES), NEG_INF)


def _select_kernel(s_ref, b_ref, *, k_sel):
    score = s_ref[...]
    col = lax.broadcasted_iota(jnp.int32, score.shape, 1)
    keep = _topk_mask(score, col, k_sel, score.shape[1])
    b_ref[...] = jnp.where(keep, 0.0, NEG_INF)


SCORE_SEQS_PER_STEP = 4
ATTEND_SEQS_PER_STEP = 2


def _sample_attn_kernel(pt_ref, q_ref, kn_ref, vn_ref, bias_ref, *rest,
                        n_seqs, n_pages, page, t_new, d_head):
    del pt_ref
    pages = rest[:2 * n_seqs * n_pages]
    o_ref, kt_ref, vt_ref = rest[2 * n_seqs * n_pages:]
    for i in range(n_seqs):
        kpages = pages[i * n_pages:(i + 1) * n_pages]
        vpages = pages[(n_seqs + i) * n_pages:(n_seqs + i + 1) * n_pages]
        o_ref[i] = _sample_attend(q_ref[i], kn_ref[i], vn_ref[i], bias_ref[i], kpages, vpages,
                                  kt_ref.at[i], vt_ref.at[i], page, t_new, d_head
                                  ).astype(o_ref.dtype)


def _sample_attend(q, kn, vn, bias, kpages, vpages, kt_ref, vt_ref, page, t_new, d_head):
    n_pages = len(kpages)
    past = n_pages * page
    rows = SUBLANES
    kv_w = N_KV_HEADS * d_head

    for p in range(n_pages):
        kt_ref[:, p * page:(p + 1) * page] = kpages[p][...].astype(BF16)
        vt_ref[:, p * page:(p + 1) * page] = vpages[p][...].astype(BF16)

    bias = jnp.concatenate([bias] * N_HEADS, axis=0)

    q = q * (d_head ** -0.5 * LOG2E)
    lane_head = lax.broadcasted_iota(jnp.int32, (rows, kv_w), 1) // d_head
    q_rows = jnp.concatenate(
        [jnp.where(lane_head == n, q[:, g * kv_w:(g + 1) * kv_w], 0.0)
         for g in range(KV_GROUP) for n in range(N_KV_HEADS)], axis=0)

    s_past = jnp.dot(q_rows.astype(BF16), kt_ref[...], preferred_element_type=F32)
    s_past = s_past + bias[:, :past]
    s_new = _lane_columns(
        [jnp.sum(q_rows * kn[j:j + 1, :], axis=1, keepdims=True) for j in range(t_new)], LANES)
    s_new = s_new + bias[:, past:]

    m = jnp.maximum(jnp.max(s_past, axis=1, keepdims=True), jnp.max(s_new, axis=1, keepdims=True))
    p_past = jnp.exp2(s_past - m)
    p_new = jnp.exp2(s_new - m)
    l = jnp.sum(p_past, axis=1, keepdims=True) + jnp.sum(p_new, axis=1, keepdims=True)
    o = _dot_nt(p_past.astype(BF16), vt_ref[...])
    for j in range(t_new):
        o = o + p_new[:, j:j + 1] * vn[j:j + 1, :]
    o = o / l

    chunks = []
    for g in range(KV_GROUP):
        acc = jnp.zeros((rows, kv_w), F32)
        for n in range(N_KV_HEADS):
            r0 = (g * N_KV_HEADS + n) * rows
            acc = acc + jnp.where(lane_head == n, o[r0:r0 + rows], 0.0)
        chunks.append(acc)
    return jnp.concatenate(chunks, axis=1)


def _seq_page_map(b, pt, *, i, p, n_seqs, n_pages):
    return (pt[(b * n_seqs + i) * n_pages + p], 0, 0)


def _sample_attention(q, small, cache_k, cache_v, cache_ki, page_table, t_new, d_head, d_idx):
    nb, n_pages = page_table.shape
    n_pool, page = cache_k.shape[0], cache_k.shape[1]
    attn_w = q.shape[1]
    kv_w = N_KV_HEADS * d_head
    qi_w = IDX_HEADS * d_idx
    rows = SUBLANES
    past = n_pages * page
    k_sel = min(TOPK_MAX, (past + t_new) // 4)

    def pad_rows(a):
        a = a.reshape(nb, t_new, a.shape[-1])
        return jnp.pad(a, ((0, 0), (0, rows - t_new), (0, 0)))

    q8 = pad_rows(q)
    qi8 = pad_rows(small[:, :qi_w])
    kw8 = pad_rows(small[:, qi_w + 2 * kv_w:])
    kn = pad_rows(small[:, qi_w:qi_w + kv_w])
    vn = pad_rows(small[:, qi_w + kv_w:qi_w + 2 * kv_w])
    kin = pad_rows(small[:, qi_w + 2 * kv_w:qi_w + 2 * kv_w + d_idx])

    ck = jnp.transpose(cache_k, (0, 2, 3, 1)).reshape(n_pool, kv_w, page)
    cv = jnp.transpose(cache_v, (0, 2, 3, 1)).reshape(n_pool, kv_w, page)
    ci = jnp.transpose(cache_ki, (0, 2, 1))

    pt_flat = page_table.reshape(-1)
    n_keys = past + LANES

    def seqs_spec(w, n_seqs):
        return pl.BlockSpec((n_seqs, rows, w), lambda b, pt: (b, 0, 0))

    def seq_page_specs(w, n_seqs):
        return [pl.BlockSpec((None, w, page),
                             functools.partial(_seq_page_map, i=i, p=p, n_seqs=n_seqs,
                                               n_pages=n_pages))
                for i in range(n_seqs) for p in range(n_pages)]

    n_seqs = math.gcd(nb, SCORE_SEQS_PER_STEP)
    score = pl.pallas_call(
        functools.partial(_sample_score_kernel, n_seqs=n_seqs, n_pages=n_pages, page=page,
                          t_new=t_new, d_idx=d_idx),
        grid_spec=pltpu.PrefetchScalarGridSpec(
            num_scalar_prefetch=1,
            grid=(nb // n_seqs,),
            in_specs=[seqs_spec(qi_w, n_seqs), seqs_spec(LANES, n_seqs), seqs_spec(d_idx, n_seqs)]
                     + seq_page_specs(d_idx, n_seqs),
            out_specs=seqs_spec(n_keys, n_seqs),
        ),
        out_shape=jax.ShapeDtypeStruct((nb, rows, n_keys), F32),
        compiler_params=_params("arbitrary"),
        name="sample_scores",
    )(pt_flat, qi8, kw8, kin, *([ci] * (n_seqs * n_pages)))

    sel_rows = math.gcd(nb * rows, 256)
    bias = pl.pallas_call(
        functools.partial(_select_kernel, k_sel=k_sel),
        grid=(nb * rows // sel_rows,),
        in_specs=[pl.BlockSpec((sel_rows, n_keys), lambda i: (i, 0))],
        out_specs=pl.BlockSpec((sel_rows, n_keys), lambda i: (i, 0)),
        out_shape=jax.ShapeDtypeStruct((nb * rows, n_keys), F32),
        compiler_params=_params("arbitrary"),
        name="sample_select",
    )(score.reshape(nb * rows, n_keys)).reshape(nb, rows, n_keys)

    n_seqs = math.gcd(nb, ATTEND_SEQS_PER_STEP)
    out = pl.pallas_call(
        functools.partial(_sample_attn_kernel, n_seqs=n_seqs, n_pages=n_pages, page=page,
                          t_new=t_new, d_head=d_head),
        grid_spec=pltpu.PrefetchScalarGridSpec(
            num_scalar_prefetch=1,
            grid=(nb // n_seqs,),
            in_specs=[seqs_spec(attn_w, n_seqs), seqs_spec(kv_w, n_seqs), seqs_spec(kv_w, n_seqs),
                      seqs_spec(n_keys, n_seqs)]
                     + seq_page_specs(kv_w, n_seqs) + seq_page_specs(kv_w, n_seqs),
            out_specs=seqs_spec(attn_w, n_seqs),
            scratch_shapes=[pltpu.VMEM((n_seqs, kv_w, past), BF16),
                            pltpu.VMEM((n_seqs, kv_w, past), BF16)],
        ),
        out_shape=jax.ShapeDtypeStruct((nb, rows, attn_w), BF16),
        compiler_params=_params("arbitrary"),
        name="sample_attention",
    )(pt_flat, q8, kn, vn, bias, *([ck] * (n_seqs * n_pages)), *([cv] * (n_seqs * n_pages)))
    return out[:, :t_new].reshape(nb * t_new, attn_w)


def _group_major_heads(a, d_head):
    n = a.shape[1]
    a = a.reshape(N_KV_HEADS, KV_GROUP, d_head, n)
    return jnp.swapaxes(a, 0, 1).reshape(N_HEADS * d_head, n)


def _s5_step(xr, xi, ar, ai, bu, half):
    br, bi = bu[:, :half], bu[:, half:]
    if xr is None:
        return br, bi
    return ar * xr - ai * xi + br, ar * xi + ai * xr + bi


def _s5_inputs(u_ref, b_ref, j):
    tr, steps, _ = u_ref.shape
    x = jnp.transpose(u_ref[:, :, j * LANES:(j + 1) * LANES], (1, 0, 2))
    us = [x[s] for s in range(steps)]
    bu = jnp.dot(jnp.concatenate(us, axis=0).astype(BF16), b_ref[j], preferred_element_type=F32)
    return us, [bu[s * tr:(s + 1) * tr] for s in range(steps)]


def _s5_local_kernel(u_ref, b_ref, ar_ref, ai_ref, z_ref, *, n_tiles):
    sw = b_ref.shape[2]
    half = sw // 2
    for j in range(n_tiles):
        ar, ai = ar_ref[j:j + 1, :], ai_ref[j:j + 1, :]
        xr = xi = None
        for bu in _s5_inputs(u_ref, b_ref, j)[1]:
            xr, xi = _s5_step(xr, xi, ar, ai, bu, half)
        z_ref[:, j * sw:j * sw + half] = xr
        z_ref[:, j * sw + half:(j + 1) * sw] = xi


def _s5_carry_kernel(z_ref, ar_ref, ai_ref, xc_ref, xf_ref, *, batch, n_chunks):
    half = z_ref.shape[1] // 2
    ar, ai = ar_ref[...], ai_ref[...]

    def body(c, carry):
        new = []
        for b in range(batch):
            xr, xi = carry[b]
            row = b * n_chunks + c
            xc_ref[pl.ds(row, 1), :] = jnp.concatenate([xr, xi], axis=1)
            z = z_ref[pl.ds(row, 1), :]
            new.append((ar * xr - ai * xi + z[:, :half], ar * xi + ai * xr + z[:, half:]))
        return tuple(new)

    zero = jnp.zeros((1, half), F32)
    final = lax.fori_loop(0, n_chunks, body, tuple((zero, zero) for _ in range(batch)),
                          unroll=math.gcd(n_chunks, 4))
    for b in range(batch):
        xf_ref[b:b + 1, :] = jnp.concatenate(list(final[b]), axis=1)


def _s5_tile_outputs(u_ref, y_ref, b_ref, c_ref, ar, ai, d, xr, xi, j):
    tr, steps, _ = u_ref.shape
    half = b_ref.shape[2] // 2
    us, bus = _s5_inputs(u_ref, b_ref, j)
    xs = []
    for bu in bus:
        xr, xi = _s5_step(xr, xi, ar, ai, bu, half)
        xs.append(jnp.concatenate([xr, xi], axis=1).astype(BF16))
    y = jnp.dot(jnp.concatenate(xs, axis=0), c_ref[j], preferred_element_type=F32)
    ys = jnp.stack([y[s * tr:(s + 1) * tr] + d * us[s] for s in range(steps)], axis=0)
    y_ref[:, :, j * LANES:(j + 1) * LANES] = jnp.transpose(ys, (1, 0, 2))
    return xr, xi


def _s5_out_kernel(u_ref, x0_ref, b_ref, c_ref, ar_ref, ai_ref, d_ref, y_ref, *, n_tiles):
    sw = b_ref.shape[2]
    half = sw // 2
    for j in range(n_tiles):
        _s5_tile_outputs(u_ref, y_ref, b_ref, c_ref, ar_ref[j:j + 1, :], ai_ref[j:j + 1, :],
                         d_ref[:, j * LANES:(j + 1) * LANES],
                         x0_ref[:, j * sw:j * sw + half], x0_ref[:, j * sw + half:(j + 1) * sw], j)


def _s5_sample_kernel(u_ref, re_ref, im_ref, b_ref, c_ref, ar_ref, ai_ref, d_ref,
                      y_ref, ore_ref, oim_ref, *, n_tiles):
    half = b_ref.shape[2] // 2
    for j in range(n_tiles):
        rows = slice(j * half, (j + 1) * half)
        xr, xi = _s5_tile_outputs(u_ref, y_ref, b_ref, c_ref, ar_ref[j:j + 1, :],
                                  ai_ref[j:j + 1, :], d_ref[:, j * LANES:(j + 1) * LANES],
                                  re_ref[rows, :].T, im_ref[rows, :].T, j)
        ore_ref[rows, :] = xr.T
        oim_ref[rows, :] = xi.T


def _s5_params(a_re, a_im, log_step, b_re, b_im, c_re, c_im, chunk_steps):
    g, p = a_re.shape
    nt = g // GROUPS_PER_TILE
    a_re, a_im = a_re.astype(F32), a_im.astype(F32)
    step = jnp.exp(log_step.astype(F32))[:, None]

    def discretise(n_steps):
        mag = jnp.exp(a_re * step * n_steps)
        return mag * jnp.cos(a_im * step * n_steps), mag * jnp.sin(a_im * step * n_steps)

    lr, li = discretise(1.0)
    cr, ci = discretise(float(chunk_steps))
    den = a_re * a_re + a_im * a_im
    fr = ((lr - 1.0) * a_re + li * a_im) / den
    fi = (li * a_re - (lr - 1.0) * a_im) / den
    b_re, b_im = b_re.astype(F32), b_im.astype(F32)
    bb_re = fr[..., None] * b_re - fi[..., None] * b_im
    bb_im = fr[..., None] * b_im + fi[..., None] * b_re
    eye = jnp.eye(GROUPS_PER_TILE, dtype=BF16)

    bb = jnp.stack([bb_re, bb_im], axis=0).reshape(2, nt, GROUPS_PER_TILE, p, S5_GROUP)
    bb = jnp.transpose(bb, (1, 2, 4, 0, 3)).astype(BF16)
    b_blk = bb[:, :, :, :, None, :] * eye[None, :, None, None, :, None]
    b_blk = b_blk.reshape(nt, LANES, 2 * GROUPS_PER_TILE * p)

    cc = jnp.stack([c_re.astype(F32), -c_im.astype(F32)], axis=0)
    cc = cc.reshape(2, nt, GROUPS_PER_TILE, S5_GROUP, p)
    cc = jnp.transpose(cc, (1, 0, 2, 4, 3)).astype(BF16)
    c_blk = cc[:, :, :, :, None, :] * eye[None, None, :, None, :, None]
    c_blk = c_blk.reshape(nt, 2 * GROUPS_PER_TILE * p, LANES)

    def tiles(z):
        return z.reshape(nt, GROUPS_PER_TILE * p)

    return b_blk, c_blk, tiles(lr), tiles(li), tiles(cr), tiles(ci)


def _tiles_to_state(x, g, p):
    nb = x.shape[0]
    nt = g // GROUPS_PER_TILE
    st = x.reshape(nb, nt, 2, GROUPS_PER_TILE, p)
    return st[:, :, 0].reshape(nb, g, p), st[:, :, 1].reshape(nb, g, p)


def _s5_weight_specs(prm, d2):
    return [pl.BlockSpec(a.shape, lambda i, nd=a.ndim: (0,) * nd) for a in (*prm, d2)]


def _s5_outputs(u3, x0, prm, d_skip, tr):
    r, steps, width = u3.shape
    nt, _, sw = prm[0].shape
    d2 = d_skip.reshape(1, width).astype(F32)
    tokens = pl.BlockSpec((tr, steps, width), lambda i: (i, 0, 0))
    return pl.pallas_call(
        functools.partial(_s5_out_kernel, n_tiles=nt),
        grid=(r // tr,),
        in_specs=[tokens, pl.BlockSpec((tr, nt * sw), lambda i: (i, 0))]
                 + _s5_weight_specs(prm, d2),
        out_specs=tokens,
        out_shape=jax.ShapeDtypeStruct(u3.shape, F32),
        compiler_params=_params("arbitrary"),
        name="s5_outputs",
    )(u3, x0, *prm, d2)


def _s5_prompt(u, batch, seq, prm_all, d_skip, g, p):
    b_blk, c_blk, ar, ai, ar_c, ai_c = prm_all
    steps = STEP_CHUNK
    n_chunks = seq // steps
    r = batch * n_chunks
    width = u.shape[1]
    u3 = u.reshape(r, steps, width)
    nt, _, sw = b_blk.shape
    tr = 128
    z = pl.pallas_call(
        functools.partial(_s5_local_kernel, n_tiles=nt),
        grid=(r // tr,),
        in_specs=[pl.BlockSpec((tr, steps, width), lambda i: (i, 0, 0)),
                  pl.BlockSpec(b_blk.shape, lambda i: (0, 0, 0)),
                  pl.BlockSpec(ar.shape, lambda i: (0, 0)),
                  pl.BlockSpec(ai.shape, lambda i: (0, 0))],
        out_specs=pl.BlockSpec((tr, nt * sw), lambda i: (i, 0)),
        out_shape=jax.ShapeDtypeStruct((r, nt * sw), F32),
        compiler_params=_params("arbitrary"),
        name="s5_chunk_states",
    )(u3, b_blk, ar, ai)
    half = sw // 2
    xc, xf = pl.pallas_call(
        functools.partial(_s5_carry_kernel, batch=batch, n_chunks=n_chunks),
        grid=(nt,),
        in_specs=[pl.BlockSpec((r, sw), lambda j: (0, j)),
                  pl.BlockSpec((None, 1, half), lambda j: (j, 0, 0)),
                  pl.BlockSpec((None, 1, half), lambda j: (j, 0, 0))],
        out_specs=[pl.BlockSpec((r, sw), lambda j: (0, j)),
                   pl.BlockSpec((batch, sw), lambda j: (0, j))],
        out_shape=[jax.ShapeDtypeStruct((r, nt * sw), F32),
                   jax.ShapeDtypeStruct((batch, nt * sw), F32)],
        compiler_params=_params("arbitrary"),
        name="s5_carry",
    )(z, ar_c.reshape(nt, 1, half), ai_c.reshape(nt, 1, half))
    y3 = _s5_outputs(u3, xc, (b_blk, c_blk, ar, ai), d_skip, tr)
    s_re, s_im = _tiles_to_state(xf, g, p)
    return y3.reshape(batch * seq, width), s_re, s_im


def _s5_sample(u, nb, t_new, prm_all, d_skip, s_re, s_im):
    prm = prm_all[:4]
    g, p = s_re.shape[1], s_re.shape[2]
    width = u.shape[1]
    nt = prm[0].shape[0]
    d2 = d_skip.reshape(1, width).astype(F32)
    state_t = lambda a: jnp.transpose(a.astype(F32), (1, 2, 0)).reshape(g * p, nb)
    tokens = pl.BlockSpec((nb, t_new, width), lambda i: (0, 0, 0))
    state = pl.BlockSpec((g * p, nb), lambda i: (0, 0))
    y3, n_re, n_im = pl.pallas_call(
        functools.partial(_s5_sample_kernel, n_tiles=nt),
        grid=(1,),
        in_specs=[tokens, state, state] + _s5_weight_specs(prm, d2),
        out_specs=[tokens, state, state],
        out_shape=[jax.ShapeDtypeStruct((nb, t_new, width), F32)]
                  + [jax.ShapeDtypeStruct((g * p, nb), F32)] * 2,
        compiler_params=_params("arbitrary"),
        name="s5_sample",
    )(u.reshape(nb, t_new, width), state_t(s_re), state_t(s_im), *prm, d2)
    back = lambda a: jnp.transpose(a.reshape(g, p, nb), (2, 0, 1))
    return y3.reshape(nb * t_new, width), back(n_re), back(n_im)


def _merge_kernel(attn_ref, y_ref, ga_ref, gb_ref, wa_ref, wg_ref, o_ref):
    d = o_ref.shape[1]
    y_a = jnp.dot(attn_ref[...], wa_ref[...], preferred_element_type=F32)
    z = _gelu_tanh(y_ref[...]).astype(BF16)
    glu = jnp.dot(z, wg_ref[...], preferred_element_type=F32)
    y_b = glu[:, :d] * _sigmoid(glu[:, d:])
    o_ref[...] = (_sigmoid(ga_ref[...]) * y_a + _sigmoid(gb_ref[...]) * y_b).astype(o_ref.dtype)


def _merge(attn, y, ga, gb, w_attn, w_glu, tm):
    m, d = ga.shape
    row = lambda w: pl.BlockSpec((tm, w), lambda i: (i, 0))
    return pl.pallas_call(
        _merge_kernel,
        grid=(m // tm,),
        in_specs=[row(attn.shape[1]), row(y.shape[1]), row(d), row(d),
                  pl.BlockSpec(w_attn.shape, lambda i: (0, 0)),
                  pl.BlockSpec(w_glu.shape, lambda i: (0, 0))],
        out_specs=row(d),
        out_shape=jax.ShapeDtypeStruct((m, d), BF16),
        compiler_params=_params("arbitrary"),
        name="merge",
    )(attn, y, ga, gb, w_attn, w_glu)


def _out_proj_kernel(m_ref, x_ref, w_ref, g_ref, x1_ref, h_ref):
    x1 = x_ref[...] + jnp.dot(m_ref[...], w_ref[...], preferred_element_type=F32)
    x1_ref[...] = x1
    ms = jnp.mean(x1 * x1, axis=-1, keepdims=True)
    h_ref[...] = (x1 * lax.rsqrt(ms + EPS) * g_ref[...]).astype(h_ref.dtype)


def _out_proj(merged, x, w_out, g, tm):
    m, d = x.shape
    row = pl.BlockSpec((tm, d), lambda i: (i, 0))
    return pl.pallas_call(
        _out_proj_kernel,
        grid=(m // tm,),
        in_specs=[row, row, pl.BlockSpec(w_out.shape, lambda i: (0, 0)),
                  pl.BlockSpec((1, d), lambda i: (0, 0))],
        out_specs=[row, row],
        out_shape=[jax.ShapeDtypeStruct((m, d), F32), jax.ShapeDtypeStruct((m, d), BF16)],
        compiler_params=_params("arbitrary"),
        name="out_proj",
    )(merged, x, w_out, g.reshape(1, d))


FF_SUB = 2


def _shift_rows(up, prev, k):
    body = pltpu.roll(up, k, axis=0)
    row = lax.broadcasted_iota(jnp.int32, prev.shape, 0)
    head = jnp.where(row < k, pltpu.roll(prev, k, axis=0), body[:SUBLANES])
    return jnp.concatenate([head, body[SUBLANES:]], axis=0)


def _conv_taps(up, u1, u2, cw, cb):
    return cb + cw[0:1, :] * u2 + cw[1:2, :] * u1 + cw[2:3, :] * up


def _ffn_finish(j, n_j, final_norm, acts, wd_ref, x1_ref, g_ref, y_ref):
    y_ref[...] += jnp.dot(jnp.concatenate(acts, axis=1), wd_ref[...], preferred_element_type=F32)

    @pl.when(j == n_j - 1)
    def _():
        x2 = x1_ref[...] + y_ref[...]
        if final_norm:
            ms = jnp.mean(x2 * x2, axis=-1, keepdims=True)
            x2 = x2 * lax.rsqrt(ms + EPS) * g_ref[...]
        y_ref[...] = x2


def _ffn_prompt_kernel(h_ref, wg_ref, wv_ref, cwg_ref, cwv_ref, cbg_ref, cbv_ref, wd_ref, x1_ref,
                       g_ref, y_ref, convg_ref, convv_ref, carry_ref,
                       *, tm, tiles_per_seq, n_j, final_norm):
    i, j = pl.program_id(0), pl.program_id(1)

    @pl.when(i % tiles_per_seq == 0)
    def _():
        carry_ref[j] = jnp.zeros(carry_ref.shape[1:], F32)

    @pl.when(j == 0)
    def _():
        y_ref[...] = jnp.zeros(y_ref.shape, F32)

    h = h_ref[...]
    prev = carry_ref[j]
    sub = FF_TILE // FF_SUB
    acts, tails = [], ([], [])
    for s in range(FF_SUB):
        cs = slice(s * sub, (s + 1) * sub)
        mixed = []
        for half, (w_ref, cw_ref, cb_ref) in enumerate(((wg_ref, cwg_ref, cbg_ref),
                                                        (wv_ref, cwv_ref, cbv_ref))):
            up = jnp.dot(h, w_ref[:, cs], preferred_element_type=F32)
            p8 = prev[:, half * FF_TILE + s * sub:half * FF_TILE + (s + 1) * sub]
            mixed.append(_conv_taps(up, _shift_rows(up, p8, 1), _shift_rows(up, p8, 2),
                                    cw_ref[:, cs], cb_ref[:, cs]))
            tails[half].append(up[tm - SUBLANES:])
        gate, val = mixed
        acts.append((gate * _sigmoid(gate) * val).astype(BF16))
    tail_g = jnp.concatenate(tails[0], axis=1)
    tail_v = jnp.concatenate(tails[1], axis=1)
    carry_ref[j] = jnp.concatenate([tail_g, tail_v], axis=1)
    convg_ref[...] = tail_g
    convv_ref[...] = tail_v
    _ffn_finish(j, n_j, final_norm, acts, wd_ref, x1_ref, g_ref, y_ref)


def _ffn_sample_kernel(h_ref, wg_ref, wv_ref, cwg_ref, cwv_ref, cbg_ref, cbv_ref, wd_ref, x1_ref,
                       g_ref, stg_ref, stv_ref, y_ref, convg_ref, convv_ref,
                       *, nb, t_new, n_j, final_norm):
    j = pl.program_id(0)

    @pl.when(j == 0)
    def _():
        y_ref[...] = jnp.zeros(y_ref.shape, F32)

    h = h_ref[...]
    sub = FF_TILE // FF_SUB
    acts = []
    for s in range(FF_SUB):
        cs = slice(s * sub, (s + 1) * sub)
        mixed = []
        for w_ref, cw_ref, cb_ref, st_ref, conv_ref in (
                (wg_ref, cwg_ref, cbg_ref, stg_ref, convg_ref),
                (wv_ref, cwv_ref, cbv_ref, stv_ref, convv_ref)):
            up = jnp.dot(h, w_ref[:, cs], preferred_element_type=F32)
            s0, s1 = st_ref[:, 0, cs], st_ref[:, 1, cs]
            u1 = jnp.concatenate([s1, up[:(t_new - 1) * nb]], axis=0)
            u2 = jnp.concatenate([s0, s1, up[:(t_new - 2) * nb]], axis=0)
            mixed.append(_conv_taps(up, u1, u2, cw_ref[:, cs], cb_ref[:, cs]))
            conv_ref[:, 0, cs] = up[(t_new - 2) * nb:(t_new - 1) * nb]
            conv_ref[:, 1, cs] = up[(t_new - 1) * nb:]
        gate, val = mixed
        acts.append((gate * _sigmoid(gate) * val).astype(BF16))
    _ffn_finish(j, n_j, final_norm, acts, wd_ref, x1_ref, g_ref, y_ref)


def _ffn_weight_specs(d, n_j, ix):
    gate = lambda rows: pl.BlockSpec((rows, FF_TILE), ix(lambda j: (0, j)))
    val = lambda rows: pl.BlockSpec((rows, FF_TILE), ix(lambda j: (0, n_j + j)))
    return [gate(d), val(d), gate(3), val(3), gate(1), val(1),
            pl.BlockSpec((FF_TILE, d), ix(lambda j: (j, 0)))]


def _ffn_prompt(h2, x1, wu, cw, cb, wd, g, batch, seq, tm, final_norm):
    m, d = x1.shape
    d_ff = wd.shape[0]
    n_j = d_ff // FF_TILE
    tps = seq // tm
    kern = functools.partial(_ffn_prompt_kernel, tm=tm, tiles_per_seq=tps, n_j=n_j,
                             final_norm=final_norm)
    ix = lambda f: (lambda i, j: f(j))
    conv_spec = pl.BlockSpec((None, SUBLANES, FF_TILE), lambda i, j: (i, 0, j))
    conv_shape = jax.ShapeDtypeStruct((m // tm, SUBLANES, d_ff), F32)
    y, cg, cv = pl.pallas_call(
        kern,
        grid=(m // tm, n_j),
        in_specs=[pl.BlockSpec((tm, d), lambda i, j: (i, 0))] + _ffn_weight_specs(d, n_j, ix)
                 + [pl.BlockSpec((tm, d), lambda i, j: (i, 0)),
                    pl.BlockSpec((1, d), lambda i, j: (0, 0))],
        out_specs=[pl.BlockSpec((tm, d), lambda i, j: (i, 0)), conv_spec, conv_spec],
        out_shape=[jax.ShapeDtypeStruct((m, d), F32), conv_shape, conv_shape],
        scratch_shapes=[pltpu.VMEM((n_j, SUBLANES, 2 * FF_TILE), F32)],
        compiler_params=_params("arbitrary", "arbitrary"),
        name="ffn_prompt",
    )(h2, wu, wu, cw, cw, cb, cb, wd, x1, g.reshape(1, d))
    return y, jnp.concatenate([cg, cv], axis=-1)[tps - 1::tps, SUBLANES - 2:]


def _ffn_sample(h2, x1, wu, cw, cb, wd, g, state, nb, t_new, final_norm):
    m, d = x1.shape
    d_ff = wd.shape[0]
    n_j = d_ff // FF_TILE
    kern = functools.partial(_ffn_sample_kernel, nb=nb, t_new=t_new, n_j=n_j,
                             final_norm=final_norm)
    ix = lambda f: f
    st_gate = pl.BlockSpec((nb, 2, FF_TILE), lambda j: (0, 0, j))
    st_val = pl.BlockSpec((nb, 2, FF_TILE), lambda j: (0, 0, n_j + j))
    conv_spec = pl.BlockSpec((nb, 2, FF_TILE), lambda j: (0, 0, j))
    y, cg, cv = pl.pallas_call(
        kern,
        grid=(n_j,),
        in_specs=[pl.BlockSpec((m, d), lambda j: (0, 0))] + _ffn_weight_specs(d, n_j, ix)
                 + [pl.BlockSpec((m, d), lambda j: (0, 0)),
                    pl.BlockSpec((1, d), lambda j: (0, 0)),
                    st_gate, st_val],
        out_specs=[pl.BlockSpec((m, d), lambda j: (0, 0)), conv_spec, conv_spec],
        out_shape=[jax.ShapeDtypeStruct((m, d), F32),
                   jax.ShapeDtypeStruct((nb, 2, d_ff), F32),
                   jax.ShapeDtypeStruct((nb, 2, d_ff), F32)],
        compiler_params=_params("arbitrary"),
        name="ffn_sample",
    )(h2, wu, wu, cw, cw, cb, cb, wd, x1, g.reshape(1, d), state, state)
    return y, jnp.concatenate([cg, cv], axis=-1)


def kernel(x_prompt, x_sample, cache_k, cache_v, cache_kidx, state_s5_re, state_s5_im, state_ffn_conv, page_table, norm_mix, w_in, w_attn_proj, s5_a_re, s5_a_im, s5_log_step, s5_b_re, s5_b_im, s5_c_re, s5_c_im, s5_d, w_glu, w_out, norm_ffn, w_up, conv_w, conv_b, w_down, norm_final):
    depth = w_in.shape[0]
    batch, seq, d_model = x_prompt.shape
    nb, t_new, _ = x_sample.shape
    d_head = cache_k.shape[-1]
    d_idx = cache_kidx.shape[-1]
    attn_w = N_HEADS * d_head
    kv_w = N_KV_HEADS * d_head
    qi_w = IDX_HEADS * d_idx
    groups, p_state = s5_a_re.shape[1], s5_a_re.shape[2]
    s5_w = groups * S5_GROUP

    xp = x_prompt.reshape(batch * seq, d_model)
    xs = x_sample.reshape(nb * t_new, d_model)
    tm_p = min(512, batch * seq)
    tm_s = nb * t_new

    outs = {name: [] for name in ("kp", "vp", "kip", "srp", "sip", "cp",
                                  "ks", "vs", "kis", "srs", "sis", "cs")}
    for l in range(depth):
        w_t = jnp.swapaxes(w_in[l], 0, 1).astype(BF16)
        o = 0
        seg = {}
        for name, width in (("q", attn_w), ("k", kv_w), ("v", kv_w), ("qi", qi_w), ("ki", d_idx),
                            ("wi", IDX_HEADS), ("u", s5_w), ("ga", d_model), ("gb", d_model)):
            seg[name] = w_t[o:o + width]
            o += width
        w_q_grouped = _group_major_heads(seg["q"], d_head)
        w_u, w_ga, w_gb = seg["u"], seg["ga"], seg["gb"]
        pad = jnp.zeros((LANES - d_idx - IDX_HEADS, d_model), BF16)
        w_small = jnp.concatenate([seg["qi"], seg["k"], seg["v"], seg["ki"], seg["wi"], pad], axis=0)
        w_qiw = jnp.concatenate([seg["qi"], seg["ki"], seg["wi"], pad], axis=0)
        w_kv_t = jnp.concatenate([seg["k"], seg["v"], seg["ki"]], axis=0)
        w_attn_grouped = _group_major_heads(w_attn_proj[l].astype(BF16), d_head)
        w_g = w_glu[l].astype(BF16)
        w_o = w_out[l].astype(BF16)
        wu = w_up[l].astype(BF16)
        cw = conv_w[l].astype(F32)
        cb = conv_b[l].astype(F32).reshape(1, -1)
        wd = w_down[l].astype(BF16)
        s5p = _s5_params(s5_a_re[l], s5_a_im[l], s5_log_step[l], s5_b_re[l], s5_b_im[l],
                         s5_c_re[l], s5_c_im[l], STEP_CHUNK)
        last = l == depth - 1

        def project(h, tm, ws):
            q, u, small = _matmul(h, [w_q_grouped, w_u, ws], tm, F32, "proj_q_u_small")
            (ga,) = _matmul(h, [w_ga], tm, F32, "proj_ga")
            (gb,) = _matmul(h, [w_gb], tm, F32, "proj_gb")
            return q, u, ga, gb, small

        h = _rmsnorm(xp, norm_mix[l], tm_p, BF16)
        q, u, ga, gb, qiw = project(h, tm_p, w_qiw)
        k_t, v_t, ki_t, kb_t, vb_t, kib_t = _kv_transposed(h, w_kv_t, batch, seq, kv_w, d_idx,
                                                            min(512, seq))
        k_sel = min(TOPK_MAX, seq // 4)
        attn = _prompt_attention(q, qiw, kib_t, kb_t, vb_t, batch, seq, d_head, d_idx, k_sel,
                                 min(256, seq))
        y5, srp, sip = _s5_prompt(u, batch, seq, s5p, s5_d[l], groups, p_state)
        merged = _merge(attn, y5, ga, gb, w_attn_grouped, w_g, min(256, batch * seq))
        x1, h2 = _out_proj(merged, xp, w_o, norm_ffn[l], tm_p)
        xp, conv_p = _ffn_prompt(h2, x1, wu, cw, cb, wd, norm_final, batch, seq, tm_p, last)

        def heads_last(a_t):
            return jnp.transpose(a_t.reshape(batch, N_KV_HEADS, d_head, seq), (0, 3, 1, 2))

        outs["kp"].append(heads_last(k_t)); outs["vp"].append(heads_last(v_t))
        outs["kip"].append(jnp.swapaxes(ki_t, 1, 2))
        outs["srp"].append(srp); outs["sip"].append(sip); outs["cp"].append(conv_p)

        h = _rmsnorm(xs, norm_mix[l], tm_s, BF16)
        q, u, ga, gb, small = project(h, tm_s, w_small)
        attn = _sample_attention(q, small, cache_k[l], cache_v[l], cache_kidx[l], page_table,
                                 t_new, d_head, d_idx)
        y5, srs, sis = _s5_sample(u, nb, t_new, s5p, s5_d[l], state_s5_re[l], state_s5_im[l])
        merged = _merge(attn, y5, ga, gb, w_attn_grouped, w_g, min(256, nb * t_new))
        x1, h2 = _out_proj(merged, xs, w_o, norm_ffn[l], tm_s)

        def time_major(a):
            return a.reshape(nb, t_new, -1).swapaxes(0, 1).reshape(nb * t_new, -1)

        y_tm, conv_s = _ffn_sample(time_major(h2), time_major(x1), wu, cw, cb, wd, norm_final,
                                   state_ffn_conv[l].astype(F32), nb, t_new, last)
        xs = y_tm.reshape(t_new, nb, d_model).swapaxes(0, 1).reshape(nb * t_new, d_model)
        k_new = small[:, qi_w:qi_w + kv_w].reshape(nb, t_new, N_KV_HEADS, d_head)
        v_new = small[:, qi_w + kv_w:qi_w + 2 * kv_w].reshape(nb, t_new, N_KV_HEADS, d_head)
        ki_new = small[:, qi_w + 2 * kv_w:qi_w + 2 * kv_w + d_idx].reshape(nb, t_new, d_idx)
        outs["ks"].append(k_new); outs["vs"].append(v_new); outs["kis"].append(ki_new)
        outs["srs"].append(srs); outs["sis"].append(sis); outs["cs"].append(conv_s)

    stk = {name: jnp.stack(v) for name, v in outs.items()}
    return (xp.reshape(batch, seq, d_model), xs.reshape(nb, t_new, d_model),
            stk["kp"], stk["vp"], stk["kip"], stk["srp"], stk["sip"], stk["cp"],
            stk["ks"], stk["vs"], stk["kis"], stk["srs"], stk["sis"], stk["cs"])
```

```python
import functools
import math

import jax
import jax.numpy as jnp
from jax import lax
from jax.experimental import pallas as pl
from jax.experimental.pallas import tpu as pltpu

F32 = jnp.float32
BF16 = jnp.bfloat16

EPS = 1e-6
LOG2E = math.log2(math.e)
TOPK_MAX = 256
N_HEADS = 16
N_KV_HEADS = 4
KV_GROUP = N_HEADS // N_KV_HEADS
IDX_HEADS = 8
S5_GROUP = 16
STEP_CHUNK = 8
LANES = 128
SUBLANES = 8
GROUPS_PER_TILE = LANES // S5_GROUP
FF_TILE = 512
VMEM_LIMIT = 48 * 1024 * 1024
NEG_INF = float("-inf")
INT_MIN = -2 ** 31


def _params(*sem):
    return pltpu.CompilerParams(dimension_semantics=sem, vmem_limit_bytes=VMEM_LIMIT)


def _sigmoid(x):
    return 0.5 * jnp.tanh(0.5 * x) + 0.5


def _gelu_tanh(x):
    c = math.sqrt(2.0 / math.pi)
    return 0.5 * x * (1.0 + jnp.tanh(c * (x + 0.044715 * (x * x * x))))


def _dot_nt(a, b):
    return lax.dot_general(a, b, (((1,), (1,)), ((), ())), preferred_element_type=F32)


def _rms_kernel(x_ref, g_ref, o_ref):
    x = x_ref[...]
    ms = jnp.mean(x * x, axis=-1, keepdims=True)
    o_ref[...] = (x * lax.rsqrt(ms + EPS) * g_ref[...]).astype(o_ref.dtype)


def _rmsnorm(x, g, tm, out_dtype):
    m, d = x.shape
    return pl.pallas_call(
        _rms_kernel,
        grid=(m // tm,),
        in_specs=[pl.BlockSpec((tm, d), lambda i: (i, 0)),
                  pl.BlockSpec((1, d), lambda i: (0, 0))],
        out_specs=pl.BlockSpec((tm, d), lambda i: (i, 0)),
        out_shape=jax.ShapeDtypeStruct((m, d), out_dtype),
        compiler_params=_params("arbitrary"),
        name="rmsnorm",
    )(x, g.reshape(1, d))


def _mm_kernel(h_ref, *refs):
    n = len(refs) // 2
    h = h_ref[...]
    for w_ref, o_ref in zip(refs[:n], refs[n:]):
        o_ref[...] = _dot_nt(h, w_ref[...]).astype(o_ref.dtype)


def _matmul(h, weights_t, tm, out_dtype, name):
    m, k = h.shape
    return pl.pallas_call(
        _mm_kernel,
        grid=(m // tm,),
        in_specs=[pl.BlockSpec((tm, k), lambda i: (i, 0))]
                 + [pl.BlockSpec(w.shape, lambda i: (0, 0)) for w in weights_t],
        out_specs=[pl.BlockSpec((tm, w.shape[0]), lambda i: (i, 0)) for w in weights_t],
        out_shape=[jax.ShapeDtypeStruct((m, w.shape[0]), out_dtype) for w in weights_t],
        compiler_params=_params("arbitrary"),
        name=name,
    )(h, *weights_t)


def _kv_t_kernel(w_ref, h_ref, k_ref, v_ref, ki_ref, kb_ref, vb_ref, kib_ref, *, kv_w):
    out = _dot_nt(w_ref[...], h_ref[...])
    for lo, hi, full_ref, half_ref in ((0, kv_w, k_ref, kb_ref), (kv_w, 2 * kv_w, v_ref, vb_ref),
                                       (2 * kv_w, out.shape[0], ki_ref, kib_ref)):
        full_ref[...] = out[lo:hi]
        half_ref[...] = out[lo:hi].astype(BF16)


def _kv_transposed(h, w_t, batch, seq, kv_w, d_idx, tn):
    d = h.shape[1]
    nt = seq // tn
    out = lambda rows: pl.BlockSpec((None, rows, tn), lambda b, i: (b, 0, i))
    shape = lambda rows, dt: jax.ShapeDtypeStruct((batch, rows, seq), dt)
    widths = (kv_w, kv_w, d_idx)
    return pl.pallas_call(
        functools.partial(_kv_t_kernel, kv_w=kv_w),
        grid=(batch, nt),
        in_specs=[pl.BlockSpec(w_t.shape, lambda b, i: (0, 0)),
                  pl.BlockSpec((tn, d), lambda b, i: (b * nt + i, 0))],
        out_specs=[out(w) for w in widths] * 2,
        out_shape=[shape(w, F32) for w in widths] + [shape(w, BF16) for w in widths],
        compiler_params=_params("arbitrary", "arbitrary"),
        name="proj_kv_transposed",
    )(w_t, h)


SEARCH_GROUP = 4


def _count(mask):
    return jnp.sum(jnp.where(mask, 1.0, 0.0), axis=1, keepdims=True)


def _topk_mask(score, col, k_sel, n_cols):
    kf = float(k_sel)
    rows = score.shape[0]

    def as_float(key):
        bits = jnp.where(key < 0, key ^ jnp.int32(0x7FFFFFFF), key)
        return lax.bitcast_convert_type(bits, F32)

    finite = score > NEG_INF
    few = _count(finite) <= kf

    cnt0 = _count(score >= 0.0)
    cand0 = jnp.where(cnt0 >= kf, jnp.int32(0), jnp.int32(INT_MIN))
    cand0 = jnp.broadcast_to(cand0, (rows, 1)).astype(jnp.int32)
    cnt0 = jnp.where(cnt0 >= kf, cnt0, float(n_cols))

    def search_body(group, carry):
        cand, cnt = carry
        for g in range(SEARCH_GROUP):
            shift = jnp.int32(30) - (group * SEARCH_GROUP + g)
            bit = jnp.where(shift >= 0, lax.shift_left(jnp.int32(1), jnp.maximum(shift, 0)), 0)
            trial = cand + bit
            c_trial = _count(score >= as_float(trial))
            ok = c_trial >= kf
            cand = jnp.where(ok, trial, cand)
            cnt = jnp.where(ok, c_trial, cnt)
        return cand, cnt

    cand, cnt = lax.fori_loop(0, -(-31 // SEARCH_GROUP), search_body, (cand0, cnt0))
    thr = as_float(cand)

    above = score > thr
    ties = score == thr
    need = kf - _count(above)
    surplus = jnp.where(few, 0.0, cnt - kf)
    n_bits = max(1, (n_cols - 1).bit_length())

    def index_step(it, m):
        trial = m + lax.shift_left(jnp.int32(1), jnp.int32(n_bits - 1) - it)
        taken = _count(jnp.logical_and(ties, col < trial))
        return jnp.where(taken <= need - 1.0, trial, m)

    def lowest_indices(_):
        return lax.fori_loop(0, n_bits, index_step, jnp.zeros((rows, 1), jnp.int32))

    def all_ties(_):
        return jnp.full((rows, 1), n_cols, jnp.int32)

    last = lax.cond(jnp.max(surplus) > 0.0, lowest_indices, all_ties, None)
    take_tie = jnp.logical_and(jnp.logical_and(ties, col <= last), need >= 1.0)
    top = jnp.logical_or(above, take_tie)
    return jnp.logical_and(finite, jnp.logical_or(few, top))


def _index_weights(kw, d_idx):
    w_scale = (d_idx ** -0.5) * (IDX_HEADS ** -0.5)
    return [kw[:, d_idx + h:d_idx + h + 1] * w_scale for h in range(IDX_HEADS)]


def _head_rows(qi, d_idx):
    return jnp.concatenate([qi[:, h * d_idx:(h + 1) * d_idx] for h in range(IDX_HEADS)], axis=0)


def _weighted_relu_sum(lg, w_cols, rows):
    score = jnp.zeros((rows, lg.shape[1]), F32)
    for h in range(IDX_HEADS):
        score = score + jnp.maximum(lg[h * rows:(h + 1) * rows], 0.0) * w_cols[h]
    return score


HEADS_PER_DOT = 2
SHORT_PREFIX_BLOCKS = 2


PROMPT_SCORE_ELEMS = 512 * 1024


def _prompt_attn_kernel(q_ref, qi_ref, kw_ref, kit_ref, kt_ref, vt_ref, prev_ref, o_ref,
                        *, tq, n_keys, k_sel, d_head, d_idx, q_block, heads_per_dot):
    del prev_ref
    n_seqs = q_ref.shape[0]
    scores = []
    for b in range(n_seqs):
        qi = qi_ref[b]
        w_cols = _index_weights(kw_ref[b], d_idx)
        kit = kit_ref[b]
        score = jnp.zeros((tq, n_keys), F32)
        for h in range(IDX_HEADS):
            lg = jnp.dot(qi[:, h * d_idx:(h + 1) * d_idx].astype(BF16), kit,
                         preferred_element_type=F32)
            score = score + jnp.maximum(lg, 0.0) * w_cols[h]
        scores.append(score)
    score = jnp.concatenate(scores, axis=0)

    col = lax.broadcasted_iota(jnp.int32, score.shape, 1)
    row = lax.broadcasted_iota(jnp.int32, (n_seqs, tq, n_keys), 1).reshape(score.shape)
    score = jnp.where(col <= q_block * tq + row, score, NEG_INF)
    bias_all = jnp.where(_topk_mask(score, col, k_sel, n_keys), 0.0, NEG_INF)

    for b in range(n_seqs):
        bias = bias_all[b * tq:(b + 1) * tq]
        q = q_ref[b] * (d_head ** -0.5 * LOG2E)
        for n in range(N_KV_HEADS):
            kt = kt_ref[b, n * d_head:(n + 1) * d_head, :]
            vt = vt_ref[b, n * d_head:(n + 1) * d_head, :]
            for g0 in range(0, KV_GROUP, heads_per_dot):
                heads = [(g0 + i) * N_KV_HEADS + n for i in range(heads_per_dot)]
                qs = jnp.concatenate([q[:, h * d_head:(h + 1) * d_head] for h in heads], axis=0)
                s = jnp.dot(qs.astype(BF16), kt, preferred_element_type=F32)
                s = (s.reshape(heads_per_dot, tq, n_keys) + bias[None]).reshape(-1, n_keys)
                m = jnp.max(s, axis=1, keepdims=True)
                p = jnp.exp2(s - m)
                l = jnp.sum(p, axis=1, keepdims=True)
                o = _dot_nt(p.astype(BF16), vt) / l
                for i, h in enumerate(heads):
                    o_ref[b, :, h * d_head:(h + 1) * d_head] = (
                        o[i * tq:(i + 1) * tq].astype(o_ref.dtype))


def _prompt_attention(q, qiw, ki_t, k_t, v_t, batch, seq, d_head, d_idx, k_sel, tq):
    attn_w = q.shape[1]
    kv_w = N_KV_HEADS * d_head
    qi_w = IDX_HEADS * d_idx
    nq = seq // tq
    q3 = q.reshape(batch, seq, attn_w)
    qiw3 = qiw.reshape(batch, seq, qiw.shape[1])
    out = jnp.zeros((batch, seq, attn_w), BF16)
    for qb in range(nq):
        n_keys = (qb + 1) * tq
        n_seqs = max(d for d in range(1, batch + 1)
                     if batch % d == 0 and (d == 1 or d * tq * n_keys <= PROMPT_SCORE_ELEMS))
        row = lambda w, c, qb=qb, n_seqs=n_seqs: pl.BlockSpec((n_seqs, tq, w), lambda b: (b, qb, c))
        keys = lambda w, n_keys=n_keys, n_seqs=n_seqs: pl.BlockSpec((n_seqs, w, n_keys),
                                                                     lambda b: (b, 0, 0))
        out = pl.pallas_call(
            functools.partial(_prompt_attn_kernel, tq=tq, n_keys=n_keys, k_sel=k_sel,
                              d_head=d_head, d_idx=d_idx, q_block=qb,
                              heads_per_dot=HEADS_PER_DOT if n_keys <= SHORT_PREFIX_BLOCKS * tq else 1),
            grid=(batch // n_seqs,),
            in_specs=[row(attn_w, 0), row(qi_w, 0), row(LANES, qi_w // LANES),
                      keys(d_idx), keys(kv_w), keys(kv_w),
                      pl.BlockSpec(memory_space=pl.ANY)],
            out_specs=row(attn_w, 0),
            out_shape=jax.ShapeDtypeStruct((batch, seq, attn_w), BF16),
            input_output_aliases={6: 0},
            compiler_params=_params("arbitrary"),
            name=f"prompt_attention_q{qb}",
        )(q3, qiw3, qiw3, ki_t, k_t, v_t, out)
    return out.reshape(batch * seq, attn_w)


def _lane_columns(cols, width):
    rows = cols[0].shape[0]
    lane = lax.broadcasted_iota(jnp.int32, (rows, width), 1)
    out = jnp.zeros((rows, width), F32)
    for j, c in enumerate(cols):
        out = jnp.where(lane == j, c, out)
    return out


def _sample_score_kernel(pt_ref, qi_ref, kw_ref, kin_ref, *rest, n_seqs, n_pages, page, t_new,
                         d_idx):
    del pt_ref
    s_ref = rest[n_seqs * n_pages]
    past = n_pages * page
    rows = SUBLANES
    pad_row = lax.broadcasted_iota(jnp.int32, (rows, page), 0) >= t_new
    first = lax.broadcasted_iota(jnp.int32, (rows, page), 1) == 0
    lane = lax.broadcasted_iota(jnp.int32, (rows, LANES), 1)
    t = lax.broadcasted_iota(jnp.int32, (rows, LANES), 0)
    causal_new = jnp.logical_and(lane <= t, t < t_new)
    for i in range(n_seqs):
        ipages = rest[i * n_pages:(i + 1) * n_pages]
        qi_rows = _head_rows(qi_ref[i], d_idx)
        qi_bf = qi_rows.astype(BF16)
        w_cols = _index_weights(kw_ref[i], d_idx)
        for p in range(n_pages):
            lg = jnp.dot(qi_bf, ipages[p][...].astype(BF16), preferred_element_type=F32)
            pad_val = jnp.where(first, 0.0, NEG_INF) if p == 0 else NEG_INF
            s_ref[i, :, p * page:(p + 1) * page] = jnp.where(
                pad_row, pad_val, _weighted_relu_sum(lg, w_cols, rows))
        kin = kin_ref[i]
        new_cols = []
        for j in range(t_new):
            lg = jnp.sum(qi_rows * kin[j:j + 1, :], axis=1, keepdims=True)
            new_cols.append(_weighted_relu_sum(lg, w_cols, rows))
        s_ref[i, :, past:] = jnp.where(causal_new, _lane_columns(new_cols, LANES), NEG_INF)


def _select_kernel(s_ref, b_ref, *, k_sel):
    score = s_ref[...]
    col = lax.broadcasted_iota(jnp.int32, score.shape, 1)
    keep = _topk_mask(score, col, k_sel, score.shape[1])
    b_ref[...] = jnp.where(keep, 0.0, NEG_INF)


SCORE_SEQS_PER_STEP = 4
ATTEND_SEQS_PER_STEP = 2


def _sample_attn_kernel(pt_ref, q_ref, kn_ref, vn_ref, bias_ref, *rest,
                        n_seqs, n_pages, page, t_new, d_head):
    del pt_ref
    pages = rest[:2 * n_seqs * n_pages]
    o_ref, kt_ref, vt_ref = rest[2 * n_seqs * n_pages:]
    for i in range(n_seqs):
        kpages = pages[i * n_pages:(i + 1) * n_pages]
        vpages = pages[(n_seqs + i) * n_pages:(n_seqs + i + 1) * n_pages]
        o_ref[i] = _sample_attend(q_ref[i], kn_ref[i], vn_ref[i], bias_ref[i], kpages, vpages,
                                  kt_ref.at[i], vt_ref.at[i], page, t_new, d_head
                                  ).astype(o_ref.dtype)


def _sample_attend(q, kn, vn, bias, kpages, vpages, kt_ref, vt_ref, page, t_new, d_head):
    n_pages = len(kpages)
    past = n_pages * page
    rows = SUBLANES
    kv_w = N_KV_HEADS * d_head

    for p in range(n_pages):
        kt_ref[:, p * page:(p + 1) * page] = kpages[p][...].astype(BF16)
        vt_ref[:, p * page:(p + 1) * page] = vpages[p][...].astype(BF16)

    bias = jnp.concatenate([bias] * N_HEADS, axis=0)

    q = q * (d_head ** -0.5 * LOG2E)
    lane_head = lax.broadcasted_iota(jnp.int32, (rows, kv_w), 1) // d_head
    q_rows = jnp.concatenate(
        [jnp.where(lane_head == n, q[:, g * kv_w:(g + 1) * kv_w], 0.0)
         for g in range(KV_GROUP) for n in range(N_KV_HEADS)], axis=0)

    s_past = jnp.dot(q_rows.astype(BF16), kt_ref[...], preferred_element_type=F32)
    s_past = s_past + bias[:, :past]
    s_new = _lane_columns(
        [jnp.sum(q_rows * kn[j:j + 1, :], axis=1, keepdims=True) for j in range(t_new)], LANES)
    s_new = s_new + bias[:, past:]

    m = jnp.maximum(jnp.max(s_past, axis=1, keepdims=True), jnp.max(s_new, axis=1, keepdims=True))
    p_past = jnp.exp2(s_past - m)
    p_new = jnp.exp2(s_new - m)
    l = jnp.sum(p_past, axis=1, keepdims=True) + jnp.sum(p_new, axis=1, keepdims=True)
    o = _dot_nt(p_past.astype(BF16), vt_ref[...])
    for j in range(t_new):
        o = o + p_new[:, j:j + 1] * vn[j:j + 1, :]
    o = o / l

    chunks = []
    for g in range(KV_GROUP):
        acc = jnp.zeros((rows, kv_w), F32)
        for n in range(N_KV_HEADS):
            r0 = (g * N_KV_HEADS + n) * rows
            acc = acc + jnp.where(lane_head == n, o[r0:r0 + rows], 0.0)
        chunks.append(acc)
    return jnp.concatenate(chunks, axis=1)


def _seq_page_map(b, pt, *, i, p, n_seqs, n_pages):
    return (pt[(b * n_seqs + i) * n_pages + p], 0, 0)


def _sample_attention(q, small, cache_k, cache_v, cache_ki, page_table, t_new, d_head, d_idx):
    nb, n_pages = page_table.shape
    n_pool, page = cache_k.shape[0], cache_k.shape[1]
    attn_w = q.shape[1]
    kv_w = N_KV_HEADS * d_head
    qi_w = IDX_HEADS * d_idx
    rows = SUBLANES
    past = n_pages * page
    k_sel = min(TOPK_MAX, (past + t_new) // 4)

    def pad_rows(a):
        a = a.reshape(nb, t_new, a.shape[-1])
        return jnp.pad(a, ((0, 0), (0, rows - t_new), (0, 0)))

    q8 = pad_rows(q)
    qi8 = pad_rows(small[:, :qi_w])
    kw8 = pad_rows(small[:, qi_w + 2 * kv_w:])
    kn = pad_rows(small[:, qi_w:qi_w + kv_w])
    vn = pad_rows(small[:, qi_w + kv_w:qi_w + 2 * kv_w])
    kin = pad_rows(small[:, qi_w + 2 * kv_w:qi_w + 2 * kv_w + d_idx])

    ck = jnp.transpose(cache_k, (0, 2, 3, 1)).reshape(n_pool, kv_w, page)
    cv = jnp.transpose(cache_v, (0, 2, 3, 1)).reshape(n_pool, kv_w, page)
    ci = jnp.transpose(cache_ki, (0, 2, 1))

    pt_flat = page_table.reshape(-1)
    n_keys = past + LANES

    def seqs_spec(w, n_seqs):
        return pl.BlockSpec((n_seqs, rows, w), lambda b, pt: (b, 0, 0))

    def seq_page_specs(w, n_seqs):
        return [pl.BlockSpec((None, w, page),
                             functools.partial(_seq_page_map, i=i, p=p, n_seqs=n_seqs,
                                               n_pages=n_pages))
                for i in range(n_seqs) for p in range(n_pages)]

    n_seqs = math.gcd(nb, SCORE_SEQS_PER_STEP)
    score = pl.pallas_call(
        functools.partial(_sample_score_kernel, n_seqs=n_seqs, n_pages=n_pages, page=page,
                          t_new=t_new, d_idx=d_idx),
        grid_spec=pltpu.PrefetchScalarGridSpec(
            num_scalar_prefetch=1,
            grid=(nb // n_seqs,),
            in_specs=[seqs_spec(qi_w, n_seqs), seqs_spec(LANES, n_seqs), seqs_spec(d_idx, n_seqs)]
                     + seq_page_specs(d_idx, n_seqs),
            out_specs=seqs_spec(n_keys, n_seqs),
        ),
        out_shape=jax.ShapeDtypeStruct((nb, rows, n_keys), F32),
        compiler_params=_params("arbitrary"),
        name="sample_scores",
    )(pt_flat, qi8, kw8, kin, *([ci] * (n_seqs * n_pages)))

    sel_rows = math.gcd(nb * rows, 256)
    bias = pl.pallas_call(
        functools.partial(_select_kernel, k_sel=k_sel),
        grid=(nb * rows // sel_rows,),
        in_specs=[pl.BlockSpec((sel_rows, n_keys), lambda i: (i, 0))],
        out_specs=pl.BlockSpec((sel_rows, n_keys), lambda i: (i, 0)),
        out_shape=jax.ShapeDtypeStruct((nb * rows, n_keys), F32),
        compiler_params=_params("arbitrary"),
        name="sample_select",
    )(score.reshape(nb * rows, n_keys)).reshape(nb, rows, n_keys)

    n_seqs = math.gcd(nb, ATTEND_SEQS_PER_STEP)
    out = pl.pallas_call(
        functools.partial(_sample_attn_kernel, n_seqs=n_seqs, n_pages=n_pages, page=page,
                          t_new=t_new, d_head=d_head),
        grid_spec=pltpu.PrefetchScalarGridSpec(
            num_scalar_prefetch=1,
            grid=(nb // n_seqs,),
            in_specs=[seqs_spec(attn_w, n_seqs), seqs_spec(kv_w, n_seqs), seqs_spec(kv_w, n_seqs),
                      seqs_spec(n_keys, n_seqs)]
                     + seq_page_specs(kv_w, n_seqs) + seq_page_specs(kv_w, n_seqs),
            out_specs=seqs_spec(attn_w, n_seqs),
            scratch_shapes=[pltpu.VMEM((n_seqs, kv_w, past), BF16),
                            pltpu.VMEM((n_seqs, kv_w, past), BF16)],
        ),
        out_shape=jax.ShapeDtypeStruct((nb, rows, attn_w), BF16),
        compiler_params=_params("arbitrary"),
        name="sample_attention",
    )(pt_flat, q8, kn, vn, bias, *([ck] * (n_seqs * n_pages)), *([cv] * (n_seqs * n_pages)))
    return out[:, :t_new].reshape(nb * t_new, attn_w)


def _group_major_heads(a, d_head):
    n = a.shape[1]
    a = a.reshape(N_KV_HEADS, KV_GROUP, d_head, n)
    return jnp.swapaxes(a, 0, 1).reshape(N_HEADS * d_head, n)


def _s5_step(xr, xi, ar, ai, bu, half):
    br, bi = bu[:, :half], bu[:, half:]
    if xr is None:
        return br, bi
    return ar * xr - ai * xi + br, ar * xi + ai * xr + bi


def _s5_inputs(u_ref, b_ref, j):
    tr, steps, _ = u_ref.shape
    x = jnp.transpose(u_ref[:, :, j * LANES:(j + 1) * LANES], (1, 0, 2))
    us = [x[s] for s in range(steps)]
    bu = jnp.dot(jnp.concatenate(us, axis=0).astype(BF16), b_ref[j], preferred_element_type=F32)
    return us, [bu[s * tr:(s + 1) * tr] for s in range(steps)]


def _s5_local_kernel(u_ref, b_ref, ar_ref, ai_ref, z_ref, *, n_tiles):
    sw = b_ref.shape[2]
    half = sw // 2
    for j in range(n_tiles):
        ar, ai = ar_ref[j:j + 1, :], ai_ref[j:j + 1, :]
        xr = xi = None
        for bu in _s5_inputs(u_ref, b_ref, j)[1]:
            xr, xi = _s5_step(xr, xi, ar, ai, bu, half)
        z_ref[:, j * sw:j * sw + half] = xr
        z_ref[:, j * sw + half:(j + 1) * sw] = xi


def _s5_carry_kernel(z_ref, ar_ref, ai_ref, xc_ref, xf_ref, *, batch, n_chunks):
    half = z_ref.shape[1] // 2
    ar, ai = ar_ref[...], ai_ref[...]

    def body(c, carry):
        new = []
        for b in range(batch):
            xr, xi = carry[b]
            row = b * n_chunks + c
            xc_ref[pl.ds(row, 1), :] = jnp.concatenate([xr, xi], axis=1)
            z = z_ref[pl.ds(row, 1), :]
            new.append((ar * xr - ai * xi + z[:, :half], ar * xi + ai * xr + z[:, half:]))
        return tuple(new)

    zero = jnp.zeros((1, half), F32)
    final = lax.fori_loop(0, n_chunks, body, tuple((zero, zero) for _ in range(batch)),
                          unroll=math.gcd(n_chunks, 4))
    for b in range(batch):
        xf_ref[b:b + 1, :] = jnp.concatenate(list(final[b]), axis=1)


def _s5_tile_outputs(u_ref, y_ref, b_ref, c_ref, ar, ai, d, xr, xi, j):
    tr, steps, _ = u_ref.shape
    half = b_ref.shape[2] // 2
    us, bus = _s5_inputs(u_ref, b_ref, j)
    xs = []
    for bu in bus:
        xr, xi = _s5_step(xr, xi, ar, ai, bu, half)
        xs.append(jnp.concatenate([xr, xi], axis=1).astype(BF16))
    y = jnp.dot(jnp.concatenate(xs, axis=0), c_ref[j], preferred_element_type=F32)
    ys = jnp.stack([y[s * tr:(s + 1) * tr] + d * us[s] for s in range(steps)], axis=0)
    y_ref[:, :, j * LANES:(j + 1) * LANES] = jnp.transpose(ys, (1, 0, 2))
    return xr, xi


def _s5_out_kernel(u_ref, x0_ref, b_ref, c_ref, ar_ref, ai_ref, d_ref, y_ref, *, n_tiles):
    sw = b_ref.shape[2]
    half = sw // 2
    for j in range(n_tiles):
        _s5_tile_outputs(u_ref, y_ref, b_ref, c_ref, ar_ref[j:j + 1, :], ai_ref[j:j + 1, :],
                         d_ref[:, j * LANES:(j + 1) * LANES],
                         x0_ref[:, j * sw:j * sw + half], x0_ref[:, j * sw + half:(j + 1) * sw], j)


def _s5_sample_kernel(u_ref, re_ref, im_ref, b_ref, c_ref, ar_ref, ai_ref, d_ref,
                      y_ref, ore_ref, oim_ref, *, n_tiles):
    half = b_ref.shape[2] // 2
    for j in range(n_tiles):
        rows = slice(j * half, (j + 1) * half)
        xr, xi = _s5_tile_outputs(u_ref, y_ref, b_ref, c_ref, ar_ref[j:j + 1, :],
                                  ai_ref[j:j + 1, :], d_ref[:, j * LANES:(j + 1) * LANES],
                                  re_ref[rows, :].T, im_ref[rows, :].T, j)
        ore_ref[rows, :] = xr.T
        oim_ref[rows, :] = xi.T


def _s5_params(a_re, a_im, log_step, b_re, b_im, c_re, c_im, chunk_steps):
    g, p = a_re.shape
    nt = g // GROUPS_PER_TILE
    a_re, a_im = a_re.astype(F32), a_im.astype(F32)
    step = jnp.exp(log_step.astype(F32))[:, None]

    def discretise(n_steps):
        mag = jnp.exp(a_re * step * n_steps)
        return mag * jnp.cos(a_im * step * n_steps), mag * jnp.sin(a_im * step * n_steps)

    lr, li = discretise(1.0)
    cr, ci = discretise(float(chunk_steps))
    den = a_re * a_re + a_im * a_im
    fr = ((lr - 1.0) * a_re + li * a_im) / den
    fi = (li * a_re - (lr - 1.0) * a_im) / den
    b_re, b_im = b_re.astype(F32), b_im.astype(F32)
    bb_re = fr[..., None] * b_re - fi[..., None] * b_im
    bb_im = fr[..., None] * b_im + fi[..., None] * b_re
    eye = jnp.eye(GROUPS_PER_TILE, dtype=BF16)

    bb = jnp.stack([bb_re, bb_im], axis=0).reshape(2, nt, GROUPS_PER_TILE, p, S5_GROUP)
    bb = jnp.transpose(bb, (1, 2, 4, 0, 3)).astype(BF16)
    b_blk = bb[:, :, :, :, None, :] * eye[None, :, None, None, :, None]
    b_blk = b_blk.reshape(nt, LANES, 2 * GROUPS_PER_TILE * p)

    cc = jnp.stack([c_re.astype(F32), -c_im.astype(F32)], axis=0)
    cc = cc.reshape(2, nt, GROUPS_PER_TILE, S5_GROUP, p)
    cc = jnp.transpose(cc, (1, 0, 2, 4, 3)).astype(BF16)
    c_blk = cc[:, :, :, :, None, :] * eye[None, None, :, None, :, None]
    c_blk = c_blk.reshape(nt, 2 * GROUPS_PER_TILE * p, LANES)

    def tiles(z):
        return z.reshape(nt, GROUPS_PER_TILE * p)

    return b_blk, c_blk, tiles(lr), tiles(li), tiles(cr), tiles(ci)


def _tiles_to_state(x, g, p):
    nb = x.shape[0]
    nt = g // GROUPS_PER_TILE
    st = x.reshape(nb, nt, 2, GROUPS_PER_TILE, p)
    return st[:, :, 0].reshape(nb, g, p), st[:, :, 1].reshape(nb, g, p)


def _s5_weight_specs(prm, d2):
    return [pl.BlockSpec(a.shape, lambda i, nd=a.ndim: (0,) * nd) for a in (*prm, d2)]


def _s5_outputs(u3, x0, prm, d_skip, tr):
    r, steps, width = u3.shape
    nt, _, sw = prm[0].shape
    d2 = d_skip.reshape(1, width).astype(F32)
    tokens = pl.BlockSpec((tr, steps, width), lambda i: (i, 0, 0))
    return pl.pallas_call(
        functools.partial(_s5_out_kernel, n_tiles=nt),
        grid=(r // tr,),
        in_specs=[tokens, pl.BlockSpec((tr, nt * sw), lambda i: (i, 0))]
                 + _s5_weight_specs(prm, d2),
        out_specs=tokens,
        out_shape=jax.ShapeDtypeStruct(u3.shape, F32),
        compiler_params=_params("arbitrary"),
        name="s5_outputs",
    )(u3, x0, *prm, d2)


def _s5_prompt(u, batch, seq, prm_all, d_skip, g, p):
    b_blk, c_blk, ar, ai, ar_c, ai_c = prm_all
    steps = STEP_CHUNK
    n_chunks = seq // steps
    r = batch * n_chunks
    width = u.shape[1]
    u3 = u.reshape(r, steps, width)
    nt, _, sw = b_blk.shape
    tr = 128
    z = pl.pallas_call(
        functools.partial(_s5_local_kernel, n_tiles=nt),
        grid=(r // tr,),
        in_specs=[pl.BlockSpec((tr, steps, width), lambda i: (i, 0, 0)),
                  pl.BlockSpec(b_blk.shape, lambda i: (0, 0, 0)),
                  pl.BlockSpec(ar.shape, lambda i: (0, 0)),
                  pl.BlockSpec(ai.shape, lambda i: (0, 0))],
        out_specs=pl.BlockSpec((tr, nt * sw), lambda i: (i, 0)),
        out_shape=jax.ShapeDtypeStruct((r, nt * sw), F32),
        compiler_params=_params("arbitrary"),
        name="s5_chunk_states",
    )(u3, b_blk, ar, ai)
    half = sw // 2
    xc, xf = pl.pallas_call(
        functools.partial(_s5_carry_kernel, batch=batch, n_chunks=n_chunks),
        grid=(nt,),
        in_specs=[pl.BlockSpec((r, sw), lambda j: (0, j)),
                  pl.BlockSpec((None, 1, half), lambda j: (j, 0, 0)),
                  pl.BlockSpec((None, 1, half), lambda j: (j, 0, 0))],
        out_specs=[pl.BlockSpec((r, sw), lambda j: (0, j)),
                   pl.BlockSpec((batch, sw), lambda j: (0, j))],
        out_shape=[jax.ShapeDtypeStruct((r, nt * sw), F32),
                   jax.ShapeDtypeStruct((batch, nt * sw), F32)],
        compiler_params=_params("arbitrary"),
        name="s5_carry",
    )(z, ar_c.reshape(nt, 1, half), ai_c.reshape(nt, 1, half))
    y3 = _s5_outputs(u3, xc, (b_blk, c_blk, ar, ai), d_skip, tr)
    s_re, s_im = _tiles_to_state(xf, g, p)
    return y3.reshape(batch * seq, width), s_re, s_im


def _s5_sample(u, nb, t_new, prm_all, d_skip, s_re, s_im):
    prm = prm_all[:4]
    g, p = s_re.shape[1], s_re.shape[2]
    width = u.shape[1]
    nt = prm[0].shape[0]
    d2 = d_skip.reshape(1, width).astype(F32)
    state_t = lambda a: jnp.transpose(a.astype(F32), (1, 2, 0)).reshape(g * p, nb)
    tokens = pl.BlockSpec((nb, t_new, width), lambda i: (0, 0, 0))
    state = pl.BlockSpec((g * p, nb), lambda i: (0, 0))
    y3, n_re, n_im = pl.pallas_call(
        functools.partial(_s5_sample_kernel, n_tiles=nt),
        grid=(1,),
        in_specs=[tokens, state, state] + _s5_weight_specs(prm, d2),
        out_specs=[tokens, state, state],
        out_shape=[jax.ShapeDtypeStruct((nb, t_new, width), F32)]
                  + [jax.ShapeDtypeStruct((g * p, nb), F32)] * 2,
        compiler_params=_params("arbitrary"),
        name="s5_sample",
    )(u.reshape(nb, t_new, width), state_t(s_re), state_t(s_im), *prm, d2)
    back = lambda a: jnp.transpose(a.reshape(g, p, nb), (2, 0, 1))
    return y3.reshape(nb * t_new, width), back(n_re), back(n_im)


def _merge_kernel(attn_ref, y_ref, ga_ref, gb_ref, wa_ref, wg_ref, o_ref):
    d = o_ref.shape[1]
    y_a = jnp.dot(attn_ref[...], wa_ref[...], preferred_element_type=F32)
    z = _gelu_tanh(y_ref[...]).astype(BF16)
    glu = jnp.dot(z, wg_ref[...], preferred_element_type=F32)
    y_b = glu[:, :d] * _sigmoid(glu[:, d:])
    o_ref[...] = (_sigmoid(ga_ref[...]) * y_a + _sigmoid(gb_ref[...]) * y_b).astype(o_ref.dtype)


def _merge(attn, y, ga, gb, w_attn, w_glu, tm):
    m, d = ga.shape
    row = lambda w: pl.BlockSpec((tm, w), lambda i: (i, 0))
    return pl.pallas_call(
        _merge_kernel,
        grid=(m // tm,),
        in_specs=[row(attn.shape[1]), row(y.shape[1]), row(d), row(d),
                  pl.BlockSpec(w_attn.shape, lambda i: (0, 0)),
                  pl.BlockSpec(w_glu.shape, lambda i: (0, 0))],
        out_specs=row(d),
        out_shape=jax.ShapeDtypeStruct((m, d), BF16),
        compiler_params=_params("arbitrary"),
        name="merge",
    )(attn, y, ga, gb, w_attn, w_glu)


def _out_proj_kernel(m_ref, x_ref, w_ref, g_ref, x1_ref, h_ref):
    x1 = x_ref[...] + jnp.dot(m_ref[...], w_ref[...], preferred_element_type=F32)
    x1_ref[...] = x1
    ms = jnp.mean(x1 * x1, axis=-1, keepdims=True)
    h_ref[...] = (x1 * lax.rsqrt(ms + EPS) * g_ref[...]).astype(h_ref.dtype)


def _out_proj(merged, x, w_out, g, tm):
    m, d = x.shape
    row = pl.BlockSpec((tm, d), lambda i: (i, 0))
    return pl.pallas_call(
        _out_proj_kernel,
        grid=(m // tm,),
        in_specs=[row, row, pl.BlockSpec(w_out.shape, lambda i: (0, 0)),
                  pl.BlockSpec((1, d), lambda i: (0, 0))],
        out_specs=[row, row],
        out_shape=[jax.ShapeDtypeStruct((m, d), F32), jax.ShapeDtypeStruct((m, d), BF16)],
        compiler_params=_params("arbitrary"),
        name="out_proj",
    )(merged, x, w_out, g.reshape(1, d))


FF_SUB = 1


def _shift_rows(up, prev, k):
    body = pltpu.roll(up, k, axis=0)
    row = lax.broadcasted_iota(jnp.int32, prev.shape, 0)
    head = jnp.where(row < k, pltpu.roll(prev, k, axis=0), body[:SUBLANES])
    return jnp.concatenate([head, body[SUBLANES:]], axis=0)


def _conv_taps(up, u1, u2, cw, cb):
    return cb + cw[0:1, :] * u2 + cw[1:2, :] * u1 + cw[2:3, :] * up


def _ffn_finish(j, n_j, final_norm, acts, wd_ref, x1_ref, g_ref, y_ref):
    y_ref[...] += jnp.dot(jnp.concatenate(acts, axis=1), wd_ref[...], preferred_element_type=F32)

    @pl.when(j == n_j - 1)
    def _():
        x2 = x1_ref[...] + y_ref[...]
        if final_norm:
            ms = jnp.mean(x2 * x2, axis=-1, keepdims=True)
            x2 = x2 * lax.rsqrt(ms + EPS) * g_ref[...]
        y_ref[...] = x2


def _ffn_prompt_kernel(h_ref, wg_ref, wv_ref, cwg_ref, cwv_ref, cbg_ref, cbv_ref, wd_ref, x1_ref,
                       g_ref, y_ref, convg_ref, convv_ref, carry_ref,
                       *, tm, tiles_per_seq, n_j, final_norm):
    i, j = pl.program_id(0), pl.program_id(1)

    @pl.when(i % tiles_per_seq == 0)
    def _():
        carry_ref[j] = jnp.zeros(carry_ref.shape[1:], F32)

    @pl.when(j == 0)
    def _():
        y_ref[...] = jnp.zeros(y_ref.shape, F32)

    h = h_ref[...]
    prev = carry_ref[j]
    sub = FF_TILE // FF_SUB
    acts, tails = [], ([], [])
    for s in range(FF_SUB):
        cs = slice(s * sub, (s + 1) * sub)
        mixed = []
        for half, (w_ref, cw_ref, cb_ref) in enumerate(((wg_ref, cwg_ref, cbg_ref),
                                                        (wv_ref, cwv_ref, cbv_ref))):
            up = jnp.dot(h, w_ref[:, cs], preferred_element_type=F32)
            p8 = prev[:, half * FF_TILE + s * sub:half * FF_TILE + (s + 1) * sub]
            mixed.append(_conv_taps(up, _shift_rows(up, p8, 1), _shift_rows(up, p8, 2),
                                    cw_ref[:, cs], cb_ref[:, cs]))
            tails[half].append(up[tm - SUBLANES:])
        gate, val = mixed
        acts.append((gate * _sigmoid(gate) * val).astype(BF16))
    tail_g = jnp.concatenate(tails[0], axis=1)
    tail_v = jnp.concatenate(tails[1], axis=1)
    carry_ref[j] = jnp.concatenate([tail_g, tail_v], axis=1)
    convg_ref[...] = tail_g
    convv_ref[...] = tail_v
    _ffn_finish(j, n_j, final_norm, acts, wd_ref, x1_ref, g_ref, y_ref)


def _ffn_sample_kernel(h_ref, wg_ref, wv_ref, cwg_ref, cwv_ref, cbg_ref, cbv_ref, wd_ref, x1_ref,
                       g_ref, stg_ref, stv_ref, y_ref, convg_ref, convv_ref,
                       *, nb, t_new, n_j, final_norm):
    j = pl.program_id(0)

    @pl.when(j == 0)
    def _():
        y_ref[...] = jnp.zeros(y_ref.shape, F32)

    h = h_ref[...]
    sub = FF_TILE // FF_SUB
    acts = []
    for s in range(FF_SUB):
        cs = slice(s * sub, (s + 1) * sub)
        mixed = []
        for w_ref, cw_ref, cb_ref, st_ref, conv_ref in (
                (wg_ref, cwg_ref, cbg_ref, stg_ref, convg_ref),
                (wv_ref, cwv_ref, cbv_ref, stv_ref, convv_ref)):
            up = jnp.dot(h, w_ref[:, cs], preferred_element_type=F32)
            s0, s1 = st_ref[:, 0, cs], st_ref[:, 1, cs]
            u1 = jnp.concatenate([s1, up[:(t_new - 1) * nb]], axis=0)
            u2 = jnp.concatenate([s0, s1, up[:(t_new - 2) * nb]], axis=0)
            mixed.append(_conv_taps(up, u1, u2, cw_ref[:, cs], cb_ref[:, cs]))
            conv_ref[:, 0, cs] = up[(t_new - 2) * nb:(t_new - 1) * nb]
            conv_ref[:, 1, cs] = up[(t_new - 1) * nb:]
        gate, val = mixed
        acts.append((gate * _sigmoid(gate) * val).astype(BF16))
    _ffn_finish(j, n_j, final_norm, acts, wd_ref, x1_ref, g_ref, y_ref)


def _ffn_weight_specs(d, n_j, ix):
    gate = lambda rows: pl.BlockSpec((rows, FF_TILE), ix(lambda j: (0, j)))
    val = lambda rows: pl.BlockSpec((rows, FF_TILE), ix(lambda j: (0, n_j + j)))
    return [gate(d), val(d), gate(3), val(3), gate(1), val(1),
            pl.BlockSpec((FF_TILE, d), ix(lambda j: (j, 0)))]


def _ffn_prompt(h2, x1, wu, cw, cb, wd, g, batch, seq, tm, final_norm):
    m, d = x1.shape
    d_ff = wd.shape[0]
    n_j = d_ff // FF_TILE
    tps = seq // tm
    kern = functools.partial(_ffn_prompt_kernel, tm=tm, tiles_per_seq=tps, n_j=n_j,
                             final_norm=final_norm)
    ix = lambda f: (lambda i, j: f(j))
    conv_spec = pl.BlockSpec((None, SUBLANES, FF_TILE), lambda i, j: (i, 0, j))
    conv_shape = jax.ShapeDtypeStruct((m // tm, SUBLANES, d_ff), F32)
    y, cg, cv = pl.pallas_call(
        kern,
        grid=(m // tm, n_j),
        in_specs=[pl.BlockSpec((tm, d), lambda i, j: (i, 0))] + _ffn_weight_specs(d, n_j, ix)
                 + [pl.BlockSpec((tm, d), lambda i, j: (i, 0)),
                    pl.BlockSpec((1, d), lambda i, j: (0, 0))],
        out_specs=[pl.BlockSpec((tm, d), lambda i, j: (i, 0)), conv_spec, conv_spec],
        out_shape=[jax.ShapeDtypeStruct((m, d), F32), conv_shape, conv_shape],
        scratch_shapes=[pltpu.VMEM((n_j, SUBLANES, 2 * FF_TILE), F32)],
        compiler_params=_params("arbitrary", "arbitrary"),
        name="ffn_prompt",
    )(h2, wu, wu, cw, cw, cb, cb, wd, x1, g.reshape(1, d))
    return y, jnp.concatenate([cg, cv], axis=-1)[tps - 1::tps, SUBLANES - 2:]


def _ffn_sample(h2, x1, wu, cw, cb, wd, g, state, nb, t_new, final_norm):
    m, d = x1.shape
    d_ff = wd.shape[0]
    n_j = d_ff // FF_TILE
    kern = functools.partial(_ffn_sample_kernel, nb=nb, t_new=t_new, n_j=n_j,
                             final_norm=final_norm)
    ix = lambda f: f
    st_gate = pl.BlockSpec((nb, 2, FF_TILE), lambda j: (0, 0, j))
    st_val = pl.BlockSpec((nb, 2, FF_TILE), lambda j: (0, 0, n_j + j))
    conv_spec = pl.BlockSpec((nb, 2, FF_TILE), lambda j: (0, 0, j))
    y, cg, cv = pl.pallas_call(
        kern,
        grid=(n_j,),
        in_specs=[pl.BlockSpec((m, d), lambda j: (0, 0))] + _ffn_weight_specs(d, n_j, ix)
                 + [pl.BlockSpec((m, d), lambda j: (0, 0)),
                    pl.BlockSpec((1, d), lambda j: (0, 0)),
                    st_gate, st_val],
        out_specs=[pl.BlockSpec((m, d), lambda j: (0, 0)), conv_spec, conv_spec],
        out_shape=[jax.ShapeDtypeStruct((m, d), F32),
                   jax.ShapeDtypeStruct((nb, 2, d_ff), F32),
                   jax.ShapeDtypeStruct((nb, 2, d_ff), F32)],
        compiler_params=_params("arbitrary"),
        name="ffn_sample",
    )(h2, wu, wu, cw, cw, cb, cb, wd, x1, g.reshape(1, d), state, state)
    return y, jnp.concatenate([cg, cv], axis=-1)


def kernel(x_prompt, x_sample, cache_k, cache_v, cache_kidx, state_s5_re, state_s5_im, state_ffn_conv, page_table, norm_mix, w_in, w_attn_proj, s5_a_re, s5_a_im, s5_log_step, s5_b_re, s5_b_im, s5_c_re, s5_c_im, s5_d, w_glu, w_out, norm_ffn, w_up, conv_w, conv_b, w_down, norm_final):
    depth = w_in.shape[0]
    batch, seq, d_model = x_prompt.shape
    nb, t_new, _ = x_sample.shape
    d_head = cache_k.shape[-1]
    d_idx = cache_kidx.shape[-1]
    attn_w = N_HEADS * d_head
    kv_w = N_KV_HEADS * d_head
    qi_w = IDX_HEADS * d_idx
    groups, p_state = s5_a_re.shape[1], s5_a_re.shape[2]
    s5_w = groups * S5_GROUP

    xp = x_prompt.reshape(batch * seq, d_model)
    xs = x_sample.reshape(nb * t_new, d_model)
    tm_p = min(512, batch * seq)
    tm_s = nb * t_new

    outs = {name: [] for name in ("kp", "vp", "kip", "srp", "sip", "cp",
                                  "ks", "vs", "kis", "srs", "sis", "cs")}
    for l in range(depth):
        w_t = jnp.swapaxes(w_in[l], 0, 1).astype(BF16)
        o = 0
        seg = {}
        for name, width in (("q", attn_w), ("k", kv_w), ("v", kv_w), ("qi", qi_w), ("ki", d_idx),
                            ("wi", IDX_HEADS), ("u", s5_w), ("ga", d_model), ("gb", d_model)):
            seg[name] = w_t[o:o + width]
            o += width
        w_q_grouped = _group_major_heads(seg["q"], d_head)
        w_u, w_ga, w_gb = seg["u"], seg["ga"], seg["gb"]
        pad = jnp.zeros((LANES - d_idx - IDX_HEADS, d_model), BF16)
        w_small = jnp.concatenate([seg["qi"], seg["k"], seg["v"], seg["ki"], seg["wi"], pad], axis=0)
        w_qiw = jnp.concatenate([seg["qi"], seg["ki"], seg["wi"], pad], axis=0)
        w_kv_t = jnp.concatenate([seg["k"], seg["v"], seg["ki"]], axis=0)
        w_attn_grouped = _group_major_heads(w_attn_proj[l].astype(BF16), d_head)
        w_g = w_glu[l].astype(BF16)
        w_o = w_out[l].astype(BF16)
        wu = w_up[l].astype(BF16)
        cw = conv_w[l].astype(F32)
        cb = conv_b[l].astype(F32).reshape(1, -1)
        wd = w_down[l].astype(BF16)
        s5p = _s5_params(s5_a_re[l], s5_a_im[l], s5_log_step[l], s5_b_re[l], s5_b_im[l],
                         s5_c_re[l], s5_c_im[l], STEP_CHUNK)
        last = l == depth - 1

        def project(h, tm, ws):
            q, u, small = _matmul(h, [w_q_grouped, w_u, ws], tm, F32, "proj_q_u_small")
            (ga,) = _matmul(h, [w_ga], tm, F32, "proj_ga")
            (gb,) = _matmul(h, [w_gb], tm, F32, "proj_gb")
            return q, u, ga, gb, small

        h = _rmsnorm(xp, norm_mix[l], tm_p, BF16)
        q, u, ga, gb, qiw = project(h, tm_p, w_qiw)
        k_t, v_t, ki_t, kb_t, vb_t, kib_t = _kv_transposed(h, w_kv_t, batch, seq, kv_w, d_idx,
                                                            min(512, seq))
        k_sel = min(TOPK_MAX, seq // 4)
        attn = _prompt_attention(q, qiw, kib_t, kb_t, vb_t, batch, seq, d_head, d_idx, k_sel,
                                 min(256, seq))
        y5, srp, sip = _s5_prompt(u, batch, seq, s5p, s5_d[l], groups, p_state)
        merged = _merge(attn, y5, ga, gb, w_attn_grouped, w_g, min(256, batch * seq))
        x1, h2 = _out_proj(merged, xp, w_o, norm_ffn[l], tm_p)
        xp, conv_p = _ffn_prompt(h2, x1, wu, cw, cb, wd, norm_final, batch, seq, tm_p, last)

        def heads_last(a_t):
            return jnp.transpose(a_t.reshape(batch, N_KV_HEADS, d_head, seq), (0, 3, 1, 2))

        outs["kp"].append(heads_last(k_t)); outs["vp"].append(heads_last(v_t))
        outs["kip"].append(jnp.swapaxes(ki_t, 1, 2))
        outs["srp"].append(srp); outs["sip"].append(sip); outs["cp"].append(conv_p)

        h = _rmsnorm(xs, norm_mix[l], tm_s, BF16)
        q, u, ga, gb, small = project(h, tm_s, w_small)
        attn = _sample_attention(q, small, cache_k[l], cache_v[l], cache_kidx[l], page_table,
                                 t_new, d_head, d_idx)
        y5, srs, sis = _s5_sample(u, nb, t_new, s5p, s5_d[l], state_s5_re[l], state_s5_im[l])
        merged = _merge(attn, y5, ga, gb, w_attn_grouped, w_g, min(256, nb * t_new))
        x1, h2 = _out_proj(merged, xs, w_o, norm_ffn[l], tm_s)

        def time_major(a):
            return a.reshape(nb, t_new, -1).swapaxes(0, 1).reshape(nb * t_new, -1)

        y_tm, conv_s = _ffn_sample(time_major(h2), time_major(x1), wu, cw, cb, wd, norm_final,
                                   state_ffn_conv[l].astype(F32), nb, t_new, last)
        xs = y_tm.reshape(t_new, nb, d_model).swapaxes(0, 1).reshape(nb * t_new, d_model)
        k_new = small[:, qi_w:qi_w + kv_w].reshape(nb, t_new, N_KV_HEADS, d_head)
        v_new = small[:, qi_w + kv_w:qi_w + 2 * kv_w].reshape(nb, t_new, N_KV_HEADS, d_head)
        ki_new = small[:, qi_w + 2 * kv_w:qi_w + 2 * kv_w + d_idx].reshape(nb, t_new, d_idx)
        outs["ks"].append(k_new); outs["vs"].append(v_new); outs["kis"].append(ki_new)
        outs["srs"].append(srs); outs["sis"].append(sis); outs["cs"].append(conv_s)

    stk = {name: jnp.stack(v) for name, v in outs.items()}
    return (xp.reshape(batch, seq, d_model), xs.reshape(nb, t_new, d_model),
            stk["kp"], stk["vp"], stk["kip"], stk["srp"], stk["sip"], stk["cp"],
            stk["ks"], stk["vs"], stk["kis"], stk["srs"], stk["sis"], stk["cs"])
```

```python
import functools
import math

import jax
import jax.numpy as jnp
from jax import lax
from jax.experimental import pallas as pl
from jax.experimental.pallas import tpu as pltpu

F32 = jnp.float32
BF16 = jnp.bfloat16

EPS = 1e-6
LOG2E = math.log2(math.e)
TOPK_MAX = 256
N_HEADS = 16
N_KV_HEADS = 4
KV_GROUP = N_HEADS // N_KV_HEADS
IDX_HEADS = 8
S5_GROUP = 16
STEP_CHUNK = 8
LANES = 128
SUBLANES = 8
GROUPS_PER_TILE = LANES // S5_GROUP
FF_TILE = 512
VMEM_LIMIT = 48 * 1024 * 1024
NEG_INF = float("-inf")
INT_MIN = -2 ** 31


def _params(*sem):
    return pltpu.CompilerParams(dimension_semantics=sem, vmem_limit_bytes=VMEM_LIMIT)


def _sigmoid(x):
    return 0.5 * jnp.tanh(0.5 * x) + 0.5


def _gelu_tanh(x):
    c = math.sqrt(2.0 / math.pi)
    return 0.5 * x * (1.0 + jnp.tanh(c * (x + 0.044715 * (x * x * x))))


def _dot_nt(a, b):
    return lax.dot_general(a, b, (((1,), (1,)), ((), ())), preferred_element_type=F32)


def _mm_kernel(h_ref, *refs):
    n = len(refs) // 2
    h = h_ref[...]
    for w_ref, o_ref in zip(refs[:n], refs[n:]):
        o_ref[...] = _dot_nt(h, w_ref[...]).astype(o_ref.dtype)


def _matmul(h, weights_t, tm, out_dtype, name):
    m, k = h.shape
    return pl.pallas_call(
        _mm_kernel,
        grid=(m // tm,),
        in_specs=[pl.BlockSpec((tm, k), lambda i: (i, 0))]
                 + [pl.BlockSpec(w.shape, lambda i: (0, 0)) for w in weights_t],
        out_specs=[pl.BlockSpec((tm, w.shape[0]), lambda i: (i, 0)) for w in weights_t],
        out_shape=[jax.ShapeDtypeStruct((m, w.shape[0]), out_dtype) for w in weights_t],
        compiler_params=_params("arbitrary"),
        name=name,
    )(h, *weights_t)


def _norm_mm_kernel(x_ref, g_ref, *refs):
    n = (len(refs) - 1) // 2
    x = x_ref[...]
    ms = jnp.mean(x * x, axis=-1, keepdims=True)
    h = (x * lax.rsqrt(ms + EPS) * g_ref[...]).astype(BF16)
    refs[2 * n][...] = h
    for w_ref, o_ref in zip(refs[:n], refs[n:2 * n]):
        o_ref[...] = _dot_nt(h, w_ref[...]).astype(o_ref.dtype)


def _norm_matmul(x, g, weights_t, tm, name):
    m, k = x.shape
    resident = lambda w: pl.BlockSpec(w.shape, lambda i: (0, 0), pipeline_mode=pl.Buffered(1))
    *outs, h = pl.pallas_call(
        _norm_mm_kernel,
        grid=(m // tm,),
        in_specs=[pl.BlockSpec((tm, k), lambda i: (i, 0)), pl.BlockSpec((1, k), lambda i: (0, 0))]
                 + [resident(w) for w in weights_t],
        out_specs=[pl.BlockSpec((tm, w.shape[0]), lambda i: (i, 0)) for w in weights_t]
                  + [pl.BlockSpec((tm, k), lambda i: (i, 0))],
        out_shape=[jax.ShapeDtypeStruct((m, w.shape[0]), F32) for w in weights_t]
                  + [jax.ShapeDtypeStruct((m, k), BF16)],
        compiler_params=_params("arbitrary"),
        name=name,
    )(x, g.reshape(1, k), *weights_t)
    return outs, h


def _kv_t_kernel(w_ref, h_ref, k_ref, v_ref, ki_ref, kb_ref, vb_ref, kib_ref, *, kv_w):
    out = _dot_nt(w_ref[...], h_ref[...])
    for lo, hi, full_ref, half_ref in ((0, kv_w, k_ref, kb_ref), (kv_w, 2 * kv_w, v_ref, vb_ref),
                                       (2 * kv_w, out.shape[0], ki_ref, kib_ref)):
        full_ref[...] = out[lo:hi]
        half_ref[...] = out[lo:hi].astype(BF16)


def _kv_transposed(h, w_t, batch, seq, kv_w, d_idx, tn):
    d = h.shape[1]
    nt = seq // tn
    out = lambda rows: pl.BlockSpec((None, rows, tn), lambda b, i: (b, 0, i))
    shape = lambda rows, dt: jax.ShapeDtypeStruct((batch, rows, seq), dt)
    widths = (kv_w, kv_w, d_idx)
    return pl.pallas_call(
        functools.partial(_kv_t_kernel, kv_w=kv_w),
        grid=(batch, nt),
        in_specs=[pl.BlockSpec(w_t.shape, lambda b, i: (0, 0)),
                  pl.BlockSpec((tn, d), lambda b, i: (b * nt + i, 0))],
        out_specs=[out(w) for w in widths] * 2,
        out_shape=[shape(w, F32) for w in widths] + [shape(w, BF16) for w in widths],
        compiler_params=_params("arbitrary", "arbitrary"),
        name="proj_kv_transposed",
    )(w_t, h)


SEARCH_GROUP = 4


def _count(mask):
    return jnp.sum(jnp.where(mask, 1.0, 0.0), axis=1, keepdims=True)


def _topk_mask(score, col, k_sel, n_cols):
    kf = float(k_sel)
    rows = score.shape[0]

    def as_float(key):
        bits = jnp.where(key < 0, key ^ jnp.int32(0x7FFFFFFF), key)
        return lax.bitcast_convert_type(bits, F32)

    finite = score > NEG_INF
    few = _count(finite) <= kf

    cnt0 = _count(score >= 0.0)
    cand0 = jnp.where(cnt0 >= kf, jnp.int32(0), jnp.int32(INT_MIN))
    cand0 = jnp.broadcast_to(cand0, (rows, 1)).astype(jnp.int32)
    cnt0 = jnp.where(cnt0 >= kf, cnt0, float(n_cols))

    def search_body(group, carry):
        cand, cnt = carry
        for g in range(SEARCH_GROUP):
            shift = jnp.int32(30) - (group * SEARCH_GROUP + g)
            bit = jnp.where(shift >= 0, lax.shift_left(jnp.int32(1), jnp.maximum(shift, 0)), 0)
            trial = cand + bit
            c_trial = _count(score >= as_float(trial))
            ok = c_trial >= kf
            cand = jnp.where(ok, trial, cand)
            cnt = jnp.where(ok, c_trial, cnt)
        return cand, cnt

    cand, cnt = lax.fori_loop(0, -(-31 // SEARCH_GROUP), search_body, (cand0, cnt0))
    thr = as_float(cand)

    above = score > thr
    ties = score == thr
    need = kf - _count(above)
    surplus = jnp.where(few, 0.0, cnt - kf)
    n_bits = max(1, (n_cols - 1).bit_length())

    def index_step(it, m):
        trial = m + lax.shift_left(jnp.int32(1), jnp.int32(n_bits - 1) - it)
        taken = _count(jnp.logical_and(ties, col < trial))
        return jnp.where(taken <= need - 1.0, trial, m)

    def lowest_indices(_):
        return lax.fori_loop(0, n_bits, index_step, jnp.zeros((rows, 1), jnp.int32))

    def all_ties(_):
        return jnp.full((rows, 1), n_cols, jnp.int32)

    last = lax.cond(jnp.max(surplus) > 0.0, lowest_indices, all_ties, None)
    take_tie = jnp.logical_and(jnp.logical_and(ties, col <= last), need >= 1.0)
    top = jnp.logical_or(above, take_tie)
    return jnp.logical_and(finite, jnp.logical_or(few, top))


def _index_weights(kw, d_idx):
    w_scale = (d_idx ** -0.5) * (IDX_HEADS ** -0.5)
    return [kw[:, d_idx + h:d_idx + h + 1] * w_scale for h in range(IDX_HEADS)]


def _head_rows(qi, d_idx):
    return jnp.concatenate([qi[:, h * d_idx:(h + 1) * d_idx] for h in range(IDX_HEADS)], axis=0)


def _weighted_relu_sum(lg, w_cols, rows):
    score = jnp.zeros((rows, lg.shape[1]), F32)
    for h in range(IDX_HEADS):
        score = score + jnp.maximum(lg[h * rows:(h + 1) * rows], 0.0) * w_cols[h]
    return score


HEADS_PER_DOT = 2
SHORT_PREFIX_BLOCKS = 2


PROMPT_SCORE_ELEMS = 512 * 1024


def _prompt_attn_kernel(q_ref, qi_ref, kw_ref, kit_ref, kt_ref, vt_ref, prev_ref, o_ref,
                        *, tq, n_keys, k_sel, d_head, d_idx, q_block, heads_per_dot):
    del prev_ref
    n_seqs = q_ref.shape[0]
    scores = []
    for b in range(n_seqs):
        qi = qi_ref[b]
        w_cols = _index_weights(kw_ref[b], d_idx)
        kit = kit_ref[b]
        score = jnp.zeros((tq, n_keys), F32)
        for h in range(IDX_HEADS):
            lg = jnp.dot(qi[:, h * d_idx:(h + 1) * d_idx].astype(BF16), kit,
                         preferred_element_type=F32)
            score = score + jnp.maximum(lg, 0.0) * w_cols[h]
        scores.append(score)
    score = jnp.concatenate(scores, axis=0)

    col = lax.broadcasted_iota(jnp.int32, score.shape, 1)
    row = lax.broadcasted_iota(jnp.int32, (n_seqs, tq, n_keys), 1).reshape(score.shape)
    score = jnp.where(col <= q_block * tq + row, score, NEG_INF)
    bias_all = jnp.where(_topk_mask(score, col, k_sel, n_keys), 0.0, NEG_INF)

    for b in range(n_seqs):
        bias = bias_all[b * tq:(b + 1) * tq]
        q = q_ref[b] * (d_head ** -0.5 * LOG2E)
        for n in range(N_KV_HEADS):
            kt = kt_ref[b, n * d_head:(n + 1) * d_head, :]
            vt = vt_ref[b, n * d_head:(n + 1) * d_head, :]
            for g0 in range(0, KV_GROUP, heads_per_dot):
                heads = [(g0 + i) * N_KV_HEADS + n for i in range(heads_per_dot)]
                qs = jnp.concatenate([q[:, h * d_head:(h + 1) * d_head] for h in heads], axis=0)
                s = jnp.dot(qs.astype(BF16), kt, preferred_element_type=F32)
                s = (s.reshape(heads_per_dot, tq, n_keys) + bias[None]).reshape(-1, n_keys)
                m = jnp.max(s, axis=1, keepdims=True)
                p = jnp.exp2(s - m)
                l = jnp.sum(p, axis=1, keepdims=True)
                o = _dot_nt(p.astype(BF16), vt) / l
                for i, h in enumerate(heads):
                    o_ref[b, :, h * d_head:(h + 1) * d_head] = (
                        o[i * tq:(i + 1) * tq].astype(o_ref.dtype))


def _prompt_attention(q, qiw, ki_t, k_t, v_t, batch, seq, d_head, d_idx, k_sel, tq):
    attn_w = q.shape[1]
    kv_w = N_KV_HEADS * d_head
    qi_w = IDX_HEADS * d_idx
    nq = seq // tq
    q3 = q.reshape(batch, seq, attn_w)
    qiw3 = qiw.reshape(batch, seq, qiw.shape[1])
    out = jnp.zeros((batch, seq, attn_w), BF16)
    for qb in range(nq):
        n_keys = (qb + 1) * tq
        n_seqs = max(d for d in range(1, batch + 1)
                     if batch % d == 0 and (d == 1 or d * tq * n_keys <= PROMPT_SCORE_ELEMS))
        row = lambda w, c, qb=qb, n_seqs=n_seqs: pl.BlockSpec((n_seqs, tq, w), lambda b: (b, qb, c))
        keys = lambda w, n_keys=n_keys, n_seqs=n_seqs: pl.BlockSpec((n_seqs, w, n_keys),
                                                                     lambda b: (b, 0, 0))
        heads_per_dot = HEADS_PER_DOT if n_keys <= SHORT_PREFIX_BLOCKS * tq else 1
        out = pl.pallas_call(
            functools.partial(_prompt_attn_kernel, tq=tq, n_keys=n_keys, k_sel=k_sel,
                              d_head=d_head, d_idx=d_idx, q_block=qb,
                              heads_per_dot=heads_per_dot),
            grid=(batch // n_seqs,),
            in_specs=[row(attn_w, 0), row(qi_w, 0), row(LANES, qi_w // LANES),
                      keys(d_idx), keys(kv_w), keys(kv_w),
                      pl.BlockSpec(memory_space=pl.ANY)],
            out_specs=row(attn_w, 0),
            out_shape=jax.ShapeDtypeStruct((batch, seq, attn_w), BF16),
            input_output_aliases={6: 0},
            compiler_params=_params("arbitrary"),
            name=f"prompt_attention_q{qb}",
        )(q3, qiw3, qiw3, ki_t, k_t, v_t, out)
    return out.reshape(batch * seq, attn_w)


def _lane_columns(cols, width):
    rows = cols[0].shape[0]
    lane = lax.broadcasted_iota(jnp.int32, (rows, width), 1)
    out = jnp.zeros((rows, width), F32)
    for j, c in enumerate(cols):
        out = jnp.where(lane == j, c, out)
    return out


def _sample_score_kernel(pt_ref, qi_ref, kw_ref, kin_ref, *rest, n_seqs, n_pages, page, t_new,
                         d_idx):
    del pt_ref
    s_ref = rest[n_seqs * n_pages]
    past = n_pages * page
    rows = SUBLANES
    pad_row = lax.broadcasted_iota(jnp.int32, (rows, page), 0) >= t_new
    first = lax.broadcasted_iota(jnp.int32, (rows, page), 1) == 0
    lane = lax.broadcasted_iota(jnp.int32, (rows, LANES), 1)
    t = lax.broadcasted_iota(jnp.int32, (rows, LANES), 0)
    causal_new = jnp.logical_and(lane <= t, t < t_new)
    for i in range(n_seqs):
        ipages = rest[i * n_pages:(i + 1) * n_pages]
        qi_rows = _head_rows(qi_ref[i], d_idx)
        qi_bf = qi_rows.astype(BF16)
        w_cols = _index_weights(kw_ref[i], d_idx)
        for p in range(n_pages):
            lg = jnp.dot(qi_bf, ipages[p][...].astype(BF16), preferred_element_type=F32)
            pad_val = jnp.where(first, 0.0, NEG_INF) if p == 0 else NEG_INF
            s_ref[i, :, p * page:(p + 1) * page] = jnp.where(
                pad_row, pad_val, _weighted_relu_sum(lg, w_cols, rows))
        kin = kin_ref[i]
        new_cols = []
        for j in range(t_new):
            lg = jnp.sum(qi_rows * kin[j:j + 1, :], axis=1, keepdims=True)
            new_cols.append(_weighted_relu_sum(lg, w_cols, rows))
        s_ref[i, :, past:] = jnp.where(causal_new, _lane_columns(new_cols, LANES), NEG_INF)


def _select_kernel(s_ref, b_ref, *, k_sel):
    score = s_ref[...]
    col = lax.broadcasted_iota(jnp.int32, score.shape, 1)
    keep = _topk_mask(score, col, k_sel, score.shape[1])
    b_ref[...] = jnp.where(keep, 0.0, NEG_INF)


SCORE_SEQS_PER_STEP = 4
ATTEND_SEQS_PER_STEP = 2


def _sample_attn_kernel(pt_ref, q_ref, kn_ref, vn_ref, bias_ref, *rest,
                        n_seqs, n_pages, page, t_new, d_head):
    del pt_ref
    pages = rest[:2 * n_seqs * n_pages]
    o_ref, kt_ref, vt_ref = rest[2 * n_seqs * n_pages:]
    for i in range(n_seqs):
        kpages = pages[i * n_pages:(i + 1) * n_pages]
        vpages = pages[(n_seqs + i) * n_pages:(n_seqs + i + 1) * n_pages]
        o_ref[i] = _sample_attend(q_ref[i], kn_ref[i], vn_ref[i], bias_ref[i], kpages, vpages,
                                  kt_ref.at[i], vt_ref.at[i], page, t_new, d_head
                                  ).astype(o_ref.dtype)


def _sample_attend(q, kn, vn, bias, kpages, vpages, kt_ref, vt_ref, page, t_new, d_head):
    n_pages = len(kpages)
    past = n_pages * page
    rows = SUBLANES
    kv_w = N_KV_HEADS * d_head

    for p in range(n_pages):
        kt_ref[:, p * page:(p + 1) * page] = kpages[p][...].astype(BF16)
        vt_ref[:, p * page:(p + 1) * page] = vpages[p][...].astype(BF16)

    bias = jnp.concatenate([bias] * N_HEADS, axis=0)

    q = q * (d_head ** -0.5 * LOG2E)
    lane_head = lax.broadcasted_iota(jnp.int32, (rows, kv_w), 1) // d_head
    q_rows = jnp.concatenate(
        [jnp.where(lane_head == n, q[:, g * kv_w:(g + 1) * kv_w], 0.0)
         for g in range(KV_GROUP) for n in range(N_KV_HEADS)], axis=0)

    s_past = jnp.dot(q_rows.astype(BF16), kt_ref[...], preferred_element_type=F32)
    s_past = s_past + bias[:, :past]
    s_new = _lane_columns(
        [jnp.sum(q_rows * kn[j:j + 1, :], axis=1, keepdims=True) for j in range(t_new)], LANES)
    s_new = s_new + bias[:, past:]

    m = jnp.maximum(jnp.max(s_past, axis=1, keepdims=True), jnp.max(s_new, axis=1, keepdims=True))
    p_past = jnp.exp2(s_past - m)
    p_new = jnp.exp2(s_new - m)
    l = jnp.sum(p_past, axis=1, keepdims=True) + jnp.sum(p_new, axis=1, keepdims=True)
    o = _dot_nt(p_past.astype(BF16), vt_ref[...])
    for j in range(t_new):
        o = o + p_new[:, j:j + 1] * vn[j:j + 1, :]
    o = o / l

    chunks = []
    for g in range(KV_GROUP):
        acc = jnp.zeros((rows, kv_w), F32)
        for n in range(N_KV_HEADS):
            r0 = (g * N_KV_HEADS + n) * rows
            acc = acc + jnp.where(lane_head == n, o[r0:r0 + rows], 0.0)
        chunks.append(acc)
    return jnp.concatenate(chunks, axis=1)


def _seq_page_map(b, pt, *, i, p, n_seqs, n_pages):
    return (pt[(b * n_seqs + i) * n_pages + p], 0, 0)


def _sample_attention(q, small, cache_k, cache_v, cache_ki, page_table, t_new, d_head, d_idx):
    nb, n_pages = page_table.shape
    n_pool, page = cache_k.shape[0], cache_k.shape[1]
    attn_w = q.shape[1]
    kv_w = N_KV_HEADS * d_head
    qi_w = IDX_HEADS * d_idx
    rows = SUBLANES
    past = n_pages * page
    k_sel = min(TOPK_MAX, (past + t_new) // 4)

    def pad_rows(a):
        a = a.reshape(nb, t_new, a.shape[-1])
        return jnp.pad(a, ((0, 0), (0, rows - t_new), (0, 0)))

    q8 = pad_rows(q)
    qi8 = pad_rows(small[:, :qi_w])
    kw8 = pad_rows(small[:, qi_w + 2 * kv_w:])
    kn = pad_rows(small[:, qi_w:qi_w + kv_w])
    vn = pad_rows(small[:, qi_w + kv_w:qi_w + 2 * kv_w])
    kin = pad_rows(small[:, qi_w + 2 * kv_w:qi_w + 2 * kv_w + d_idx])

    ck = jnp.transpose(cache_k, (0, 2, 3, 1)).reshape(n_pool, kv_w, page)
    cv = jnp.transpose(cache_v, (0, 2, 3, 1)).reshape(n_pool, kv_w, page)
    ci = jnp.transpose(cache_ki, (0, 2, 1))

    pt_flat = page_table.reshape(-1)
    n_keys = past + LANES

    def seqs_spec(w, n_seqs):
        return pl.BlockSpec((n_seqs, rows, w), lambda b, pt: (b, 0, 0))

    def seq_page_specs(w, n_seqs):
        return [pl.BlockSpec((None, w, page),
                             functools.partial(_seq_page_map, i=i, p=p, n_seqs=n_seqs,
                                               n_pages=n_pages))
                for i in range(n_seqs) for p in range(n_pages)]

    n_seqs = math.gcd(nb, SCORE_SEQS_PER_STEP)
    score = pl.pallas_call(
        functools.partial(_sample_score_kernel, n_seqs=n_seqs, n_pages=n_pages, page=page,
                          t_new=t_new, d_idx=d_idx),
        grid_spec=pltpu.PrefetchScalarGridSpec(
            num_scalar_prefetch=1,
            grid=(nb // n_seqs,),
            in_specs=[seqs_spec(qi_w, n_seqs), seqs_spec(LANES, n_seqs), seqs_spec(d_idx, n_seqs)]
                     + seq_page_specs(d_idx, n_seqs),
            out_specs=seqs_spec(n_keys, n_seqs),
        ),
        out_shape=jax.ShapeDtypeStruct((nb, rows, n_keys), F32),
        compiler_params=_params("arbitrary"),
        name="sample_scores",
    )(pt_flat, qi8, kw8, kin, *([ci] * (n_seqs * n_pages)))

    sel_rows = math.gcd(nb * rows, 256)
    bias = pl.pallas_call(
        functools.partial(_select_kernel, k_sel=k_sel),
        grid=(nb * rows // sel_rows,),
        in_specs=[pl.BlockSpec((sel_rows, n_keys), lambda i: (i, 0))],
        out_specs=pl.BlockSpec((sel_rows, n_keys), lambda i: (i, 0)),
        out_shape=jax.ShapeDtypeStruct((nb * rows, n_keys), F32),
        compiler_params=_params("arbitrary"),
        name="sample_select",
    )(score.reshape(nb * rows, n_keys)).reshape(nb, rows, n_keys)

    n_seqs = math.gcd(nb, ATTEND_SEQS_PER_STEP)
    out = pl.pallas_call(
        functools.partial(_sample_attn_kernel, n_seqs=n_seqs, n_pages=n_pages, page=page,
                          t_new=t_new, d_head=d_head),
        grid_spec=pltpu.PrefetchScalarGridSpec(
            num_scalar_prefetch=1,
            grid=(nb // n_seqs,),
            in_specs=[seqs_spec(attn_w, n_seqs), seqs_spec(kv_w, n_seqs), seqs_spec(kv_w, n_seqs),
                      seqs_spec(n_keys, n_seqs)]
                     + seq_page_specs(kv_w, n_seqs) + seq_page_specs(kv_w, n_seqs),
            out_specs=seqs_spec(attn_w, n_seqs),
            scratch_shapes=[pltpu.VMEM((n_seqs, kv_w, past), BF16),
                            pltpu.VMEM((n_seqs, kv_w, past), BF16)],
        ),
        out_shape=jax.ShapeDtypeStruct((nb, rows, attn_w), BF16),
        compiler_params=_params("arbitrary"),
        name="sample_attention",
    )(pt_flat, q8, kn, vn, bias, *([ck] * (n_seqs * n_pages)), *([cv] * (n_seqs * n_pages)))
    return out[:, :t_new].reshape(nb * t_new, attn_w)


def _group_major_heads(a, d_head):
    n = a.shape[1]
    a = a.reshape(N_KV_HEADS, KV_GROUP, d_head, n)
    return jnp.swapaxes(a, 0, 1).reshape(N_HEADS * d_head, n)


def _s5_step(xr, xi, ar, ai, bu, half):
    br, bi = bu[:, :half], bu[:, half:]
    if xr is None:
        return br, bi
    return ar * xr - ai * xi + br, ar * xi + ai * xr + bi


def _s5_inputs(u_ref, b_ref, j):
    tr, steps, _ = u_ref.shape
    x = jnp.transpose(u_ref[:, :, j * LANES:(j + 1) * LANES], (1, 0, 2))
    us = [x[s] for s in range(steps)]
    bu = jnp.dot(jnp.concatenate(us, axis=0).astype(BF16), b_ref[j], preferred_element_type=F32)
    return us, [bu[s * tr:(s + 1) * tr] for s in range(steps)]


def _s5_local_kernel(u_ref, b_ref, ar_ref, ai_ref, z_ref, *, n_tiles):
    sw = b_ref.shape[2]
    half = sw // 2
    for j in range(n_tiles):
        ar, ai = ar_ref[j:j + 1, :], ai_ref[j:j + 1, :]
        xr = xi = None
        for bu in _s5_inputs(u_ref, b_ref, j)[1]:
            xr, xi = _s5_step(xr, xi, ar, ai, bu, half)
        z_ref[:, j * sw:j * sw + half] = xr
        z_ref[:, j * sw + half:(j + 1) * sw] = xi


def _s5_carry_kernel(z_ref, ar_ref, ai_ref, xc_ref, xf_ref, *, batch, n_chunks):
    half = z_ref.shape[1] // 2
    ar, ai = ar_ref[...], ai_ref[...]

    def body(c, carry):
        new = []
        for b in range(batch):
            xr, xi = carry[b]
            row = b * n_chunks + c
            xc_ref[pl.ds(row, 1), :] = jnp.concatenate([xr, xi], axis=1)
            z = z_ref[pl.ds(row, 1), :]
            new.append((ar * xr - ai * xi + z[:, :half], ar * xi + ai * xr + z[:, half:]))
        return tuple(new)

    zero = jnp.zeros((1, half), F32)
    final = lax.fori_loop(0, n_chunks, body, tuple((zero, zero) for _ in range(batch)),
                          unroll=math.gcd(n_chunks, 4))
    for b in range(batch):
        xf_ref[b:b + 1, :] = jnp.concatenate(list(final[b]), axis=1)


def _s5_tile_outputs(u_ref, y_ref, b_ref, c_ref, ar, ai, d, xr, xi, j):
    tr, steps, _ = u_ref.shape
    half = b_ref.shape[2] // 2
    us, bus = _s5_inputs(u_ref, b_ref, j)
    xs = []
    for bu in bus:
        xr, xi = _s5_step(xr, xi, ar, ai, bu, half)
        xs.append(jnp.concatenate([xr, xi], axis=1).astype(BF16))
    y = jnp.dot(jnp.concatenate(xs, axis=0), c_ref[j], preferred_element_type=F32)
    ys = jnp.stack([y[s * tr:(s + 1) * tr] + d * us[s] for s in range(steps)], axis=0)
    y_ref[:, :, j * LANES:(j + 1) * LANES] = jnp.transpose(ys, (1, 0, 2))
    return xr, xi


def _s5_out_kernel(u_ref, x0_ref, b_ref, c_ref, ar_ref, ai_ref, d_ref, y_ref, *, n_tiles):
    sw = b_ref.shape[2]
    half = sw // 2
    for j in range(n_tiles):
        _s5_tile_outputs(u_ref, y_ref, b_ref, c_ref, ar_ref[j:j + 1, :], ai_ref[j:j + 1, :],
                         d_ref[:, j * LANES:(j + 1) * LANES],
                         x0_ref[:, j * sw:j * sw + half], x0_ref[:, j * sw + half:(j + 1) * sw], j)


def _s5_sample_kernel(u_ref, re_ref, im_ref, b_ref, c_ref, ar_ref, ai_ref, d_ref,
                      y_ref, ore_ref, oim_ref, *, n_tiles):
    half = b_ref.shape[2] // 2
    for j in range(n_tiles):
        rows = slice(j * half, (j + 1) * half)
        xr, xi = _s5_tile_outputs(u_ref, y_ref, b_ref, c_ref, ar_ref[j:j + 1, :],
                                  ai_ref[j:j + 1, :], d_ref[:, j * LANES:(j + 1) * LANES],
                                  re_ref[rows, :].T, im_ref[rows, :].T, j)
        ore_ref[rows, :] = xr.T
        oim_ref[rows, :] = xi.T


def _s5_params(a_re, a_im, log_step, b_re, b_im, c_re, c_im, chunk_steps):
    g, p = a_re.shape
    nt = g // GROUPS_PER_TILE
    a_re, a_im = a_re.astype(F32), a_im.astype(F32)
    step = jnp.exp(log_step.astype(F32))[:, None]

    def discretise(n_steps):
        mag = jnp.exp(a_re * step * n_steps)
        return mag * jnp.cos(a_im * step * n_steps), mag * jnp.sin(a_im * step * n_steps)

    lr, li = discretise(1.0)
    cr, ci = discretise(float(chunk_steps))
    den = a_re * a_re + a_im * a_im
    fr = ((lr - 1.0) * a_re + li * a_im) / den
    fi = (li * a_re - (lr - 1.0) * a_im) / den
    b_re, b_im = b_re.astype(F32), b_im.astype(F32)
    bb_re = fr[..., None] * b_re - fi[..., None] * b_im
    bb_im = fr[..., None] * b_im + fi[..., None] * b_re
    eye = jnp.eye(GROUPS_PER_TILE, dtype=BF16)

    bb = jnp.stack([bb_re, bb_im], axis=0).reshape(2, nt, GROUPS_PER_TILE, p, S5_GROUP)
    bb = jnp.transpose(bb, (1, 2, 4, 0, 3)).astype(BF16)
    b_blk = bb[:, :, :, :, None, :] * eye[None, :, None, None, :, None]
    b_blk = b_blk.reshape(nt, LANES, 2 * GROUPS_PER_TILE * p)

    cc = jnp.stack([c_re.astype(F32), -c_im.astype(F32)], axis=0)
    cc = cc.reshape(2, nt, GROUPS_PER_TILE, S5_GROUP, p)
    cc = jnp.transpose(cc, (1, 0, 2, 4, 3)).astype(BF16)
    c_blk = cc[:, :, :, :, None, :] * eye[None, None, :, None, :, None]
    c_blk = c_blk.reshape(nt, 2 * GROUPS_PER_TILE * p, LANES)

    def tiles(z):
        return z.reshape(nt, GROUPS_PER_TILE * p)

    return b_blk, c_blk, tiles(lr), tiles(li), tiles(cr), tiles(ci)


def _tiles_to_state(x, g, p):
    nb = x.shape[0]
    nt = g // GROUPS_PER_TILE
    st = x.reshape(nb, nt, 2, GROUPS_PER_TILE, p)
    return st[:, :, 0].reshape(nb, g, p), st[:, :, 1].reshape(nb, g, p)


def _s5_weight_specs(prm, d2):
    return [pl.BlockSpec(a.shape, lambda i, nd=a.ndim: (0,) * nd) for a in (*prm, d2)]


def _s5_outputs(u3, x0, prm, d_skip, tr):
    r, steps, width = u3.shape
    nt, _, sw = prm[0].shape
    d2 = d_skip.reshape(1, width).astype(F32)
    tokens = pl.BlockSpec((tr, steps, width), lambda i: (i, 0, 0))
    return pl.pallas_call(
        functools.partial(_s5_out_kernel, n_tiles=nt),
        grid=(r // tr,),
        in_specs=[tokens, pl.BlockSpec((tr, nt * sw), lambda i: (i, 0))]
                 + _s5_weight_specs(prm, d2),
        out_specs=tokens,
        out_shape=jax.ShapeDtypeStruct(u3.shape, F32),
        compiler_params=_params("arbitrary"),
        name="s5_outputs",
    )(u3, x0, *prm, d2)


def _s5_prompt(u, batch, seq, prm_all, d_skip, g, p):
    b_blk, c_blk, ar, ai, ar_c, ai_c = prm_all
    steps = STEP_CHUNK
    n_chunks = seq // steps
    r = batch * n_chunks
    width = u.shape[1]
    u3 = u.reshape(r, steps, width)
    nt, _, sw = b_blk.shape
    tr = 128
    z = pl.pallas_call(
        functools.partial(_s5_local_kernel, n_tiles=nt),
        grid=(r // tr,),
        in_specs=[pl.BlockSpec((tr, steps, width), lambda i: (i, 0, 0)),
                  pl.BlockSpec(b_blk.shape, lambda i: (0, 0, 0)),
                  pl.BlockSpec(ar.shape, lambda i: (0, 0)),
                  pl.BlockSpec(ai.shape, lambda i: (0, 0))],
        out_specs=pl.BlockSpec((tr, nt * sw), lambda i: (i, 0)),
        out_shape=jax.ShapeDtypeStruct((r, nt * sw), F32),
        compiler_params=_params("arbitrary"),
        name="s5_chunk_states",
    )(u3, b_blk, ar, ai)
    half = sw // 2
    xc, xf = pl.pallas_call(
        functools.partial(_s5_carry_kernel, batch=batch, n_chunks=n_chunks),
        grid=(nt,),
        in_specs=[pl.BlockSpec((r, sw), lambda j: (0, j)),
                  pl.BlockSpec((None, 1, half), lambda j: (j, 0, 0)),
                  pl.BlockSpec((None, 1, half), lambda j: (j, 0, 0))],
        out_specs=[pl.BlockSpec((r, sw), lambda j: (0, j)),
                   pl.BlockSpec((batch, sw), lambda j: (0, j))],
        out_shape=[jax.ShapeDtypeStruct((r, nt * sw), F32),
                   jax.ShapeDtypeStruct((batch, nt * sw), F32)],
        compiler_params=_params("arbitrary"),
        name="s5_carry",
    )(z, ar_c.reshape(nt, 1, half), ai_c.reshape(nt, 1, half))
    y3 = _s5_outputs(u3, xc, (b_blk, c_blk, ar, ai), d_skip, tr)
    s_re, s_im = _tiles_to_state(xf, g, p)
    return y3.reshape(batch * seq, width), s_re, s_im


def _s5_sample(u, nb, t_new, prm_all, d_skip, s_re, s_im):
    prm = prm_all[:4]
    g, p = s_re.shape[1], s_re.shape[2]
    width = u.shape[1]
    nt = prm[0].shape[0]
    d2 = d_skip.reshape(1, width).astype(F32)
    state_t = lambda a: jnp.transpose(a.astype(F32), (1, 2, 0)).reshape(g * p, nb)
    tokens = pl.BlockSpec((nb, t_new, width), lambda i: (0, 0, 0))
    state = pl.BlockSpec((g * p, nb), lambda i: (0, 0))
    y3, n_re, n_im = pl.pallas_call(
        functools.partial(_s5_sample_kernel, n_tiles=nt),
        grid=(1,),
        in_specs=[tokens, state, state] + _s5_weight_specs(prm, d2),
        out_specs=[tokens, state, state],
        out_shape=[jax.ShapeDtypeStruct((nb, t_new, width), F32)]
                  + [jax.ShapeDtypeStruct((g * p, nb), F32)] * 2,
        compiler_params=_params("arbitrary"),
        name="s5_sample",
    )(u.reshape(nb, t_new, width), state_t(s_re), state_t(s_im), *prm, d2)
    back = lambda a: jnp.transpose(a.reshape(g, p, nb), (2, 0, 1))
    return y3.reshape(nb * t_new, width), back(n_re), back(n_im)


def _merge_kernel(attn_ref, y_ref, ga_ref, gb_ref, wa_ref, wg_ref, o_ref):
    d = o_ref.shape[1]
    y_a = jnp.dot(attn_ref[...], wa_ref[...], preferred_element_type=F32)
    z = _gelu_tanh(y_ref[...]).astype(BF16)
    glu = jnp.dot(z, wg_ref[...], preferred_element_type=F32)
    y_b = glu[:, :d] * _sigmoid(glu[:, d:])
    o_ref[...] = (_sigmoid(ga_ref[...]) * y_a + _sigmoid(gb_ref[...]) * y_b).astype(o_ref.dtype)


def _merge(attn, y, ga, gb, w_attn, w_glu, tm):
    m, d = ga.shape
    row = lambda w: pl.BlockSpec((tm, w), lambda i: (i, 0))
    return pl.pallas_call(
        _merge_kernel,
        grid=(m // tm,),
        in_specs=[row(attn.shape[1]), row(y.shape[1]), row(d), row(d),
                  pl.BlockSpec(w_attn.shape, lambda i: (0, 0)),
                  pl.BlockSpec(w_glu.shape, lambda i: (0, 0))],
        out_specs=row(d),
        out_shape=jax.ShapeDtypeStruct((m, d), BF16),
        compiler_params=_params("arbitrary"),
        name="merge",
    )(attn, y, ga, gb, w_attn, w_glu)


def _out_proj_kernel(m_ref, x_ref, w_ref, g_ref, x1_ref, h_ref):
    x1 = x_ref[...] + jnp.dot(m_ref[...], w_ref[...], preferred_element_type=F32)
    x1_ref[...] = x1
    ms = jnp.mean(x1 * x1, axis=-1, keepdims=True)
    h_ref[...] = (x1 * lax.rsqrt(ms + EPS) * g_ref[...]).astype(h_ref.dtype)


def _out_proj(merged, x, w_out, g, tm):
    m, d = x.shape
    row = pl.BlockSpec((tm, d), lambda i: (i, 0))
    return pl.pallas_call(
        _out_proj_kernel,
        grid=(m // tm,),
        in_specs=[row, row, pl.BlockSpec(w_out.shape, lambda i: (0, 0)),
                  pl.BlockSpec((1, d), lambda i: (0, 0))],
        out_specs=[row, row],
        out_shape=[jax.ShapeDtypeStruct((m, d), F32), jax.ShapeDtypeStruct((m, d), BF16)],
        compiler_params=_params("arbitrary"),
        name="out_proj",
    )(merged, x, w_out, g.reshape(1, d))


FF_SUB = 1


def _shift_rows(up, prev, k):
    body = pltpu.roll(up, k, axis=0)
    row = lax.broadcasted_iota(jnp.int32, prev.shape, 0)
    head = jnp.where(row < k, pltpu.roll(prev, k, axis=0), body[:SUBLANES])
    return jnp.concatenate([head, body[SUBLANES:]], axis=0)


def _conv_taps(up, u1, u2, cw, cb):
    return cb + cw[0:1, :] * u2 + cw[1:2, :] * u1 + cw[2:3, :] * up


def _ffn_finish(j, n_j, final_norm, acts, wd_ref, x1_ref, g_ref, y_ref):
    y_ref[...] += jnp.dot(jnp.concatenate(acts, axis=1), wd_ref[...], preferred_element_type=F32)

    @pl.when(j == n_j - 1)
    def _():
        x2 = x1_ref[...] + y_ref[...]
        if final_norm:
            ms = jnp.mean(x2 * x2, axis=-1, keepdims=True)
            x2 = x2 * lax.rsqrt(ms + EPS) * g_ref[...]
        y_ref[...] = x2


def _ffn_prompt_kernel(h_ref, wg_ref, wv_ref, cwg_ref, cwv_ref, cbg_ref, cbv_ref, wd_ref, x1_ref,
                       g_ref, y_ref, convg_ref, convv_ref, carry_ref,
                       *, tm, tiles_per_seq, n_j, final_norm):
    i, j = pl.program_id(0), pl.program_id(1)

    @pl.when(i % tiles_per_seq == 0)
    def _():
        carry_ref[j] = jnp.zeros(carry_ref.shape[1:], F32)

    @pl.when(j == 0)
    def _():
        y_ref[...] = jnp.zeros(y_ref.shape, F32)

    h = h_ref[...]
    prev = carry_ref[j]
    sub = FF_TILE // FF_SUB
    acts, tails = [], ([], [])
    for s in range(FF_SUB):
        cs = slice(s * sub, (s + 1) * sub)
        mixed = []
        for half, (w_ref, cw_ref, cb_ref) in enumerate(((wg_ref, cwg_ref, cbg_ref),
                                                        (wv_ref, cwv_ref, cbv_ref))):
            up = jnp.dot(h, w_ref[:, cs], preferred_element_type=F32)
            p8 = prev[:, half * FF_TILE + s * sub:half * FF_TILE + (s + 1) * sub]
            mixed.append(_conv_taps(up, _shift_rows(up, p8, 1), _shift_rows(up, p8, 2),
                                    cw_ref[:, cs], cb_ref[:, cs]))
            tails[half].append(up[tm - SUBLANES:])
        gate, val = mixed
        acts.append((gate * _sigmoid(gate) * val).astype(BF16))
    tail_g = jnp.concatenate(tails[0], axis=1)
    tail_v = jnp.concatenate(tails[1], axis=1)
    carry_ref[j] = jnp.concatenate([tail_g, tail_v], axis=1)
    convg_ref[...] = tail_g
    convv_ref[...] = tail_v
    _ffn_finish(j, n_j, final_norm, acts, wd_ref, x1_ref, g_ref, y_ref)


def _ffn_sample_kernel(h_ref, wg_ref, wv_ref, cwg_ref, cwv_ref, cbg_ref, cbv_ref, wd_ref, x1_ref,
                       g_ref, stg_ref, stv_ref, y_ref, convg_ref, convv_ref,
                       *, nb, t_new, n_j, final_norm):
    j = pl.program_id(0)

    @pl.when(j == 0)
    def _():
        y_ref[...] = jnp.zeros(y_ref.shape, F32)

    h = h_ref[...]
    sub = FF_TILE // FF_SUB
    acts = []
    for s in range(FF_SUB):
        cs = slice(s * sub, (s + 1) * sub)
        mixed = []
        for w_ref, cw_ref, cb_ref, st_ref, conv_ref in (
                (wg_ref, cwg_ref, cbg_ref, stg_ref, convg_ref),
                (wv_ref, cwv_ref, cbv_ref, stv_ref, convv_ref)):
            up = jnp.dot(h, w_ref[:, cs], preferred_element_type=F32)
            s0, s1 = st_ref[:, 0, cs], st_ref[:, 1, cs]
            u1 = jnp.concatenate([s1, up[:(t_new - 1) * nb]], axis=0)
            u2 = jnp.concatenate([s0, s1, up[:(t_new - 2) * nb]], axis=0)
            mixed.append(_conv_taps(up, u1, u2, cw_ref[:, cs], cb_ref[:, cs]))
            conv_ref[:, 0, cs] = up[(t_new - 2) * nb:(t_new - 1) * nb]
            conv_ref[:, 1, cs] = up[(t_new - 1) * nb:]
        gate, val = mixed
        acts.append((gate * _sigmoid(gate) * val).astype(BF16))
    _ffn_finish(j, n_j, final_norm, acts, wd_ref, x1_ref, g_ref, y_ref)


def _ffn_weight_specs(d, n_j, ix):
    gate = lambda rows: pl.BlockSpec((rows, FF_TILE), ix(lambda j: (0, j)))
    val = lambda rows: pl.BlockSpec((rows, FF_TILE), ix(lambda j: (0, n_j + j)))
    return [gate(d), val(d), gate(3), val(3), gate(1), val(1),
            pl.BlockSpec((FF_TILE, d), ix(lambda j: (j, 0)))]


def _ffn_prompt(h2, x1, wu, cw, cb, wd, g, batch, seq, tm, final_norm):
    m, d = x1.shape
    d_ff = wd.shape[0]
    n_j = d_ff // FF_TILE
    tps = seq // tm
    kern = functools.partial(_ffn_prompt_kernel, tm=tm, tiles_per_seq=tps, n_j=n_j,
                             final_norm=final_norm)
    ix = lambda f: (lambda i, j: f(j))
    conv_spec = pl.BlockSpec((None, SUBLANES, FF_TILE), lambda i, j: (i, 0, j))
    conv_shape = jax.ShapeDtypeStruct((m // tm, SUBLANES, d_ff), F32)
    y, cg, cv = pl.pallas_call(
        kern,
        grid=(m // tm, n_j),
        in_specs=[pl.BlockSpec((tm, d), lambda i, j: (i, 0))] + _ffn_weight_specs(d, n_j, ix)
                 + [pl.BlockSpec((tm, d), lambda i, j: (i, 0)),
                    pl.BlockSpec((1, d), lambda i, j: (0, 0))],
        out_specs=[pl.BlockSpec((tm, d), lambda i, j: (i, 0)), conv_spec, conv_spec],
        out_shape=[jax.ShapeDtypeStruct((m, d), F32), conv_shape, conv_shape],
        scratch_shapes=[pltpu.VMEM((n_j, SUBLANES, 2 * FF_TILE), F32)],
        compiler_params=_params("arbitrary", "arbitrary"),
        name="ffn_prompt",
    )(h2, wu, wu, cw, cw, cb, cb, wd, x1, g.reshape(1, d))
    return y, jnp.concatenate([cg, cv], axis=-1)[tps - 1::tps, SUBLANES - 2:]


def _ffn_sample(h2, x1, wu, cw, cb, wd, g, state, nb, t_new, final_norm):
    m, d = x1.shape
    d_ff = wd.shape[0]
    n_j = d_ff // FF_TILE
    kern = functools.partial(_ffn_sample_kernel, nb=nb, t_new=t_new, n_j=n_j,
                             final_norm=final_norm)
    ix = lambda f: f
    st_gate = pl.BlockSpec((nb, 2, FF_TILE), lambda j: (0, 0, j))
    st_val = pl.BlockSpec((nb, 2, FF_TILE), lambda j: (0, 0, n_j + j))
    conv_spec = pl.BlockSpec((nb, 2, FF_TILE), lambda j: (0, 0, j))
    y, cg, cv = pl.pallas_call(
        kern,
        grid=(n_j,),
        in_specs=[pl.BlockSpec((m, d), lambda j: (0, 0))] + _ffn_weight_specs(d, n_j, ix)
                 + [pl.BlockSpec((m, d), lambda j: (0, 0)),
                    pl.BlockSpec((1, d), lambda j: (0, 0)),
                    st_gate, st_val],
        out_specs=[pl.BlockSpec((m, d), lambda j: (0, 0)), conv_spec, conv_spec],
        out_shape=[jax.ShapeDtypeStruct((m, d), F32),
                   jax.ShapeDtypeStruct((nb, 2, d_ff), F32),
                   jax.ShapeDtypeStruct((nb, 2, d_ff), F32)],
        compiler_params=_params("arbitrary"),
        name="ffn_sample",
    )(h2, wu, wu, cw, cw, cb, cb, wd, x1, g.reshape(1, d), state, state)
    return y, jnp.concatenate([cg, cv], axis=-1)


def kernel(x_prompt, x_sample, cache_k, cache_v, cache_kidx, state_s5_re, state_s5_im, state_ffn_conv, page_table, norm_mix, w_in, w_attn_proj, s5_a_re, s5_a_im, s5_log_step, s5_b_re, s5_b_im, s5_c_re, s5_c_im, s5_d, w_glu, w_out, norm_ffn, w_up, conv_w, conv_b, w_down, norm_final):
    depth = w_in.shape[0]
    batch, seq, d_model = x_prompt.shape
    nb, t_new, _ = x_sample.shape
    d_head = cache_k.shape[-1]
    d_idx = cache_kidx.shape[-1]
    attn_w = N_HEADS * d_head
    kv_w = N_KV_HEADS * d_head
    qi_w = IDX_HEADS * d_idx
    groups, p_state = s5_a_re.shape[1], s5_a_re.shape[2]
    s5_w = groups * S5_GROUP

    xp = x_prompt.reshape(batch * seq, d_model)
    xs = x_sample.reshape(nb * t_new, d_model)
    tm_p = min(512, batch * seq)
    tm_s = nb * t_new

    outs = {name: [] for name in ("kp", "vp", "kip", "srp", "sip", "cp",
                                  "ks", "vs", "kis", "srs", "sis", "cs")}
    for l in range(depth):
        w_t = jnp.swapaxes(w_in[l], 0, 1).astype(BF16)
        o = 0
        seg = {}
        for name, width in (("q", attn_w), ("k", kv_w), ("v", kv_w), ("qi", qi_w), ("ki", d_idx),
                            ("wi", IDX_HEADS), ("u", s5_w), ("ga", d_model), ("gb", d_model)):
            seg[name] = w_t[o:o + width]
            o += width
        w_q_grouped = _group_major_heads(seg["q"], d_head)
        w_u, w_ga, w_gb = seg["u"], seg["ga"], seg["gb"]
        pad = jnp.zeros((LANES - d_idx - IDX_HEADS, d_model), BF16)
        w_small = jnp.concatenate([seg["qi"], seg["k"], seg["v"], seg["ki"], seg["wi"], pad], axis=0)
        w_qiw = jnp.concatenate([seg["qi"], seg["ki"], seg["wi"], pad], axis=0)
        w_kv_t = jnp.concatenate([seg["k"], seg["v"], seg["ki"]], axis=0)
        w_attn_grouped = _group_major_heads(w_attn_proj[l].astype(BF16), d_head)
        w_g = w_glu[l].astype(BF16)
        w_o = w_out[l].astype(BF16)
        wu = w_up[l].astype(BF16)
        cw = conv_w[l].astype(F32)
        cb = conv_b[l].astype(F32).reshape(1, -1)
        wd = w_down[l].astype(BF16)
        s5p = _s5_params(s5_a_re[l], s5_a_im[l], s5_log_step[l], s5_b_re[l], s5_b_im[l],
                         s5_c_re[l], s5_c_im[l], STEP_CHUNK)
        last = l == depth - 1

        def project(x, tm, ws):
            (q, u, small), h = _norm_matmul(x, norm_mix[l], [w_q_grouped, w_u, ws], tm,
                                            "norm_proj_q_u_small")
            (ga,) = _matmul(h, [w_ga], tm, F32, "proj_ga")
            (gb,) = _matmul(h, [w_gb], tm, F32, "proj_gb")
            return h, q, u, ga, gb, small

        h, q, u, ga, gb, qiw = project(xp, tm_p, w_qiw)
        k_t, v_t, ki_t, kb_t, vb_t, kib_t = _kv_transposed(h, w_kv_t, batch, seq, kv_w, d_idx,
                                                            min(512, seq))
        k_sel = min(TOPK_MAX, seq // 4)
        attn = _prompt_attention(q, qiw, kib_t, kb_t, vb_t, batch, seq, d_head, d_idx, k_sel,
                                 min(256, seq))
        y5, srp, sip = _s5_prompt(u, batch, seq, s5p, s5_d[l], groups, p_state)
        merged = _merge(attn, y5, ga, gb, w_attn_grouped, w_g, min(256, batch * seq))
        x1, h2 = _out_proj(merged, xp, w_o, norm_ffn[l], tm_p)
        xp, conv_p = _ffn_prompt(h2, x1, wu, cw, cb, wd, norm_final, batch, seq, tm_p, last)

        def heads_last(a_t):
            return jnp.transpose(a_t.reshape(batch, N_KV_HEADS, d_head, seq), (0, 3, 1, 2))

        outs["kp"].append(heads_last(k_t)); outs["vp"].append(heads_last(v_t))
        outs["kip"].append(jnp.swapaxes(ki_t, 1, 2))
        outs["srp"].append(srp); outs["sip"].append(sip); outs["cp"].append(conv_p)

        _, q, u, ga, gb, small = project(xs, tm_s, w_small)
        attn = _sample_attention(q, small, cache_k[l], cache_v[l], cache_kidx[l], page_table,
                                 t_new, d_head, d_idx)
        y5, srs, sis = _s5_sample(u, nb, t_new, s5p, s5_d[l], state_s5_re[l], state_s5_im[l])
        merged = _merge(attn, y5, ga, gb, w_attn_grouped, w_g, min(256, nb * t_new))
        x1, h2 = _out_proj(merged, xs, w_o, norm_ffn[l], tm_s)

        def time_major(a):
            return a.reshape(nb, t_new, -1).swapaxes(0, 1).reshape(nb * t_new, -1)

        y_tm, conv_s = _ffn_sample(time_major(h2), time_major(x1), wu, cw, cb, wd, norm_final,
                                   state_ffn_conv[l].astype(F32), nb, t_new, last)
        xs = y_tm.reshape(t_new, nb, d_model).swapaxes(0, 1).reshape(nb * t_new, d_model)
        k_new = small[:, qi_w:qi_w + kv_w].reshape(nb, t_new, N_KV_HEADS, d_head)
        v_new = small[:, qi_w + kv_w:qi_w + 2 * kv_w].reshape(nb, t_new, N_KV_HEADS, d_head)
        ki_new = small[:, qi_w + 2 * kv_w:qi_w + 2 * kv_w + d_idx].reshape(nb, t_new, d_idx)
        outs["ks"].append(k_new); outs["vs"].append(v_new); outs["kis"].append(ki_new)
        outs["srs"].append(srs); outs["sis"].append(sis); outs["cs"].append(conv_s)

    stk = {name: jnp.stack(v) for name, v in outs.items()}
    return (xp.reshape(batch, seq, d_model), xs.reshape(nb, t_new, d_model),
            stk["kp"], stk["vp"], stk["kip"], stk["srp"], stk["sip"], stk["cp"],
            stk["ks"], stk["vs"], stk["kis"], stk["srs"], stk["sis"], stk["cs"])
```

```python
import functools
import math

import jax
import jax.numpy as jnp
from jax import lax
from jax.experimental import pallas as pl
from jax.experimental.pallas import tpu as pltpu

F32 = jnp.float32
BF16 = jnp.bfloat16

EPS = 1e-6
LOG2E = math.log2(math.e)
TOPK_MAX = 256
N_HEADS = 16
N_KV_HEADS = 4
KV_GROUP = N_HEADS // N_KV_HEADS
IDX_HEADS = 8
S5_GROUP = 16
STEP_CHUNK = 8
CARRY_TILES_PER_STEP = 2
LANES = 128
SUBLANES = 8
GROUPS_PER_TILE = LANES // S5_GROUP
FF_TILE = 512
VMEM_LIMIT = 48 * 1024 * 1024
NEG_INF = float("-inf")
INT_MIN = -2 ** 31


def _params(*sem):
    return pltpu.CompilerParams(dimension_semantics=sem, vmem_limit_bytes=VMEM_LIMIT)


def _sigmoid(x):
    return 0.5 * jnp.tanh(0.5 * x) + 0.5


def _gelu_tanh(x):
    c = math.sqrt(2.0 / math.pi)
    return 0.5 * x * (1.0 + jnp.tanh(c * (x + 0.044715 * (x * x * x))))


def _dot_nt(a, b):
    return lax.dot_general(a, b, (((1,), (1,)), ((), ())), preferred_element_type=F32)


def _resident(w):
    return pl.BlockSpec(w.shape, lambda i: (0, 0), pipeline_mode=pl.Buffered(1))


def _mm_kernel(h_ref, *refs):
    n = len(refs) // 2
    h = h_ref[...]
    for w_ref, o_ref in zip(refs[:n], refs[n:]):
        o_ref[...] = _dot_nt(h, w_ref[...]).astype(o_ref.dtype)


def _matmul(h, weights_t, tm, out_dtype, name):
    m, k = h.shape
    return pl.pallas_call(
        _mm_kernel,
        grid=(m // tm,),
        in_specs=[pl.BlockSpec((tm, k), lambda i: (i, 0))]
                 + [_resident(w) for w in weights_t],
        out_specs=[pl.BlockSpec((tm, w.shape[0]), lambda i: (i, 0)) for w in weights_t],
        out_shape=[jax.ShapeDtypeStruct((m, w.shape[0]), out_dtype) for w in weights_t],
        compiler_params=_params("arbitrary"),
        name=name,
    )(h, *weights_t)


def _norm_mm_kernel(x_ref, g_ref, *refs):
    n = (len(refs) - 1) // 2
    x = x_ref[...]
    ms = jnp.mean(x * x, axis=-1, keepdims=True)
    h = (x * lax.rsqrt(ms + EPS) * g_ref[...]).astype(BF16)
    refs[2 * n][...] = h
    for w_ref, o_ref in zip(refs[:n], refs[n:2 * n]):
        o_ref[...] = _dot_nt(h, w_ref[...]).astype(o_ref.dtype)


def _norm_matmul(x, g, weights_t, tm, name):
    m, k = x.shape
    *outs, h = pl.pallas_call(
        _norm_mm_kernel,
        grid=(m // tm,),
        in_specs=[pl.BlockSpec((tm, k), lambda i: (i, 0)), pl.BlockSpec((1, k), lambda i: (0, 0))]
                 + [_resident(w) for w in weights_t],
        out_specs=[pl.BlockSpec((tm, w.shape[0]), lambda i: (i, 0)) for w in weights_t]
                  + [pl.BlockSpec((tm, k), lambda i: (i, 0))],
        out_shape=[jax.ShapeDtypeStruct((m, w.shape[0]), F32) for w in weights_t]
                  + [jax.ShapeDtypeStruct((m, k), BF16)],
        compiler_params=_params("arbitrary"),
        name=name,
    )(x, g.reshape(1, k), *weights_t)
    return outs, h


def _kv_t_kernel(w_ref, h_ref, k_ref, v_ref, ki_ref, kb_ref, vb_ref, kib_ref, *, kv_w):
    out = _dot_nt(w_ref[...], h_ref[...])
    for lo, hi, full_ref, half_ref in ((0, kv_w, k_ref, kb_ref), (kv_w, 2 * kv_w, v_ref, vb_ref),
                                       (2 * kv_w, out.shape[0], ki_ref, kib_ref)):
        full_ref[...] = out[lo:hi]
        half_ref[...] = out[lo:hi].astype(BF16)


def _kv_transposed(h, w_t, batch, seq, kv_w, d_idx, tn):
    d = h.shape[1]
    nt = seq // tn
    out = lambda rows: pl.BlockSpec((None, rows, tn), lambda b, i: (b, 0, i))
    shape = lambda rows, dt: jax.ShapeDtypeStruct((batch, rows, seq), dt)
    widths = (kv_w, kv_w, d_idx)
    return pl.pallas_call(
        functools.partial(_kv_t_kernel, kv_w=kv_w),
        grid=(batch, nt),
        in_specs=[pl.BlockSpec(w_t.shape, lambda b, i: (0, 0)),
                  pl.BlockSpec((tn, d), lambda b, i: (b * nt + i, 0))],
        out_specs=[out(w) for w in widths] * 2,
        out_shape=[shape(w, F32) for w in widths] + [shape(w, BF16) for w in widths],
        compiler_params=_params("arbitrary", "arbitrary"),
        name="proj_kv_transposed",
    )(w_t, h)


SEARCH_GROUP = 4


def _count(mask):
    return jnp.sum(jnp.where(mask, 1.0, 0.0), axis=1, keepdims=True)


def _topk_mask(score, col, k_sel, n_cols):
    kf = float(k_sel)
    rows = score.shape[0]

    def as_float(key):
        bits = jnp.where(key < 0, key ^ jnp.int32(0x7FFFFFFF), key)
        return lax.bitcast_convert_type(bits, F32)

    finite = score > NEG_INF
    few = _count(finite) <= kf

    cnt0 = _count(score >= 0.0)
    cand0 = jnp.where(cnt0 >= kf, jnp.int32(0), jnp.int32(INT_MIN))
    cand0 = jnp.broadcast_to(cand0, (rows, 1)).astype(jnp.int32)
    cnt0 = jnp.where(cnt0 >= kf, cnt0, float(n_cols))

    def search_body(group, carry):
        cand, cnt = carry
        for g in range(SEARCH_GROUP):
            shift = jnp.int32(30) - (group * SEARCH_GROUP + g)
            bit = jnp.where(shift >= 0, lax.shift_left(jnp.int32(1), jnp.maximum(shift, 0)), 0)
            trial = cand + bit
            c_trial = _count(score >= as_float(trial))
            ok = c_trial >= kf
            cand = jnp.where(ok, trial, cand)
            cnt = jnp.where(ok, c_trial, cnt)
        return cand, cnt

    cand, cnt = lax.fori_loop(0, -(-31 // SEARCH_GROUP), search_body, (cand0, cnt0))
    thr = as_float(cand)

    above = score > thr
    ties = score == thr
    need = kf - _count(above)
    surplus = jnp.where(few, 0.0, cnt - kf)
    n_bits = max(1, (n_cols - 1).bit_length())

    def index_step(it, m):
        trial = m + lax.shift_left(jnp.int32(1), jnp.int32(n_bits - 1) - it)
        taken = _count(jnp.logical_and(ties, col < trial))
        return jnp.where(taken <= need - 1.0, trial, m)

    def lowest_indices(_):
        return lax.fori_loop(0, n_bits, index_step, jnp.zeros((rows, 1), jnp.int32))

    def all_ties(_):
        return jnp.full((rows, 1), n_cols, jnp.int32)

    last = lax.cond(jnp.max(surplus) > 0.0, lowest_indices, all_ties, None)
    take_tie = jnp.logical_and(jnp.logical_and(ties, col <= last), need >= 1.0)
    top = jnp.logical_or(above, take_tie)
    return jnp.logical_and(finite, jnp.logical_or(few, top))


def _index_weights(kw, d_idx):
    w_scale = (d_idx ** -0.5) * (IDX_HEADS ** -0.5)
    return [kw[:, d_idx + h:d_idx + h + 1] * w_scale for h in range(IDX_HEADS)]


def _head_rows(qi, d_idx):
    return jnp.concatenate([qi[:, h * d_idx:(h + 1) * d_idx] for h in range(IDX_HEADS)], axis=0)


def _weighted_relu_sum(lg, w_cols, rows):
    score = jnp.zeros((rows, lg.shape[1]), F32)
    for h in range(IDX_HEADS):
        score = score + jnp.maximum(lg[h * rows:(h + 1) * rows], 0.0) * w_cols[h]
    return score


HEADS_PER_DOT = 2
SHORT_PREFIX_BLOCKS = 2


PROMPT_SCORE_ELEMS = 512 * 1024


def _prompt_attn_kernel(q_ref, qi_ref, kw_ref, kit_ref, kt_ref, vt_ref, prev_ref, o_ref,
                        *, tq, n_keys, k_sel, d_head, d_idx, q_block, heads_per_dot):
    del prev_ref
    n_seqs = q_ref.shape[0]
    scores = []
    for b in range(n_seqs):
        qi = qi_ref[b]
        w_cols = _index_weights(kw_ref[b], d_idx)
        kit = kit_ref[b]
        score = jnp.zeros((tq, n_keys), F32)
        for h in range(IDX_HEADS):
            lg = jnp.dot(qi[:, h * d_idx:(h + 1) * d_idx].astype(BF16), kit,
                         preferred_element_type=F32)
            score = score + jnp.maximum(lg, 0.0) * w_cols[h]
        scores.append(score)
    score = jnp.concatenate(scores, axis=0)

    col = lax.broadcasted_iota(jnp.int32, score.shape, 1)
    row = lax.broadcasted_iota(jnp.int32, (n_seqs, tq, n_keys), 1).reshape(score.shape)
    score = jnp.where(col <= q_block * tq + row, score, NEG_INF)
    bias_all = jnp.where(_topk_mask(score, col, k_sel, n_keys), 0.0, NEG_INF)

    for b in range(n_seqs):
        bias = bias_all[b * tq:(b + 1) * tq]
        q = q_ref[b] * (d_head ** -0.5 * LOG2E)
        for n in range(N_KV_HEADS):
            kt = kt_ref[b, n * d_head:(n + 1) * d_head, :]
            vt = vt_ref[b, n * d_head:(n + 1) * d_head, :]
            for g0 in range(0, KV_GROUP, heads_per_dot):
                heads = [(g0 + i) * N_KV_HEADS + n for i in range(heads_per_dot)]
                qs = jnp.concatenate([q[:, h * d_head:(h + 1) * d_head] for h in heads], axis=0)
                s = jnp.dot(qs.astype(BF16), kt, preferred_element_type=F32)
                s = (s.reshape(heads_per_dot, tq, n_keys) + bias[None]).reshape(-1, n_keys)
                m = jnp.max(s, axis=1, keepdims=True)
                p = jnp.exp2(s - m)
                l = jnp.sum(p, axis=1, keepdims=True)
                o = _dot_nt(p.astype(BF16), vt) / l
                for i, h in enumerate(heads):
                    o_ref[b, :, h * d_head:(h + 1) * d_head] = (
                        o[i * tq:(i + 1) * tq].astype(o_ref.dtype))


def _prompt_attention(q, qiw, ki_t, k_t, v_t, batch, seq, d_head, d_idx, k_sel, tq):
    attn_w = q.shape[1]
    kv_w = N_KV_HEADS * d_head
    qi_w = IDX_HEADS * d_idx
    nq = seq // tq
    q3 = q.reshape(batch, seq, attn_w)
    qiw3 = qiw.reshape(batch, seq, qiw.shape[1])
    out = jnp.zeros((batch, seq, attn_w), BF16)
    for qb in range(nq):
        n_keys = (qb + 1) * tq
        n_seqs = max(d for d in range(1, batch + 1)
                     if batch % d == 0 and (d == 1 or d * tq * n_keys <= PROMPT_SCORE_ELEMS))
        row = lambda w, c, qb=qb, n_seqs=n_seqs: pl.BlockSpec((n_seqs, tq, w), lambda b: (b, qb, c))
        keys = lambda w, n_keys=n_keys, n_seqs=n_seqs: pl.BlockSpec((n_seqs, w, n_keys),
                                                                     lambda b: (b, 0, 0))
        heads_per_dot = HEADS_PER_DOT if n_keys <= SHORT_PREFIX_BLOCKS * tq else 1
        out = pl.pallas_call(
            functools.partial(_prompt_attn_kernel, tq=tq, n_keys=n_keys, k_sel=k_sel,
                              d_head=d_head, d_idx=d_idx, q_block=qb,
                              heads_per_dot=heads_per_dot),
            grid=(batch // n_seqs,),
            in_specs=[row(attn_w, 0), row(qi_w, 0), row(LANES, qi_w // LANES),
                      keys(d_idx), keys(kv_w), keys(kv_w),
                      pl.BlockSpec(memory_space=pl.ANY)],
            out_specs=row(attn_w, 0),
            out_shape=jax.ShapeDtypeStruct((batch, seq, attn_w), BF16),
            input_output_aliases={6: 0},
            compiler_params=_params("arbitrary"),
            name=f"prompt_attention_q{qb}",
        )(q3, qiw3, qiw3, ki_t, k_t, v_t, out)
    return out.reshape(batch * seq, attn_w)


def _lane_columns(cols, width):
    rows = cols[0].shape[0]
    lane = lax.broadcasted_iota(jnp.int32, (rows, width), 1)
    out = jnp.zeros((rows, width), F32)
    for j, c in enumerate(cols):
        out = jnp.where(lane == j, c, out)
    return out


def _sample_score_kernel(pt_ref, qi_ref, kw_ref, kin_ref, *rest, n_seqs, n_pages, page, t_new,
                         d_idx):
    del pt_ref
    s_ref = rest[n_seqs * n_pages]
    past = n_pages * page
    rows = SUBLANES
    pad_row = lax.broadcasted_iota(jnp.int32, (rows, page), 0) >= t_new
    first = lax.broadcasted_iota(jnp.int32, (rows, page), 1) == 0
    lane = lax.broadcasted_iota(jnp.int32, (rows, LANES), 1)
    t = lax.broadcasted_iota(jnp.int32, (rows, LANES), 0)
    causal_new = jnp.logical_and(lane <= t, t < t_new)
    for i in range(n_seqs):
        ipages = rest[i * n_pages:(i + 1) * n_pages]
        qi_rows = _head_rows(qi_ref[i], d_idx)
        qi_bf = qi_rows.astype(BF16)
        w_cols = _index_weights(kw_ref[i], d_idx)
        for p in range(n_pages):
            lg = jnp.dot(qi_bf, ipages[p][...].astype(BF16), preferred_element_type=F32)
            pad_val = jnp.where(first, 0.0, NEG_INF) if p == 0 else NEG_INF
            s_ref[i, :, p * page:(p + 1) * page] = jnp.where(
                pad_row, pad_val, _weighted_relu_sum(lg, w_cols, rows))
        kin = kin_ref[i]
        new_cols = []
        for j in range(t_new):
            lg = jnp.sum(qi_rows * kin[j:j + 1, :], axis=1, keepdims=True)
            new_cols.append(_weighted_relu_sum(lg, w_cols, rows))
        s_ref[i, :, past:] = jnp.where(causal_new, _lane_columns(new_cols, LANES), NEG_INF)


def _select_kernel(s_ref, b_ref, *, k_sel):
    score = s_ref[...]
    col = lax.broadcasted_iota(jnp.int32, score.shape, 1)
    keep = _topk_mask(score, col, k_sel, score.shape[1])
    b_ref[...] = jnp.where(keep, 0.0, NEG_INF)


SCORE_SEQS_PER_STEP = 4
ATTEND_SEQS_PER_STEP = 2


def _sample_attn_kernel(pt_ref, q_ref, kn_ref, vn_ref, bias_ref, *rest,
                        n_seqs, n_pages, page, t_new, d_head):
    del pt_ref
    pages = rest[:2 * n_seqs * n_pages]
    o_ref, kt_ref, vt_ref = rest[2 * n_seqs * n_pages:]
    for i in range(n_seqs):
        kpages = pages[i * n_pages:(i + 1) * n_pages]
        vpages = pages[(n_seqs + i) * n_pages:(n_seqs + i + 1) * n_pages]
        o_ref[i] = _sample_attend(q_ref[i], kn_ref[i], vn_ref[i], bias_ref[i], kpages, vpages,
                                  kt_ref.at[i], vt_ref.at[i], page, t_new, d_head
                                  ).astype(o_ref.dtype)


def _sample_attend(q, kn, vn, bias, kpages, vpages, kt_ref, vt_ref, page, t_new, d_head):
    n_pages = len(kpages)
    past = n_pages * page
    rows = SUBLANES
    kv_w = N_KV_HEADS * d_head

    for p in range(n_pages):
        kt_ref[:, p * page:(p + 1) * page] = kpages[p][...].astype(BF16)
        vt_ref[:, p * page:(p + 1) * page] = vpages[p][...].astype(BF16)

    bias = jnp.concatenate([bias] * N_HEADS, axis=0)

    q = q * (d_head ** -0.5 * LOG2E)
    lane_head = lax.broadcasted_iota(jnp.int32, (rows, kv_w), 1) // d_head
    q_rows = jnp.concatenate(
        [jnp.where(lane_head == n, q[:, g * kv_w:(g + 1) * kv_w], 0.0)
         for g in range(KV_GROUP) for n in range(N_KV_HEADS)], axis=0)

    s_past = jnp.dot(q_rows.astype(BF16), kt_ref[...], preferred_element_type=F32)
    s_past = s_past + bias[:, :past]
    s_new = _lane_columns(
        [jnp.sum(q_rows * kn[j:j + 1, :], axis=1, keepdims=True) for j in range(t_new)], LANES)
    s_new = s_new + bias[:, past:]

    m = jnp.maximum(jnp.max(s_past, axis=1, keepdims=True), jnp.max(s_new, axis=1, keepdims=True))
    p_past = jnp.exp2(s_past - m)
    p_new = jnp.exp2(s_new - m)
    l = jnp.sum(p_past, axis=1, keepdims=True) + jnp.sum(p_new, axis=1, keepdims=True)
    o = _dot_nt(p_past.astype(BF16), vt_ref[...])
    for j in range(t_new):
        o = o + p_new[:, j:j + 1] * vn[j:j + 1, :]
    o = o / l

    chunks = []
    for g in range(KV_GROUP):
        acc = jnp.zeros((rows, kv_w), F32)
        for n in range(N_KV_HEADS):
            r0 = (g * N_KV_HEADS + n) * rows
            acc = acc + jnp.where(lane_head == n, o[r0:r0 + rows], 0.0)
        chunks.append(acc)
    return jnp.concatenate(chunks, axis=1)


def _seq_page_map(b, pt, *, i, p, n_seqs, n_pages):
    return (pt[(b * n_seqs + i) * n_pages + p], 0, 0)


def _sample_attention(q, small, cache_k, cache_v, cache_ki, page_table, t_new, d_head, d_idx):
    nb, n_pages = page_table.shape
    n_pool, page = cache_k.shape[0], cache_k.shape[1]
    attn_w = q.shape[1]
    kv_w = N_KV_HEADS * d_head
    qi_w = IDX_HEADS * d_idx
    rows = SUBLANES
    past = n_pages * page
    k_sel = min(TOPK_MAX, (past + t_new) // 4)

    def pad_rows(a):
        a = a.reshape(nb, t_new, a.shape[-1])
        return jnp.pad(a, ((0, 0), (0, rows - t_new), (0, 0)))

    q8 = pad_rows(q)
    qi8 = pad_rows(small[:, :qi_w])
    kw8 = pad_rows(small[:, qi_w + 2 * kv_w:])
    kn = pad_rows(small[:, qi_w:qi_w + kv_w])
    vn = pad_rows(small[:, qi_w + kv_w:qi_w + 2 * kv_w])
    kin = pad_rows(small[:, qi_w + 2 * kv_w:qi_w + 2 * kv_w + d_idx])

    ck = jnp.transpose(cache_k, (0, 2, 3, 1)).reshape(n_pool, kv_w, page)
    cv = jnp.transpose(cache_v, (0, 2, 3, 1)).reshape(n_pool, kv_w, page)
    ci = jnp.transpose(cache_ki, (0, 2, 1))

    pt_flat = page_table.reshape(-1)
    n_keys = past + LANES

    def seqs_spec(w, n_seqs):
        return pl.BlockSpec((n_seqs, rows, w), lambda b, pt: (b, 0, 0))

    def seq_page_specs(w, n_seqs):
        return [pl.BlockSpec((None, w, page),
                             functools.partial(_seq_page_map, i=i, p=p, n_seqs=n_seqs,
                                               n_pages=n_pages))
                for i in range(n_seqs) for p in range(n_pages)]

    n_seqs = math.gcd(nb, SCORE_SEQS_PER_STEP)
    score = pl.pallas_call(
        functools.partial(_sample_score_kernel, n_seqs=n_seqs, n_pages=n_pages, page=page,
                          t_new=t_new, d_idx=d_idx),
        grid_spec=pltpu.PrefetchScalarGridSpec(
            num_scalar_prefetch=1,
            grid=(nb // n_seqs,),
            in_specs=[seqs_spec(qi_w, n_seqs), seqs_spec(LANES, n_seqs), seqs_spec(d_idx, n_seqs)]
                     + seq_page_specs(d_idx, n_seqs),
            out_specs=seqs_spec(n_keys, n_seqs),
        ),
        out_shape=jax.ShapeDtypeStruct((nb, rows, n_keys), F32),
        compiler_params=_params("arbitrary"),
        name="sample_scores",
    )(pt_flat, qi8, kw8, kin, *([ci] * (n_seqs * n_pages)))

    sel_rows = math.gcd(nb * rows, 256)
    bias = pl.pallas_call(
        functools.partial(_select_kernel, k_sel=k_sel),
        grid=(nb * rows // sel_rows,),
        in_specs=[pl.BlockSpec((sel_rows, n_keys), lambda i: (i, 0))],
        out_specs=pl.BlockSpec((sel_rows, n_keys), lambda i: (i, 0)),
        out_shape=jax.ShapeDtypeStruct((nb * rows, n_keys), F32),
        compiler_params=_params("arbitrary"),
        name="sample_select",
    )(score.reshape(nb * rows, n_keys)).reshape(nb, rows, n_keys)

    n_seqs = math.gcd(nb, ATTEND_SEQS_PER_STEP)
    out = pl.pallas_call(
        functools.partial(_sample_attn_kernel, n_seqs=n_seqs, n_pages=n_pages, page=page,
                          t_new=t_new, d_head=d_head),
        grid_spec=pltpu.PrefetchScalarGridSpec(
            num_scalar_prefetch=1,
            grid=(nb // n_seqs,),
            in_specs=[seqs_spec(attn_w, n_seqs), seqs_spec(kv_w, n_seqs), seqs_spec(kv_w, n_seqs),
                      seqs_spec(n_keys, n_seqs)]
                     + seq_page_specs(kv_w, n_seqs) + seq_page_specs(kv_w, n_seqs),
            out_specs=seqs_spec(attn_w, n_seqs),
            scratch_shapes=[pltpu.VMEM((n_seqs, kv_w, past), BF16),
                            pltpu.VMEM((n_seqs, kv_w, past), BF16)],
        ),
        out_shape=jax.ShapeDtypeStruct((nb, rows, attn_w), BF16),
        compiler_params=_params("arbitrary"),
        name="sample_attention",
    )(pt_flat, q8, kn, vn, bias, *([ck] * (n_seqs * n_pages)), *([cv] * (n_seqs * n_pages)))
    return out[:, :t_new].reshape(nb * t_new, attn_w)


def _group_major_heads(a, d_head):
    n = a.shape[1]
    a = a.reshape(N_KV_HEADS, KV_GROUP, d_head, n)
    return jnp.swapaxes(a, 0, 1).reshape(N_HEADS * d_head, n)


def _s5_step(xr, xi, ar, ai, bu, half):
    br, bi = bu[:, :half], bu[:, half:]
    if xr is None:
        return br, bi
    return ar * xr - ai * xi + br, ar * xi + ai * xr + bi


def _s5_inputs(u_ref, b_ref, j):
    tr, steps, _ = u_ref.shape
    x = jnp.transpose(u_ref[:, :, j * LANES:(j + 1) * LANES], (1, 0, 2))
    us = [x[s] for s in range(steps)]
    bu = jnp.dot(jnp.concatenate(us, axis=0).astype(BF16), b_ref[j], preferred_element_type=F32)
    return us, [bu[s * tr:(s + 1) * tr] for s in range(steps)]


def _s5_local_kernel(u_ref, b_ref, ar_ref, ai_ref, z_ref, *, n_tiles):
    sw = b_ref.shape[2]
    half = sw // 2
    for j in range(n_tiles):
        ar, ai = ar_ref[j:j + 1, :], ai_ref[j:j + 1, :]
        xr = xi = None
        for bu in _s5_inputs(u_ref, b_ref, j)[1]:
            xr, xi = _s5_step(xr, xi, ar, ai, bu, half)
        z_ref[:, j * sw:j * sw + half] = xr
        z_ref[:, j * sw + half:(j + 1) * sw] = xi


def _s5_carry_kernel(z_ref, ar_ref, ai_ref, xc_ref, xf_ref, *, batch, n_chunks):
    tiles, _, half = ar_ref.shape
    sw = 2 * half

    def flat(states):
        return jnp.concatenate([v for pair in states for v in pair], axis=1)

    def body(c, carry):
        new = []
        for b in range(batch):
            row = b * n_chunks + c
            xc_ref[pl.ds(row, 1), :] = flat(carry[b])
            z = z_ref[pl.ds(row, 1), :]
            nxt = []
            for t, (xr, xi) in enumerate(carry[b]):
                ar, ai = ar_ref[t], ai_ref[t]
                zr, zi = z[:, t * sw:t * sw + half], z[:, t * sw + half:(t + 1) * sw]
                nxt.append((ar * xr - ai * xi + zr, ar * xi + ai * xr + zi))
            new.append(tuple(nxt))
        return tuple(new)

    zero = jnp.zeros((1, half), F32)
    init = tuple(tuple((zero, zero) for _ in range(tiles)) for _ in range(batch))
    final = lax.fori_loop(0, n_chunks, body, init, unroll=math.gcd(n_chunks, 4))
    for b in range(batch):
        xf_ref[b:b + 1, :] = flat(final[b])


def _s5_tile_outputs(u_ref, y_ref, b_ref, c_ref, ar, ai, d, xr, xi, j):
    tr, steps, _ = u_ref.shape
    half = b_ref.shape[2] // 2
    us, bus = _s5_inputs(u_ref, b_ref, j)
    xs = []
    for bu in bus:
        xr, xi = _s5_step(xr, xi, ar, ai, bu, half)
        xs.append(jnp.concatenate([xr, xi], axis=1).astype(BF16))
    y = jnp.dot(jnp.concatenate(xs, axis=0), c_ref[j], preferred_element_type=F32)
    ys = jnp.stack([y[s * tr:(s + 1) * tr] + d * us[s] for s in range(steps)], axis=0)
    y_ref[:, :, j * LANES:(j + 1) * LANES] = jnp.transpose(ys, (1, 0, 2))
    return xr, xi


def _s5_out_kernel(u_ref, x0_ref, b_ref, c_ref, ar_ref, ai_ref, d_ref, y_ref, *, n_tiles):
    sw = b_ref.shape[2]
    half = sw // 2
    for j in range(n_tiles):
        _s5_tile_outputs(u_ref, y_ref, b_ref, c_ref, ar_ref[j:j + 1, :], ai_ref[j:j + 1, :],
                         d_ref[:, j * LANES:(j + 1) * LANES],
                         x0_ref[:, j * sw:j * sw + half], x0_ref[:, j * sw + half:(j + 1) * sw], j)


def _s5_sample_kernel(u_ref, re_ref, im_ref, b_ref, c_ref, ar_ref, ai_ref, d_ref,
                      y_ref, ore_ref, oim_ref, *, n_tiles):
    half = b_ref.shape[2] // 2
    for j in range(n_tiles):
        rows = slice(j * half, (j + 1) * half)
        xr, xi = _s5_tile_outputs(u_ref, y_ref, b_ref, c_ref, ar_ref[j:j + 1, :],
                                  ai_ref[j:j + 1, :], d_ref[:, j * LANES:(j + 1) * LANES],
                                  re_ref[rows, :].T, im_ref[rows, :].T, j)
        ore_ref[rows, :] = xr.T
        oim_ref[rows, :] = xi.T


def _s5_params(a_re, a_im, log_step, b_re, b_im, c_re, c_im, chunk_steps):
    g, p = a_re.shape
    nt = g // GROUPS_PER_TILE
    a_re, a_im = a_re.astype(F32), a_im.astype(F32)
    step = jnp.exp(log_step.astype(F32))[:, None]

    def discretise(n_steps):
        mag = jnp.exp(a_re * step * n_steps)
        return mag * jnp.cos(a_im * step * n_steps), mag * jnp.sin(a_im * step * n_steps)

    lr, li = discretise(1.0)
    cr, ci = discretise(float(chunk_steps))
    den = a_re * a_re + a_im * a_im
    fr = ((lr - 1.0) * a_re + li * a_im) / den
    fi = (li * a_re - (lr - 1.0) * a_im) / den
    b_re, b_im = b_re.astype(F32), b_im.astype(F32)
    bb_re = fr[..., None] * b_re - fi[..., None] * b_im
    bb_im = fr[..., None] * b_im + fi[..., None] * b_re
    eye = jnp.eye(GROUPS_PER_TILE, dtype=BF16)

    bb = jnp.stack([bb_re, bb_im], axis=0).reshape(2, nt, GROUPS_PER_TILE, p, S5_GROUP)
    bb = jnp.transpose(bb, (1, 2, 4, 0, 3)).astype(BF16)
    b_blk = bb[:, :, :, :, None, :] * eye[None, :, None, None, :, None]
    b_blk = b_blk.reshape(nt, LANES, 2 * GROUPS_PER_TILE * p)

    cc = jnp.stack([c_re.astype(F32), -c_im.astype(F32)], axis=0)
    cc = cc.reshape(2, nt, GROUPS_PER_TILE, S5_GROUP, p)
    cc = jnp.transpose(cc, (1, 0, 2, 4, 3)).astype(BF16)
    c_blk = cc[:, :, :, :, None, :] * eye[None, None, :, None, :, None]
    c_blk = c_blk.reshape(nt, 2 * GROUPS_PER_TILE * p, LANES)

    def tiles(z):
        return z.reshape(nt, GROUPS_PER_TILE * p)

    return b_blk, c_blk, tiles(lr), tiles(li), tiles(cr), tiles(ci)


def _tiles_to_state(x, g, p):
    nb = x.shape[0]
    nt = g // GROUPS_PER_TILE
    st = x.reshape(nb, nt, 2, GROUPS_PER_TILE, p)
    return st[:, :, 0].reshape(nb, g, p), st[:, :, 1].reshape(nb, g, p)


def _s5_weight_specs(prm, d2):
    return [pl.BlockSpec(a.shape, lambda i, nd=a.ndim: (0,) * nd) for a in (*prm, d2)]


def _s5_outputs(u3, x0, prm, d_skip, tr):
    r, steps, width = u3.shape
    nt, _, sw = prm[0].shape
    d2 = d_skip.reshape(1, width).astype(F32)
    tokens = pl.BlockSpec((tr, steps, width), lambda i: (i, 0, 0))
    return pl.pallas_call(
        functools.partial(_s5_out_kernel, n_tiles=nt),
        grid=(r // tr,),
        in_specs=[tokens, pl.BlockSpec((tr, nt * sw), lambda i: (i, 0))]
                 + _s5_weight_specs(prm, d2),
        out_specs=tokens,
        out_shape=jax.ShapeDtypeStruct(u3.shape, F32),
        compiler_params=_params("arbitrary"),
        name="s5_outputs",
    )(u3, x0, *prm, d2)


def _s5_prompt(u, batch, seq, prm_all, d_skip, g, p):
    b_blk, c_blk, ar, ai, ar_c, ai_c = prm_all
    steps = STEP_CHUNK
    n_chunks = seq // steps
    r = batch * n_chunks
    width = u.shape[1]
    u3 = u.reshape(r, steps, width)
    nt, _, sw = b_blk.shape
    tr = 128
    z = pl.pallas_call(
        functools.partial(_s5_local_kernel, n_tiles=nt),
        grid=(r // tr,),
        in_specs=[pl.BlockSpec((tr, steps, width), lambda i: (i, 0, 0)),
                  pl.BlockSpec(b_blk.shape, lambda i: (0, 0, 0)),
                  pl.BlockSpec(ar.shape, lambda i: (0, 0)),
                  pl.BlockSpec(ai.shape, lambda i: (0, 0))],
        out_specs=pl.BlockSpec((tr, nt * sw), lambda i: (i, 0)),
        out_shape=jax.ShapeDtypeStruct((r, nt * sw), F32),
        compiler_params=_params("arbitrary"),
        name="s5_chunk_states",
    )(u3, b_blk, ar, ai)
    half = sw // 2
    tiles = math.gcd(nt, CARRY_TILES_PER_STEP)
    xc, xf = pl.pallas_call(
        functools.partial(_s5_carry_kernel, batch=batch, n_chunks=n_chunks),
        grid=(nt // tiles,),
        in_specs=[pl.BlockSpec((r, tiles * sw), lambda j: (0, j)),
                  pl.BlockSpec((tiles, 1, half), lambda j: (j, 0, 0)),
                  pl.BlockSpec((tiles, 1, half), lambda j: (j, 0, 0))],
        out_specs=[pl.BlockSpec((r, tiles * sw), lambda j: (0, j)),
                   pl.BlockSpec((batch, tiles * sw), lambda j: (0, j))],
        out_shape=[jax.ShapeDtypeStruct((r, nt * sw), F32),
                   jax.ShapeDtypeStruct((batch, nt * sw), F32)],
        compiler_params=_params("arbitrary"),
        name="s5_carry",
    )(z, ar_c.reshape(nt, 1, half), ai_c.reshape(nt, 1, half))
    y3 = _s5_outputs(u3, xc, (b_blk, c_blk, ar, ai), d_skip, tr)
    s_re, s_im = _tiles_to_state(xf, g, p)
    return y3.reshape(batch * seq, width), s_re, s_im


def _s5_sample(u, nb, t_new, prm_all, d_skip, s_re, s_im):
    prm = prm_all[:4]
    g, p = s_re.shape[1], s_re.shape[2]
    width = u.shape[1]
    nt = prm[0].shape[0]
    d2 = d_skip.reshape(1, width).astype(F32)
    state_t = lambda a: jnp.transpose(a.astype(F32), (1, 2, 0)).reshape(g * p, nb)
    tokens = pl.BlockSpec((nb, t_new, width), lambda i: (0, 0, 0))
    state = pl.BlockSpec((g * p, nb), lambda i: (0, 0))
    y3, n_re, n_im = pl.pallas_call(
        functools.partial(_s5_sample_kernel, n_tiles=nt),
        grid=(1,),
        in_specs=[tokens, state, state] + _s5_weight_specs(prm, d2),
        out_specs=[tokens, state, state],
        out_shape=[jax.ShapeDtypeStruct((nb, t_new, width), F32)]
                  + [jax.ShapeDtypeStruct((g * p, nb), F32)] * 2,
        compiler_params=_params("arbitrary"),
        name="s5_sample",
    )(u.reshape(nb, t_new, width), state_t(s_re), state_t(s_im), *prm, d2)
    back = lambda a: jnp.transpose(a.reshape(g, p, nb), (2, 0, 1))
    return y3.reshape(nb * t_new, width), back(n_re), back(n_im)


def _merge_kernel(attn_ref, y_ref, ga_ref, gb_ref, wa_ref, wg_ref, o_ref):
    d = o_ref.shape[1]
    y_a = jnp.dot(attn_ref[...], wa_ref[...], preferred_element_type=F32)
    z = _gelu_tanh(y_ref[...]).astype(BF16)
    glu = jnp.dot(z, wg_ref[...], preferred_element_type=F32)
    y_b = glu[:, :d] * _sigmoid(glu[:, d:])
    o_ref[...] = (_sigmoid(ga_ref[...]) * y_a + _sigmoid(gb_ref[...]) * y_b).astype(o_ref.dtype)


def _merge(attn, y, ga, gb, w_attn, w_glu, tm):
    m, d = ga.shape
    row = lambda w: pl.BlockSpec((tm, w), lambda i: (i, 0))
    return pl.pallas_call(
        _merge_kernel,
        grid=(m // tm,),
        in_specs=[row(attn.shape[1]), row(y.shape[1]), row(d), row(d),
                  pl.BlockSpec(w_attn.shape, lambda i: (0, 0)),
                  pl.BlockSpec(w_glu.shape, lambda i: (0, 0))],
        out_specs=row(d),
        out_shape=jax.ShapeDtypeStruct((m, d), BF16),
        compiler_params=_params("arbitrary"),
        name="merge",
    )(attn, y, ga, gb, w_attn, w_glu)


def _out_proj_kernel(m_ref, x_ref, w_ref, g_ref, x1_ref, h_ref):
    x1 = x_ref[...] + jnp.dot(m_ref[...], w_ref[...], preferred_element_type=F32)
    x1_ref[...] = x1
    ms = jnp.mean(x1 * x1, axis=-1, keepdims=True)
    h_ref[...] = (x1 * lax.rsqrt(ms + EPS) * g_ref[...]).astype(h_ref.dtype)


def _out_proj(merged, x, w_out, g, tm):
    m, d = x.shape
    row = pl.BlockSpec((tm, d), lambda i: (i, 0))
    return pl.pallas_call(
        _out_proj_kernel,
        grid=(m // tm,),
        in_specs=[row, row, pl.BlockSpec(w_out.shape, lambda i: (0, 0)),
                  pl.BlockSpec((1, d), lambda i: (0, 0))],
        out_specs=[row, row],
        out_shape=[jax.ShapeDtypeStruct((m, d), F32), jax.ShapeDtypeStruct((m, d), BF16)],
        compiler_params=_params("arbitrary"),
        name="out_proj",
    )(merged, x, w_out, g.reshape(1, d))


FF_SUB = 1


def _shift_rows(up, prev, k):
    body = pltpu.roll(up, k, axis=0)
    row = lax.broadcasted_iota(jnp.int32, prev.shape, 0)
    head = jnp.where(row < k, pltpu.roll(prev, k, axis=0), body[:SUBLANES])
    return jnp.concatenate([head, body[SUBLANES:]], axis=0)


def _conv_taps(up, u1, u2, cw, cb):
    return cb + cw[0:1, :] * u2 + cw[1:2, :] * u1 + cw[2:3, :] * up


def _ffn_finish(j, n_j, final_norm, acts, wd_ref, x1_ref, g_ref, y_ref):
    y_ref[...] += jnp.dot(jnp.concatenate(acts, axis=1), wd_ref[...], preferred_element_type=F32)

    @pl.when(j == n_j - 1)
    def _():
        x2 = x1_ref[...] + y_ref[...]
        if final_norm:
            ms = jnp.mean(x2 * x2, axis=-1, keepdims=True)
            x2 = x2 * lax.rsqrt(ms + EPS) * g_ref[...]
        y_ref[...] = x2


def _ffn_prompt_kernel(h_ref, wg_ref, wv_ref, cwg_ref, cwv_ref, cbg_ref, cbv_ref, wd_ref, x1_ref,
                       g_ref, y_ref, convg_ref, convv_ref, carry_ref,
                       *, tm, tiles_per_seq, n_j, final_norm):
    i, j = pl.program_id(0), pl.program_id(1)

    @pl.when(i % tiles_per_seq == 0)
    def _():
        carry_ref[j] = jnp.zeros(carry_ref.shape[1:], F32)

    @pl.when(j == 0)
    def _():
        y_ref[...] = jnp.zeros(y_ref.shape, F32)

    h = h_ref[...]
    prev = carry_ref[j]
    sub = FF_TILE // FF_SUB
    acts, tails = [], ([], [])
    for s in range(FF_SUB):
        cs = slice(s * sub, (s + 1) * sub)
        mixed = []
        for half, (w_ref, cw_ref, cb_ref) in enumerate(((wg_ref, cwg_ref, cbg_ref),
                                                        (wv_ref, cwv_ref, cbv_ref))):
            up = jnp.dot(h, w_ref[:, cs], preferred_element_type=F32)
            p8 = prev[:, half * FF_TILE + s * sub:half * FF_TILE + (s + 1) * sub]
            mixed.append(_conv_taps(up, _shift_rows(up, p8, 1), _shift_rows(up, p8, 2),
                                    cw_ref[:, cs], cb_ref[:, cs]))
            tails[half].append(up[tm - SUBLANES:])
        gate, val = mixed
        acts.append((gate * _sigmoid(gate) * val).astype(BF16))
    tail_g = jnp.concatenate(tails[0], axis=1)
    tail_v = jnp.concatenate(tails[1], axis=1)
    carry_ref[j] = jnp.concatenate([tail_g, tail_v], axis=1)
    convg_ref[...] = tail_g
    convv_ref[...] = tail_v
    _ffn_finish(j, n_j, final_norm, acts, wd_ref, x1_ref, g_ref, y_ref)


def _ffn_sample_kernel(h_ref, wg_ref, wv_ref, cwg_ref, cwv_ref, cbg_ref, cbv_ref, wd_ref, x1_ref,
                       g_ref, stg_ref, stv_ref, y_ref, convg_ref, convv_ref,
                       *, nb, t_new, n_j, final_norm):
    j = pl.program_id(0)

    @pl.when(j == 0)
    def _():
        y_ref[...] = jnp.zeros(y_ref.shape, F32)

    h = h_ref[...]
    sub = FF_TILE // FF_SUB
    acts = []
    for s in range(FF_SUB):
        cs = slice(s * sub, (s + 1) * sub)
        mixed = []
        for w_ref, cw_ref, cb_ref, st_ref, conv_ref in (
                (wg_ref, cwg_ref, cbg_ref, stg_ref, convg_ref),
                (wv_ref, cwv_ref, cbv_ref, stv_ref, convv_ref)):
            up = jnp.dot(h, w_ref[:, cs], preferred_element_type=F32)
            s0, s1 = st_ref[:, 0, cs], st_ref[:, 1, cs]
            u1 = jnp.concatenate([s1, up[:(t_new - 1) * nb]], axis=0)
            u2 = jnp.concatenate([s0, s1, up[:(t_new - 2) * nb]], axis=0)
            mixed.append(_conv_taps(up, u1, u2, cw_ref[:, cs], cb_ref[:, cs]))
            conv_ref[:, 0, cs] = up[(t_new - 2) * nb:(t_new - 1) * nb]
            conv_ref[:, 1, cs] = up[(t_new - 1) * nb:]
        gate, val = mixed
        acts.append((gate * _sigmoid(gate) * val).astype(BF16))
    _ffn_finish(j, n_j, final_norm, acts, wd_ref, x1_ref, g_ref, y_ref)


def _ffn_weight_specs(d, n_j, ix):
    gate = lambda rows: pl.BlockSpec((rows, FF_TILE), ix(lambda j: (0, j)))
    val = lambda rows: pl.BlockSpec((rows, FF_TILE), ix(lambda j: (0, n_j + j)))
    return [gate(d), val(d), gate(3), val(3), gate(1), val(1),
            pl.BlockSpec((FF_TILE, d), ix(lambda j: (j, 0)))]


def _ffn_prompt(h2, x1, wu, cw, cb, wd, g, batch, seq, tm, final_norm):
    m, d = x1.shape
    d_ff = wd.shape[0]
    n_j = d_ff // FF_TILE
    tps = seq // tm
    kern = functools.partial(_ffn_prompt_kernel, tm=tm, tiles_per_seq=tps, n_j=n_j,
                             final_norm=final_norm)
    ix = lambda f: (lambda i, j: f(j))
    conv_spec = pl.BlockSpec((None, SUBLANES, FF_TILE), lambda i, j: (i, 0, j))
    conv_shape = jax.ShapeDtypeStruct((m // tm, SUBLANES, d_ff), F32)
    y, cg, cv = pl.pallas_call(
        kern,
        grid=(m // tm, n_j),
        in_specs=[pl.BlockSpec((tm, d), lambda i, j: (i, 0))] + _ffn_weight_specs(d, n_j, ix)
                 + [pl.BlockSpec((tm, d), lambda i, j: (i, 0)),
                    pl.BlockSpec((1, d), lambda i, j: (0, 0))],
        out_specs=[pl.BlockSpec((tm, d), lambda i, j: (i, 0)), conv_spec, conv_spec],
        out_shape=[jax.ShapeDtypeStruct((m, d), F32), conv_shape, conv_shape],
        scratch_shapes=[pltpu.VMEM((n_j, SUBLANES, 2 * FF_TILE), F32)],
        compiler_params=_params("arbitrary", "arbitrary"),
        name="ffn_prompt",
    )(h2, wu, wu, cw, cw, cb, cb, wd, x1, g.reshape(1, d))
    return y, jnp.concatenate([cg, cv], axis=-1)[tps - 1::tps, SUBLANES - 2:]


def _ffn_sample(h2, x1, wu, cw, cb, wd, g, state, nb, t_new, final_norm):
    m, d = x1.shape
    d_ff = wd.shape[0]
    n_j = d_ff // FF_TILE
    kern = functools.partial(_ffn_sample_kernel, nb=nb, t_new=t_new, n_j=n_j,
                             final_norm=final_norm)
    ix = lambda f: f
    st_gate = pl.BlockSpec((nb, 2, FF_TILE), lambda j: (0, 0, j))
    st_val = pl.BlockSpec((nb, 2, FF_TILE), lambda j: (0, 0, n_j + j))
    conv_spec = pl.BlockSpec((nb, 2, FF_TILE), lambda j: (0, 0, j))
    y, cg, cv = pl.pallas_call(
        kern,
        grid=(n_j,),
        in_specs=[pl.BlockSpec((m, d), lambda j: (0, 0))] + _ffn_weight_specs(d, n_j, ix)
                 + [pl.BlockSpec((m, d), lambda j: (0, 0)),
                    pl.BlockSpec((1, d), lambda j: (0, 0)),
                    st_gate, st_val],
        out_specs=[pl.BlockSpec((m, d), lambda j: (0, 0)), conv_spec, conv_spec],
        out_shape=[jax.ShapeDtypeStruct((m, d), F32),
                   jax.ShapeDtypeStruct((nb, 2, d_ff), F32),
                   jax.ShapeDtypeStruct((nb, 2, d_ff), F32)],
        compiler_params=_params("arbitrary"),
        name="ffn_sample",
    )(h2, wu, wu, cw, cw, cb, cb, wd, x1, g.reshape(1, d), state, state)
    return y, jnp.concatenate([cg, cv], axis=-1)


def kernel(x_prompt, x_sample, cache_k, cache_v, cache_kidx, state_s5_re, state_s5_im, state_ffn_conv, page_table, norm_mix, w_in, w_attn_proj, s5_a_re, s5_a_im, s5_log_step, s5_b_re, s5_b_im, s5_c_re, s5_c_im, s5_d, w_glu, w_out, norm_ffn, w_up, conv_w, conv_b, w_down, norm_final):
    depth = w_in.shape[0]
    batch, seq, d_model = x_prompt.shape
    nb, t_new, _ = x_sample.shape
    d_head = cache_k.shape[-1]
    d_idx = cache_kidx.shape[-1]
    attn_w = N_HEADS * d_head
    kv_w = N_KV_HEADS * d_head
    qi_w = IDX_HEADS * d_idx
    groups, p_state = s5_a_re.shape[1], s5_a_re.shape[2]
    s5_w = groups * S5_GROUP

    xp = x_prompt.reshape(batch * seq, d_model)
    xs = x_sample.reshape(nb * t_new, d_model)
    tm_p = min(512, batch * seq)
    tm_s = nb * t_new

    outs = {name: [] for name in ("kp", "vp", "kip", "srp", "sip", "cp",
                                  "ks", "vs", "kis", "srs", "sis", "cs")}
    for l in range(depth):
        w_t = jnp.swapaxes(w_in[l], 0, 1).astype(BF16)
        o = 0
        seg = {}
        for name, width in (("q", attn_w), ("k", kv_w), ("v", kv_w), ("qi", qi_w), ("ki", d_idx),
                            ("wi", IDX_HEADS), ("u", s5_w), ("ga", d_model), ("gb", d_model)):
            seg[name] = w_t[o:o + width]
            o += width
        w_q_grouped = _group_major_heads(seg["q"], d_head)
        w_u, w_ga, w_gb = seg["u"], seg["ga"], seg["gb"]
        pad = jnp.zeros((LANES - d_idx - IDX_HEADS, d_model), BF16)
        w_small = jnp.concatenate([seg["qi"], seg["k"], seg["v"], seg["ki"], seg["wi"], pad], axis=0)
        w_qiw = jnp.concatenate([seg["qi"], seg["ki"], seg["wi"], pad], axis=0)
        w_kv_t = jnp.concatenate([seg["k"], seg["v"], seg["ki"]], axis=0)
        w_attn_grouped = _group_major_heads(w_attn_proj[l].astype(BF16), d_head)
        w_g = w_glu[l].astype(BF16)
        w_o = w_out[l].astype(BF16)
        wu = w_up[l].astype(BF16)
        cw = conv_w[l].astype(F32)
        cb = conv_b[l].astype(F32).reshape(1, -1)
        wd = w_down[l].astype(BF16)
        s5p = _s5_params(s5_a_re[l], s5_a_im[l], s5_log_step[l], s5_b_re[l], s5_b_im[l],
                         s5_c_re[l], s5_c_im[l], STEP_CHUNK)
        last = l == depth - 1

        def project(x, tm, ws):
            (q, u, small), h = _norm_matmul(x, norm_mix[l], [w_q_grouped, w_u, ws], tm,
                                            "norm_proj_q_u_small")
            ga, gb = _matmul(h, [w_ga, w_gb], tm, F32, "proj_gates")
            return h, q, u, ga, gb, small

        h, q, u, ga, gb, qiw = project(xp, tm_p, w_qiw)
        k_t, v_t, ki_t, kb_t, vb_t, kib_t = _kv_transposed(h, w_kv_t, batch, seq, kv_w, d_idx,
                                                            min(512, seq))
        k_sel = min(TOPK_MAX, seq // 4)
        attn = _prompt_attention(q, qiw, kib_t, kb_t, vb_t, batch, seq, d_head, d_idx, k_sel,
                                 min(256, seq))
        y5, srp, sip = _s5_prompt(u, batch, seq, s5p, s5_d[l], groups, p_state)
        merged = _merge(attn, y5, ga, gb, w_attn_grouped, w_g, min(256, batch * seq))
        x1, h2 = _out_proj(merged, xp, w_o, norm_ffn[l], tm_p)
        xp, conv_p = _ffn_prompt(h2, x1, wu, cw, cb, wd, norm_final, batch, seq, tm_p, last)

        def heads_last(a_t):
            return jnp.transpose(a_t.reshape(batch, N_KV_HEADS, d_head, seq), (0, 3, 1, 2))

        outs["kp"].append(heads_last(k_t)); outs["vp"].append(heads_last(v_t))
        outs["kip"].append(jnp.swapaxes(ki_t, 1, 2))
        outs["srp"].append(srp); outs["sip"].append(sip); outs["cp"].append(conv_p)

        _, q, u, ga, gb, small = project(xs, tm_s, w_small)
        attn = _sample_attention(q, small, cache_k[l], cache_v[l], cache_kidx[l], page_table,
                                 t_new, d_head, d_idx)
        y5, srs, sis = _s5_sample(u, nb, t_new, s5p, s5_d[l], state_s5_re[l], state_s5_im[l])
        merged = _merge(attn, y5, ga, gb, w_attn_grouped, w_g, min(256, nb * t_new))
        x1, h2 = _out_proj(merged, xs, w_o, norm_ffn[l], tm_s)

        def time_major(a):
            return a.reshape(nb, t_new, -1).swapaxes(0, 1).reshape(nb * t_new, -1)

        y_tm, conv_s = _ffn_sample(time_major(h2), time_major(x1), wu, cw, cb, wd, norm_final,
                                   state_ffn_conv[l].astype(F32), nb, t_new, last)
        xs = y_tm.reshape(t_new, nb, d_model).swapaxes(0, 1).reshape(nb * t_new, d_model)
        k_new = small[:, qi_w:qi_w + kv_w].reshape(nb, t_new, N_KV_HEADS, d_head)
        v_new = small[:, qi_w + kv_w:qi_w + 2 * kv_w].reshape(nb, t_new, N_KV_HEADS, d_head)
        ki_new = small[:, qi_w + 2 * kv_w:qi_w + 2 * kv_w + d_idx].reshape(nb, t_new, d_idx)
        outs["ks"].append(k_new); outs["vs"].append(v_new); outs["kis"].append(ki_new)
        outs["srs"].append(srs); outs["sis"].append(sis); outs["cs"].append(conv_s)

    stk = {name: jnp.stack(v) for name, v in outs.items()}
    return (xp.reshape(batch, seq, d_model), xs.reshape(nb, t_new, d_model),
            stk["kp"], stk["vp"], stk["kip"], stk["srp"], stk["sip"], stk["cp"],
            stk["ks"], stk["vs"], stk["kis"], stk["srs"], stk["sis"], stk["cs"])
```

```python
import functools
import math

import jax
import jax.numpy as jnp
from jax import lax
from jax.experimental import pallas as pl
from jax.experimental.pallas import tpu as pltpu

F32 = jnp.float32
BF16 = jnp.bfloat16

EPS = 1e-6
LOG2E = math.log2(math.e)
TOPK_MAX = 256
N_HEADS = 16
N_KV_HEADS = 4
KV_GROUP = N_HEADS // N_KV_HEADS
IDX_HEADS = 8
S5_GROUP = 16
STEP_CHUNK = 8
CARRY_TILES_PER_STEP = 2
LANES = 128
SUBLANES = 8
GROUPS_PER_TILE = LANES // S5_GROUP
FF_TILE = 512
VMEM_LIMIT = 48 * 1024 * 1024
NEG_INF = float("-inf")
INT_MIN = -2 ** 31


def _params(*sem):
    return pltpu.CompilerParams(dimension_semantics=sem, vmem_limit_bytes=VMEM_LIMIT)


def _sigmoid(x):
    return 0.5 * jnp.tanh(0.5 * x) + 0.5


def _gelu_tanh(x):
    c = math.sqrt(2.0 / math.pi)
    return 0.5 * x * (1.0 + jnp.tanh(c * (x + 0.044715 * (x * x * x))))


def _dot_nt(a, b):
    return lax.dot_general(a, b, (((1,), (1,)), ((), ())), preferred_element_type=F32)


def _resident(w):
    return pl.BlockSpec(w.shape, lambda i: (0, 0), pipeline_mode=pl.Buffered(1))


def _mm_kernel(h_ref, *refs):
    n = len(refs) // 2
    h = h_ref[...]
    for w_ref, o_ref in zip(refs[:n], refs[n:]):
        o_ref[...] = _dot_nt(h, w_ref[...]).astype(o_ref.dtype)


def _matmul(h, weights_t, tm, out_dtype, name):
    m, k = h.shape
    return pl.pallas_call(
        _mm_kernel,
        grid=(m // tm,),
        in_specs=[pl.BlockSpec((tm, k), lambda i: (i, 0))]
                 + [_resident(w) for w in weights_t],
        out_specs=[pl.BlockSpec((tm, w.shape[0]), lambda i: (i, 0)) for w in weights_t],
        out_shape=[jax.ShapeDtypeStruct((m, w.shape[0]), out_dtype) for w in weights_t],
        compiler_params=_params("arbitrary"),
        name=name,
    )(h, *weights_t)


def _norm_mm_kernel(x_ref, g_ref, *refs):
    n = (len(refs) - 1) // 2
    x = x_ref[...]
    ms = jnp.mean(x * x, axis=-1, keepdims=True)
    h = (x * lax.rsqrt(ms + EPS) * g_ref[...]).astype(BF16)
    refs[2 * n][...] = h
    for w_ref, o_ref in zip(refs[:n], refs[n:2 * n]):
        o_ref[...] = _dot_nt(h, w_ref[...]).astype(o_ref.dtype)


def _norm_matmul(x, g, weights_t, tm, name):
    m, k = x.shape
    *outs, h = pl.pallas_call(
        _norm_mm_kernel,
        grid=(m // tm,),
        in_specs=[pl.BlockSpec((tm, k), lambda i: (i, 0)), pl.BlockSpec((1, k), lambda i: (0, 0))]
                 + [_resident(w) for w in weights_t],
        out_specs=[pl.BlockSpec((tm, w.shape[0]), lambda i: (i, 0)) for w in weights_t]
                  + [pl.BlockSpec((tm, k), lambda i: (i, 0))],
        out_shape=[jax.ShapeDtypeStruct((m, w.shape[0]), F32) for w in weights_t]
                  + [jax.ShapeDtypeStruct((m, k), BF16)],
        compiler_params=_params("arbitrary"),
        name=name,
    )(x, g.reshape(1, k), *weights_t)
    return outs, h


def _kv_t_kernel(w_ref, h_ref, k_ref, v_ref, ki_ref, kb_ref, vb_ref, kib_ref, *, kv_w):
    out = _dot_nt(w_ref[...], h_ref[...])
    for lo, hi, full_ref, half_ref in ((0, kv_w, k_ref, kb_ref), (kv_w, 2 * kv_w, v_ref, vb_ref),
                                       (2 * kv_w, out.shape[0], ki_ref, kib_ref)):
        full_ref[...] = out[lo:hi]
        half_ref[...] = out[lo:hi].astype(BF16)


def _kv_transposed(h, w_t, batch, seq, kv_w, d_idx, tn):
    d = h.shape[1]
    nt = seq // tn
    out = lambda rows: pl.BlockSpec((None, rows, tn), lambda b, i: (b, 0, i))
    shape = lambda rows, dt: jax.ShapeDtypeStruct((batch, rows, seq), dt)
    widths = (kv_w, kv_w, d_idx)
    return pl.pallas_call(
        functools.partial(_kv_t_kernel, kv_w=kv_w),
        grid=(batch, nt),
        in_specs=[pl.BlockSpec(w_t.shape, lambda b, i: (0, 0)),
                  pl.BlockSpec((tn, d), lambda b, i: (b * nt + i, 0))],
        out_specs=[out(w) for w in widths] * 2,
        out_shape=[shape(w, F32) for w in widths] + [shape(w, BF16) for w in widths],
        compiler_params=_params("arbitrary", "arbitrary"),
        name="proj_kv_transposed",
    )(w_t, h)


SEARCH_GROUP = 4


def _count(mask):
    return jnp.sum(jnp.where(mask, 1.0, 0.0), axis=1, keepdims=True)


def _topk_mask(score, col, k_sel, n_cols):
    kf = float(k_sel)
    rows = score.shape[0]

    def as_float(key):
        bits = jnp.where(key < 0, key ^ jnp.int32(0x7FFFFFFF), key)
        return lax.bitcast_convert_type(bits, F32)

    finite = score > NEG_INF
    few = _count(finite) <= kf

    cnt0 = _count(score >= 0.0)
    cand0 = jnp.where(cnt0 >= kf, jnp.int32(0), jnp.int32(INT_MIN))
    cand0 = jnp.broadcast_to(cand0, (rows, 1)).astype(jnp.int32)
    cnt0 = jnp.where(cnt0 >= kf, cnt0, float(n_cols))

    def search_body(group, carry):
        cand, cnt = carry
        for g in range(SEARCH_GROUP):
            shift = jnp.int32(30) - (group * SEARCH_GROUP + g)
            bit = jnp.where(shift >= 0, lax.shift_left(jnp.int32(1), jnp.maximum(shift, 0)), 0)
            trial = cand + bit
            c_trial = _count(score >= as_float(trial))
            ok = c_trial >= kf
            cand = jnp.where(ok, trial, cand)
            cnt = jnp.where(ok, c_trial, cnt)
        return cand, cnt

    cand, cnt = lax.fori_loop(0, -(-31 // SEARCH_GROUP), search_body, (cand0, cnt0))
    thr = as_float(cand)

    above = score > thr
    ties = score == thr
    need = kf - _count(above)
    surplus = jnp.where(few, 0.0, cnt - kf)
    n_bits = max(1, (n_cols - 1).bit_length())

    def index_step(it, m):
        trial = m + lax.shift_left(jnp.int32(1), jnp.int32(n_bits - 1) - it)
        taken = _count(jnp.logical_and(ties, col < trial))
        return jnp.where(taken <= need - 1.0, trial, m)

    def lowest_indices(_):
        return lax.fori_loop(0, n_bits, index_step, jnp.zeros((rows, 1), jnp.int32))

    def all_ties(_):
        return jnp.full((rows, 1), n_cols, jnp.int32)

    last = lax.cond(jnp.max(surplus) > 0.0, lowest_indices, all_ties, None)
    take_tie = jnp.logical_and(jnp.logical_and(ties, col <= last), need >= 1.0)
    top = jnp.logical_or(above, take_tie)
    return jnp.logical_and(finite, jnp.logical_or(few, top))


def _index_weights(kw, d_idx):
    w_scale = (d_idx ** -0.5) * (IDX_HEADS ** -0.5)
    return [kw[:, d_idx + h:d_idx + h + 1] * w_scale for h in range(IDX_HEADS)]


def _head_rows(qi, d_idx):
    return jnp.concatenate([qi[:, h * d_idx:(h + 1) * d_idx] for h in range(IDX_HEADS)], axis=0)


def _weighted_relu_sum(lg, w_cols, rows):
    score = jnp.zeros((rows, lg.shape[1]), F32)
    for h in range(IDX_HEADS):
        score = score + jnp.maximum(lg[h * rows:(h + 1) * rows], 0.0) * w_cols[h]
    return score


HEADS_PER_DOT = 2
SHORT_PREFIX_BLOCKS = 2


PROMPT_SCORE_ELEMS = 512 * 1024


def _prompt_attn_kernel(q_ref, qi_ref, kw_ref, kit_ref, kt_ref, vt_ref, prev_ref, o_ref,
                        *, tq, n_keys, k_sel, d_head, d_idx, q_block, heads_per_dot):
    del prev_ref
    n_seqs = q_ref.shape[0]
    scores = []
    for b in range(n_seqs):
        qi = qi_ref[b]
        w_cols = _index_weights(kw_ref[b], d_idx)
        kit = kit_ref[b]
        score = jnp.zeros((tq, n_keys), F32)
        for h in range(IDX_HEADS):
            lg = jnp.dot(qi[:, h * d_idx:(h + 1) * d_idx].astype(BF16), kit,
                         preferred_element_type=F32)
            score = score + jnp.maximum(lg, 0.0) * w_cols[h]
        scores.append(score)
    score = jnp.concatenate(scores, axis=0)

    col = lax.broadcasted_iota(jnp.int32, score.shape, 1)
    row = lax.broadcasted_iota(jnp.int32, (n_seqs, tq, n_keys), 1).reshape(score.shape)
    score = jnp.where(col <= q_block * tq + row, score, NEG_INF)
    bias_all = jnp.where(_topk_mask(score, col, k_sel, n_keys), 0.0, NEG_INF)

    for b in range(n_seqs):
        bias = bias_all[b * tq:(b + 1) * tq]
        q = q_ref[b] * (d_head ** -0.5 * LOG2E)
        for n in range(N_KV_HEADS):
            kt = kt_ref[b, n * d_head:(n + 1) * d_head, :]
            vt = vt_ref[b, n * d_head:(n + 1) * d_head, :]
            for g0 in range(0, KV_GROUP, heads_per_dot):
                heads = [(g0 + i) * N_KV_HEADS + n for i in range(heads_per_dot)]
                qs = jnp.concatenate([q[:, h * d_head:(h + 1) * d_head] for h in heads], axis=0)
                s = jnp.dot(qs.astype(BF16), kt, preferred_element_type=F32)
                s = (s.reshape(heads_per_dot, tq, n_keys) + bias[None]).reshape(-1, n_keys)
                m = jnp.max(s, axis=1, keepdims=True)
                p = jnp.exp2(s - m)
                l = jnp.sum(p, axis=1, keepdims=True)
                o = _dot_nt(p.astype(BF16), vt) / l
                for i, h in enumerate(heads):
                    o_ref[b, :, h * d_head:(h + 1) * d_head] = (
                        o[i * tq:(i + 1) * tq].astype(o_ref.dtype))


def _prompt_attention(q, qiw, ki_t, k_t, v_t, batch, seq, d_head, d_idx, k_sel, tq):
    attn_w = q.shape[1]
    kv_w = N_KV_HEADS * d_head
    qi_w = IDX_HEADS * d_idx
    nq = seq // tq
    q3 = q.reshape(batch, seq, attn_w)
    qiw3 = qiw.reshape(batch, seq, qiw.shape[1])
    out = jnp.zeros((batch, seq, attn_w), BF16)
    for qb in range(nq):
        n_keys = (qb + 1) * tq
        n_seqs = max(d for d in range(1, batch + 1)
                     if batch % d == 0 and (d == 1 or d * tq * n_keys <= PROMPT_SCORE_ELEMS))
        row = lambda w, c, qb=qb, n_seqs=n_seqs: pl.BlockSpec((n_seqs, tq, w), lambda b: (b, qb, c))
        keys = lambda w, n_keys=n_keys, n_seqs=n_seqs: pl.BlockSpec((n_seqs, w, n_keys),
                                                                     lambda b: (b, 0, 0))
        heads_per_dot = HEADS_PER_DOT if n_keys <= SHORT_PREFIX_BLOCKS * tq else 1
        out = pl.pallas_call(
            functools.partial(_prompt_attn_kernel, tq=tq, n_keys=n_keys, k_sel=k_sel,
                              d_head=d_head, d_idx=d_idx, q_block=qb,
                              heads_per_dot=heads_per_dot),
            grid=(batch // n_seqs,),
            in_specs=[row(attn_w, 0), row(qi_w, 0), row(LANES, qi_w // LANES),
                      keys(d_idx), keys(kv_w), keys(kv_w),
                      pl.BlockSpec(memory_space=pl.ANY)],
            out_specs=row(attn_w, 0),
            out_shape=jax.ShapeDtypeStruct((batch, seq, attn_w), BF16),
            input_output_aliases={6: 0},
            compiler_params=_params("arbitrary"),
            name=f"prompt_attention_q{qb}",
        )(q3, qiw3, qiw3, ki_t, k_t, v_t, out)
    return out.reshape(batch * seq, attn_w)


def _lane_columns(cols, width):
    rows = cols[0].shape[0]
    lane = lax.broadcasted_iota(jnp.int32, (rows, width), 1)
    out = jnp.zeros((rows, width), F32)
    for j, c in enumerate(cols):
        out = jnp.where(lane == j, c, out)
    return out


def _sample_score_kernel(pt_ref, qi_ref, kw_ref, kin_ref, *rest, n_seqs, n_pages, page, t_new,
                         d_idx):
    del pt_ref
    s_ref = rest[n_seqs * n_pages]
    past = n_pages * page
    rows = SUBLANES
    pad_row = lax.broadcasted_iota(jnp.int32, (rows, page), 0) >= t_new
    first = lax.broadcasted_iota(jnp.int32, (rows, page), 1) == 0
    lane = lax.broadcasted_iota(jnp.int32, (rows, LANES), 1)
    t = lax.broadcasted_iota(jnp.int32, (rows, LANES), 0)
    causal_new = jnp.logical_and(lane <= t, t < t_new)
    for i in range(n_seqs):
        ipages = rest[i * n_pages:(i + 1) * n_pages]
        qi_rows = _head_rows(qi_ref[i], d_idx)
        qi_bf = qi_rows.astype(BF16)
        w_cols = _index_weights(kw_ref[i], d_idx)
        for p in range(n_pages):
            lg = jnp.dot(qi_bf, ipages[p][...].astype(BF16), preferred_element_type=F32)
            pad_val = jnp.where(first, 0.0, NEG_INF) if p == 0 else NEG_INF
            s_ref[i, :, p * page:(p + 1) * page] = jnp.where(
                pad_row, pad_val, _weighted_relu_sum(lg, w_cols, rows))
        kin = kin_ref[i]
        new_cols = []
        for j in range(t_new):
            lg = jnp.sum(qi_rows * kin[j:j + 1, :], axis=1, keepdims=True)
            new_cols.append(_weighted_relu_sum(lg, w_cols, rows))
        s_ref[i, :, past:] = jnp.where(causal_new, _lane_columns(new_cols, LANES), NEG_INF)


def _select_kernel(s_ref, b_ref, *, k_sel):
    score = s_ref[...]
    col = lax.broadcasted_iota(jnp.int32, score.shape, 1)
    keep = _topk_mask(score, col, k_sel, score.shape[1])
    b_ref[...] = jnp.where(keep, 0.0, NEG_INF)


SELECT_ROWS = 512
SCORE_SEQS_PER_STEP = 4
ATTEND_SEQS_PER_STEP = 2


def _sample_attn_kernel(pt_ref, q_ref, kn_ref, vn_ref, bias_ref, *rest,
                        n_seqs, n_pages, page, t_new, d_head):
    del pt_ref
    pages = rest[:2 * n_seqs * n_pages]
    o_ref, kt_ref, vt_ref = rest[2 * n_seqs * n_pages:]
    for i in range(n_seqs):
        kpages = pages[i * n_pages:(i + 1) * n_pages]
        vpages = pages[(n_seqs + i) * n_pages:(n_seqs + i + 1) * n_pages]
        o_ref[i] = _sample_attend(q_ref[i], kn_ref[i], vn_ref[i], bias_ref[i], kpages, vpages,
                                  kt_ref.at[i], vt_ref.at[i], page, t_new, d_head
                                  ).astype(o_ref.dtype)


def _sample_attend(q, kn, vn, bias, kpages, vpages, kt_ref, vt_ref, page, t_new, d_head):
    n_pages = len(kpages)
    past = n_pages * page
    rows = SUBLANES
    kv_w = N_KV_HEADS * d_head

    for p in range(n_pages):
        kt_ref[:, p * page:(p + 1) * page] = kpages[p][...].astype(BF16)
        vt_ref[:, p * page:(p + 1) * page] = vpages[p][...].astype(BF16)

    bias = jnp.concatenate([bias] * N_HEADS, axis=0)

    q = q * (d_head ** -0.5 * LOG2E)
    lane_head = lax.broadcasted_iota(jnp.int32, (rows, kv_w), 1) // d_head
    q_rows = jnp.concatenate(
        [jnp.where(lane_head == n, q[:, g * kv_w:(g + 1) * kv_w], 0.0)
         for g in range(KV_GROUP) for n in range(N_KV_HEADS)], axis=0)

    s_past = jnp.dot(q_rows.astype(BF16), kt_ref[...], preferred_element_type=F32)
    s_past = s_past + bias[:, :past]
    s_new = _lane_columns(
        [jnp.sum(q_rows * kn[j:j + 1, :], axis=1, keepdims=True) for j in range(t_new)], LANES)
    s_new = s_new + bias[:, past:]

    m = jnp.maximum(jnp.max(s_past, axis=1, keepdims=True), jnp.max(s_new, axis=1, keepdims=True))
    p_past = jnp.exp2(s_past - m)
    p_new = jnp.exp2(s_new - m)
    l = jnp.sum(p_past, axis=1, keepdims=True) + jnp.sum(p_new, axis=1, keepdims=True)
    o = _dot_nt(p_past.astype(BF16), vt_ref[...])
    for j in range(t_new):
        o = o + p_new[:, j:j + 1] * vn[j:j + 1, :]
    o = o / l

    chunks = []
    for g in range(KV_GROUP):
        acc = jnp.zeros((rows, kv_w), F32)
        for n in range(N_KV_HEADS):
            r0 = (g * N_KV_HEADS + n) * rows
            acc = acc + jnp.where(lane_head == n, o[r0:r0 + rows], 0.0)
        chunks.append(acc)
    return jnp.concatenate(chunks, axis=1)


def _seq_page_map(b, pt, *, i, p, n_seqs, n_pages):
    return (pt[(b * n_seqs + i) * n_pages + p], 0, 0)


def _sample_attention(q, small, cache_k, cache_v, cache_ki, page_table, t_new, d_head, d_idx):
    nb, n_pages = page_table.shape
    n_pool, page = cache_k.shape[0], cache_k.shape[1]
    attn_w = q.shape[1]
    kv_w = N_KV_HEADS * d_head
    qi_w = IDX_HEADS * d_idx
    rows = SUBLANES
    past = n_pages * page
    k_sel = min(TOPK_MAX, (past + t_new) // 4)

    def pad_rows(a):
        a = a.reshape(nb, t_new, a.shape[-1])
        return jnp.pad(a, ((0, 0), (0, rows - t_new), (0, 0)))

    q8 = pad_rows(q)
    qi8 = pad_rows(small[:, :qi_w])
    kw8 = pad_rows(small[:, qi_w + 2 * kv_w:])
    kn = pad_rows(small[:, qi_w:qi_w + kv_w])
    vn = pad_rows(small[:, qi_w + kv_w:qi_w + 2 * kv_w])
    kin = pad_rows(small[:, qi_w + 2 * kv_w:qi_w + 2 * kv_w + d_idx])

    ck = jnp.transpose(cache_k, (0, 2, 3, 1)).reshape(n_pool, kv_w, page)
    cv = jnp.transpose(cache_v, (0, 2, 3, 1)).reshape(n_pool, kv_w, page)
    ci = jnp.transpose(cache_ki, (0, 2, 1))

    pt_flat = page_table.reshape(-1)
    n_keys = past + LANES

    def seqs_spec(w, n_seqs):
        return pl.BlockSpec((n_seqs, rows, w), lambda b, pt: (b, 0, 0))

    def seq_page_specs(w, n_seqs):
        return [pl.BlockSpec((None, w, page),
                             functools.partial(_seq_page_map, i=i, p=p, n_seqs=n_seqs,
                                               n_pages=n_pages))
                for i in range(n_seqs) for p in range(n_pages)]

    n_seqs = math.gcd(nb, SCORE_SEQS_PER_STEP)
    score = pl.pallas_call(
        functools.partial(_sample_score_kernel, n_seqs=n_seqs, n_pages=n_pages, page=page,
                          t_new=t_new, d_idx=d_idx),
        grid_spec=pltpu.PrefetchScalarGridSpec(
            num_scalar_prefetch=1,
            grid=(nb // n_seqs,),
            in_specs=[seqs_spec(qi_w, n_seqs), seqs_spec(LANES, n_seqs), seqs_spec(d_idx, n_seqs)]
                     + seq_page_specs(d_idx, n_seqs),
            out_specs=seqs_spec(n_keys, n_seqs),
        ),
        out_shape=jax.ShapeDtypeStruct((nb, rows, n_keys), F32),
        compiler_params=_params("arbitrary"),
        name="sample_scores",
    )(pt_flat, qi8, kw8, kin, *([ci] * (n_seqs * n_pages)))

    sel_rows = math.gcd(nb * rows, SELECT_ROWS)
    bias = pl.pallas_call(
        functools.partial(_select_kernel, k_sel=k_sel),
        grid=(nb * rows // sel_rows,),
        in_specs=[pl.BlockSpec((sel_rows, n_keys), lambda i: (i, 0))],
        out_specs=pl.BlockSpec((sel_rows, n_keys), lambda i: (i, 0)),
        out_shape=jax.ShapeDtypeStruct((nb * rows, n_keys), F32),
        compiler_params=_params("arbitrary"),
        name="sample_select",
    )(score.reshape(nb * rows, n_keys)).reshape(nb, rows, n_keys)

    n_seqs = math.gcd(nb, ATTEND_SEQS_PER_STEP)
    out = pl.pallas_call(
        functools.partial(_sample_attn_kernel, n_seqs=n_seqs, n_pages=n_pages, page=page,
                          t_new=t_new, d_head=d_head),
        grid_spec=pltpu.PrefetchScalarGridSpec(
            num_scalar_prefetch=1,
            grid=(nb // n_seqs,),
            in_specs=[seqs_spec(attn_w, n_seqs), seqs_spec(kv_w, n_seqs), seqs_spec(kv_w, n_seqs),
                      seqs_spec(n_keys, n_seqs)]
                     + seq_page_specs(kv_w, n_seqs) + seq_page_specs(kv_w, n_seqs),
            out_specs=seqs_spec(attn_w, n_seqs),
            scratch_shapes=[pltpu.VMEM((n_seqs, kv_w, past), BF16),
                            pltpu.VMEM((n_seqs, kv_w, past), BF16)],
        ),
        out_shape=jax.ShapeDtypeStruct((nb, rows, attn_w), BF16),
        compiler_params=_params("arbitrary"),
        name="sample_attention",
    )(pt_flat, q8, kn, vn, bias, *([ck] * (n_seqs * n_pages)), *([cv] * (n_seqs * n_pages)))
    return out[:, :t_new].reshape(nb * t_new, attn_w)


def _group_major_heads(a, d_head):
    n = a.shape[1]
    a = a.reshape(N_KV_HEADS, KV_GROUP, d_head, n)
    return jnp.swapaxes(a, 0, 1).reshape(N_HEADS * d_head, n)


def _s5_step(xr, xi, ar, ai, bu, half):
    br, bi = bu[:, :half], bu[:, half:]
    if xr is None:
        return br, bi
    return ar * xr - ai * xi + br, ar * xi + ai * xr + bi


def _s5_inputs(u_ref, b_ref, j):
    tr, steps, _ = u_ref.shape
    x = jnp.transpose(u_ref[:, :, j * LANES:(j + 1) * LANES], (1, 0, 2))
    us = [x[s] for s in range(steps)]
    bu = jnp.dot(jnp.concatenate(us, axis=0).astype(BF16), b_ref[j], preferred_element_type=F32)
    return us, [bu[s * tr:(s + 1) * tr] for s in range(steps)]


def _s5_local_kernel(u_ref, b_ref, ar_ref, ai_ref, z_ref, *, n_tiles):
    sw = b_ref.shape[2]
    half = sw // 2
    for j in range(n_tiles):
        ar, ai = ar_ref[j:j + 1, :], ai_ref[j:j + 1, :]
        xr = xi = None
        for bu in _s5_inputs(u_ref, b_ref, j)[1]:
            xr, xi = _s5_step(xr, xi, ar, ai, bu, half)
        z_ref[:, j * sw:j * sw + half] = xr
        z_ref[:, j * sw + half:(j + 1) * sw] = xi


def _s5_carry_kernel(z_ref, ar_ref, ai_ref, xc_ref, xf_ref, *, batch, n_chunks):
    tiles, _, half = ar_ref.shape
    sw = 2 * half

    def flat(states):
        return jnp.concatenate([v for pair in states for v in pair], axis=1)

    def body(c, carry):
        new = []
        for b in range(batch):
            row = b * n_chunks + c
            xc_ref[pl.ds(row, 1), :] = flat(carry[b])
            z = z_ref[pl.ds(row, 1), :]
            nxt = []
            for t, (xr, xi) in enumerate(carry[b]):
                ar, ai = ar_ref[t], ai_ref[t]
                zr, zi = z[:, t * sw:t * sw + half], z[:, t * sw + half:(t + 1) * sw]
                nxt.append((ar * xr - ai * xi + zr, ar * xi + ai * xr + zi))
            new.append(tuple(nxt))
        return tuple(new)

    zero = jnp.zeros((1, half), F32)
    init = tuple(tuple((zero, zero) for _ in range(tiles)) for _ in range(batch))
    final = lax.fori_loop(0, n_chunks, body, init, unroll=math.gcd(n_chunks, 4))
    for b in range(batch):
        xf_ref[b:b + 1, :] = flat(final[b])


def _s5_tile_outputs(u_ref, y_ref, b_ref, c_ref, ar, ai, d, xr, xi, j):
    tr, steps, _ = u_ref.shape
    half = b_ref.shape[2] // 2
    us, bus = _s5_inputs(u_ref, b_ref, j)
    xs = []
    for bu in bus:
        xr, xi = _s5_step(xr, xi, ar, ai, bu, half)
        xs.append(jnp.concatenate([xr, xi], axis=1).astype(BF16))
    y = jnp.dot(jnp.concatenate(xs, axis=0), c_ref[j], preferred_element_type=F32)
    ys = jnp.stack([y[s * tr:(s + 1) * tr] + d * us[s] for s in range(steps)], axis=0)
    y_ref[:, :, j * LANES:(j + 1) * LANES] = jnp.transpose(ys, (1, 0, 2))
    return xr, xi


def _s5_out_kernel(u_ref, x0_ref, b_ref, c_ref, ar_ref, ai_ref, d_ref, y_ref, *, n_tiles):
    sw = b_ref.shape[2]
    half = sw // 2
    for j in range(n_tiles):
        _s5_tile_outputs(u_ref, y_ref, b_ref, c_ref, ar_ref[j:j + 1, :], ai_ref[j:j + 1, :],
                         d_ref[:, j * LANES:(j + 1) * LANES],
                         x0_ref[:, j * sw:j * sw + half], x0_ref[:, j * sw + half:(j + 1) * sw], j)


def _s5_sample_kernel(u_ref, re_ref, im_ref, b_ref, c_ref, ar_ref, ai_ref, d_ref,
                      y_ref, ore_ref, oim_ref, *, n_tiles):
    half = b_ref.shape[2] // 2
    for j in range(n_tiles):
        rows = slice(j * half, (j + 1) * half)
        xr, xi = _s5_tile_outputs(u_ref, y_ref, b_ref, c_ref, ar_ref[j:j + 1, :],
                                  ai_ref[j:j + 1, :], d_ref[:, j * LANES:(j + 1) * LANES],
                                  re_ref[rows, :].T, im_ref[rows, :].T, j)
        ore_ref[rows, :] = xr.T
        oim_ref[rows, :] = xi.T


def _s5_params(a_re, a_im, log_step, b_re, b_im, c_re, c_im, chunk_steps):
    g, p = a_re.shape
    nt = g // GROUPS_PER_TILE
    a_re, a_im = a_re.astype(F32), a_im.astype(F32)
    step = jnp.exp(log_step.astype(F32))[:, None]

    def discretise(n_steps):
        mag = jnp.exp(a_re * step * n_steps)
        return mag * jnp.cos(a_im * step * n_steps), mag * jnp.sin(a_im * step * n_steps)

    lr, li = discretise(1.0)
    cr, ci = discretise(float(chunk_steps))
    den = a_re * a_re + a_im * a_im
    fr = ((lr - 1.0) * a_re + li * a_im) / den
    fi = (li * a_re - (lr - 1.0) * a_im) / den
    b_re, b_im = b_re.astype(F32), b_im.astype(F32)
    bb_re = fr[..., None] * b_re - fi[..., None] * b_im
    bb_im = fr[..., None] * b_im + fi[..., None] * b_re
    eye = jnp.eye(GROUPS_PER_TILE, dtype=BF16)

    bb = jnp.stack([bb_re, bb_im], axis=0).reshape(2, nt, GROUPS_PER_TILE, p, S5_GROUP)
    bb = jnp.transpose(bb, (1, 2, 4, 0, 3)).astype(BF16)
    b_blk = bb[:, :, :, :, None, :] * eye[None, :, None, None, :, None]
    b_blk = b_blk.reshape(nt, LANES, 2 * GROUPS_PER_TILE * p)

    cc = jnp.stack([c_re.astype(F32), -c_im.astype(F32)], axis=0)
    cc = cc.reshape(2, nt, GROUPS_PER_TILE, S5_GROUP, p)
    cc = jnp.transpose(cc, (1, 0, 2, 4, 3)).astype(BF16)
    c_blk = cc[:, :, :, :, None, :] * eye[None, None, :, None, :, None]
    c_blk = c_blk.reshape(nt, 2 * GROUPS_PER_TILE * p, LANES)

    def tiles(z):
        return z.reshape(nt, GROUPS_PER_TILE * p)

    return b_blk, c_blk, tiles(lr), tiles(li), tiles(cr), tiles(ci)


def _tiles_to_state(x, g, p):
    nb = x.shape[0]
    nt = g // GROUPS_PER_TILE
    st = x.reshape(nb, nt, 2, GROUPS_PER_TILE, p)
    return st[:, :, 0].reshape(nb, g, p), st[:, :, 1].reshape(nb, g, p)


def _s5_weight_specs(prm, d2):
    return [pl.BlockSpec(a.shape, lambda i, nd=a.ndim: (0,) * nd) for a in (*prm, d2)]


def _s5_outputs(u3, x0, prm, d_skip, tr):
    r, steps, width = u3.shape
    nt, _, sw = prm[0].shape
    d2 = d_skip.reshape(1, width).astype(F32)
    tokens = pl.BlockSpec((tr, steps, width), lambda i: (i, 0, 0))
    return pl.pallas_call(
        functools.partial(_s5_out_kernel, n_tiles=nt),
        grid=(r // tr,),
        in_specs=[tokens, pl.BlockSpec((tr, nt * sw), lambda i: (i, 0))]
                 + _s5_weight_specs(prm, d2),
        out_specs=tokens,
        out_shape=jax.ShapeDtypeStruct(u3.shape, F32),
        compiler_params=_params("arbitrary"),
        name="s5_outputs",
    )(u3, x0, *prm, d2)


def _s5_prompt(u, batch, seq, prm_all, d_skip, g, p):
    b_blk, c_blk, ar, ai, ar_c, ai_c = prm_all
    steps = STEP_CHUNK
    n_chunks = seq // steps
    r = batch * n_chunks
    width = u.shape[1]
    u3 = u.reshape(r, steps, width)
    nt, _, sw = b_blk.shape
    tr = 128
    z = pl.pallas_call(
        functools.partial(_s5_local_kernel, n_tiles=nt),
        grid=(r // tr,),
        in_specs=[pl.BlockSpec((tr, steps, width), lambda i: (i, 0, 0)),
                  pl.BlockSpec(b_blk.shape, lambda i: (0, 0, 0)),
                  pl.BlockSpec(ar.shape, lambda i: (0, 0)),
                  pl.BlockSpec(ai.shape, lambda i: (0, 0))],
        out_specs=pl.BlockSpec((tr, nt * sw), lambda i: (i, 0)),
        out_shape=jax.ShapeDtypeStruct((r, nt * sw), F32),
        compiler_params=_params("arbitrary"),
        name="s5_chunk_states",
    )(u3, b_blk, ar, ai)
    half = sw // 2
    tiles = math.gcd(nt, CARRY_TILES_PER_STEP)
    xc, xf = pl.pallas_call(
        functools.partial(_s5_carry_kernel, batch=batch, n_chunks=n_chunks),
        grid=(nt // tiles,),
        in_specs=[pl.BlockSpec((r, tiles * sw), lambda j: (0, j)),
                  pl.BlockSpec((tiles, 1, half), lambda j: (j, 0, 0)),
                  pl.BlockSpec((tiles, 1, half), lambda j: (j, 0, 0))],
        out_specs=[pl.BlockSpec((r, tiles * sw), lambda j: (0, j)),
                   pl.BlockSpec((batch, tiles * sw), lambda j: (0, j))],
        out_shape=[jax.ShapeDtypeStruct((r, nt * sw), F32),
                   jax.ShapeDtypeStruct((batch, nt * sw), F32)],
        compiler_params=_params("arbitrary"),
        name="s5_carry",
    )(z, ar_c.reshape(nt, 1, half), ai_c.reshape(nt, 1, half))
    y3 = _s5_outputs(u3, xc, (b_blk, c_blk, ar, ai), d_skip, tr)
    s_re, s_im = _tiles_to_state(xf, g, p)
    return y3.reshape(batch * seq, width), s_re, s_im


def _s5_sample(u, nb, t_new, prm_all, d_skip, s_re, s_im):
    prm = prm_all[:4]
    g, p = s_re.shape[1], s_re.shape[2]
    width = u.shape[1]
    nt = prm[0].shape[0]
    d2 = d_skip.reshape(1, width).astype(F32)
    state_t = lambda a: jnp.transpose(a.astype(F32), (1, 2, 0)).reshape(g * p, nb)
    tokens = pl.BlockSpec((nb, t_new, width), lambda i: (0, 0, 0))
    state = pl.BlockSpec((g * p, nb), lambda i: (0, 0))
    y3, n_re, n_im = pl.pallas_call(
        functools.partial(_s5_sample_kernel, n_tiles=nt),
        grid=(1,),
        in_specs=[tokens, state, state] + _s5_weight_specs(prm, d2),
        out_specs=[tokens, state, state],
        out_shape=[jax.ShapeDtypeStruct((nb, t_new, width), F32)]
                  + [jax.ShapeDtypeStruct((g * p, nb), F32)] * 2,
        compiler_params=_params("arbitrary"),
        name="s5_sample",
    )(u.reshape(nb, t_new, width), state_t(s_re), state_t(s_im), *prm, d2)
    back = lambda a: jnp.transpose(a.reshape(g, p, nb), (2, 0, 1))
    return y3.reshape(nb * t_new, width), back(n_re), back(n_im)


def _merge_kernel(attn_ref, y_ref, ga_ref, gb_ref, wa_ref, wg_ref, o_ref):
    d = o_ref.shape[1]
    y_a = jnp.dot(attn_ref[...], wa_ref[...], preferred_element_type=F32)
    z = _gelu_tanh(y_ref[...]).astype(BF16)
    glu = jnp.dot(z, wg_ref[...], preferred_element_type=F32)
    y_b = glu[:, :d] * _sigmoid(glu[:, d:])
    o_ref[...] = (_sigmoid(ga_ref[...]) * y_a + _sigmoid(gb_ref[...]) * y_b).astype(o_ref.dtype)


def _merge(attn, y, ga, gb, w_attn, w_glu, tm):
    m, d = ga.shape
    row = lambda w: pl.BlockSpec((tm, w), lambda i: (i, 0))
    return pl.pallas_call(
        _merge_kernel,
        grid=(m // tm,),
        in_specs=[row(attn.shape[1]), row(y.shape[1]), row(d), row(d),
                  pl.BlockSpec(w_attn.shape, lambda i: (0, 0)),
                  pl.BlockSpec(w_glu.shape, lambda i: (0, 0))],
        out_specs=row(d),
        out_shape=jax.ShapeDtypeStruct((m, d), BF16),
        compiler_params=_params("arbitrary"),
        name="merge",
    )(attn, y, ga, gb, w_attn, w_glu)


def _out_proj_kernel(m_ref, x_ref, w_ref, g_ref, x1_ref, h_ref):
    x1 = x_ref[...] + jnp.dot(m_ref[...], w_ref[...], preferred_element_type=F32)
    x1_ref[...] = x1
    ms = jnp.mean(x1 * x1, axis=-1, keepdims=True)
    h_ref[...] = (x1 * lax.rsqrt(ms + EPS) * g_ref[...]).astype(h_ref.dtype)


def _out_proj(merged, x, w_out, g, tm):
    m, d = x.shape
    row = pl.BlockSpec((tm, d), lambda i: (i, 0))
    return pl.pallas_call(
        _out_proj_kernel,
        grid=(m // tm,),
        in_specs=[row, row, pl.BlockSpec(w_out.shape, lambda i: (0, 0)),
                  pl.BlockSpec((1, d), lambda i: (0, 0))],
        out_specs=[row, row],
        out_shape=[jax.ShapeDtypeStruct((m, d), F32), jax.ShapeDtypeStruct((m, d), BF16)],
        compiler_params=_params("arbitrary"),
        name="out_proj",
    )(merged, x, w_out, g.reshape(1, d))


FF_SUB = 1


def _shift_rows(up, prev, k):
    body = pltpu.roll(up, k, axis=0)
    row = lax.broadcasted_iota(jnp.int32, prev.shape, 0)
    head = jnp.where(row < k, pltpu.roll(prev, k, axis=0), body[:SUBLANES])
    return jnp.concatenate([head, body[SUBLANES:]], axis=0)


def _conv_taps(up, u1, u2, cw, cb):
    return cb + cw[0:1, :] * u2 + cw[1:2, :] * u1 + cw[2:3, :] * up


def _ffn_finish(j, n_j, final_norm, acts, wd_ref, x1_ref, g_ref, y_ref):
    y_ref[...] += jnp.dot(jnp.concatenate(acts, axis=1), wd_ref[...], preferred_element_type=F32)

    @pl.when(j == n_j - 1)
    def _():
        x2 = x1_ref[...] + y_ref[...]
        if final_norm:
            ms = jnp.mean(x2 * x2, axis=-1, keepdims=True)
            x2 = x2 * lax.rsqrt(ms + EPS) * g_ref[...]
        y_ref[...] = x2


def _ffn_prompt_kernel(h_ref, wg_ref, wv_ref, cwg_ref, cwv_ref, cbg_ref, cbv_ref, wd_ref, x1_ref,
                       g_ref, y_ref, convg_ref, convv_ref, carry_ref,
                       *, tm, tiles_per_seq, n_j, final_norm):
    i, j = pl.program_id(0), pl.program_id(1)

    @pl.when(i % tiles_per_seq == 0)
    def _():
        carry_ref[j] = jnp.zeros(carry_ref.shape[1:], F32)

    @pl.when(j == 0)
    def _():
        y_ref[...] = jnp.zeros(y_ref.shape, F32)

    h = h_ref[...]
    prev = carry_ref[j]
    sub = FF_TILE // FF_SUB
    acts, tails = [], ([], [])
    for s in range(FF_SUB):
        cs = slice(s * sub, (s + 1) * sub)
        mixed = []
        for half, (w_ref, cw_ref, cb_ref) in enumerate(((wg_ref, cwg_ref, cbg_ref),
                                                        (wv_ref, cwv_ref, cbv_ref))):
            up = jnp.dot(h, w_ref[:, cs], preferred_element_type=F32)
            p8 = prev[:, half * FF_TILE + s * sub:half * FF_TILE + (s + 1) * sub]
            mixed.append(_conv_taps(up, _shift_rows(up, p8, 1), _shift_rows(up, p8, 2),
                                    cw_ref[:, cs], cb_ref[:, cs]))
            tails[half].append(up[tm - SUBLANES:])
        gate, val = mixed
        acts.append((gate * _sigmoid(gate) * val).astype(BF16))
    tail_g = jnp.concatenate(tails[0], axis=1)
    tail_v = jnp.concatenate(tails[1], axis=1)
    carry_ref[j] = jnp.concatenate([tail_g, tail_v], axis=1)
    convg_ref[...] = tail_g
    convv_ref[...] = tail_v
    _ffn_finish(j, n_j, final_norm, acts, wd_ref, x1_ref, g_ref, y_ref)


def _ffn_sample_kernel(h_ref, wg_ref, wv_ref, cwg_ref, cwv_ref, cbg_ref, cbv_ref, wd_ref, x1_ref,
                       g_ref, stg_ref, stv_ref, y_ref, convg_ref, convv_ref,
                       *, nb, t_new, n_j, final_norm):
    j = pl.program_id(0)

    @pl.when(j == 0)
    def _():
        y_ref[...] = jnp.zeros(y_ref.shape, F32)

    h = h_ref[...]
    sub = FF_TILE // FF_SUB
    acts = []
    for s in range(FF_SUB):
        cs = slice(s * sub, (s + 1) * sub)
        mixed = []
        for w_ref, cw_ref, cb_ref, st_ref, conv_ref in (
                (wg_ref, cwg_ref, cbg_ref, stg_ref, convg_ref),
                (wv_ref, cwv_ref, cbv_ref, stv_ref, convv_ref)):
            up = jnp.dot(h, w_ref[:, cs], preferred_element_type=F32)
            s0, s1 = st_ref[:, 0, cs], st_ref[:, 1, cs]
            u1 = jnp.concatenate([s1, up[:(t_new - 1) * nb]], axis=0)
            u2 = jnp.concatenate([s0, s1, up[:(t_new - 2) * nb]], axis=0)
            mixed.append(_conv_taps(up, u1, u2, cw_ref[:, cs], cb_ref[:, cs]))
            conv_ref[:, 0, cs] = up[(t_new - 2) * nb:(t_new - 1) * nb]
            conv_ref[:, 1, cs] = up[(t_new - 1) * nb:]
        gate, val = mixed
        acts.append((gate * _sigmoid(gate) * val).astype(BF16))
    _ffn_finish(j, n_j, final_norm, acts, wd_ref, x1_ref, g_ref, y_ref)


def _ffn_weight_specs(d, n_j, ix):
    gate = lambda rows: pl.BlockSpec((rows, FF_TILE), ix(lambda j: (0, j)))
    val = lambda rows: pl.BlockSpec((rows, FF_TILE), ix(lambda j: (0, n_j + j)))
    return [gate(d), val(d), gate(3), val(3), gate(1), val(1),
            pl.BlockSpec((FF_TILE, d), ix(lambda j: (j, 0)))]


def _ffn_prompt(h2, x1, wu, cw, cb, wd, g, batch, seq, tm, final_norm):
    m, d = x1.shape
    d_ff = wd.shape[0]
    n_j = d_ff // FF_TILE
    tps = seq // tm
    kern = functools.partial(_ffn_prompt_kernel, tm=tm, tiles_per_seq=tps, n_j=n_j,
                             final_norm=final_norm)
    ix = lambda f: (lambda i, j: f(j))
    conv_spec = pl.BlockSpec((None, SUBLANES, FF_TILE), lambda i, j: (i, 0, j))
    conv_shape = jax.ShapeDtypeStruct((m // tm, SUBLANES, d_ff), F32)
    y, cg, cv = pl.pallas_call(
        kern,
        grid=(m // tm, n_j),
        in_specs=[pl.BlockSpec((tm, d), lambda i, j: (i, 0))] + _ffn_weight_specs(d, n_j, ix)
                 + [pl.BlockSpec((tm, d), lambda i, j: (i, 0)),
                    pl.BlockSpec((1, d), lambda i, j: (0, 0))],
        out_specs=[pl.BlockSpec((tm, d), lambda i, j: (i, 0)), conv_spec, conv_spec],
        out_shape=[jax.ShapeDtypeStruct((m, d), F32), conv_shape, conv_shape],
        scratch_shapes=[pltpu.VMEM((n_j, SUBLANES, 2 * FF_TILE), F32)],
        compiler_params=_params("arbitrary", "arbitrary"),
        name="ffn_prompt",
    )(h2, wu, wu, cw, cw, cb, cb, wd, x1, g.reshape(1, d))
    return y, jnp.concatenate([cg, cv], axis=-1)[tps - 1::tps, SUBLANES - 2:]


def _ffn_sample(h2, x1, wu, cw, cb, wd, g, state, nb, t_new, final_norm):
    m, d = x1.shape
    d_ff = wd.shape[0]
    n_j = d_ff // FF_TILE
    kern = functools.partial(_ffn_sample_kernel, nb=nb, t_new=t_new, n_j=n_j,
                             final_norm=final_norm)
    ix = lambda f: f
    st_gate = pl.BlockSpec((nb, 2, FF_TILE), lambda j: (0, 0, j))
    st_val = pl.BlockSpec((nb, 2, FF_TILE), lambda j: (0, 0, n_j + j))
    conv_spec = pl.BlockSpec((nb, 2, FF_TILE), lambda j: (0, 0, j))
    y, cg, cv = pl.pallas_call(
        kern,
        grid=(n_j,),
        in_specs=[pl.BlockSpec((m, d), lambda j: (0, 0))] + _ffn_weight_specs(d, n_j, ix)
                 + [pl.BlockSpec((m, d), lambda j: (0, 0)),
                    pl.BlockSpec((1, d), lambda j: (0, 0)),
                    st_gate, st_val],
        out_specs=[pl.BlockSpec((m, d), lambda j: (0, 0)), conv_spec, conv_spec],
        out_shape=[jax.ShapeDtypeStruct((m, d), F32),
                   jax.ShapeDtypeStruct((nb, 2, d_ff), F32),
                   jax.ShapeDtypeStruct((nb, 2, d_ff), F32)],
        compiler_params=_params("arbitrary"),
        name="ffn_sample",
    )(h2, wu, wu, cw, cw, cb, cb, wd, x1, g.reshape(1, d), state, state)
    return y, jnp.concatenate([cg, cv], axis=-1)


def kernel(x_prompt, x_sample, cache_k, cache_v, cache_kidx, state_s5_re, state_s5_im, state_ffn_conv, page_table, norm_mix, w_in, w_attn_proj, s5_a_re, s5_a_im, s5_log_step, s5_b_re, s5_b_im, s5_c_re, s5_c_im, s5_d, w_glu, w_out, norm_ffn, w_up, conv_w, conv_b, w_down, norm_final):
    depth = w_in.shape[0]
    batch, seq, d_model = x_prompt.shape
    nb, t_new, _ = x_sample.shape
    d_head = cache_k.shape[-1]
    d_idx = cache_kidx.shape[-1]
    attn_w = N_HEADS * d_head
    kv_w = N_KV_HEADS * d_head
    qi_w = IDX_HEADS * d_idx
    groups, p_state = s5_a_re.shape[1], s5_a_re.shape[2]
    s5_w = groups * S5_GROUP

    xp = x_prompt.reshape(batch * seq, d_model)
    xs = x_sample.reshape(nb * t_new, d_model)
    tm_p = min(512, batch * seq)
    tm_s = nb * t_new

    outs = {name: [] for name in ("kp", "vp", "kip", "srp", "sip", "cp",
                                  "ks", "vs", "kis", "srs", "sis", "cs")}
    for l in range(depth):
        w_t = jnp.swapaxes(w_in[l], 0, 1).astype(BF16)
        o = 0
        seg = {}
        for name, width in (("q", attn_w), ("k", kv_w), ("v", kv_w), ("qi", qi_w), ("ki", d_idx),
                            ("wi", IDX_HEADS), ("u", s5_w), ("ga", d_model), ("gb", d_model)):
            seg[name] = w_t[o:o + width]
            o += width
        w_q_grouped = _group_major_heads(seg["q"], d_head)
        w_u, w_ga, w_gb = seg["u"], seg["ga"], seg["gb"]
        pad = jnp.zeros((LANES - d_idx - IDX_HEADS, d_model), BF16)
        w_small = jnp.concatenate([seg["qi"], seg["k"], seg["v"], seg["ki"], seg["wi"], pad], axis=0)
        w_qiw = jnp.concatenate([seg["qi"], seg["ki"], seg["wi"], pad], axis=0)
        w_kv_t = jnp.concatenate([seg["k"], seg["v"], seg["ki"]], axis=0)
        w_attn_grouped = _group_major_heads(w_attn_proj[l].astype(BF16), d_head)
        w_g = w_glu[l].astype(BF16)
        w_o = w_out[l].astype(BF16)
        wu = w_up[l].astype(BF16)
        cw = conv_w[l].astype(F32)
        cb = conv_b[l].astype(F32).reshape(1, -1)
        wd = w_down[l].astype(BF16)
        s5p = _s5_params(s5_a_re[l], s5_a_im[l], s5_log_step[l], s5_b_re[l], s5_b_im[l],
                         s5_c_re[l], s5_c_im[l], STEP_CHUNK)
        last = l == depth - 1

        def project(x, tm, ws):
            (q, u, small), h = _norm_matmul(x, norm_mix[l], [w_q_grouped, w_u, ws], tm,
                                            "norm_proj_q_u_small")
            ga, gb = _matmul(h, [w_ga, w_gb], tm, F32, "proj_gates")
            return h, q, u, ga, gb, small

        h, q, u, ga, gb, qiw = project(xp, tm_p, w_qiw)
        k_t, v_t, ki_t, kb_t, vb_t, kib_t = _kv_transposed(h, w_kv_t, batch, seq, kv_w, d_idx,
                                                            min(512, seq))
        k_sel = min(TOPK_MAX, seq // 4)
        attn = _prompt_attention(q, qiw, kib_t, kb_t, vb_t, batch, seq, d_head, d_idx, k_sel,
                                 min(256, seq))
        y5, srp, sip = _s5_prompt(u, batch, seq, s5p, s5_d[l], groups, p_state)
        merged = _merge(attn, y5, ga, gb, w_attn_grouped, w_g, min(256, batch * seq))
        x1, h2 = _out_proj(merged, xp, w_o, norm_ffn[l], tm_p)
        xp, conv_p = _ffn_prompt(h2, x1, wu, cw, cb, wd, norm_final, batch, seq, tm_p, last)

        def heads_last(a_t):
            return jnp.transpose(a_t.reshape(batch, N_KV_HEADS, d_head, seq), (0, 3, 1, 2))

        outs["kp"].append(heads_last(k_t)); outs["vp"].append(heads_last(v_t))
        outs["kip"].append(jnp.swapaxes(ki_t, 1, 2))
        outs["srp"].append(srp); outs["sip"].append(sip); outs["cp"].append(conv_p)

        _, q, u, ga, gb, small = project(xs, tm_s, w_small)
        attn = _sample_attention(q, small, cache_k[l], cache_v[l], cache_kidx[l], page_table,
                                 t_new, d_head, d_idx)
        y5, srs, sis = _s5_sample(u, nb, t_new, s5p, s5_d[l], state_s5_re[l], state_s5_im[l])
        merged = _merge(attn, y5, ga, gb, w_attn_grouped, w_g, min(256, nb * t_new))
        x1, h2 = _out_proj(merged, xs, w_o, norm_ffn[l], tm_s)

        def time_major(a):
            return a.reshape(nb, t_new, -1).swapaxes(0, 1).reshape(nb * t_new, -1)

        y_tm, conv_s = _ffn_sample(time_major(h2), time_major(x1), wu, cw, cb, wd, norm_final,
                                   state_ffn_conv[l].astype(F32), nb, t_new, last)
        xs = y_tm.reshape(t_new, nb, d_model).swapaxes(0, 1).reshape(nb * t_new, d_model)
        k_new = small[:, qi_w:qi_w + kv_w].reshape(nb, t_new, N_KV_HEADS, d_head)
        v_new = small[:, qi_w + kv_w:qi_w + 2 * kv_w].reshape(nb, t_new, N_KV_HEADS, d_head)
        ki_new = small[:, qi_w + 2 * kv_w:qi_w + 2 * kv_w + d_idx].reshape(nb, t_new, d_idx)
        outs["ks"].append(k_new); outs["vs"].append(v_new); outs["kis"].append(ki_new)
        outs["srs"].append(srs); outs["sis"].append(sis); outs["cs"].append(conv_s)

    stk = {name: jnp.stack(v) for name, v in outs.items()}
    return (xp.reshape(batch, seq, d_model), xs.reshape(nb, t_new, d_model),
            stk["kp"], stk["vp"], stk["kip"], stk["srp"], stk["sip"], stk["cp"],
            stk["ks"], stk["vs"], stk["kis"], stk["srs"], stk["sis"], stk["cs"])
```

```python
import functools
import math

import jax
import jax.numpy as jnp
from jax import lax
from jax.experimental import pallas as pl
from jax.experimental.pallas import tpu as pltpu

F32 = jnp.float32
BF16 = jnp.bfloat16

EPS = 1e-6
LOG2E = math.log2(math.e)
TOPK_MAX = 256
N_HEADS = 16
N_KV_HEADS = 4
KV_GROUP = N_HEADS // N_KV_HEADS
IDX_HEADS = 8
S5_GROUP = 16
STEP_CHUNK = 8
CARRY_TILES_PER_STEP = 2
LANES = 128
SUBLANES = 8
GROUPS_PER_TILE = LANES // S5_GROUP
FF_TILE = 512
VMEM_LIMIT = 48 * 1024 * 1024
NEG_INF = float("-inf")
INT_MIN = -2 ** 31


def _params(*sem):
    return pltpu.CompilerParams(dimension_semantics=sem, vmem_limit_bytes=VMEM_LIMIT)


def _sigmoid(x):
    return 0.5 * jnp.tanh(0.5 * x) + 0.5


def _gelu_tanh(x):
    c = math.sqrt(2.0 / math.pi)
    return 0.5 * x * (1.0 + jnp.tanh(c * (x + 0.044715 * (x * x * x))))


def _dot_nt(a, b):
    return lax.dot_general(a, b, (((1,), (1,)), ((), ())), preferred_element_type=F32)


def _resident(w):
    return pl.BlockSpec(w.shape, lambda i: (0, 0), pipeline_mode=pl.Buffered(1))


def _mm_kernel(h_ref, *refs):
    n = len(refs) // 2
    h = h_ref[...]
    for w_ref, o_ref in zip(refs[:n], refs[n:]):
        o_ref[...] = _dot_nt(h, w_ref[...]).astype(o_ref.dtype)


def _matmul(h, weights_t, tm, out_dtype, name):
    m, k = h.shape
    return pl.pallas_call(
        _mm_kernel,
        grid=(m // tm,),
        in_specs=[pl.BlockSpec((tm, k), lambda i: (i, 0))]
                 + [_resident(w) for w in weights_t],
        out_specs=[pl.BlockSpec((tm, w.shape[0]), lambda i: (i, 0)) for w in weights_t],
        out_shape=[jax.ShapeDtypeStruct((m, w.shape[0]), out_dtype) for w in weights_t],
        compiler_params=_params("arbitrary"),
        name=name,
    )(h, *weights_t)


def _norm_mm_kernel(x_ref, g_ref, *refs):
    n = (len(refs) - 1) // 2
    x = x_ref[...]
    ms = jnp.mean(x * x, axis=-1, keepdims=True)
    h = (x * lax.rsqrt(ms + EPS) * g_ref[...]).astype(BF16)
    refs[2 * n][...] = h
    for w_ref, o_ref in zip(refs[:n], refs[n:2 * n]):
        o_ref[...] = _dot_nt(h, w_ref[...]).astype(o_ref.dtype)


def _norm_matmul(x, g, weights_t, tm, name):
    m, k = x.shape
    *outs, h = pl.pallas_call(
        _norm_mm_kernel,
        grid=(m // tm,),
        in_specs=[pl.BlockSpec((tm, k), lambda i: (i, 0)), pl.BlockSpec((1, k), lambda i: (0, 0))]
                 + [_resident(w) for w in weights_t],
        out_specs=[pl.BlockSpec((tm, w.shape[0]), lambda i: (i, 0)) for w in weights_t]
                  + [pl.BlockSpec((tm, k), lambda i: (i, 0))],
        out_shape=[jax.ShapeDtypeStruct((m, w.shape[0]), F32) for w in weights_t]
                  + [jax.ShapeDtypeStruct((m, k), BF16)],
        compiler_params=_params("arbitrary"),
        name=name,
    )(x, g.reshape(1, k), *weights_t)
    return outs, h


def _kv_t_kernel(w_ref, h_ref, k_ref, v_ref, ki_ref, kb_ref, vb_ref, kib_ref, *, kv_w):
    out = _dot_nt(w_ref[...], h_ref[...])
    for lo, hi, full_ref, half_ref in ((0, kv_w, k_ref, kb_ref), (kv_w, 2 * kv_w, v_ref, vb_ref),
                                       (2 * kv_w, out.shape[0], ki_ref, kib_ref)):
        full_ref[...] = out[lo:hi]
        half_ref[...] = out[lo:hi].astype(BF16)


def _kv_transposed(h, w_t, batch, seq, kv_w, d_idx, tn):
    d = h.shape[1]
    nt = seq // tn
    out = lambda rows: pl.BlockSpec((None, rows, tn), lambda b, i: (b, 0, i))
    shape = lambda rows, dt: jax.ShapeDtypeStruct((batch, rows, seq), dt)
    widths = (kv_w, kv_w, d_idx)
    return pl.pallas_call(
        functools.partial(_kv_t_kernel, kv_w=kv_w),
        grid=(batch, nt),
        in_specs=[pl.BlockSpec(w_t.shape, lambda b, i: (0, 0)),
                  pl.BlockSpec((tn, d), lambda b, i: (b * nt + i, 0))],
        out_specs=[out(w) for w in widths] * 2,
        out_shape=[shape(w, F32) for w in widths] + [shape(w, BF16) for w in widths],
        compiler_params=_params("arbitrary", "arbitrary"),
        name="proj_kv_transposed",
    )(w_t, h)


SEARCH_GROUP = 4


def _count(mask):
    return jnp.sum(jnp.where(mask, 1.0, 0.0), axis=1, keepdims=True)


def _topk_mask(score, col, k_sel, n_cols):
    if n_cols <= k_sel:
        return score > NEG_INF
    kf = float(k_sel)
    rows = score.shape[0]

    def as_float(key):
        bits = jnp.where(key < 0, key ^ jnp.int32(0x7FFFFFFF), key)
        return lax.bitcast_convert_type(bits, F32)

    finite = score > NEG_INF
    few = _count(finite) <= kf

    cnt0 = _count(score >= 0.0)
    cand0 = jnp.where(cnt0 >= kf, jnp.int32(0), jnp.int32(INT_MIN))
    cand0 = jnp.broadcast_to(cand0, (rows, 1)).astype(jnp.int32)
    cnt0 = jnp.where(cnt0 >= kf, cnt0, float(n_cols))

    def search_body(group, carry):
        cand, cnt = carry
        for g in range(SEARCH_GROUP):
            shift = jnp.int32(30) - (group * SEARCH_GROUP + g)
            bit = jnp.where(shift >= 0, lax.shift_left(jnp.int32(1), jnp.maximum(shift, 0)), 0)
            trial = cand + bit
            c_trial = _count(score >= as_float(trial))
            ok = c_trial >= kf
            cand = jnp.where(ok, trial, cand)
            cnt = jnp.where(ok, c_trial, cnt)
        return cand, cnt

    cand, cnt = lax.fori_loop(0, -(-31 // SEARCH_GROUP), search_body, (cand0, cnt0))
    thr = as_float(cand)

    above = score > thr
    ties = score == thr
    need = kf - _count(above)
    surplus = jnp.where(few, 0.0, cnt - kf)
    n_bits = max(1, (n_cols - 1).bit_length())

    def index_step(it, m):
        trial = m + lax.shift_left(jnp.int32(1), jnp.int32(n_bits - 1) - it)
        taken = _count(jnp.logical_and(ties, col < trial))
        return jnp.where(taken <= need - 1.0, trial, m)

    def lowest_indices(_):
        return lax.fori_loop(0, n_bits, index_step, jnp.zeros((rows, 1), jnp.int32))

    def all_ties(_):
        return jnp.full((rows, 1), n_cols, jnp.int32)

    last = lax.cond(jnp.max(surplus) > 0.0, lowest_indices, all_ties, None)
    take_tie = jnp.logical_and(jnp.logical_and(ties, col <= last), need >= 1.0)
    top = jnp.logical_or(above, take_tie)
    return jnp.logical_and(finite, jnp.logical_or(few, top))


def _index_weights(kw, d_idx):
    w_scale = (d_idx ** -0.5) * (IDX_HEADS ** -0.5)
    return [kw[:, d_idx + h:d_idx + h + 1] * w_scale for h in range(IDX_HEADS)]


def _head_rows(qi, d_idx):
    return jnp.concatenate([qi[:, h * d_idx:(h + 1) * d_idx] for h in range(IDX_HEADS)], axis=0)


def _weighted_relu_sum(lg, w_cols, rows):
    score = jnp.zeros((rows, lg.shape[1]), F32)
    for h in range(IDX_HEADS):
        score = score + jnp.maximum(lg[h * rows:(h + 1) * rows], 0.0) * w_cols[h]
    return score


HEADS_PER_DOT = 2
SHORT_PREFIX_BLOCKS = 2


PROMPT_SCORE_ELEMS = 512 * 1024


def _prompt_attn_kernel(q_ref, qi_ref, kw_ref, kit_ref, kt_ref, vt_ref, prev_ref, o_ref,
                        *, tq, n_keys, k_sel, d_head, d_idx, q_block, heads_per_dot):
    del prev_ref
    n_seqs = q_ref.shape[0]
    scores = []
    for b in range(n_seqs):
        qi = qi_ref[b]
        w_cols = _index_weights(kw_ref[b], d_idx)
        kit = kit_ref[b]
        score = jnp.zeros((tq, n_keys), F32)
        for h in range(IDX_HEADS):
            lg = jnp.dot(qi[:, h * d_idx:(h + 1) * d_idx].astype(BF16), kit,
                         preferred_element_type=F32)
            score = score + jnp.maximum(lg, 0.0) * w_cols[h]
        scores.append(score)
    score = jnp.concatenate(scores, axis=0)

    col = lax.broadcasted_iota(jnp.int32, score.shape, 1)
    row = lax.broadcasted_iota(jnp.int32, (n_seqs, tq, n_keys), 1).reshape(score.shape)
    score = jnp.where(col <= q_block * tq + row, score, NEG_INF)
    bias_all = jnp.where(_topk_mask(score, col, k_sel, n_keys), 0.0, NEG_INF)

    for b in range(n_seqs):
        bias = bias_all[b * tq:(b + 1) * tq]
        q = q_ref[b] * (d_head ** -0.5 * LOG2E)
        for n in range(N_KV_HEADS):
            kt = kt_ref[b, n * d_head:(n + 1) * d_head, :]
            vt = vt_ref[b, n * d_head:(n + 1) * d_head, :]
            for g0 in range(0, KV_GROUP, heads_per_dot):
                heads = [(g0 + i) * N_KV_HEADS + n for i in range(heads_per_dot)]
                qs = jnp.concatenate([q[:, h * d_head:(h + 1) * d_head] for h in heads], axis=0)
                s = jnp.dot(qs.astype(BF16), kt, preferred_element_type=F32)
                s = (s.reshape(heads_per_dot, tq, n_keys) + bias[None]).reshape(-1, n_keys)
                m = jnp.max(s, axis=1, keepdims=True)
                p = jnp.exp2(s - m)
                l = jnp.sum(p, axis=1, keepdims=True)
                o = _dot_nt(p.astype(BF16), vt) / l
                for i, h in enumerate(heads):
                    o_ref[b, :, h * d_head:(h + 1) * d_head] = (
                        o[i * tq:(i + 1) * tq].astype(o_ref.dtype))


def _prompt_attention(q, qiw, ki_t, k_t, v_t, batch, seq, d_head, d_idx, k_sel, tq):
    attn_w = q.shape[1]
    kv_w = N_KV_HEADS * d_head
    qi_w = IDX_HEADS * d_idx
    nq = seq // tq
    q3 = q.reshape(batch, seq, attn_w)
    qiw3 = qiw.reshape(batch, seq, qiw.shape[1])
    out = jnp.zeros((batch, seq, attn_w), BF16)
    for qb in range(nq):
        n_keys = (qb + 1) * tq
        n_seqs = max(d for d in range(1, batch + 1)
                     if batch % d == 0 and (d == 1 or d * tq * n_keys <= PROMPT_SCORE_ELEMS))
        row = lambda w, c, qb=qb, n_seqs=n_seqs: pl.BlockSpec((n_seqs, tq, w), lambda b: (b, qb, c))
        keys = lambda w, n_keys=n_keys, n_seqs=n_seqs: pl.BlockSpec((n_seqs, w, n_keys),
                                                                     lambda b: (b, 0, 0))
        heads_per_dot = HEADS_PER_DOT if n_keys <= SHORT_PREFIX_BLOCKS * tq else 1
        out = pl.pallas_call(
            functools.partial(_prompt_attn_kernel, tq=tq, n_keys=n_keys, k_sel=k_sel,
                              d_head=d_head, d_idx=d_idx, q_block=qb,
                              heads_per_dot=heads_per_dot),
            grid=(batch // n_seqs,),
            in_specs=[row(attn_w, 0), row(qi_w, 0), row(LANES, qi_w // LANES),
                      keys(d_idx), keys(kv_w), keys(kv_w),
                      pl.BlockSpec(memory_space=pl.ANY)],
            out_specs=row(attn_w, 0),
            out_shape=jax.ShapeDtypeStruct((batch, seq, attn_w), BF16),
            input_output_aliases={6: 0},
            compiler_params=_params("arbitrary"),
            name=f"prompt_attention_q{qb}",
        )(q3, qiw3, qiw3, ki_t, k_t, v_t, out)
    return out.reshape(batch * seq, attn_w)


def _lane_columns(cols, width):
    rows = cols[0].shape[0]
    lane = lax.broadcasted_iota(jnp.int32, (rows, width), 1)
    out = jnp.zeros((rows, width), F32)
    for j, c in enumerate(cols):
        out = jnp.where(lane == j, c, out)
    return out


def _sample_score_kernel(pt_ref, qi_ref, kw_ref, kin_ref, *rest, n_seqs, n_pages, page, t_new,
                         d_idx):
    del pt_ref
    s_ref = rest[n_seqs * n_pages]
    past = n_pages * page
    rows = SUBLANES
    pad_row = lax.broadcasted_iota(jnp.int32, (rows, page), 0) >= t_new
    first = lax.broadcasted_iota(jnp.int32, (rows, page), 1) == 0
    lane = lax.broadcasted_iota(jnp.int32, (rows, LANES), 1)
    t = lax.broadcasted_iota(jnp.int32, (rows, LANES), 0)
    causal_new = jnp.logical_and(lane <= t, t < t_new)
    for i in range(n_seqs):
        ipages = rest[i * n_pages:(i + 1) * n_pages]
        qi_rows = _head_rows(qi_ref[i], d_idx)
        qi_bf = qi_rows.astype(BF16)
        w_cols = _index_weights(kw_ref[i], d_idx)
        for p in range(n_pages):
            lg = jnp.dot(qi_bf, ipages[p][...].astype(BF16), preferred_element_type=F32)
            pad_val = jnp.where(first, 0.0, NEG_INF) if p == 0 else NEG_INF
            s_ref[i, :, p * page:(p + 1) * page] = jnp.where(
                pad_row, pad_val, _weighted_relu_sum(lg, w_cols, rows))
        kin = kin_ref[i]
        new_cols = []
        for j in range(t_new):
            lg = jnp.sum(qi_rows * kin[j:j + 1, :], axis=1, keepdims=True)
            new_cols.append(_weighted_relu_sum(lg, w_cols, rows))
        s_ref[i, :, past:] = jnp.where(causal_new, _lane_columns(new_cols, LANES), NEG_INF)


def _select_kernel(s_ref, b_ref, *, k_sel):
    score = s_ref[...]
    col = lax.broadcasted_iota(jnp.int32, score.shape, 1)
    keep = _topk_mask(score, col, k_sel, score.shape[1])
    b_ref[...] = jnp.where(keep, 0.0, NEG_INF)


SELECT_ROWS = 512
SCORE_SEQS_PER_STEP = 4
ATTEND_SEQS_PER_STEP = 2


def _sample_attn_kernel(pt_ref, q_ref, kn_ref, vn_ref, bias_ref, *rest,
                        n_seqs, n_pages, page, t_new, d_head):
    del pt_ref
    pages = rest[:2 * n_seqs * n_pages]
    o_ref, kt_ref, vt_ref = rest[2 * n_seqs * n_pages:]
    for i in range(n_seqs):
        kpages = pages[i * n_pages:(i + 1) * n_pages]
        vpages = pages[(n_seqs + i) * n_pages:(n_seqs + i + 1) * n_pages]
        o_ref[i] = _sample_attend(q_ref[i], kn_ref[i], vn_ref[i], bias_ref[i], kpages, vpages,
                                  kt_ref.at[i], vt_ref.at[i], page, t_new, d_head
                                  ).astype(o_ref.dtype)


def _sample_attend(q, kn, vn, bias, kpages, vpages, kt_ref, vt_ref, page, t_new, d_head):
    n_pages = len(kpages)
    past = n_pages * page
    rows = SUBLANES
    kv_w = N_KV_HEADS * d_head

    for p in range(n_pages):
        kt_ref[:, p * page:(p + 1) * page] = kpages[p][...].astype(BF16)
        vt_ref[:, p * page:(p + 1) * page] = vpages[p][...].astype(BF16)

    bias = jnp.concatenate([bias] * N_HEADS, axis=0)

    q = q * (d_head ** -0.5 * LOG2E)
    lane_head = lax.broadcasted_iota(jnp.int32, (rows, kv_w), 1) // d_head
    q_rows = jnp.concatenate(
        [jnp.where(lane_head == n, q[:, g * kv_w:(g + 1) * kv_w], 0.0)
         for g in range(KV_GROUP) for n in range(N_KV_HEADS)], axis=0)

    s_past = jnp.dot(q_rows.astype(BF16), kt_ref[...], preferred_element_type=F32)
    s_past = s_past + bias[:, :past]
    s_new = _lane_columns(
        [jnp.sum(q_rows * kn[j:j + 1, :], axis=1, keepdims=True) for j in range(t_new)], LANES)
    s_new = s_new + bias[:, past:]

    m = jnp.maximum(jnp.max(s_past, axis=1, keepdims=True), jnp.max(s_new, axis=1, keepdims=True))
    p_past = jnp.exp2(s_past - m)
    p_new = jnp.exp2(s_new - m)
    l = jnp.sum(p_past, axis=1, keepdims=True) + jnp.sum(p_new, axis=1, keepdims=True)
    o = _dot_nt(p_past.astype(BF16), vt_ref[...])
    for j in range(t_new):
        o = o + p_new[:, j:j + 1] * vn[j:j + 1, :]
    o = o / l

    chunks = []
    for g in range(KV_GROUP):
        acc = jnp.zeros((rows, kv_w), F32)
        for n in range(N_KV_HEADS):
            r0 = (g * N_KV_HEADS + n) * rows
            acc = acc + jnp.where(lane_head == n, o[r0:r0 + rows], 0.0)
        chunks.append(acc)
    return jnp.concatenate(chunks, axis=1)


def _seq_page_map(b, pt, *, i, p, n_seqs, n_pages):
    return (pt[(b * n_seqs + i) * n_pages + p], 0, 0)


def _sample_attention(q, small, cache_k, cache_v, cache_ki, page_table, t_new, d_head, d_idx):
    nb, n_pages = page_table.shape
    n_pool, page = cache_k.shape[0], cache_k.shape[1]
    attn_w = q.shape[1]
    kv_w = N_KV_HEADS * d_head
    qi_w = IDX_HEADS * d_idx
    rows = SUBLANES
    past = n_pages * page
    k_sel = min(TOPK_MAX, (past + t_new) // 4)

    def pad_rows(a):
        a = a.reshape(nb, t_new, a.shape[-1])
        return jnp.pad(a, ((0, 0), (0, rows - t_new), (0, 0)))

    q8 = pad_rows(q)
    qi8 = pad_rows(small[:, :qi_w])
    kw8 = pad_rows(small[:, qi_w + 2 * kv_w:])
    kn = pad_rows(small[:, qi_w:qi_w + kv_w])
    vn = pad_rows(small[:, qi_w + kv_w:qi_w + 2 * kv_w])
    kin = pad_rows(small[:, qi_w + 2 * kv_w:qi_w + 2 * kv_w + d_idx])

    ck = jnp.transpose(cache_k, (0, 2, 3, 1)).reshape(n_pool, kv_w, page)
    cv = jnp.transpose(cache_v, (0, 2, 3, 1)).reshape(n_pool, kv_w, page)
    ci = jnp.transpose(cache_ki, (0, 2, 1))

    pt_flat = page_table.reshape(-1)
    n_keys = past + LANES

    def seqs_spec(w, n_seqs):
        return pl.BlockSpec((n_seqs, rows, w), lambda b, pt: (b, 0, 0))

    def seq_page_specs(w, n_seqs):
        return [pl.BlockSpec((None, w, page),
                             functools.partial(_seq_page_map, i=i, p=p, n_seqs=n_seqs,
                                               n_pages=n_pages))
                for i in range(n_seqs) for p in range(n_pages)]

    n_seqs = math.gcd(nb, SCORE_SEQS_PER_STEP)
    score = pl.pallas_call(
        functools.partial(_sample_score_kernel, n_seqs=n_seqs, n_pages=n_pages, page=page,
                          t_new=t_new, d_idx=d_idx),
        grid_spec=pltpu.PrefetchScalarGridSpec(
            num_scalar_prefetch=1,
            grid=(nb // n_seqs,),
            in_specs=[seqs_spec(qi_w, n_seqs), seqs_spec(LANES, n_seqs), seqs_spec(d_idx, n_seqs)]
                     + seq_page_specs(d_idx, n_seqs),
            out_specs=seqs_spec(n_keys, n_seqs),
        ),
        out_shape=jax.ShapeDtypeStruct((nb, rows, n_keys), F32),
        compiler_params=_params("arbitrary"),
        name="sample_scores",
    )(pt_flat, qi8, kw8, kin, *([ci] * (n_seqs * n_pages)))

    sel_rows = math.gcd(nb * rows, SELECT_ROWS)
    bias = pl.pallas_call(
        functools.partial(_select_kernel, k_sel=k_sel),
        grid=(nb * rows // sel_rows,),
        in_specs=[pl.BlockSpec((sel_rows, n_keys), lambda i: (i, 0))],
        out_specs=pl.BlockSpec((sel_rows, n_keys), lambda i: (i, 0)),
        out_shape=jax.ShapeDtypeStruct((nb * rows, n_keys), F32),
        compiler_params=_params("arbitrary"),
        name="sample_select",
    )(score.reshape(nb * rows, n_keys)).reshape(nb, rows, n_keys)

    n_seqs = math.gcd(nb, ATTEND_SEQS_PER_STEP)
    out = pl.pallas_call(
        functools.partial(_sample_attn_kernel, n_seqs=n_seqs, n_pages=n_pages, page=page,
                          t_new=t_new, d_head=d_head),
        grid_spec=pltpu.PrefetchScalarGridSpec(
            num_scalar_prefetch=1,
            grid=(nb // n_seqs,),
            in_specs=[seqs_spec(attn_w, n_seqs), seqs_spec(kv_w, n_seqs), seqs_spec(kv_w, n_seqs),
                      seqs_spec(n_keys, n_seqs)]
                     + seq_page_specs(kv_w, n_seqs) + seq_page_specs(kv_w, n_seqs),
            out_specs=seqs_spec(attn_w, n_seqs),
            scratch_shapes=[pltpu.VMEM((n_seqs, kv_w, past), BF16),
                            pltpu.VMEM((n_seqs, kv_w, past), BF16)],
        ),
        out_shape=jax.ShapeDtypeStruct((nb, rows, attn_w), BF16),
        compiler_params=_params("arbitrary"),
        name="sample_attention",
    )(pt_flat, q8, kn, vn, bias, *([ck] * (n_seqs * n_pages)), *([cv] * (n_seqs * n_pages)))
    return out[:, :t_new].reshape(nb * t_new, attn_w)


def _group_major_heads(a, d_head):
    n = a.shape[1]
    a = a.reshape(N_KV_HEADS, KV_GROUP, d_head, n)
    return jnp.swapaxes(a, 0, 1).reshape(N_HEADS * d_head, n)


def _s5_step(xr, xi, ar, ai, bu, half):
    br, bi = bu[:, :half], bu[:, half:]
    if xr is None:
        return br, bi
    return ar * xr - ai * xi + br, ar * xi + ai * xr + bi


def _s5_inputs(u_ref, b_ref, j):
    tr, steps, _ = u_ref.shape
    x = jnp.transpose(u_ref[:, :, j * LANES:(j + 1) * LANES], (1, 0, 2))
    us = [x[s] for s in range(steps)]
    bu = jnp.dot(jnp.concatenate(us, axis=0).astype(BF16), b_ref[j], preferred_element_type=F32)
    return us, [bu[s * tr:(s + 1) * tr] for s in range(steps)]


def _s5_local_kernel(u_ref, b_ref, ar_ref, ai_ref, z_ref, *, n_tiles):
    sw = b_ref.shape[2]
    half = sw // 2
    for j in range(n_tiles):
        ar, ai = ar_ref[j:j + 1, :], ai_ref[j:j + 1, :]
        xr = xi = None
        for bu in _s5_inputs(u_ref, b_ref, j)[1]:
            xr, xi = _s5_step(xr, xi, ar, ai, bu, half)
        z_ref[:, j * sw:j * sw + half] = xr
        z_ref[:, j * sw + half:(j + 1) * sw] = xi


def _s5_carry_kernel(z_ref, ar_ref, ai_ref, xc_ref, xf_ref, *, batch, n_chunks):
    tiles, _, half = ar_ref.shape
    sw = 2 * half

    def flat(states):
        return jnp.concatenate([v for pair in states for v in pair], axis=1)

    def body(c, carry):
        new = []
        for b in range(batch):
            row = b * n_chunks + c
            xc_ref[pl.ds(row, 1), :] = flat(carry[b])
            z = z_ref[pl.ds(row, 1), :]
            nxt = []
            for t, (xr, xi) in enumerate(carry[b]):
                ar, ai = ar_ref[t], ai_ref[t]
                zr, zi = z[:, t * sw:t * sw + half], z[:, t * sw + half:(t + 1) * sw]
                nxt.append((ar * xr - ai * xi + zr, ar * xi + ai * xr + zi))
            new.append(tuple(nxt))
        return tuple(new)

    zero = jnp.zeros((1, half), F32)
    init = tuple(tuple((zero, zero) for _ in range(tiles)) for _ in range(batch))
    final = lax.fori_loop(0, n_chunks, body, init, unroll=math.gcd(n_chunks, 4))
    for b in range(batch):
        xf_ref[b:b + 1, :] = flat(final[b])


def _s5_tile_outputs(u_ref, y_ref, b_ref, c_ref, ar, ai, d, xr, xi, j):
    tr, steps, _ = u_ref.shape
    half = b_ref.shape[2] // 2
    us, bus = _s5_inputs(u_ref, b_ref, j)
    xs = []
    for bu in bus:
        xr, xi = _s5_step(xr, xi, ar, ai, bu, half)
        xs.append(jnp.concatenate([xr, xi], axis=1).astype(BF16))
    y = jnp.dot(jnp.concatenate(xs, axis=0), c_ref[j], preferred_element_type=F32)
    ys = jnp.stack([y[s * tr:(s + 1) * tr] + d * us[s] for s in range(steps)], axis=0)
    y_ref[:, :, j * LANES:(j + 1) * LANES] = jnp.transpose(ys, (1, 0, 2))
    return xr, xi


def _s5_out_kernel(u_ref, x0_ref, b_ref, c_ref, ar_ref, ai_ref, d_ref, y_ref, *, n_tiles):
    sw = b_ref.shape[2]
    half = sw // 2
    for j in range(n_tiles):
        _s5_tile_outputs(u_ref, y_ref, b_ref, c_ref, ar_ref[j:j + 1, :], ai_ref[j:j + 1, :],
                         d_ref[:, j * LANES:(j + 1) * LANES],
                         x0_ref[:, j * sw:j * sw + half], x0_ref[:, j * sw + half:(j + 1) * sw], j)


def _s5_sample_kernel(u_ref, re_ref, im_ref, b_ref, c_ref, ar_ref, ai_ref, d_ref,
                      y_ref, ore_ref, oim_ref, *, n_tiles):
    half = b_ref.shape[2] // 2
    for j in range(n_tiles):
        rows = slice(j * half, (j + 1) * half)
        xr, xi = _s5_tile_outputs(u_ref, y_ref, b_ref, c_ref, ar_ref[j:j + 1, :],
                                  ai_ref[j:j + 1, :], d_ref[:, j * LANES:(j + 1) * LANES],
                                  re_ref[rows, :].T, im_ref[rows, :].T, j)
        ore_ref[rows, :] = xr.T
        oim_ref[rows, :] = xi.T


def _s5_params(a_re, a_im, log_step, b_re, b_im, c_re, c_im, chunk_steps):
    g, p = a_re.shape
    nt = g // GROUPS_PER_TILE
    a_re, a_im = a_re.astype(F32), a_im.astype(F32)
    step = jnp.exp(log_step.astype(F32))[:, None]

    def discretise(n_steps):
        mag = jnp.exp(a_re * step * n_steps)
        return mag * jnp.cos(a_im * step * n_steps), mag * jnp.sin(a_im * step * n_steps)

    lr, li = discretise(1.0)
    cr, ci = discretise(float(chunk_steps))
    den = a_re * a_re + a_im * a_im
    fr = ((lr - 1.0) * a_re + li * a_im) / den
    fi = (li * a_re - (lr - 1.0) * a_im) / den
    b_re, b_im = b_re.astype(F32), b_im.astype(F32)
    bb_re = fr[..., None] * b_re - fi[..., None] * b_im
    bb_im = fr[..., None] * b_im + fi[..., None] * b_re
    eye = jnp.eye(GROUPS_PER_TILE, dtype=BF16)

    bb = jnp.stack([bb_re, bb_im], axis=0).reshape(2, nt, GROUPS_PER_TILE, p, S5_GROUP)
    bb = jnp.transpose(bb, (1, 2, 4, 0, 3)).astype(BF16)
    b_blk = bb[:, :, :, :, None, :] * eye[None, :, None, None, :, None]
    b_blk = b_blk.reshape(nt, LANES, 2 * GROUPS_PER_TILE * p)

    cc = jnp.stack([c_re.astype(F32), -c_im.astype(F32)], axis=0)
    cc = cc.reshape(2, nt, GROUPS_PER_TILE, S5_GROUP, p)
    cc = jnp.transpose(cc, (1, 0, 2, 4, 3)).astype(BF16)
    c_blk = cc[:, :, :, :, None, :] * eye[None, None, :, None, :, None]
    c_blk = c_blk.reshape(nt, 2 * GROUPS_PER_TILE * p, LANES)

    def tiles(z):
        return z.reshape(nt, GROUPS_PER_TILE * p)

    return b_blk, c_blk, tiles(lr), tiles(li), tiles(cr), tiles(ci)


def _tiles_to_state(x, g, p):
    nb = x.shape[0]
    nt = g // GROUPS_PER_TILE
    st = x.reshape(nb, nt, 2, GROUPS_PER_TILE, p)
    return st[:, :, 0].reshape(nb, g, p), st[:, :, 1].reshape(nb, g, p)


def _s5_weight_specs(prm, d2):
    return [pl.BlockSpec(a.shape, lambda i, nd=a.ndim: (0,) * nd) for a in (*prm, d2)]


def _s5_outputs(u3, x0, prm, d_skip, tr):
    r, steps, width = u3.shape
    nt, _, sw = prm[0].shape
    d2 = d_skip.reshape(1, width).astype(F32)
    tokens = pl.BlockSpec((tr, steps, width), lambda i: (i, 0, 0))
    return pl.pallas_call(
        functools.partial(_s5_out_kernel, n_tiles=nt),
        grid=(r // tr,),
        in_specs=[tokens, pl.BlockSpec((tr, nt * sw), lambda i: (i, 0))]
                 + _s5_weight_specs(prm, d2),
        out_specs=tokens,
        out_shape=jax.ShapeDtypeStruct(u3.shape, F32),
        compiler_params=_params("arbitrary"),
        name="s5_outputs",
    )(u3, x0, *prm, d2)


def _s5_prompt(u, batch, seq, prm_all, d_skip, g, p):
    b_blk, c_blk, ar, ai, ar_c, ai_c = prm_all
    steps = STEP_CHUNK
    n_chunks = seq // steps
    r = batch * n_chunks
    width = u.shape[1]
    u3 = u.reshape(r, steps, width)
    nt, _, sw = b_blk.shape
    tr = 128
    z = pl.pallas_call(
        functools.partial(_s5_local_kernel, n_tiles=nt),
        grid=(r // tr,),
        in_specs=[pl.BlockSpec((tr, steps, width), lambda i: (i, 0, 0)),
                  pl.BlockSpec(b_blk.shape, lambda i: (0, 0, 0)),
                  pl.BlockSpec(ar.shape, lambda i: (0, 0)),
                  pl.BlockSpec(ai.shape, lambda i: (0, 0))],
        out_specs=pl.BlockSpec((tr, nt * sw), lambda i: (i, 0)),
        out_shape=jax.ShapeDtypeStruct((r, nt * sw), F32),
        compiler_params=_params("arbitrary"),
        name="s5_chunk_states",
    )(u3, b_blk, ar, ai)
    half = sw // 2
    tiles = math.gcd(nt, CARRY_TILES_PER_STEP)
    xc, xf = pl.pallas_call(
        functools.partial(_s5_carry_kernel, batch=batch, n_chunks=n_chunks),
        grid=(nt // tiles,),
        in_specs=[pl.BlockSpec((r, tiles * sw), lambda j: (0, j)),
                  pl.BlockSpec((tiles, 1, half), lambda j: (j, 0, 0)),
                  pl.BlockSpec((tiles, 1, half), lambda j: (j, 0, 0))],
        out_specs=[pl.BlockSpec((r, tiles * sw), lambda j: (0, j)),
                   pl.BlockSpec((batch, tiles * sw), lambda j: (0, j))],
        out_shape=[jax.ShapeDtypeStruct((r, nt * sw), F32),
                   jax.ShapeDtypeStruct((batch, nt * sw), F32)],
        compiler_params=_params("arbitrary"),
        name="s5_carry",
    )(z, ar_c.reshape(nt, 1, half), ai_c.reshape(nt, 1, half))
    y3 = _s5_outputs(u3, xc, (b_blk, c_blk, ar, ai), d_skip, tr)
    s_re, s_im = _tiles_to_state(xf, g, p)
    return y3.reshape(batch * seq, width), s_re, s_im


def _s5_sample(u, nb, t_new, prm_all, d_skip, s_re, s_im):
    prm = prm_all[:4]
    g, p = s_re.shape[1], s_re.shape[2]
    width = u.shape[1]
    nt = prm[0].shape[0]
    d2 = d_skip.reshape(1, width).astype(F32)
    state_t = lambda a: jnp.transpose(a.astype(F32), (1, 2, 0)).reshape(g * p, nb)
    tokens = pl.BlockSpec((nb, t_new, width), lambda i: (0, 0, 0))
    state = pl.BlockSpec((g * p, nb), lambda i: (0, 0))
    y3, n_re, n_im = pl.pallas_call(
        functools.partial(_s5_sample_kernel, n_tiles=nt),
        grid=(1,),
        in_specs=[tokens, state, state] + _s5_weight_specs(prm, d2),
        out_specs=[tokens, state, state],
        out_shape=[jax.ShapeDtypeStruct((nb, t_new, width), F32)]
                  + [jax.ShapeDtypeStruct((g * p, nb), F32)] * 2,
        compiler_params=_params("arbitrary"),
        name="s5_sample",
    )(u.reshape(nb, t_new, width), state_t(s_re), state_t(s_im), *prm, d2)
    back = lambda a: jnp.transpose(a.reshape(g, p, nb), (2, 0, 1))
    return y3.reshape(nb * t_new, width), back(n_re), back(n_im)


def _merge_kernel(attn_ref, y_ref, ga_ref, gb_ref, wa_ref, wg_ref, o_ref):
    d = o_ref.shape[1]
    y_a = jnp.dot(attn_ref[...], wa_ref[...], preferred_element_type=F32)
    z = _gelu_tanh(y_ref[...]).astype(BF16)
    glu = jnp.dot(z, wg_ref[...], preferred_element_type=F32)
    y_b = glu[:, :d] * _sigmoid(glu[:, d:])
    o_ref[...] = (_sigmoid(ga_ref[...]) * y_a + _sigmoid(gb_ref[...]) * y_b).astype(o_ref.dtype)


def _merge(attn, y, ga, gb, w_attn, w_glu, tm):
    m, d = ga.shape
    row = lambda w: pl.BlockSpec((tm, w), lambda i: (i, 0))
    return pl.pallas_call(
        _merge_kernel,
        grid=(m // tm,),
        in_specs=[row(attn.shape[1]), row(y.shape[1]), row(d), row(d),
                  pl.BlockSpec(w_attn.shape, lambda i: (0, 0)),
                  pl.BlockSpec(w_glu.shape, lambda i: (0, 0))],
        out_specs=row(d),
        out_shape=jax.ShapeDtypeStruct((m, d), BF16),
        compiler_params=_params("arbitrary"),
        name="merge",
    )(attn, y, ga, gb, w_attn, w_glu)


def _out_proj_kernel(m_ref, x_ref, w_ref, g_ref, x1_ref, h_ref):
    x1 = x_ref[...] + jnp.dot(m_ref[...], w_ref[...], preferred_element_type=F32)
    x1_ref[...] = x1
    ms = jnp.mean(x1 * x1, axis=-1, keepdims=True)
    h_ref[...] = (x1 * lax.rsqrt(ms + EPS) * g_ref[...]).astype(h_ref.dtype)


def _out_proj(merged, x, w_out, g, tm):
    m, d = x.shape
    row = pl.BlockSpec((tm, d), lambda i: (i, 0))
    return pl.pallas_call(
        _out_proj_kernel,
        grid=(m // tm,),
        in_specs=[row, row, pl.BlockSpec(w_out.shape, lambda i: (0, 0)),
                  pl.BlockSpec((1, d), lambda i: (0, 0))],
        out_specs=[row, row],
        out_shape=[jax.ShapeDtypeStruct((m, d), F32), jax.ShapeDtypeStruct((m, d), BF16)],
        compiler_params=_params("arbitrary"),
        name="out_proj",
    )(merged, x, w_out, g.reshape(1, d))


FF_SUB = 1


def _shift_rows(up, prev, k):
    body = pltpu.roll(up, k, axis=0)
    row = lax.broadcasted_iota(jnp.int32, prev.shape, 0)
    head = jnp.where(row < k, pltpu.roll(prev, k, axis=0), body[:SUBLANES])
    return jnp.concatenate([head, body[SUBLANES:]], axis=0)


def _conv_taps(up, u1, u2, cw, cb):
    return cb + cw[0:1, :] * u2 + cw[1:2, :] * u1 + cw[2:3, :] * up


def _ffn_finish(j, n_j, final_norm, acts, wd_ref, x1_ref, g_ref, y_ref):
    y_ref[...] += jnp.dot(jnp.concatenate(acts, axis=1), wd_ref[...], preferred_element_type=F32)

    @pl.when(j == n_j - 1)
    def _():
        x2 = x1_ref[...] + y_ref[...]
        if final_norm:
            ms = jnp.mean(x2 * x2, axis=-1, keepdims=True)
            x2 = x2 * lax.rsqrt(ms + EPS) * g_ref[...]
        y_ref[...] = x2


def _ffn_prompt_kernel(h_ref, wg_ref, wv_ref, cwg_ref, cwv_ref, cbg_ref, cbv_ref, wd_ref, x1_ref,
                       g_ref, y_ref, convg_ref, convv_ref, carry_ref,
                       *, tm, tiles_per_seq, n_j, final_norm):
    i, j = pl.program_id(0), pl.program_id(1)

    @pl.when(i % tiles_per_seq == 0)
    def _():
        carry_ref[j] = jnp.zeros(carry_ref.shape[1:], F32)

    @pl.when(j == 0)
    def _():
        y_ref[...] = jnp.zeros(y_ref.shape, F32)

    h = h_ref[...]
    prev = carry_ref[j]
    sub = FF_TILE // FF_SUB
    acts, tails = [], ([], [])
    for s in range(FF_SUB):
        cs = slice(s * sub, (s + 1) * sub)
        mixed = []
        for half, (w_ref, cw_ref, cb_ref) in enumerate(((wg_ref, cwg_ref, cbg_ref),
                                                        (wv_ref, cwv_ref, cbv_ref))):
            up = jnp.dot(h, w_ref[:, cs], preferred_element_type=F32)
            p8 = prev[:, half * FF_TILE + s * sub:half * FF_TILE + (s + 1) * sub]
            mixed.append(_conv_taps(up, _shift_rows(up, p8, 1), _shift_rows(up, p8, 2),
                                    cw_ref[:, cs], cb_ref[:, cs]))
            tails[half].append(up[tm - SUBLANES:])
        gate, val = mixed
        acts.append((gate * _sigmoid(gate) * val).astype(BF16))
    tail_g = jnp.concatenate(tails[0], axis=1)
    tail_v = jnp.concatenate(tails[1], axis=1)
    carry_ref[j] = jnp.concatenate([tail_g, tail_v], axis=1)
    convg_ref[...] = tail_g
    convv_ref[...] = tail_v
    _ffn_finish(j, n_j, final_norm, acts, wd_ref, x1_ref, g_ref, y_ref)


def _ffn_sample_kernel(h_ref, wg_ref, wv_ref, cwg_ref, cwv_ref, cbg_ref, cbv_ref, wd_ref, x1_ref,
                       g_ref, stg_ref, stv_ref, y_ref, convg_ref, convv_ref,
                       *, nb, t_new, n_j, final_norm):
    j = pl.program_id(0)

    @pl.when(j == 0)
    def _():
        y_ref[...] = jnp.zeros(y_ref.shape, F32)

    h = h_ref[...]
    sub = FF_TILE // FF_SUB
    acts = []
    for s in range(FF_SUB):
        cs = slice(s * sub, (s + 1) * sub)
        mixed = []
        for w_ref, cw_ref, cb_ref, st_ref, conv_ref in (
                (wg_ref, cwg_ref, cbg_ref, stg_ref, convg_ref),
                (wv_ref, cwv_ref, cbv_ref, stv_ref, convv_ref)):
            up = jnp.dot(h, w_ref[:, cs], preferred_element_type=F32)
            s0, s1 = st_ref[:, 0, cs], st_ref[:, 1, cs]
            u1 = jnp.concatenate([s1, up[:(t_new - 1) * nb]], axis=0)
            u2 = jnp.concatenate([s0, s1, up[:(t_new - 2) * nb]], axis=0)
            mixed.append(_conv_taps(up, u1, u2, cw_ref[:, cs], cb_ref[:, cs]))
            conv_ref[:, 0, cs] = up[(t_new - 2) * nb:(t_new - 1) * nb]
            conv_ref[:, 1, cs] = up[(t_new - 1) * nb:]
        gate, val = mixed
        acts.append((gate * _sigmoid(gate) * val).astype(BF16))
    _ffn_finish(j, n_j, final_norm, acts, wd_ref, x1_ref, g_ref, y_ref)


def _ffn_weight_specs(d, n_j, ix):
    gate = lambda rows: pl.BlockSpec((rows, FF_TILE), ix(lambda j: (0, j)))
    val = lambda rows: pl.BlockSpec((rows, FF_TILE), ix(lambda j: (0, n_j + j)))
    return [gate(d), val(d), gate(3), val(3), gate(1), val(1),
            pl.BlockSpec((FF_TILE, d), ix(lambda j: (j, 0)))]


def _ffn_prompt(h2, x1, wu, cw, cb, wd, g, batch, seq, tm, final_norm):
    m, d = x1.shape
    d_ff = wd.shape[0]
    n_j = d_ff // FF_TILE
    tps = seq // tm
    kern = functools.partial(_ffn_prompt_kernel, tm=tm, tiles_per_seq=tps, n_j=n_j,
                             final_norm=final_norm)
    ix = lambda f: (lambda i, j: f(j))
    conv_spec = pl.BlockSpec((None, SUBLANES, FF_TILE), lambda i, j: (i, 0, j))
    conv_shape = jax.ShapeDtypeStruct((m // tm, SUBLANES, d_ff), F32)
    y, cg, cv = pl.pallas_call(
        kern,
        grid=(m // tm, n_j),
        in_specs=[pl.BlockSpec((tm, d), lambda i, j: (i, 0))] + _ffn_weight_specs(d, n_j, ix)
                 + [pl.BlockSpec((tm, d), lambda i, j: (i, 0)),
                    pl.BlockSpec((1, d), lambda i, j: (0, 0))],
        out_specs=[pl.BlockSpec((tm, d), lambda i, j: (i, 0)), conv_spec, conv_spec],
        out_shape=[jax.ShapeDtypeStruct((m, d), F32), conv_shape, conv_shape],
        scratch_shapes=[pltpu.VMEM((n_j, SUBLANES, 2 * FF_TILE), F32)],
        compiler_params=_params("arbitrary", "arbitrary"),
        name="ffn_prompt",
    )(h2, wu, wu, cw, cw, cb, cb, wd, x1, g.reshape(1, d))
    return y, jnp.concatenate([cg, cv], axis=-1)[tps - 1::tps, SUBLANES - 2:]


def _ffn_sample(h2, x1, wu, cw, cb, wd, g, state, nb, t_new, final_norm):
    m, d = x1.shape
    d_ff = wd.shape[0]
    n_j = d_ff // FF_TILE
    kern = functools.partial(_ffn_sample_kernel, nb=nb, t_new=t_new, n_j=n_j,
                             final_norm=final_norm)
    ix = lambda f: f
    st_gate = pl.BlockSpec((nb, 2, FF_TILE), lambda j: (0, 0, j))
    st_val = pl.BlockSpec((nb, 2, FF_TILE), lambda j: (0, 0, n_j + j))
    conv_spec = pl.BlockSpec((nb, 2, FF_TILE), lambda j: (0, 0, j))
    y, cg, cv = pl.pallas_call(
        kern,
        grid=(n_j,),
        in_specs=[pl.BlockSpec((m, d), lambda j: (0, 0))] + _ffn_weight_specs(d, n_j, ix)
                 + [pl.BlockSpec((m, d), lambda j: (0, 0)),
                    pl.BlockSpec((1, d), lambda j: (0, 0)),
                    st_gate, st_val],
        out_specs=[pl.BlockSpec((m, d), lambda j: (0, 0)), conv_spec, conv_spec],
        out_shape=[jax.ShapeDtypeStruct((m, d), F32),
                   jax.ShapeDtypeStruct((nb, 2, d_ff), F32),
                   jax.ShapeDtypeStruct((nb, 2, d_ff), F32)],
        compiler_params=_params("arbitrary"),
        name="ffn_sample",
    )(h2, wu, wu, cw, cw, cb, cb, wd, x1, g.reshape(1, d), state, state)
    return y, jnp.concatenate([cg, cv], axis=-1)


def kernel(x_prompt, x_sample, cache_k, cache_v, cache_kidx, state_s5_re, state_s5_im, state_ffn_conv, page_table, norm_mix, w_in, w_attn_proj, s5_a_re, s5_a_im, s5_log_step, s5_b_re, s5_b_im, s5_c_re, s5_c_im, s5_d, w_glu, w_out, norm_ffn, w_up, conv_w, conv_b, w_down, norm_final):
    depth = w_in.shape[0]
    batch, seq, d_model = x_prompt.shape
    nb, t_new, _ = x_sample.shape
    d_head = cache_k.shape[-1]
    d_idx = cache_kidx.shape[-1]
    attn_w = N_HEADS * d_head
    kv_w = N_KV_HEADS * d_head
    qi_w = IDX_HEADS * d_idx
    groups, p_state = s5_a_re.shape[1], s5_a_re.shape[2]
    s5_w = groups * S5_GROUP

    xp = x_prompt.reshape(batch * seq, d_model)
    xs = x_sample.reshape(nb * t_new, d_model)
    tm_p = min(512, batch * seq)
    tm_s = nb * t_new

    outs = {name: [] for name in ("kp", "vp", "kip", "srp", "sip", "cp",
                                  "ks", "vs", "kis", "srs", "sis", "cs")}
    for l in range(depth):
        w_t = jnp.swapaxes(w_in[l], 0, 1).astype(BF16)
        o = 0
        seg = {}
        for name, width in (("q", attn_w), ("k", kv_w), ("v", kv_w), ("qi", qi_w), ("ki", d_idx),
                            ("wi", IDX_HEADS), ("u", s5_w), ("ga", d_model), ("gb", d_model)):
            seg[name] = w_t[o:o + width]
            o += width
        w_q_grouped = _group_major_heads(seg["q"], d_head)
        w_u, w_ga, w_gb = seg["u"], seg["ga"], seg["gb"]
        pad = jnp.zeros((LANES - d_idx - IDX_HEADS, d_model), BF16)
        w_small = jnp.concatenate([seg["qi"], seg["k"], seg["v"], seg["ki"], seg["wi"], pad], axis=0)
        w_qiw = jnp.concatenate([seg["qi"], seg["ki"], seg["wi"], pad], axis=0)
        w_kv_t = jnp.concatenate([seg["k"], seg["v"], seg["ki"]], axis=0)
        w_attn_grouped = _group_major_heads(w_attn_proj[l].astype(BF16), d_head)
        w_g = w_glu[l].astype(BF16)
        w_o = w_out[l].astype(BF16)
        wu = w_up[l].astype(BF16)
        cw = conv_w[l].astype(F32)
        cb = conv_b[l].astype(F32).reshape(1, -1)
        wd = w_down[l].astype(BF16)
        s5p = _s5_params(s5_a_re[l], s5_a_im[l], s5_log_step[l], s5_b_re[l], s5_b_im[l],
                         s5_c_re[l], s5_c_im[l], STEP_CHUNK)
        last = l == depth - 1

        def project(x, tm, ws):
            (q, u, small), h = _norm_matmul(x, norm_mix[l], [w_q_grouped, w_u, ws], tm,
                                            "norm_proj_q_u_small")
            ga, gb = _matmul(h, [w_ga, w_gb], tm, F32, "proj_gates")
            return h, q, u, ga, gb, small

        h, q, u, ga, gb, qiw = project(xp, tm_p, w_qiw)
        k_t, v_t, ki_t, kb_t, vb_t, kib_t = _kv_transposed(h, w_kv_t, batch, seq, kv_w, d_idx,
                                                            min(512, seq))
        k_sel = min(TOPK_MAX, seq // 4)
        attn = _prompt_attention(q, qiw, kib_t, kb_t, vb_t, batch, seq, d_head, d_idx, k_sel,
                                 min(256, seq))
        y5, srp, sip = _s5_prompt(u, batch, seq, s5p, s5_d[l], groups, p_state)
        merged = _merge(attn, y5, ga, gb, w_attn_grouped, w_g, min(256, batch * seq))
        x1, h2 = _out_proj(merged, xp, w_o, norm_ffn[l], tm_p)
        xp, conv_p = _ffn_prompt(h2, x1, wu, cw, cb, wd, norm_final, batch, seq, tm_p, last)

        def heads_last(a_t):
            return jnp.transpose(a_t.reshape(batch, N_KV_HEADS, d_head, seq), (0, 3, 1, 2))

        outs["kp"].append(heads_last(k_t)); outs["vp"].append(heads_last(v_t))
        outs["kip"].append(jnp.swapaxes(ki_t, 1, 2))
        outs["srp"].append(srp); outs["sip"].append(sip); outs["cp"].append(conv_p)

        _, q, u, ga, gb, small = project(xs, tm_s, w_small)
        attn = _sample_attention(q, small, cache_k[l], cache_v[l], cache_kidx[l], page_table,
                                 t_new, d_head, d_idx)
        y5, srs, sis = _s5_sample(u, nb, t_new, s5p, s5_d[l], state_s5_re[l], state_s5_im[l])
        merged = _merge(attn, y5, ga, gb, w_attn_grouped, w_g, min(256, nb * t_new))
        x1, h2 = _out_proj(merged, xs, w_o, norm_ffn[l], tm_s)

        def time_major(a):
            return a.reshape(nb, t_new, -1).swapaxes(0, 1).reshape(nb * t_new, -1)

        y_tm, conv_s = _ffn_sample(time_major(h2), time_major(x1), wu, cw, cb, wd, norm_final,
                                   state_ffn_conv[l].astype(F32), nb, t_new, last)
        xs = y_tm.reshape(t_new, nb, d_model).swapaxes(0, 1).reshape(nb * t_new, d_model)
        k_new = small[:, qi_w:qi_w + kv_w].reshape(nb, t_new, N_KV_HEADS, d_head)
        v_new = small[:, qi_w + kv_w:qi_w + 2 * kv_w].reshape(nb, t_new, N_KV_HEADS, d_head)
        ki_new = small[:, qi_w + 2 * kv_w:qi_w + 2 * kv_w + d_idx].reshape(nb, t_new, d_idx)
        outs["ks"].append(k_new); outs["vs"].append(v_new); outs["kis"].append(ki_new)
        outs["srs"].append(srs); outs["sis"].append(sis); outs["cs"].append(conv_s)

    stk = {name: jnp.stack(v) for name, v in outs.items()}
    return (xp.reshape(batch, seq, d_model), xs.reshape(nb, t_new, d_model),
            stk["kp"], stk["vp"], stk["kip"], stk["srp"], stk["sip"], stk["cp"],
            stk["ks"], stk["vs"], stk["kis"], stk["srs"], stk["sis"], stk["cs"])
```
